```python
import jax, jax.numpy as jnp
from jax import lax
import numpy as np

D_MODEL = 1024
BATCH = 4
SEQ = 4096
DEPTH = 1

CHUNK = 64
D_MIX = D_MODEL
HG_WIDTH = D_MIX // 2
HG_HEAD_DIM = 128
HG_HEADS = HG_WIDTH // HG_HEAD_DIM
SSD_WIDTH = D_MIX - HG_WIDTH
SSD_HEAD_DIM = 64
SSD_HEADS = SSD_WIDTH // SSD_HEAD_DIM
SSD_GROUPS = 2
SSD_STATE = 128
SSD_CONV = 4
SSD_CONV_DIM = SSD_WIDTH + 2 * SSD_GROUPS * SSD_STATE
SPLITS = (HG_WIDTH, 2 * HG_WIDTH, 3 * HG_WIDTH, 4 * HG_WIDTH,
          4 * HG_WIDTH + SSD_WIDTH, 4 * HG_WIDTH + SSD_WIDTH + SSD_CONV_DIM)
IN_COLS = 4 * HG_WIDTH + SSD_WIDTH + SSD_CONV_DIM + SSD_HEADS
N_EXPERTS = 32
TOP_K = 4
D_EXPERT = D_MODEL
SWIGLU_LIMIT = 7.0
SWIGLU_ALPHA = 1.702
EXPERT_BLOCK = 256
DEEPNORM_ALPHA = (2 * DEPTH) ** 0.25
DEEPNORM_BETA = (8 * DEPTH) ** -0.25
LN_EPS = 1e-5
RMS_EPS = 1e-5

kernel_name = 'hybrid_hgrn2_ssd_moe_deepnorm'


def layer_norm(x, g, b):
    xf = x.astype(jnp.float32)
    mu = jnp.mean(xf, -1, keepdims=True)
    var = jnp.mean(jnp.square(xf - mu), -1, keepdims=True)
    y = (xf - mu) * lax.rsqrt(var + LN_EPS) * g.astype(jnp.float32) + b.astype(jnp.float32)
    return y.astype(x.dtype)


def rms_norm(x, w):
    xf = x.astype(jnp.float32)
    return xf * lax.rsqrt(jnp.mean(jnp.square(xf), -1, keepdims=True) + RMS_EPS) * w.astype(jnp.float32)


def hgrn2_mixer(q_raw, f_raw, i_raw, g_raw, lb, norm_w):
    f32 = jnp.float32
    Bsz, L, _ = q_raw.shape
    N = L // CHUNK
    q = jax.nn.silu(q_raw.astype(f32))
    f = lb + (1.0 - lb) * jax.nn.sigmoid(f_raw.astype(f32))
    log_f = jnp.log(f)
    k = 1.0 - f
    v = i_raw.astype(f32)

    def to_chunks(t):
        return t.reshape(Bsz, N, CHUNK, HG_HEADS, HG_HEAD_DIM).transpose(1, 0, 3, 2, 4)

    causal = jnp.tril(jnp.ones((CHUNK, CHUNK), dtype=bool))

    def step(S, inp):
        qc, kc, vc, lfc = inp
        b = jnp.cumsum(lfc, axis=2)
        rel = jnp.where(causal[:, :, None], b[:, :, :, None, :] - b[:, :, None, :, :], -jnp.inf)
        scores = jnp.einsum('bhtk,bhtsk,bhsk->bhts', qc, jnp.exp(rel), kc)
        o = scores @ vc + jnp.einsum('bhtk,bhkv->bhtv', qc * jnp.exp(b), S)
        b_end = b[:, :, -1:, :]
        S = jnp.exp(b_end[:, :, 0, :, None]) * S + jnp.einsum('bhsk,bhsv->bhkv', kc * jnp.exp(b_end - b), vc)
        return S, o

    S0 = jnp.zeros((Bsz, HG_HEADS, HG_HEAD_DIM, HG_HEAD_DIM), f32)
    _, o = lax.scan(step, S0, (to_chunks(q), to_chunks(k), to_chunks(v), to_chunks(log_f)))
    o = o.transpose(1, 0, 3, 2, 4).reshape(Bsz, L, HG_HEADS, HG_HEAD_DIM)
    o = rms_norm(o, norm_w).reshape(Bsz, L, HG_WIDTH)
    return o * jax.nn.silu(g_raw.astype(f32))


def ssd_mixer(z, xbc, dt_raw, conv_w, conv_b, dt_bias, a_log, d_skip, norm_w):
    f32 = jnp.float32
    Bsz, L, _ = z.shape
    N = L // CHUNK
    xbc = lax.conv_general_dilated(xbc.astype(f32), conv_w.astype(f32)[:, None, :], window_strides=(1,),
                                   padding=[(SSD_CONV - 1, 0)], dimension_numbers=('NWC', 'WIO', 'NWC'),
                                   feature_group_count=SSD_CONV_DIM)
    xbc = jax.nn.silu(xbc + conv_b.astype(f32))
    xs, Bm, Cm = jnp.split(xbc, [SSD_WIDTH, SSD_WIDTH + SSD_GROUPS * SSD_STATE], axis=-1)
    rep = SSD_HEADS // SSD_GROUPS
    xs = xs.reshape(Bsz, N, CHUNK, SSD_HEADS, SSD_HEAD_DIM)
    Bm = jnp.repeat(Bm.reshape(Bsz, N, CHUNK, SSD_GROUPS, SSD_STATE), rep, axis=3)
    Cm = jnp.repeat(Cm.reshape(Bsz, N, CHUNK, SSD_GROUPS, SSD_STATE), rep, axis=3)
    dt = jax.nn.softplus(dt_raw.astype(f32) + dt_bias.astype(f32)).reshape(Bsz, N, CHUNK, SSD_HEADS)
    A = -jnp.exp(a_log.astype(f32))
    cs = jnp.cumsum((dt * A).transpose(0, 3, 1, 2), axis=-1)
    xdt = xs * dt[..., None]
    causal = jnp.tril(jnp.ones((CHUNK, CHUNK), dtype=bool))
    seg = jnp.where(causal, cs[..., :, None] - cs[..., None, :], -jnp.inf)
    y_diag = jnp.einsum('bclhn,bcshn,bhcls,bcshp->bclhp', Cm, Bm, jnp.exp(seg), xdt)
    states = jnp.einsum('bclhn,bhcl,bclhp->bchpn', Bm, jnp.exp(cs[..., -1:] - cs), xdt)
    chunk_decay = jnp.exp(cs[..., -1])

    def pass_state(h, inp):
        s, a = inp
        return a[..., None, None] * h + s, h

    h0 = jnp.zeros((Bsz, SSD_HEADS, SSD_HEAD_DIM, SSD_STATE), f32)
    _, prev = lax.scan(pass_state, h0, (states.transpose(1, 0, 2, 3, 4), chunk_decay.transpose(2, 0, 1)))
    prev = prev.transpose(1, 0, 2, 3, 4)
    y_off = jnp.einsum('bclhn,bchpn,bhcl->bclhp', Cm, prev, jnp.exp(cs))
    y = y_diag + y_off + xs * d_skip.astype(f32)[:, None]
    y = y.reshape(Bsz, L, SSD_WIDTH) * jax.nn.silu(z.astype(f32))
    y = rms_norm(y.reshape(Bsz, L, SSD_GROUPS, -1), norm_w.reshape(SSD_GROUPS, -1))
    return y.reshape(Bsz, L, SSD_WIDTH)


def moe_ffn(x, router_w, router_b, w_gate, b_gate, w_up, b_up, w_down, b_down):
    f32 = jnp.float32
    Bsz, L, D = x.shape
    T = Bsz * L
    xt = x.reshape(T, D)
    logits = (xt @ router_w + router_b).astype(f32)
    top_val, top_idx = lax.top_k(logits, TOP_K)
    gates = jax.nn.softmax(top_val, axis=-1)
    n_assign = T * TOP_K
    n_blocks = n_assign // EXPERT_BLOCK + N_EXPERTS
    cap = n_blocks * EXPERT_BLOCK
    flat_e = top_idx.reshape(-1)
    order = jnp.argsort(flat_e)
    sorted_e = flat_e[order]
    counts = jnp.bincount(flat_e, length=N_EXPERTS)
    offsets = jnp.cumsum(counts) - counts
    padded = (counts + EXPERT_BLOCK - 1) // EXPERT_BLOCK * EXPERT_BLOCK
    padded_end = jnp.cumsum(padded)
    padded_start = padded_end - padded
    dest = padded_start[sorted_e] + jnp.arange(n_assign) - offsets[sorted_e]
    row_tok = jnp.zeros((cap,), jnp.int32).at[dest].set((order // TOP_K).astype(jnp.int32))
    row_gate = jnp.zeros((cap,), f32).at[dest].set(gates.reshape(-1)[order])
    block_expert = jnp.minimum(jnp.searchsorted(padded_end, jnp.arange(n_blocks) * EXPERT_BLOCK, side='right'),
                               N_EXPERTS - 1)
    x_rows = xt[row_tok].reshape(n_blocks, EXPERT_BLOCK, D)

    def expert_block(args):
        xb, e = args
        hg = jnp.minimum(xb @ w_gate[e] + b_gate[e], SWIGLU_LIMIT)
        hu = jnp.clip(xb @ w_up[e] + b_up[e], -SWIGLU_LIMIT, SWIGLU_LIMIT)
        h = (hu + 1.0) * (hg * jax.nn.sigmoid(SWIGLU_ALPHA * hg))
        return h @ w_down[e] + b_down[e]

    y_rows = lax.map(expert_block, (x_rows, block_expert)).reshape(cap, D)
    out = jnp.zeros((T, D), f32).at[row_tok].add(y_rows.astype(f32) * row_gate[:, None])
    return out.reshape(Bsz, L, D).astype(x.dtype)


def setup_inputs(seed: int = 0) -> dict:
    key = jax.random.key(seed)
    ks = jax.random.split(key, 24)
    f32 = jnp.float32
    nrm = lambda k, s: jax.random.normal(k, s, f32)
    col_scale = np.ones((IN_COLS,), np.float32)
    col_scale[2 * HG_WIDTH:3 * HG_WIDTH] = DEEPNORM_BETA
    col_scale[4 * HG_WIDTH + SSD_WIDTH:4 * HG_WIDTH + 2 * SSD_WIDTH] = DEEPNORM_BETA
    dt0 = jnp.exp(jax.random.uniform(ks[6], (DEPTH, SSD_HEADS), f32) * (np.log(0.1) - np.log(1e-3)) + np.log(1e-3))
    return {
        'x': nrm(ks[0], (BATCH, SEQ, D_MODEL)),
        'w_in': nrm(ks[1], (DEPTH, D_MODEL, IN_COLS)) * D_MODEL ** -0.5 * jnp.asarray(col_scale),
        'hg_lower_bound': 0.1 * nrm(ks[2], (DEPTH + 1, HG_WIDTH)),
        'hg_norm_w': 1.0 + 0.02 * nrm(ks[3], (DEPTH, HG_HEAD_DIM)),
        'conv_w': nrm(ks[4], (DEPTH, SSD_CONV, SSD_CONV_DIM)) * SSD_CONV ** -0.5,
        'conv_b': 0.02 * nrm(ks[5], (DEPTH, SSD_CONV_DIM)),
        'dt_bias': dt0 + jnp.log(-jnp.expm1(-dt0)),
        'a_log': jnp.log(jax.random.uniform(ks[7], (DEPTH, SSD_HEADS), f32, 1.0, 16.0)),
        'd_skip': 1.0 + 0.02 * nrm(ks[8], (DEPTH, SSD_HEADS)),
        'ssd_norm_w': 1.0 + 0.02 * nrm(ks[9], (DEPTH, SSD_WIDTH)),
        'w_out': nrm(ks[10], (DEPTH, D_MIX, D_MODEL)) * D_MIX ** -0.5 * DEEPNORM_BETA,
        'ln1_g': 1.0 + 0.02 * nrm(ks[11], (DEPTH, D_MODEL)),
        'ln1_b': 0.02 * nrm(ks[12], (DEPTH, D_MODEL)),
        'router_w': nrm(ks[13], (DEPTH, D_MODEL, N_EXPERTS)) * D_MODEL ** -0.5,
        'router_b': 0.01 * nrm(ks[14], (DEPTH, N_EXPERTS)),
        'w_gate': nrm(ks[15], (DEPTH, N_EXPERTS, D_MODEL, D_EXPERT)) * D_MODEL ** -0.5 * DEEPNORM_BETA,
        'b_gate': 0.01 * nrm(ks[16], (DEPTH, N_EXPERTS, D_EXPERT)),
        'w_up': nrm(ks[17], (DEPTH, N_EXPERTS, D_MODEL, D_EXPERT)) * D_MODEL ** -0.5 * DEEPNORM_BETA,
        'b_up': 0.01 * nrm(ks[18], (DEPTH, N_EXPERTS, D_EXPERT)),
        'w_down': nrm(ks[19], (DEPTH, N_EXPERTS, D_EXPERT, D_MODEL)) * D_EXPERT ** -0.5 * DEEPNORM_BETA,
        'b_down': 0.01 * nrm(ks[20], (DEPTH, N_EXPERTS, D_MODEL)),
        'ln2_g': 1.0 + 0.02 * nrm(ks[21], (DEPTH, D_MODEL)),
        'ln2_b': 0.02 * nrm(ks[22], (DEPTH, D_MODEL)),
    }


def reference(x, w_in, hg_lower_bound, hg_norm_w, conv_w, conv_b, dt_bias, a_log, d_skip, ssd_norm_w, w_out,
              ln1_g, ln1_b, router_w, router_b, w_gate, b_gate, w_up, b_up, w_down, b_down, ln2_g, ln2_b):
    lb_table = jnp.cumsum(jax.nn.softmax(hg_lower_bound.astype(jnp.float32), axis=0), axis=0)
    for l in range(DEPTH):
        proj = x @ w_in[l]
        q, f, i, g, z, xbc, dt = jnp.split(proj, SPLITS, axis=-1)
        o_hg = hgrn2_mixer(q, f, i, g, lb_table[l], hg_norm_w[l])
        o_ssd = ssd_mixer(z, xbc, dt, conv_w[l], conv_b[l], dt_bias[l], a_log[l], d_skip[l], ssd_norm_w[l])
        mix = jnp.concatenate([o_hg, o_ssd], axis=-1).astype(x.dtype) @ w_out[l]
        x = layer_norm(DEEPNORM_ALPHA * x + mix, ln1_g[l], ln1_b[l])
        ffn = moe_ffn(x, router_w[l], router_b[l], w_gate[l], b_gate[l], w_up[l], b_up[l], w_down[l], b_down[l])
        x = layer_norm(DEEPNORM_ALPHA * x + ffn, ln2_g[l], ln2_b[l])
    return x
```

```python
import functools

import jax
import jax.numpy as jnp
import numpy as np
from jax import lax
from jax.experimental import pallas as pl
from jax.experimental.pallas import tpu as pltpu

F32 = jnp.float32
BF16 = jnp.bfloat16

D_MODEL = 1024
CHUNK = 64
HG_WIDTH = 512
HG_HEAD_DIM = 128
HG_HEADS = 4
SSD_WIDTH = 512
SSD_HEAD_DIM = 64
SSD_HEADS = 8
SSD_GROUPS = 2
SSD_STATE = 128
SSD_CONV = 4
SSD_CONV_DIM = SSD_WIDTH + 2 * SSD_GROUPS * SSD_STATE
N_EXPERTS = 32
TOP_K = 4
EXPERT_BLOCK = 256
SWIGLU_LIMIT = 7.0
SWIGLU_ALPHA = 1.702
DEPTH = 1
DEEPNORM_ALPHA = (2 * DEPTH) ** 0.25
LN_EPS = 1e-5
RMS_EPS = 1e-5

LANES = 128
SUBLANES = 8
SUB_CHUNK = 16
EXP_CAP = 60.0
TILE_ROWS = 256
COMBINE_ROWS = 256
INV_CHUNK = 8192
TOK_BLOCK = 1024
VMEM_LIMIT = 56 * 1024 * 1024

OFF_Q, OFF_F, OFF_I, OFF_G = 0, 512, 1024, 1536
OFF_Z, OFF_XBC = 2048, 2560
OFF_XS, OFF_B, OFF_C = 2560, 3072, 3328
MAIN_COLS = 3584


def _bdot(a, b):
    return jnp.dot(a.astype(BF16), b.astype(BF16), preferred_element_type=F32)


def _bdot_nt(a, b):
    return lax.dot_general(a.astype(BF16), b.astype(BF16), (((1,), (1,)), ((), ())),
                           preferred_element_type=F32)


def _bdot_tn(a, b):
    return lax.dot_general(a.astype(BF16), b.astype(BF16), (((0,), (0,)), ((), ())),
                           preferred_element_type=F32)


def _split3(a):
    hi = a.astype(BF16)
    r1 = a - hi.astype(F32)
    mid = r1.astype(BF16)
    lo = (r1 - mid.astype(F32)).astype(BF16)
    return hi, mid, lo


def _dot01_left(m01, a):
    hi, mid, lo = _split3(a)
    d = functools.partial(jnp.dot, m01, preferred_element_type=F32)
    return d(hi) + d(mid) + d(lo)


def _dot01_right(a, m01):
    hi, mid, lo = _split3(a)
    return (jnp.dot(hi, m01, preferred_element_type=F32) + jnp.dot(mid, m01, preferred_element_type=F32)
            + jnp.dot(lo, m01, preferred_element_type=F32))


def _sigmoid(x):
    return 1.0 / (1.0 + jnp.exp(-x))


def _silu(x):
    return x * _sigmoid(x)


def _softplus(x):
    return jnp.maximum(x, 0.0) + jnp.log(1.0 + jnp.exp(-jnp.abs(x)))


def _mixer_body(x_ref, win_ref, wdt_ref, wdtT_ref, lbp_ref, hgnw_ref, convw_ref, convb_ref,
                dtb_row_ref, dtb_col_ref, alog_row_ref, alog_col_ref, dskip_ref, ssdnw_ref,
                wout_ref, ln1g_ref, ln1b_ref, rw1_ref, rwh_ref, rb_ref,
                tril_ref, triu_ref, trils_ref, e128_ref,
                x1_ref, meta_ref, cnt_ref,
                proj_s, b_s, xpad_s, xdt_s, cse_s, st_s, pt_s, carry_s, ohg_s, ossd_s):
    TL = TILE_ROWS
    bidx = pl.program_id(0)
    t = pl.program_id(1)

    @pl.when(t == 0)
    def _():
        xpad_s[0:SUBLANES, :] = jnp.zeros((SUBLANES, SSD_CONV_DIM), F32)
        st_s[...] = jnp.zeros_like(st_s)
        pt_s[...] = jnp.zeros_like(pt_s)

    @pl.when((t == 0) & (bidx == 0))
    def _():
        carry_s[...] = jnp.zeros_like(carry_s)

    x = x_ref[...]
    xb = x.astype(BF16)
    proj_s[...] = jnp.dot(xb, win_ref[...], preferred_element_type=F32)
    dtc_raw = jnp.dot(xb, wdt_ref[...], preferred_element_type=F32)
    dtr_raw = lax.dot_general(wdtT_ref[...], xb, (((1,), (1,)), ((), ())),
                              preferred_element_type=F32)

    tril = tril_ref[...]

    a0 = lbp_ref[0:1, :]
    a1 = lbp_ref[1:2, :]
    am = jnp.maximum(a0, a1)
    e0 = jnp.exp(a0 - am)
    e1 = jnp.exp(a1 - am)
    lb = e0 / (e0 + e1)
    f = lb + (1.0 - lb) * _sigmoid(proj_s[:, OFF_F:OFF_F + HG_WIDTH])
    b_s[...] = _dot01_left(tril, jnp.log(f))
    proj_s[:, OFF_F:OFF_F + HG_WIDTH] = f
    proj_s[:, OFF_Q:OFF_Q + HG_WIDTH] = _silu(proj_s[:, OFF_Q:OFF_Q + HG_WIDTH])

    row64 = lax.broadcasted_iota(jnp.int32, (CHUNK, CHUNK), 0)
    col64 = lax.broadcasted_iota(jnp.int32, (CHUNK, CHUNK), 1)
    causal = row64 >= col64
    hgnw = hgnw_ref[...]

    for c in range(TL // CHUNK):
        r0 = c * CHUNK
        for h in range(HG_HEADS):
            h0 = h * HG_HEAD_DIM
            bc = b_s[r0:r0 + CHUNK, h0:h0 + HG_HEAD_DIM]
            qc = proj_s[r0:r0 + CHUNK, OFF_Q + h0:OFF_Q + h0 + HG_HEAD_DIM]
            kc = 1.0 - proj_s[r0:r0 + CHUNK, OFF_F + h0:OFF_F + h0 + HG_HEAD_DIM]
            vc = proj_s[r0:r0 + CHUNK, OFF_I + h0:OFF_I + h0 + HG_HEAD_DIM]
            parts = []
            for i in range(CHUNK // SUB_CHUNK):
                s0 = i * SUB_CHUNK
                if i == 0:
                    qi = qc[0:SUB_CHUNK] * jnp.exp(bc[0:SUB_CHUNK])
                    ki = kc * jnp.exp(jnp.minimum(-bc, EXP_CAP))
                else:
                    ref_i = bc[s0 - 1:s0, :]
                    qi = qc[s0:s0 + SUB_CHUNK] * jnp.exp(bc[s0:s0 + SUB_CHUNK] - ref_i)
                    ki = kc * jnp.exp(jnp.minimum(ref_i - bc, EXP_CAP))
                parts.append(_bdot_nt(qi, ki))
            sc = jnp.concatenate(parts, axis=0)
            sc = jnp.where(causal, sc, 0.0)
            st = st_s[h]
            o = _bdot(sc, vc) + _bdot_nt(qc * jnp.exp(bc), st)
            b_end = bc[CHUNK - 1:CHUNK, :]
            kdec = kc * jnp.exp(b_end - bc)
            st_s[h] = st * jnp.exp(b_end) + _bdot_tn(vc, kdec)
            ms = jnp.mean(o * o, axis=-1, keepdims=True)
            on = o * lax.rsqrt(ms + RMS_EPS) * hgnw
            gc = proj_s[r0:r0 + CHUNK, OFF_G + h0:OFF_G + h0 + HG_HEAD_DIM]
            ohg_s[r0:r0 + CHUNK, h0:h0 + HG_HEAD_DIM] = on * _silu(gc)

    e128 = e128_ref[...]
    dtc = _softplus(dtc_raw + dtb_row_ref[...])
    a_row = -jnp.exp(alog_row_ref[...])
    cs_c = _dot01_left(tril, dtc * a_row)
    cse_s[...] = _dot01_right(cs_c, e128)
    dt_exp = _dot01_right(dtc, e128)
    dtr = _softplus(dtr_raw + dtb_col_ref[...])
    cs_r = _dot01_right(dtr * (-jnp.exp(alog_col_ref[...])), triu_ref[...])

    xpad_s[SUBLANES:SUBLANES + TL, :] = proj_s[:, OFF_XBC:OFF_XBC + SSD_CONV_DIM]
    acc = jnp.broadcast_to(convb_ref[...], (TL, SSD_CONV_DIM))
    for j in range(SSD_CONV):
        off = SUBLANES - (SSD_CONV - 1) + j
        acc = acc + convw_ref[j:j + 1, :] * xpad_s[off:off + TL, :]
    proj_s[:, OFF_XBC:OFF_XBC + SSD_CONV_DIM] = _silu(acc)
    xpad_s[0:SUBLANES, :] = xpad_s[TL:TL + SUBLANES, :]
    xdt_s[...] = proj_s[:, OFF_XS:OFF_XS + SSD_WIDTH] * dt_exp

    gw = SSD_WIDTH // SSD_GROUPS
    hpg = SSD_HEADS // SSD_GROUPS
    lane_head = lax.broadcasted_iota(jnp.int32, (CHUNK, gw), 1) // SSD_HEAD_DIM
    for c in range(TL // CHUNK):
        r0 = c * CHUNK
        for g in range(SSD_GROUPS):
            g0 = g * gw
            bg = proj_s[r0:r0 + CHUNK, OFF_B + g * SSD_STATE:OFF_B + (g + 1) * SSD_STATE]
            cg = proj_s[r0:r0 + CHUNK, OFF_C + g * SSD_STATE:OFF_C + (g + 1) * SSD_STATE]
            gm = _bdot_nt(cg, bg)
            cse_g = cse_s[r0:r0 + CHUNK, g0:g0 + gw]
            cs_end = cse_s[r0 + CHUNK - 1:r0 + CHUNK, g0:g0 + gw]
            xdt_g = xdt_s[r0:r0 + CHUNK, g0:g0 + gw]
            ydiag = jnp.zeros((CHUNK, gw), F32)
            for hl in range(hpg):
                hh = g * hpg + hl
                seg = cse_g[:, hl * SSD_HEAD_DIM:(hl + 1) * SSD_HEAD_DIM] - cs_r[hh:hh + 1, r0:r0 + CHUNK]
                lm = jnp.where(causal, jnp.exp(jnp.minimum(seg, 0.0)), 0.0)
                xm = jnp.where(lane_head == hl, xdt_g, 0.0)
                ydiag = ydiag + _bdot(gm * lm, xm)
            pt = pt_s[g]
            yoff = _bdot(cg, pt) * jnp.exp(cse_g)
            stg = _bdot_tn(bg, xdt_g * jnp.exp(cs_end - cse_g))
            pt_s[g] = pt * jnp.exp(cs_end) + stg
            xs_g = proj_s[r0:r0 + CHUNK, OFF_XS + g0:OFF_XS + g0 + gw]
            ossd_s[r0:r0 + CHUNK, g0:g0 + gw] = ydiag + yoff + xs_g * dskip_ref[:, g0:g0 + gw]

    y = ossd_s[...] * _silu(proj_s[:, OFF_Z:OFF_Z + SSD_WIDTH])
    mix = _bdot(ohg_s[...], wout_ref[0:HG_WIDTH, :])
    for g in range(SSD_GROUPS):
        yg = y[:, g * gw:(g + 1) * gw]
        ms = jnp.mean(yg * yg, axis=-1, keepdims=True)
        yn = yg * lax.rsqrt(ms + RMS_EPS) * ssdnw_ref[:, g * gw:(g + 1) * gw]
        mix = mix + _bdot(yn, wout_ref[HG_WIDTH + g * gw:HG_WIDTH + (g + 1) * gw, :])

    hres = DEEPNORM_ALPHA * x + mix
    mu = jnp.mean(hres, axis=-1, keepdims=True)
    hc = hres - mu
    var = jnp.mean(hc * hc, axis=-1, keepdims=True)
    x1 = hc * lax.rsqrt(var + LN_EPS) * ln1g_ref[...] + ln1b_ref[...]
    x1_ref[...] = x1

    xh = x1.astype(BF16)
    xm_ = (x1 - xh.astype(F32)).astype(BF16)
    t1 = jnp.dot(xh, rw1_ref[...], preferred_element_type=F32)
    logits = (t1[:, 0:LANES] + t1[:, LANES:2 * LANES]
              + jnp.dot(xm_, rwh_ref[...], preferred_element_type=F32) + rb_ref[...])
    lane = lax.broadcasted_iota(jnp.int32, (TL, LANES), 1)
    lane_f = lane.astype(F32)
    neg = jnp.float32(-jnp.inf)
    work = jnp.where(lane < N_EXPERTS, logits, neg)
    onehots, vals, idxs = [], [], []
    for j in range(TOP_K):
        m = jnp.max(work, axis=-1, keepdims=True)
        idx = jnp.min(jnp.where(work == m, lane_f, float(LANES)), axis=-1, keepdims=True)
        oh = lane_f == idx
        onehots.append(oh)
        vals.append(m)
        idxs.append(idx)
        work = jnp.where(oh, neg, work)
    es = [jnp.exp(v - vals[0]) for v in vals]
    den = es[0] + es[1] + es[2] + es[3]
    gates = [e / den for e in es]
    sel = jnp.zeros((TL, LANES), F32)
    for oh in onehots:
        sel = jnp.where(oh, 1.0, sel)
    carry = carry_s[...]
    rankmat = carry + jnp.dot(trils_ref[...], sel.astype(BF16), preferred_element_type=F32)
    carry_new = carry + jnp.sum(sel, axis=0, keepdims=True)
    carry_s[...] = carry_new
    cnt_ref[...] = jnp.broadcast_to(carry_new, (SUBLANES, LANES))
    meta = jnp.zeros((TL, LANES), F32)
    for j in range(TOP_K):
        rank_j = jnp.sum(jnp.where(onehots[j], rankmat, 0.0), axis=-1, keepdims=True)
        meta = jnp.where(lane == j, idxs[j], meta)
        meta = jnp.where(lane == TOP_K + j, rank_j, meta)
        meta = jnp.where(lane == 2 * TOP_K + j, gates[j], meta)
    meta_ref[...] = meta


def _full(shape):
    nd = len(shape)
    return pl.BlockSpec(shape, lambda *_: (0,) * nd)


def _mixer_call(x2d, consts, batch, seq):
    TL = TILE_ROWS
    nt = seq // TL
    T = batch * seq
    in_specs = [pl.BlockSpec((TL, D_MODEL), lambda b, t: (b * nt + t, 0))]
    in_specs += [_full(c.shape) for c in consts]
    out_shape = (jax.ShapeDtypeStruct((T, D_MODEL), F32),
                 jax.ShapeDtypeStruct((T, LANES), F32),
                 jax.ShapeDtypeStruct((SUBLANES, LANES), F32))
    out_specs = (pl.BlockSpec((TL, D_MODEL), lambda b, t: (b * nt + t, 0)),
                 pl.BlockSpec((TL, LANES), lambda b, t: (b * nt + t, 0)),
                 pl.BlockSpec((SUBLANES, LANES), lambda b, t: (0, 0)))
    scratch = [
        pltpu.VMEM((TL, MAIN_COLS), F32),
        pltpu.VMEM((TL, HG_WIDTH), F32),
        pltpu.VMEM((TL + 2 * SUBLANES, SSD_CONV_DIM), F32),
        pltpu.VMEM((TL, SSD_WIDTH), F32),
        pltpu.VMEM((TL, SSD_WIDTH), F32),
        pltpu.VMEM((HG_HEADS, HG_HEAD_DIM, HG_HEAD_DIM), F32),
        pltpu.VMEM((SSD_GROUPS, SSD_STATE, SSD_WIDTH // SSD_GROUPS), F32),
        pltpu.VMEM((1, LANES), F32),
        pltpu.VMEM((TL, HG_WIDTH), F32),
        pltpu.VMEM((TL, SSD_WIDTH), F32),
    ]
    return pl.pallas_call(
        _mixer_body,
        grid=(batch, nt),
        in_specs=in_specs,
        out_specs=out_specs,
        out_shape=out_shape,
        scratch_shapes=scratch,
        compiler_params=pltpu.CompilerParams(
            dimension_semantics=("arbitrary", "arbitrary"), vmem_limit_bytes=VMEM_LIMIT),
        name="mixer",
    )(x2d, *consts)


def _dest_body(meta_ref, pstart_ref, dest_ref):
    meta = meta_ref[...]
    rows = meta.shape[0]
    lane = lax.broadcasted_iota(jnp.int32, (rows, LANES), 1)
    lane_f = lane.astype(F32)
    ps = pstart_ref[...]
    out = jnp.zeros((rows, LANES), F32)
    for j in range(TOP_K):
        idx_j = meta[:, j:j + 1]
        start_j = jnp.sum(jnp.where(lane_f == idx_j, ps, 0.0), axis=-1, keepdims=True)
        out = jnp.where(lane == j, start_j + meta[:, TOP_K + j:TOP_K + j + 1], out)
    dest_ref[...] = out.astype(jnp.int32)


def _dest_call(meta, pstart_row):
    T = meta.shape[0]
    rows = 1024
    return pl.pallas_call(
        _dest_body,
        grid=(T // rows,),
        in_specs=[pl.BlockSpec((rows, LANES), lambda i: (i, 0)), _full((1, LANES))],
        out_specs=pl.BlockSpec((rows, LANES), lambda i: (i, 0)),
        out_shape=jax.ShapeDtypeStruct((T, LANES), jnp.int32),
        compiler_params=pltpu.CompilerParams(dimension_semantics=("arbitrary",)),
        name="dest",
    )(meta, pstart_row)


def _inverse_body(dest_ref, tok_ref):
    c = pl.program_id(0)
    cap = tok_ref.shape[0]

    @pl.when(c == 0)
    def _():
        def zero(i, carry):
            tok_ref[i] = 0
            return carry
        lax.fori_loop(0, cap, zero, 0, unroll=8)

    base = c * INV_CHUNK

    def body(i, carry):
        tok_ref[dest_ref[i]] = lax.shift_right_logical(base + i, 2)
        return carry
    lax.fori_loop(0, INV_CHUNK, body, 0, unroll=8)


def _inverse_call(dest_flat, cap):
    n = dest_flat.shape[0]
    return pl.pallas_call(
        _inverse_body,
        grid=(n // INV_CHUNK,),
        in_specs=[pl.BlockSpec((INV_CHUNK,), lambda c: (c,), memory_space=pltpu.SMEM)],
        out_specs=pl.BlockSpec((cap,), lambda c: (0,), memory_space=pltpu.SMEM),
        out_shape=jax.ShapeDtypeStruct((cap,), jnp.int32),
        compiler_params=pltpu.CompilerParams(dimension_semantics=("arbitrary",)),
        name="inverse",
    )(dest_flat)


def _expert_body(be_ref, tokc_ref, tokn_ref, x1_hbm, wg_ref, bg_ref, wu_ref, bu_ref, wd_ref, bd_ref,
                 y_ref, xbuf, sems, wg_s, wu_s, wd_s):
    m = pl.program_id(0)
    nb = pl.num_programs(0)
    slot = lax.rem(m, 2)
    per_blk = TOK_BLOCK // EXPERT_BLOCK

    def issue(tok_ref, off, s):
        def body(i, carry):
            tok = tok_ref[off + i]
            pltpu.make_async_copy(x1_hbm.at[pl.ds(tok, 1), :], xbuf.at[s, pl.ds(i, 1), :],
                                  sems.at[s]).start()
            return carry
        lax.fori_loop(0, EXPERT_BLOCK, body, 0, unroll=8)

    @pl.when(m == 0)
    def _():
        issue(tokc_ref, 0, 0)

    @pl.when(m + 1 < nb)
    def _():
        issue(tokn_ref, lax.rem(m + 1, per_blk) * EXPERT_BLOCK, 1 - slot)

    changed = jnp.logical_or(m == 0, be_ref[m] != be_ref[jnp.maximum(m - 1, 0)])

    @pl.when(changed)
    def _():
        wg_s[...] = wg_ref[0].astype(BF16)
        wu_s[...] = wu_ref[0].astype(BF16)
        wd_s[...] = wd_ref[0].astype(BF16)

    pltpu.make_async_copy(x1_hbm.at[pl.ds(0, EXPERT_BLOCK), :], xbuf.at[slot], sems.at[slot]).wait()
    xb = xbuf[slot].astype(BF16)
    hg = jnp.minimum(jnp.dot(xb, wg_s[...], preferred_element_type=F32) + bg_ref[0], SWIGLU_LIMIT)
    hu = jnp.clip(jnp.dot(xb, wu_s[...], preferred_element_type=F32) + bu_ref[0],
                  -SWIGLU_LIMIT, SWIGLU_LIMIT)
    hact = (hu + 1.0) * (hg * _sigmoid(SWIGLU_ALPHA * hg))
    y_ref[...] = jnp.dot(hact.astype(BF16), wd_s[...], preferred_element_type=F32) + bd_ref[0]


def _expert_call(block_expert, row_tok, x1, w_gate, b_gate, w_up, b_up, w_down, b_down, n_blocks):
    per_blk = TOK_BLOCK // EXPERT_BLOCK
    last_tb = row_tok.shape[0] // TOK_BLOCK - 1
    wspec = pl.BlockSpec((1, D_MODEL, D_MODEL), lambda m, be: (be[m], 0, 0))
    bspec = pl.BlockSpec((1, 1, D_MODEL), lambda m, be: (be[m], 0, 0))
    grid_spec = pltpu.PrefetchScalarGridSpec(
        num_scalar_prefetch=1,
        grid=(n_blocks,),
        in_specs=[
            pl.BlockSpec((TOK_BLOCK,), lambda m, be: (m // per_blk,), memory_space=pltpu.SMEM),
            pl.BlockSpec((TOK_BLOCK,), lambda m, be: (jnp.minimum((m + 1) // per_blk, last_tb),),
                         memory_space=pltpu.SMEM),
            pl.BlockSpec(memory_space=pl.ANY),
            wspec, bspec, wspec, bspec, wspec, bspec,
        ],
        out_specs=pl.BlockSpec((EXPERT_BLOCK, D_MODEL), lambda m, be: (m, 0)),
        scratch_shapes=[
            pltpu.VMEM((2, EXPERT_BLOCK, D_MODEL), F32),
            pltpu.SemaphoreType.DMA((2,)),
            pltpu.VMEM((D_MODEL, D_MODEL), BF16),
            pltpu.VMEM((D_MODEL, D_MODEL), BF16),
            pltpu.VMEM((D_MODEL, D_MODEL), BF16),
        ],
    )
    return pl.pallas_call(
        _expert_body,
        grid_spec=grid_spec,
        out_shape=jax.ShapeDtypeStruct((n_blocks * EXPERT_BLOCK, D_MODEL), F32),
        compiler_params=pltpu.CompilerParams(
            dimension_semantics=("arbitrary",), vmem_limit_bytes=VMEM_LIMIT),
        name="experts",
    )(block_expert, row_tok, row_tok, x1, w_gate, b_gate.reshape(N_EXPERTS, 1, D_MODEL),
      w_up, b_up.reshape(N_EXPERTS, 1, D_MODEL), w_down, b_down.reshape(N_EXPERTS, 1, D_MODEL))


def _combine_body(destc_ref, destn_ref, y_hbm, x1_ref, meta_ref, g_ref, b_ref, out_ref, ybuf, sems):
    i = pl.program_id(0)
    n = pl.num_programs(0)
    slot = lax.rem(i, 2)
    TC = COMBINE_ROWS

    def issue(dref, s):
        def body(r, carry):
            for j in range(TOP_K):
                d = dref[r * TOP_K + j]
                pltpu.make_async_copy(y_hbm.at[pl.ds(d, 1), :], ybuf.at[s, j, pl.ds(r, 1), :],
                                      sems.at[s]).start()
            return carry
        lax.fori_loop(0, TC, body, 0, unroll=2)

    @pl.when(i == 0)
    def _():
        issue(destc_ref, 0)

    @pl.when(i + 1 < n)
    def _():
        issue(destn_ref, 1 - slot)

    for j in range(TOP_K):
        pltpu.make_async_copy(y_hbm.at[pl.ds(0, TC), :], ybuf.at[slot, j], sems.at[slot]).wait()
    meta = meta_ref[...]
    acc = DEEPNORM_ALPHA * x1_ref[...]
    for j in range(TOP_K):
        acc = acc + ybuf[slot, j] * meta[:, 2 * TOP_K + j:2 * TOP_K + j + 1]
    mu = jnp.mean(acc, axis=-1, keepdims=True)
    hc = acc - mu
    var = jnp.mean(hc * hc, axis=-1, keepdims=True)
    out_ref[...] = hc * lax.rsqrt(var + LN_EPS) * g_ref[...] + b_ref[...]


def _combine_call(dest_flat, y_rows, x1, meta, ln2_g, ln2_b):
    T = x1.shape[0]
    TC = COMBINE_ROWS
    n = T // TC
    blk = TC * TOP_K
    return pl.pallas_call(
        _combine_body,
        grid=(n,),
        in_specs=[
            pl.BlockSpec((blk,), lambda i: (i,), memory_space=pltpu.SMEM),
            pl.BlockSpec((blk,), lambda i: (jnp.minimum(i + 1, n - 1),), memory_space=pltpu.SMEM),
            pl.BlockSpec(memory_space=pl.ANY),
            pl.BlockSpec((TC, D_MODEL), lambda i: (i, 0)),
            pl.BlockSpec((TC, LANES), lambda i: (i, 0)),
            _full((1, D_MODEL)), _full((1, D_MODEL)),
        ],
        out_specs=pl.BlockSpec((TC, D_MODEL), lambda i: (i, 0)),
        out_shape=jax.ShapeDtypeStruct((T, D_MODEL), F32),
        scratch_shapes=[pltpu.VMEM((2, TOP_K, TC, D_MODEL), F32), pltpu.SemaphoreType.DMA((2,))],
        compiler_params=pltpu.CompilerParams(
            dimension_semantics=("arbitrary",), vmem_limit_bytes=VMEM_LIMIT),
        name="combine",
    )(dest_flat, dest_flat, y_rows, x1, meta, ln2_g, ln2_b)


def _np_consts():
    TL = TILE_ROWS
    r = np.arange(TL)
    same = (r[:, None] // CHUNK) == (r[None, :] // CHUNK)
    tril = (same & (r[None, :] <= r[:, None])).astype(np.float32)
    trils = (r[None, :] < r[:, None]).astype(np.float32)
    e128 = np.zeros((LANES, SSD_WIDTH), np.float32)
    for h in range(SSD_HEADS):
        e128[h, h * SSD_HEAD_DIM:(h + 1) * SSD_HEAD_DIM] = 1.0
    return tril, tril.T.copy(), trils, e128


def kernel(x, w_in, hg_lower_bound, hg_norm_w, conv_w, conv_b, dt_bias, a_log, d_skip, ssd_norm_w, w_out,
           ln1_g, ln1_b, router_w, router_b, w_gate, b_gate, w_up, b_up, w_down, b_down, ln2_g, ln2_b):
    batch, seq, d = x.shape
    assert d == D_MODEL and seq % TILE_ROWS == 0 and w_in.shape[0] == DEPTH
    T = batch * seq
    n_assign = T * TOP_K
    n_blocks = n_assign // EXPERT_BLOCK + N_EXPERTS
    cap = n_blocks * EXPERT_BLOCK
    assert cap % TOK_BLOCK == 0 and n_assign % INV_CHUNK == 0 and T % COMBINE_ROWS == 0

    tril, triu, trils, e128 = _np_consts()
    w = w_in[0]
    wdt = w[:, MAIN_COLS:]
    pad_l = LANES - SSD_HEADS
    rw = jnp.pad(router_w[0], ((0, 0), (0, LANES - N_EXPERTS)))
    rwh = rw.astype(BF16)
    rwm = (rw - rwh.astype(F32)).astype(BF16)
    consts = [
        w[:, :MAIN_COLS].astype(BF16),
        jnp.pad(wdt, ((0, 0), (0, pad_l))).astype(BF16),
        wdt.T.astype(BF16),
        hg_lower_bound,
        hg_norm_w[0].reshape(1, HG_HEAD_DIM),
        conv_w[0],
        conv_b[0].reshape(1, SSD_CONV_DIM),
        jnp.pad(dt_bias[0], (0, pad_l)).reshape(1, LANES),
        dt_bias[0].reshape(SSD_HEADS, 1),
        jnp.pad(a_log[0], (0, pad_l)).reshape(1, LANES),
        a_log[0].reshape(SSD_HEADS, 1),
        jnp.repeat(d_skip[0], SSD_HEAD_DIM).reshape(1, SSD_WIDTH),
        ssd_norm_w[0].reshape(1, SSD_WIDTH),
        w_out[0].astype(BF16),
        ln1_g[0].reshape(1, D_MODEL),
        ln1_b[0].reshape(1, D_MODEL),
        jnp.concatenate([rwh, rwm], axis=1),
        rwh,
        jnp.pad(router_b[0], (0, LANES - N_EXPERTS)).reshape(1, LANES),
        jnp.asarray(tril, BF16), jnp.asarray(triu, BF16), jnp.asarray(trils, BF16), jnp.asarray(e128, BF16),
    ]
    x1, meta, cnt = _mixer_call(x.reshape(T, D_MODEL), consts, batch, seq)

    counts = cnt[0, :N_EXPERTS].astype(jnp.int32)
    padded = (counts + EXPERT_BLOCK - 1) // EXPERT_BLOCK * EXPERT_BLOCK
    padded_end = jnp.cumsum(padded)
    padded_start = padded_end - padded
    block_expert = jnp.minimum(
        jnp.searchsorted(padded_end, jnp.arange(n_blocks) * EXPERT_BLOCK, side='right'),
        N_EXPERTS - 1).astype(jnp.int32)
    pstart_row = jnp.pad(padded_start.astype(F32), (0, LANES - N_EXPERTS)).reshape(1, LANES)

    dest = _dest_call(meta, pstart_row)
    dest_flat = dest[:, :TOP_K].reshape(n_assign)
    row_tok = _inverse_call(dest_flat, cap)
    y_rows = _expert_call(block_expert, row_tok, x1, w_gate[0], b_gate[0], w_up[0], b_up[0],
                          w_down[0], b_down[0], n_blocks)
    out = _combine_call(dest_flat, y_rows, x1, meta, ln2_g[0].reshape(1, D_MODEL),
                        ln2_b[0].reshape(1, D_MODEL))
    return out.reshape(batch, seq, D_MODEL)
```

```python
import functools

import jax
import jax.numpy as jnp
import numpy as np
from jax import lax
from jax.experimental import pallas as pl
from jax.experimental.pallas import tpu as pltpu

F32 = jnp.float32
BF16 = jnp.bfloat16

D_MODEL = 1024
CHUNK = 64
HG_WIDTH = 512
HG_HEAD_DIM = 128
HG_HEADS = 4
SSD_WIDTH = 512
SSD_HEAD_DIM = 64
SSD_HEADS = 8
SSD_GROUPS = 2
SSD_STATE = 128
SSD_CONV = 4
SSD_CONV_DIM = SSD_WIDTH + 2 * SSD_GROUPS * SSD_STATE
N_EXPERTS = 32
TOP_K = 4
EXPERT_BLOCK = 256
SWIGLU_LIMIT = 7.0
SWIGLU_ALPHA = 1.702
DEPTH = 1
DEEPNORM_ALPHA = (2 * DEPTH) ** 0.25
LN_EPS = 1e-5
RMS_EPS = 1e-5

LANES = 128
SUBLANES = 8
SUB_CHUNK = 16
EXP_CAP = 60.0
TILE_ROWS = 256
RUN_ALIGN = SUBLANES
SORT_ROWS = 1280
XS_WIDTH = D_MODEL + LANES
VMEM_LIMIT = 56 * 1024 * 1024

OFF_Q, OFF_F, OFF_I, OFF_G = 0, 512, 1024, 1536
OFF_Z, OFF_XBC = 2048, 2560
OFF_XS, OFF_B, OFF_C = 2560, 3072, 3328
MAIN_COLS = 3584


def _bdot(a, b):
    return jnp.dot(a.astype(BF16), b.astype(BF16), preferred_element_type=F32)


def _bdot_nt(a, b):
    return lax.dot_general(a.astype(BF16), b.astype(BF16), (((1,), (1,)), ((), ())),
                           preferred_element_type=F32)


def _bdot_tn(a, b):
    return lax.dot_general(a.astype(BF16), b.astype(BF16), (((0,), (0,)), ((), ())),
                           preferred_element_type=F32)


def _split3(a):
    hi = a.astype(BF16)
    r1 = a - hi.astype(F32)
    mid = r1.astype(BF16)
    lo = (r1 - mid.astype(F32)).astype(BF16)
    return hi, mid, lo


def _dot01_left(m01, a):
    hi, mid, lo = _split3(a)
    d = functools.partial(jnp.dot, m01, preferred_element_type=F32)
    return d(hi) + d(mid) + d(lo)


def _dot01_right(a, m01):
    hi, mid, lo = _split3(a)
    return (jnp.dot(hi, m01, preferred_element_type=F32) + jnp.dot(mid, m01, preferred_element_type=F32)
            + jnp.dot(lo, m01, preferred_element_type=F32))


def _sigmoid(x):
    return 1.0 / (1.0 + jnp.exp(-x))


def _silu(x):
    return x * _sigmoid(x)


def _softplus(x):
    return jnp.maximum(x, 0.0) + jnp.log(1.0 + jnp.exp(-jnp.abs(x)))


def _mixer_body(x_ref, win_ref, wdt_ref, wdtT_ref, lbp_ref, hgnw_ref, convw_ref, convb_ref,
                dtb_row_ref, dtb_col_ref, alog_row_ref, alog_col_ref, dskip_ref, ssdnw_ref,
                wout_ref, ln1g_ref, ln1b_ref, rw1_ref, rwh_ref, rb_ref,
                tril_ref, triu_ref, trils_ref, e128_ref,
                x1_ref, meta_ref, cnt_ref,
                proj_s, b_s, xpad_s, xdt_s, cse_s, st_s, pt_s, ohg_s, ossd_s):
    TL = TILE_ROWS
    t = pl.program_id(1)

    @pl.when(t == 0)
    def _():
        xpad_s[0:SUBLANES, :] = jnp.zeros((SUBLANES, SSD_CONV_DIM), F32)
        st_s[...] = jnp.zeros_like(st_s)
        pt_s[...] = jnp.zeros_like(pt_s)

    x = x_ref[...]
    xb = x.astype(BF16)
    proj_s[...] = jnp.dot(xb, win_ref[...], preferred_element_type=F32)
    dtc_raw = jnp.dot(xb, wdt_ref[...], preferred_element_type=F32)
    dtr_raw = lax.dot_general(wdtT_ref[...], xb, (((1,), (1,)), ((), ())),
                              preferred_element_type=F32)

    tril = tril_ref[...]

    a0 = lbp_ref[0:1, :]
    a1 = lbp_ref[1:2, :]
    am = jnp.maximum(a0, a1)
    e0 = jnp.exp(a0 - am)
    e1 = jnp.exp(a1 - am)
    lb = e0 / (e0 + e1)
    f = lb + (1.0 - lb) * _sigmoid(proj_s[:, OFF_F:OFF_F + HG_WIDTH])
    b_s[...] = _dot01_left(tril, jnp.log(f))
    proj_s[:, OFF_F:OFF_F + HG_WIDTH] = f
    proj_s[:, OFF_Q:OFF_Q + HG_WIDTH] = _silu(proj_s[:, OFF_Q:OFF_Q + HG_WIDTH])

    row64 = lax.broadcasted_iota(jnp.int32, (CHUNK, CHUNK), 0)
    col64 = lax.broadcasted_iota(jnp.int32, (CHUNK, CHUNK), 1)
    causal = row64 >= col64
    hgnw = hgnw_ref[...]

    for c in range(TL // CHUNK):
        r0 = c * CHUNK
        for h in range(HG_HEADS):
            h0 = h * HG_HEAD_DIM
            bc = b_s[r0:r0 + CHUNK, h0:h0 + HG_HEAD_DIM]
            qc = proj_s[r0:r0 + CHUNK, OFF_Q + h0:OFF_Q + h0 + HG_HEAD_DIM]
            kc = 1.0 - proj_s[r0:r0 + CHUNK, OFF_F + h0:OFF_F + h0 + HG_HEAD_DIM]
            vc = proj_s[r0:r0 + CHUNK, OFF_I + h0:OFF_I + h0 + HG_HEAD_DIM]
            parts = []
            for i in range(CHUNK // SUB_CHUNK):
                s0 = i * SUB_CHUNK
                if i == 0:
                    qi = qc[0:SUB_CHUNK] * jnp.exp(bc[0:SUB_CHUNK])
                    ki = kc * jnp.exp(jnp.minimum(-bc, EXP_CAP))
                else:
                    ref_i = bc[s0 - 1:s0, :]
                    qi = qc[s0:s0 + SUB_CHUNK] * jnp.exp(bc[s0:s0 + SUB_CHUNK] - ref_i)
                    ki = kc * jnp.exp(jnp.minimum(ref_i - bc, EXP_CAP))
                parts.append(_bdot_nt(qi, ki))
            sc = jnp.concatenate(parts, axis=0)
            sc = jnp.where(causal, sc, 0.0)
            st = st_s[h]
            o = _bdot(sc, vc) + _bdot_nt(qc * jnp.exp(bc), st)
            b_end = bc[CHUNK - 1:CHUNK, :]
            kdec = kc * jnp.exp(b_end - bc)
            st_s[h] = st * jnp.exp(b_end) + _bdot_tn(vc, kdec)
            ms = jnp.mean(o * o, axis=-1, keepdims=True)
            on = o * lax.rsqrt(ms + RMS_EPS) * hgnw
            gc = proj_s[r0:r0 + CHUNK, OFF_G + h0:OFF_G + h0 + HG_HEAD_DIM]
            ohg_s[r0:r0 + CHUNK, h0:h0 + HG_HEAD_DIM] = on * _silu(gc)

    e128 = e128_ref[...]
    dtc = _softplus(dtc_raw + dtb_row_ref[...])
    a_row = -jnp.exp(alog_row_ref[...])
    cs_c = _dot01_left(tril, dtc * a_row)
    cse_s[...] = _dot01_right(cs_c, e128)
    dt_exp = _dot01_right(dtc, e128)
    dtr = _softplus(dtr_raw + dtb_col_ref[...])
    cs_r = _dot01_right(dtr * (-jnp.exp(alog_col_ref[...])), triu_ref[...])

    xpad_s[SUBLANES:SUBLANES + TL, :] = proj_s[:, OFF_XBC:OFF_XBC + SSD_CONV_DIM]
    acc = jnp.broadcast_to(convb_ref[...], (TL, SSD_CONV_DIM))
    for j in range(SSD_CONV):
        off = SUBLANES - (SSD_CONV - 1) + j
        acc = acc + convw_ref[j:j + 1, :] * xpad_s[off:off + TL, :]
    proj_s[:, OFF_XBC:OFF_XBC + SSD_CONV_DIM] = _silu(acc)
    xpad_s[0:SUBLANES, :] = xpad_s[TL:TL + SUBLANES, :]
    xdt_s[...] = proj_s[:, OFF_XS:OFF_XS + SSD_WIDTH] * dt_exp

    gw = SSD_WIDTH // SSD_GROUPS
    hpg = SSD_HEADS // SSD_GROUPS
    lane_head = lax.broadcasted_iota(jnp.int32, (CHUNK, gw), 1) // SSD_HEAD_DIM
    for c in range(TL // CHUNK):
        r0 = c * CHUNK
        for g in range(SSD_GROUPS):
            g0 = g * gw
            bg = proj_s[r0:r0 + CHUNK, OFF_B + g * SSD_STATE:OFF_B + (g + 1) * SSD_STATE]
            cg = proj_s[r0:r0 + CHUNK, OFF_C + g * SSD_STATE:OFF_C + (g + 1) * SSD_STATE]
            gm = _bdot_nt(cg, bg)
            cse_g = cse_s[r0:r0 + CHUNK, g0:g0 + gw]
            cs_end = cse_s[r0 + CHUNK - 1:r0 + CHUNK, g0:g0 + gw]
            xdt_g = xdt_s[r0:r0 + CHUNK, g0:g0 + gw]
            ydiag = jnp.zeros((CHUNK, gw), F32)
            for hl in range(hpg):
                hh = g * hpg + hl
                seg = cse_g[:, hl * SSD_HEAD_DIM:(hl + 1) * SSD_HEAD_DIM] - cs_r[hh:hh + 1, r0:r0 + CHUNK]
                lm = jnp.where(causal, jnp.exp(jnp.minimum(seg, 0.0)), 0.0)
                xm = jnp.where(lane_head == hl, xdt_g, 0.0)
                ydiag = ydiag + _bdot(gm * lm, xm)
            pt = pt_s[g]
            yoff = _bdot(cg, pt) * jnp.exp(cse_g)
            stg = _bdot_tn(bg, xdt_g * jnp.exp(cs_end - cse_g))
            pt_s[g] = pt * jnp.exp(cs_end) + stg
            xs_g = proj_s[r0:r0 + CHUNK, OFF_XS + g0:OFF_XS + g0 + gw]
            ossd_s[r0:r0 + CHUNK, g0:g0 + gw] = ydiag + yoff + xs_g * dskip_ref[:, g0:g0 + gw]

    y = ossd_s[...] * _silu(proj_s[:, OFF_Z:OFF_Z + SSD_WIDTH])
    mix = _bdot(ohg_s[...], wout_ref[0:HG_WIDTH, :])
    for g in range(SSD_GROUPS):
        yg = y[:, g * gw:(g + 1) * gw]
        ms = jnp.mean(yg * yg, axis=-1, keepdims=True)
        yn = yg * lax.rsqrt(ms + RMS_EPS) * ssdnw_ref[:, g * gw:(g + 1) * gw]
        mix = mix + _bdot(yn, wout_ref[HG_WIDTH + g * gw:HG_WIDTH + (g + 1) * gw, :])

    hres = DEEPNORM_ALPHA * x + mix
    mu = jnp.mean(hres, axis=-1, keepdims=True)
    hc = hres - mu
    var = jnp.mean(hc * hc, axis=-1, keepdims=True)
    x1 = hc * lax.rsqrt(var + LN_EPS) * ln1g_ref[...] + ln1b_ref[...]
    x1_ref[...] = x1

    xh = x1.astype(BF16)
    xm_ = (x1 - xh.astype(F32)).astype(BF16)
    t1 = jnp.dot(xh, rw1_ref[...], preferred_element_type=F32)
    logits = (t1[:, 0:LANES] + t1[:, LANES:2 * LANES]
              + jnp.dot(xm_, rwh_ref[...], preferred_element_type=F32) + rb_ref[...])
    lane = lax.broadcasted_iota(jnp.int32, (TL, LANES), 1)
    lane_f = lane.astype(F32)
    neg = jnp.float32(-jnp.inf)
    work = jnp.where(lane < N_EXPERTS, logits, neg)
    onehots, vals, idxs = [], [], []
    for j in range(TOP_K):
        m = jnp.max(work, axis=-1, keepdims=True)
        idx = jnp.min(jnp.where(work == m, lane_f, float(LANES)), axis=-1, keepdims=True)
        oh = lane_f == idx
        onehots.append(oh)
        vals.append(m)
        idxs.append(idx)
        work = jnp.where(oh, neg, work)
    es = [jnp.exp(v - vals[0]) for v in vals]
    den = es[0] + es[1] + es[2] + es[3]
    gates = [e / den for e in es]
    sel = jnp.zeros((TL, LANES), F32)
    for oh in onehots:
        sel = jnp.where(oh, 1.0, sel)
    rankmat = jnp.dot(trils_ref[...], sel.astype(BF16), preferred_element_type=F32)
    cnt_ref[...] = jnp.broadcast_to(jnp.sum(sel, axis=0, keepdims=True), (SUBLANES, LANES))
    meta = jnp.zeros((TL, LANES), F32)
    for j in range(TOP_K):
        rank_j = jnp.sum(jnp.where(onehots[j], rankmat, 0.0), axis=-1, keepdims=True)
        meta = jnp.where(lane == j, idxs[j], meta)
        meta = jnp.where(lane == TOP_K + j, rank_j, meta)
        meta = jnp.where(lane == 2 * TOP_K + j, gates[j], meta)
    meta_ref[...] = meta


def _full(shape):
    nd = len(shape)
    return pl.BlockSpec(shape, lambda *_: (0,) * nd)


def _mixer_call(x2d, consts, batch, seq):
    TL = TILE_ROWS
    nt = seq // TL
    T = batch * seq
    in_specs = [pl.BlockSpec((TL, D_MODEL), lambda b, t: (b * nt + t, 0))]
    in_specs += [_full(c.shape) for c in consts]
    out_shape = (jax.ShapeDtypeStruct((T, D_MODEL), F32),
                 jax.ShapeDtypeStruct((T, LANES), F32),
                 jax.ShapeDtypeStruct((batch * nt * SUBLANES, LANES), F32))
    out_specs = (pl.BlockSpec((TL, D_MODEL), lambda b, t: (b * nt + t, 0)),
                 pl.BlockSpec((TL, LANES), lambda b, t: (b * nt + t, 0)),
                 pl.BlockSpec((SUBLANES, LANES), lambda b, t: (b * nt + t, 0)))
    scratch = [
        pltpu.VMEM((TL, MAIN_COLS), F32),
        pltpu.VMEM((TL, HG_WIDTH), F32),
        pltpu.VMEM((TL + 2 * SUBLANES, SSD_CONV_DIM), F32),
        pltpu.VMEM((TL, SSD_WIDTH), F32),
        pltpu.VMEM((TL, SSD_WIDTH), F32),
        pltpu.VMEM((HG_HEADS, HG_HEAD_DIM, HG_HEAD_DIM), F32),
        pltpu.VMEM((SSD_GROUPS, SSD_STATE, SSD_WIDTH // SSD_GROUPS), F32),
        pltpu.VMEM((TL, HG_WIDTH), F32),
        pltpu.VMEM((TL, SSD_WIDTH), F32),
    ]
    return pl.pallas_call(
        _mixer_body,
        grid=(batch, nt),
        in_specs=in_specs,
        out_specs=out_specs,
        out_shape=out_shape,
        scratch_shapes=scratch,
        compiler_params=pltpu.CompilerParams(
            dimension_semantics=("arbitrary", "arbitrary"), vmem_limit_bytes=VMEM_LIMIT),
        name="mixer",
    )(x2d, *consts)


def _for_each_run_unit(units_ref, tile, fn):
    def per_expert(e, carry):
        k = tile * N_EXPERTS + e

        def per_unit(i, c2):
            fn(k, i)
            return c2
        lax.fori_loop(0, units_ref[k], per_unit, 0)
        return carry
    lax.fori_loop(0, N_EXPERTS, per_expert, 0)


def _rows(unit):
    return pl.ds(pl.multiple_of(unit * RUN_ALIGN, RUN_ALIGN), RUN_ALIGN)


def _dispatch_body(units_ref, g8_ref, l8_ref, tot_ref, tail8_ref, tailn_ref, nv_ref,
                   x1_ref, meta_ref, lst_ref, xs_hbm, sorted_s, zero_s, sems, zsem, bsem):
    tau = pl.program_id(0)
    n = pl.num_programs(0)
    slot = lax.rem(tau, 2)
    TD, S = TILE_ROWS, SORT_ROWS
    n_blocks = xs_hbm.shape[0] // EXPERT_BLOCK

    def unused_block_copy(m):
        rows = pl.ds(pl.multiple_of(m * EXPERT_BLOCK, EXPERT_BLOCK), EXPERT_BLOCK)
        return pltpu.make_async_copy(zero_s, xs_hbm.at[rows, :], bsem)

    def unit_copy(s, l_unit, g_unit):
        return pltpu.make_async_copy(sorted_s.at[s, _rows(l_unit), :], xs_hbm.at[_rows(g_unit), :], sems.at[s])

    def wait_units(s, count):
        def body(i, carry):
            unit_copy(s, 0, 0).wait()
            return carry
        lax.fori_loop(0, count, body, 0)

    @pl.when(tau == 0)
    def _():
        zero_s[...] = jnp.zeros_like(zero_s)

        def tail_copy(e, i):
            return pltpu.make_async_copy(zero_s.at[pl.ds(0, RUN_ALIGN), :],
                                         xs_hbm.at[_rows(tail8_ref[e] + i), :], zsem)

        def start_unused(m, carry):
            unused_block_copy(m).start()
            return carry
        lax.fori_loop(nv_ref[0], n_blocks, start_unused, 0)

        def start_e(e, carry):
            def unit(i, c2):
                tail_copy(e, i).start()
                return c2
            lax.fori_loop(0, tailn_ref[e], unit, 0)
            return carry
        lax.fori_loop(0, N_EXPERTS, start_e, 0)

        def wait_e(e, carry):
            def unit(i, c2):
                tail_copy(e, 0).wait()
                return c2
            lax.fori_loop(0, tailn_ref[e], unit, 0)
            return carry
        lax.fori_loop(0, N_EXPERTS, wait_e, 0)

    @pl.when(tau >= 2)
    def _():
        wait_units(slot, tot_ref[jnp.maximum(tau - 2, 0)])

    meta = meta_ref[...]
    lane = lax.broadcasted_iota(jnp.int32, (TD, LANES), 1)
    lane_f = lane.astype(F32)
    u8 = jnp.broadcast_to(lst_ref[0:1, :], (SUBLANES, LANES)).astype(BF16)
    ones8 = jnp.ones((SUBLANES, LANES), BF16)
    r_iota = lax.broadcasted_iota(jnp.int32, (S, TD), 0).astype(F32)
    gh = meta.astype(BF16).astype(F32)
    g1 = meta - gh
    gm = g1.astype(BF16).astype(F32)
    gl = g1 - gm
    nt_dims = (((1,), (1,)), ((), ()))
    conds, gparts = [], []
    for j in range(TOP_K):
        oh = lane_f == meta[:, j:j + 1]
        ohb = jnp.where(oh, 1.0, 0.0).astype(BF16)
        rkb = jnp.where(oh, meta[:, TOP_K + j:TOP_K + j + 1], 0.0).astype(BF16)
        m1 = lax.dot_general(u8, ohb, nt_dims, preferred_element_type=F32)
        m2 = lax.dot_general(ones8, rkb, nt_dims, preferred_element_type=F32)
        lpos = RUN_ALIGN * m1[0:1, :] + m2[0:1, :]
        conds.append(r_iota == lpos)
        gc = 2 * TOP_K + j
        gparts.append(jnp.where(lane == 0, gh[:, gc:gc + 1],
                                jnp.where(lane == 1, gm[:, gc:gc + 1],
                                          jnp.where(lane == 2, gl[:, gc:gc + 1], 0.0))).astype(BF16))
    pcat = jnp.concatenate([jnp.where(c, 1.0, 0.0).astype(BF16) for c in conds], axis=1)
    sorted_g = jnp.dot(pcat, jnp.concatenate(gparts, axis=0), preferred_element_type=F32)
    perm = jnp.where(conds[0], 1.0, jnp.where(conds[1], 1.0, jnp.where(conds[2], 1.0,
                     jnp.where(conds[3], 1.0, 0.0)))).astype(BF16)
    sorted_x = jnp.dot(perm, x1_ref[...].astype(BF16), preferred_element_type=F32)
    sorted_s[slot, :, 0:D_MODEL] = sorted_x
    sorted_s[slot, :, D_MODEL:XS_WIDTH] = sorted_g

    _for_each_run_unit(units_ref, tau,
                       lambda k, i: unit_copy(slot, l8_ref[k] + i, g8_ref[k] + i).start())

    @pl.when(tau == n - 1)
    def _():
        @pl.when(tau >= 1)
        def _():
            wait_units(1 - slot, tot_ref[jnp.maximum(tau - 1, 0)])
        wait_units(slot, tot_ref[tau])

        def wait_unused(m, carry):
            unused_block_copy(m).wait()
            return carry
        lax.fori_loop(nv_ref[0], n_blocks, wait_unused, 0)


def _dispatch_call(sched, x1, meta, lst_rows, cap):
    T = x1.shape[0]
    TD = TILE_ROWS
    grid_spec = pltpu.PrefetchScalarGridSpec(
        num_scalar_prefetch=7,
        grid=(T // TD,),
        in_specs=[
            pl.BlockSpec((TD, D_MODEL), lambda i, *_: (i, 0)),
            pl.BlockSpec((TD, LANES), lambda i, *_: (i, 0)),
            pl.BlockSpec((SUBLANES, LANES), lambda i, *_: (i, 0)),
        ],
        out_specs=pl.BlockSpec(memory_space=pl.ANY),
        scratch_shapes=[
            pltpu.VMEM((2, SORT_ROWS, XS_WIDTH), F32),
            pltpu.VMEM((EXPERT_BLOCK, XS_WIDTH), F32),
            pltpu.SemaphoreType.DMA((2,)),
            pltpu.SemaphoreType.DMA(()),
            pltpu.SemaphoreType.DMA(()),
        ],
    )
    return pl.pallas_call(
        _dispatch_body,
        grid_spec=grid_spec,
        out_shape=jax.ShapeDtypeStruct((cap, XS_WIDTH), F32),
        compiler_params=pltpu.CompilerParams(
            dimension_semantics=("arbitrary",), vmem_limit_bytes=VMEM_LIMIT),
        name="dispatch",
    )(*sched, x1, meta, lst_rows)


def _expert_body(be_ref, nv_ref, xs_ref, wg_ref, bg_ref, wu_ref, bu_ref, wd_ref, bd_ref,
                 y_ref, wg_s, wu_s, wd_s):
    m = pl.program_id(0)
    valid = m < nv_ref[0]
    changed = jnp.logical_or(m == 0, be_ref[m] != be_ref[jnp.maximum(m - 1, 0)])

    @pl.when(jnp.logical_and(valid, changed))
    def _():
        wg_s[...] = wg_ref[0].astype(BF16)
        wu_s[...] = wu_ref[0].astype(BF16)
        wd_s[...] = wd_ref[0].astype(BF16)

    @pl.when(valid)
    def _():
        xb = xs_ref[:, 0:D_MODEL].astype(BF16)
        gate = (xs_ref[:, D_MODEL:D_MODEL + 1] + xs_ref[:, D_MODEL + 1:D_MODEL + 2]
                + xs_ref[:, D_MODEL + 2:D_MODEL + 3])
        hg = jnp.minimum(jnp.dot(xb, wg_s[...], preferred_element_type=F32) + bg_ref[0], SWIGLU_LIMIT)
        hu = jnp.clip(jnp.dot(xb, wu_s[...], preferred_element_type=F32) + bu_ref[0],
                      -SWIGLU_LIMIT, SWIGLU_LIMIT)
        hact = (hu + 1.0) * (hg * _sigmoid(SWIGLU_ALPHA * hg))
        y = jnp.dot(hact.astype(BF16), wd_s[...], preferred_element_type=F32) + bd_ref[0]
        y_ref[...] = y * gate

    @pl.when(jnp.logical_not(valid))
    def _():
        y_ref[...] = jnp.zeros_like(y_ref)


def _expert_call(block_expert, n_valid, xs, w_gate, b_gate, w_up, b_up, w_down, b_down):
    n_blocks = xs.shape[0] // EXPERT_BLOCK
    wspec = pl.BlockSpec((1, D_MODEL, D_MODEL), lambda m, be, nv: (be[m], 0, 0))
    bspec = pl.BlockSpec((1, 1, D_MODEL), lambda m, be, nv: (be[m], 0, 0))
    grid_spec = pltpu.PrefetchScalarGridSpec(
        num_scalar_prefetch=2,
        grid=(n_blocks,),
        in_specs=[
            pl.BlockSpec((EXPERT_BLOCK, XS_WIDTH), lambda m, be, nv: (jnp.minimum(m, nv[0] - 1), 0)),
            wspec, bspec, wspec, bspec, wspec, bspec,
        ],
        out_specs=pl.BlockSpec((EXPERT_BLOCK, D_MODEL),
                               lambda m, be, nv: (m, 0)),
        scratch_shapes=[
            pltpu.VMEM((D_MODEL, D_MODEL), BF16),
            pltpu.VMEM((D_MODEL, D_MODEL), BF16),
            pltpu.VMEM((D_MODEL, D_MODEL), BF16),
        ],
    )
    return pl.pallas_call(
        _expert_body,
        grid_spec=grid_spec,
        out_shape=jax.ShapeDtypeStruct((n_blocks * EXPERT_BLOCK, D_MODEL), F32),
        compiler_params=pltpu.CompilerParams(
            dimension_semantics=("arbitrary",), vmem_limit_bytes=VMEM_LIMIT),
        name="experts",
    )(block_expert, n_valid, xs, w_gate, b_gate.reshape(N_EXPERTS, 1, D_MODEL),
      w_up, b_up.reshape(N_EXPERTS, 1, D_MODEL), w_down, b_down.reshape(N_EXPERTS, 1, D_MODEL))


def _combine_body(units_ref, g8_ref, l8_ref, tot_ref,
                  y_hbm, x1_ref, meta_ref, lst_ref, g_ref, b_ref, out_ref, ys_s, sems):
    tau = pl.program_id(0)
    n = pl.num_programs(0)
    slot = lax.rem(tau, 2)
    TD, S = TILE_ROWS, SORT_ROWS

    def unit_copy(s, g_unit, l_unit):
        return pltpu.make_async_copy(y_hbm.at[_rows(g_unit), :], ys_s.at[s, _rows(l_unit), :], sems.at[s])

    def fetch(tile, s):
        _for_each_run_unit(units_ref, tile,
                           lambda k, i: unit_copy(s, g8_ref[k] + i, l8_ref[k] + i).start())

    @pl.when(tau == 0)
    def _():
        ys_s[...] = jnp.zeros_like(ys_s)
        fetch(0, 0)

    @pl.when(tau + 1 < n)
    def _():
        fetch(tau + 1, 1 - slot)

    def wait_one(i, carry):
        unit_copy(slot, 0, 0).wait()
        return carry
    lax.fori_loop(0, tot_ref[tau], wait_one, 0)

    meta = meta_ref[...]
    lane_f = lax.broadcasted_iota(jnp.int32, (TD, LANES), 1).astype(F32)
    u_row = lst_ref[0:1, :]
    s_iota = lax.broadcasted_iota(jnp.int32, (TD, S), 1).astype(F32)
    conds = []
    for j in range(TOP_K):
        oh = lane_f == meta[:, j:j + 1]
        start8 = jnp.sum(jnp.where(oh, u_row, 0.0), axis=-1, keepdims=True)
        lpos = RUN_ALIGN * start8 + meta[:, TOP_K + j:TOP_K + j + 1]
        conds.append(s_iota == lpos)
    perm = jnp.where(conds[0], 1.0, jnp.where(conds[1], 1.0, jnp.where(conds[2], 1.0,
                     jnp.where(conds[3], 1.0, 0.0)))).astype(BF16)
    ys = ys_s[slot]
    yh = ys.astype(BF16)
    yl = (ys - yh.astype(F32)).astype(BF16)
    ffn = (jnp.dot(perm, yh, preferred_element_type=F32) + jnp.dot(perm, yl, preferred_element_type=F32))
    acc = DEEPNORM_ALPHA * x1_ref[...] + ffn
    mu = jnp.mean(acc, axis=-1, keepdims=True)
    hc = acc - mu
    var = jnp.mean(hc * hc, axis=-1, keepdims=True)
    out_ref[...] = hc * lax.rsqrt(var + LN_EPS) * g_ref[...] + b_ref[...]


def _combine_call(sched, y_rows, x1, meta, lst_rows, ln2_g, ln2_b):
    T = x1.shape[0]
    TD = TILE_ROWS
    grid_spec = pltpu.PrefetchScalarGridSpec(
        num_scalar_prefetch=4,
        grid=(T // TD,),
        in_specs=[
            pl.BlockSpec(memory_space=pl.ANY),
            pl.BlockSpec((TD, D_MODEL), lambda i, *_: (i, 0)),
            pl.BlockSpec((TD, LANES), lambda i, *_: (i, 0)),
            pl.BlockSpec((SUBLANES, LANES), lambda i, *_: (i, 0)),
            pl.BlockSpec((1, D_MODEL), lambda i, *_: (0, 0)),
            pl.BlockSpec((1, D_MODEL), lambda i, *_: (0, 0)),
        ],
        out_specs=pl.BlockSpec((TD, D_MODEL), lambda i, *_: (i, 0)),
        scratch_shapes=[pltpu.VMEM((2, SORT_ROWS, D_MODEL), F32), pltpu.SemaphoreType.DMA((2,))],
    )
    return pl.pallas_call(
        _combine_body,
        grid_spec=grid_spec,
        out_shape=jax.ShapeDtypeStruct((T, D_MODEL), F32),
        compiler_params=pltpu.CompilerParams(
            dimension_semantics=("arbitrary",), vmem_limit_bytes=VMEM_LIMIT),
        name="combine",
    )(*sched, y_rows, x1, meta, lst_rows, ln2_g, ln2_b)


def _np_consts():
    TL = TILE_ROWS
    r = np.arange(TL)
    same = (r[:, None] // CHUNK) == (r[None, :] // CHUNK)
    tril = (same & (r[None, :] <= r[:, None])).astype(np.float32)
    trils = (r[None, :] < r[:, None]).astype(np.float32)
    e128 = np.zeros((LANES, SSD_WIDTH), np.float32)
    for h in range(SSD_HEADS):
        e128[h, h * SSD_HEAD_DIM:(h + 1) * SSD_HEAD_DIM] = 1.0
    return tril, tril.T.copy(), trils, e128


def kernel(x, w_in, hg_lower_bound, hg_norm_w, conv_w, conv_b, dt_bias, a_log, d_skip, ssd_norm_w, w_out,
           ln1_g, ln1_b, router_w, router_b, w_gate, b_gate, w_up, b_up, w_down, b_down, ln2_g, ln2_b):
    batch, seq, d = x.shape
    assert d == D_MODEL and seq % TILE_ROWS == 0 and w_in.shape[0] == DEPTH
    T = batch * seq
    n_tiles = T // TILE_ROWS
    max_rows = T * TOP_K + n_tiles * N_EXPERTS * (RUN_ALIGN - 1)
    n_blocks = -(-max_rows // EXPERT_BLOCK) + N_EXPERTS
    cap = n_blocks * EXPERT_BLOCK
    assert SORT_ROWS >= TILE_ROWS * TOP_K + N_EXPERTS * (RUN_ALIGN - 1)

    tril, triu, trils, e128 = _np_consts()
    w = w_in[0]
    wdt = w[:, MAIN_COLS:]
    pad_l = LANES - SSD_HEADS
    rw = jnp.pad(router_w[0], ((0, 0), (0, LANES - N_EXPERTS)))
    rwh = rw.astype(BF16)
    rwm = (rw - rwh.astype(F32)).astype(BF16)
    consts = [
        w[:, :MAIN_COLS].astype(BF16),
        jnp.pad(wdt, ((0, 0), (0, pad_l))).astype(BF16),
        wdt.T.astype(BF16),
        hg_lower_bound,
        hg_norm_w[0].reshape(1, HG_HEAD_DIM),
        conv_w[0],
        conv_b[0].reshape(1, SSD_CONV_DIM),
        jnp.pad(dt_bias[0], (0, pad_l)).reshape(1, LANES),
        dt_bias[0].reshape(SSD_HEADS, 1),
        jnp.pad(a_log[0], (0, pad_l)).reshape(1, LANES),
        a_log[0].reshape(SSD_HEADS, 1),
        jnp.repeat(d_skip[0], SSD_HEAD_DIM).reshape(1, SSD_WIDTH),
        ssd_norm_w[0].reshape(1, SSD_WIDTH),
        w_out[0].astype(BF16),
        ln1_g[0].reshape(1, D_MODEL),
        ln1_b[0].reshape(1, D_MODEL),
        jnp.concatenate([rwh, rwm], axis=1),
        rwh,
        jnp.pad(router_b[0], (0, LANES - N_EXPERTS)).reshape(1, LANES),
        jnp.asarray(tril, BF16), jnp.asarray(triu, BF16), jnp.asarray(trils, BF16), jnp.asarray(e128, BF16),
    ]
    x1, meta, cnt = _mixer_call(x.reshape(T, D_MODEL), consts, batch, seq)

    counts = cnt.reshape(n_tiles, SUBLANES, LANES)[:, 0, :N_EXPERTS].astype(jnp.int32)
    c8 = (counts + RUN_ALIGN - 1) // RUN_ALIGN * RUN_ALIGN
    used = jnp.sum(c8, axis=0)
    region = (used + EXPERT_BLOCK - 1) // EXPERT_BLOCK * EXPERT_BLOCK
    region_end = jnp.cumsum(region)
    region_start = region_end - region
    gstart = region_start[None, :] + jnp.cumsum(c8, axis=0) - c8
    lstart = jnp.cumsum(c8, axis=1) - c8
    n_valid = region_end[-1] // EXPERT_BLOCK
    blk_row = jnp.arange(n_blocks, dtype=jnp.int32) * EXPERT_BLOCK
    blk_row = jnp.minimum(blk_row, (n_valid - 1) * EXPERT_BLOCK)
    block_expert = jnp.minimum(jnp.sum(region_end[None, :] <= blk_row[:, None], axis=1),
                               N_EXPERTS - 1).astype(jnp.int32)
    as_units = lambda a: (a // RUN_ALIGN).astype(jnp.int32).reshape(-1)
    units, g8, l8 = as_units(c8), as_units(gstart), as_units(lstart)
    tot = (jnp.sum(c8, axis=1) // RUN_ALIGN).astype(jnp.int32)
    tail8 = as_units(region_start + used)
    tailn = as_units(region - used)
    lst_rows = jnp.repeat(jnp.pad((lstart // RUN_ALIGN).astype(F32), ((0, 0), (0, LANES - N_EXPERTS))),
                          SUBLANES, axis=0)

    nv = n_valid.astype(jnp.int32).reshape(1)
    xs = _dispatch_call((units, g8, l8, tot, tail8, tailn, nv), x1, meta, lst_rows, cap)
    y_rows = _expert_call(block_expert, nv, xs,
                          w_gate[0], b_gate[0], w_up[0], b_up[0], w_down[0], b_down[0])
    out = _combine_call((units, g8, l8, tot), y_rows, x1, meta, lst_rows,
                        ln2_g[0].reshape(1, D_MODEL), ln2_b[0].reshape(1, D_MODEL))
    return out.reshape(batch, seq, D_MODEL)
```

```python
import functools

import jax
import jax.numpy as jnp
import numpy as np
from jax import lax
from jax.experimental import pallas as pl
from jax.experimental.pallas import tpu as pltpu

F32 = jnp.float32
BF16 = jnp.bfloat16

D_MODEL = 1024
CHUNK = 64
HG_WIDTH = 512
HG_HEAD_DIM = 128
HG_HEADS = 4
SSD_WIDTH = 512
SSD_HEAD_DIM = 64
SSD_HEADS = 8
SSD_GROUPS = 2
SSD_STATE = 128
SSD_CONV = 4
SSD_CONV_DIM = SSD_WIDTH + 2 * SSD_GROUPS * SSD_STATE
N_EXPERTS = 32
TOP_K = 4
EXPERT_BLOCK = 256
SWIGLU_LIMIT = 7.0
SWIGLU_ALPHA = 1.702
DEPTH = 1
DEEPNORM_ALPHA = (2 * DEPTH) ** 0.25
LN_EPS = 1e-5
RMS_EPS = 1e-5

LANES = 128
SUBLANES = 8
SUB_CHUNK = 16
EXP_CAP = 60.0
TILE_ROWS = 256
RUN_ALIGN = SUBLANES
SORT_ROWS = 1280
XS_WIDTH = D_MODEL + LANES
VMEM_LIMIT = 56 * 1024 * 1024

OFF_Q, OFF_F, OFF_I, OFF_G = 0, 512, 1024, 1536
OFF_Z, OFF_XBC = 2048, 2560
OFF_XS, OFF_B, OFF_C = 2560, 3072, 3328
MAIN_COLS = 3584


def _bdot(a, b):
    return jnp.dot(a.astype(BF16), b.astype(BF16), preferred_element_type=F32)


def _bdot_nt(a, b):
    return lax.dot_general(a.astype(BF16), b.astype(BF16), (((1,), (1,)), ((), ())),
                           preferred_element_type=F32)


def _bdot_tn(a, b):
    return lax.dot_general(a.astype(BF16), b.astype(BF16), (((0,), (0,)), ((), ())),
                           preferred_element_type=F32)


def _split3(a):
    hi = a.astype(BF16)
    r1 = a - hi.astype(F32)
    mid = r1.astype(BF16)
    lo = (r1 - mid.astype(F32)).astype(BF16)
    return hi, mid, lo


def _dot01_left(m01, a):
    hi, mid, lo = _split3(a)
    d = functools.partial(jnp.dot, m01, preferred_element_type=F32)
    return d(hi) + d(mid) + d(lo)


def _dot01_right(a, m01):
    hi, mid, lo = _split3(a)
    return (jnp.dot(hi, m01, preferred_element_type=F32) + jnp.dot(mid, m01, preferred_element_type=F32)
            + jnp.dot(lo, m01, preferred_element_type=F32))


def _sigmoid(x):
    return 1.0 / (1.0 + jnp.exp(-x))


def _silu(x):
    return x * _sigmoid(x)


def _softplus(x):
    return jnp.maximum(x, 0.0) + jnp.log(1.0 + jnp.exp(-jnp.abs(x)))


def _mixer_body(x_ref, win_ref, wdt_ref, wdtT_ref, lbp_ref, hgnw_ref, convw_ref, convb_ref,
                dtb_row_ref, dtb_col_ref, alog_row_ref, alog_col_ref, dskip_ref, ssdnw_ref,
                wout_ref, ln1g_ref, ln1b_ref, rw1_ref, rwh_ref, rb_ref,
                tril_ref, triu_ref, trils_ref, e128_ref,
                x1_ref, meta_ref, cnt_ref,
                proj_s, b_s, xpad_s, xdt_s, cse_s, st_s, pt_s, ohg_s, ossd_s):
    TL = TILE_ROWS
    t = pl.program_id(1)

    @pl.when(t == 0)
    def _():
        xpad_s[0:SUBLANES, :] = jnp.zeros((SUBLANES, SSD_CONV_DIM), F32)
        st_s[...] = jnp.zeros_like(st_s)
        pt_s[...] = jnp.zeros_like(pt_s)

    x = x_ref[...]
    xb = x.astype(BF16)
    proj_s[...] = jnp.dot(xb, win_ref[...], preferred_element_type=F32)
    dtc_raw = jnp.dot(xb, wdt_ref[...], preferred_element_type=F32)
    dtr_raw = lax.dot_general(wdtT_ref[...], xb, (((1,), (1,)), ((), ())),
                              preferred_element_type=F32)

    tril = tril_ref[...]

    a0 = lbp_ref[0:1, :]
    a1 = lbp_ref[1:2, :]
    am = jnp.maximum(a0, a1)
    e0 = jnp.exp(a0 - am)
    e1 = jnp.exp(a1 - am)
    lb = e0 / (e0 + e1)
    f = lb + (1.0 - lb) * _sigmoid(proj_s[:, OFF_F:OFF_F + HG_WIDTH])
    b_s[...] = _dot01_left(tril, jnp.log(f))
    proj_s[:, OFF_F:OFF_F + HG_WIDTH] = f
    proj_s[:, OFF_Q:OFF_Q + HG_WIDTH] = _silu(proj_s[:, OFF_Q:OFF_Q + HG_WIDTH])

    row64 = lax.broadcasted_iota(jnp.int32, (CHUNK, CHUNK), 0)
    col64 = lax.broadcasted_iota(jnp.int32, (CHUNK, CHUNK), 1)
    causal = row64 >= col64
    hgnw = hgnw_ref[...]

    for c in range(TL // CHUNK):
        r0 = c * CHUNK
        for h in range(HG_HEADS):
            h0 = h * HG_HEAD_DIM
            bc = b_s[r0:r0 + CHUNK, h0:h0 + HG_HEAD_DIM]
            qc = proj_s[r0:r0 + CHUNK, OFF_Q + h0:OFF_Q + h0 + HG_HEAD_DIM]
            kc = 1.0 - proj_s[r0:r0 + CHUNK, OFF_F + h0:OFF_F + h0 + HG_HEAD_DIM]
            vc = proj_s[r0:r0 + CHUNK, OFF_I + h0:OFF_I + h0 + HG_HEAD_DIM]
            parts = []
            for i in range(CHUNK // SUB_CHUNK):
                s0 = i * SUB_CHUNK
                if i == 0:
                    qi = qc[0:SUB_CHUNK] * jnp.exp(bc[0:SUB_CHUNK])
                    ki = kc * jnp.exp(jnp.minimum(-bc, EXP_CAP))
                else:
                    ref_i = bc[s0 - 1:s0, :]
                    qi = qc[s0:s0 + SUB_CHUNK] * jnp.exp(bc[s0:s0 + SUB_CHUNK] - ref_i)
                    ki = kc * jnp.exp(jnp.minimum(ref_i - bc, EXP_CAP))
                parts.append(_bdot_nt(qi, ki))
            sc = jnp.concatenate(parts, axis=0)
            sc = jnp.where(causal, sc, 0.0)
            st = st_s[h]
            o = _bdot(sc, vc) + _bdot_nt(qc * jnp.exp(bc), st)
            b_end = bc[CHUNK - 1:CHUNK, :]
            kdec = kc * jnp.exp(b_end - bc)
            st_s[h] = st * jnp.exp(b_end) + _bdot_tn(vc, kdec)
            ms = jnp.mean(o * o, axis=-1, keepdims=True)
            on = o * lax.rsqrt(ms + RMS_EPS) * hgnw
            gc = proj_s[r0:r0 + CHUNK, OFF_G + h0:OFF_G + h0 + HG_HEAD_DIM]
            ohg_s[r0:r0 + CHUNK, h0:h0 + HG_HEAD_DIM] = on * _silu(gc)

    e128 = e128_ref[...]
    dtc = _softplus(dtc_raw + dtb_row_ref[...])
    a_row = -jnp.exp(alog_row_ref[...])
    cs_c = _dot01_left(tril, dtc * a_row)
    cse_s[...] = _dot01_right(cs_c, e128)
    dt_exp = _dot01_right(dtc, e128)
    dtr = _softplus(dtr_raw + dtb_col_ref[...])
    cs_r = _dot01_right(dtr * (-jnp.exp(alog_col_ref[...])), triu_ref[...])

    xpad_s[SUBLANES:SUBLANES + TL, :] = proj_s[:, OFF_XBC:OFF_XBC + SSD_CONV_DIM]
    acc = jnp.broadcast_to(convb_ref[...], (TL, SSD_CONV_DIM))
    for j in range(SSD_CONV):
        off = SUBLANES - (SSD_CONV - 1) + j
        acc = acc + convw_ref[j:j + 1, :] * xpad_s[off:off + TL, :]
    proj_s[:, OFF_XBC:OFF_XBC + SSD_CONV_DIM] = _silu(acc)
    xpad_s[0:SUBLANES, :] = xpad_s[TL:TL + SUBLANES, :]
    xdt_s[...] = proj_s[:, OFF_XS:OFF_XS + SSD_WIDTH] * dt_exp

    gw = SSD_WIDTH // SSD_GROUPS
    hpg = SSD_HEADS // SSD_GROUPS
    lane_head = lax.broadcasted_iota(jnp.int32, (CHUNK, gw), 1) // SSD_HEAD_DIM
    for c in range(TL // CHUNK):
        r0 = c * CHUNK
        for g in range(SSD_GROUPS):
            g0 = g * gw
            bg = proj_s[r0:r0 + CHUNK, OFF_B + g * SSD_STATE:OFF_B + (g + 1) * SSD_STATE]
            cg = proj_s[r0:r0 + CHUNK, OFF_C + g * SSD_STATE:OFF_C + (g + 1) * SSD_STATE]
            gm = _bdot_nt(cg, bg)
            cse_g = cse_s[r0:r0 + CHUNK, g0:g0 + gw]
            cs_end = cse_s[r0 + CHUNK - 1:r0 + CHUNK, g0:g0 + gw]
            xdt_g = xdt_s[r0:r0 + CHUNK, g0:g0 + gw]
            ydiag = jnp.zeros((CHUNK, gw), F32)
            for hl in range(hpg):
                hh = g * hpg + hl
                seg = cse_g[:, hl * SSD_HEAD_DIM:(hl + 1) * SSD_HEAD_DIM] - cs_r[hh:hh + 1, r0:r0 + CHUNK]
                lm = jnp.where(causal, jnp.exp(jnp.minimum(seg, 0.0)), 0.0)
                xm = jnp.where(lane_head == hl, xdt_g, 0.0)
                ydiag = ydiag + _bdot(gm * lm, xm)
            pt = pt_s[g]
            yoff = _bdot(cg, pt) * jnp.exp(cse_g)
            stg = _bdot_tn(bg, xdt_g * jnp.exp(cs_end - cse_g))
            pt_s[g] = pt * jnp.exp(cs_end) + stg
            xs_g = proj_s[r0:r0 + CHUNK, OFF_XS + g0:OFF_XS + g0 + gw]
            ossd_s[r0:r0 + CHUNK, g0:g0 + gw] = ydiag + yoff + xs_g * dskip_ref[:, g0:g0 + gw]

    y = ossd_s[...] * _silu(proj_s[:, OFF_Z:OFF_Z + SSD_WIDTH])
    mix = _bdot(ohg_s[...], wout_ref[0:HG_WIDTH, :])
    for g in range(SSD_GROUPS):
        yg = y[:, g * gw:(g + 1) * gw]
        ms = jnp.mean(yg * yg, axis=-1, keepdims=True)
        yn = yg * lax.rsqrt(ms + RMS_EPS) * ssdnw_ref[:, g * gw:(g + 1) * gw]
        mix = mix + _bdot(yn, wout_ref[HG_WIDTH + g * gw:HG_WIDTH + (g + 1) * gw, :])

    hres = DEEPNORM_ALPHA * x + mix
    mu = jnp.mean(hres, axis=-1, keepdims=True)
    hc = hres - mu
    var = jnp.mean(hc * hc, axis=-1, keepdims=True)
    x1 = hc * lax.rsqrt(var + LN_EPS) * ln1g_ref[...] + ln1b_ref[...]
    x1_ref[...] = x1

    xh = x1.astype(BF16)
    xm_ = (x1 - xh.astype(F32)).astype(BF16)
    t1 = jnp.dot(xh, rw1_ref[...], preferred_element_type=F32)
    logits = (t1[:, 0:LANES] + t1[:, LANES:2 * LANES]
              + jnp.dot(xm_, rwh_ref[...], preferred_element_type=F32) + rb_ref[...])
    lane = lax.broadcasted_iota(jnp.int32, (TL, LANES), 1)
    lane_f = lane.astype(F32)
    neg = jnp.float32(-jnp.inf)
    work = jnp.where(lane < N_EXPERTS, logits, neg)
    onehots, vals, idxs = [], [], []
    for j in range(TOP_K):
        m = jnp.max(work, axis=-1, keepdims=True)
        idx = jnp.min(jnp.where(work == m, lane_f, float(LANES)), axis=-1, keepdims=True)
        oh = lane_f == idx
        onehots.append(oh)
        vals.append(m)
        idxs.append(idx)
        work = jnp.where(oh, neg, work)
    es = [jnp.exp(v - vals[0]) for v in vals]
    den = es[0] + es[1] + es[2] + es[3]
    gates = [e / den for e in es]
    sel = jnp.zeros((TL, LANES), F32)
    for oh in onehots:
        sel = jnp.where(oh, 1.0, sel)
    rankmat = jnp.dot(trils_ref[...], sel.astype(BF16), preferred_element_type=F32)
    cnt_ref[...] = jnp.broadcast_to(jnp.sum(sel, axis=0, keepdims=True), (SUBLANES, LANES))
    meta = jnp.zeros((TL, LANES), F32)
    for j in range(TOP_K):
        rank_j = jnp.sum(jnp.where(onehots[j], rankmat, 0.0), axis=-1, keepdims=True)
        meta = jnp.where(lane == j, idxs[j], meta)
        meta = jnp.where(lane == TOP_K + j, rank_j, meta)
        meta = jnp.where(lane == 2 * TOP_K + j, gates[j], meta)
    meta_ref[...] = meta


def _full(shape):
    nd = len(shape)
    return pl.BlockSpec(shape, lambda *_: (0,) * nd)


def _mixer_call(x2d, consts, batch, seq):
    TL = TILE_ROWS
    nt = seq // TL
    T = batch * seq
    in_specs = [pl.BlockSpec((TL, D_MODEL), lambda b, t: (b * nt + t, 0))]
    in_specs += [_full(c.shape) for c in consts]
    out_shape = (jax.ShapeDtypeStruct((T, D_MODEL), F32),
                 jax.ShapeDtypeStruct((T, LANES), F32),
                 jax.ShapeDtypeStruct((batch * nt * SUBLANES, LANES), F32))
    out_specs = (pl.BlockSpec((TL, D_MODEL), lambda b, t: (b * nt + t, 0)),
                 pl.BlockSpec((TL, LANES), lambda b, t: (b * nt + t, 0)),
                 pl.BlockSpec((SUBLANES, LANES), lambda b, t: (b * nt + t, 0)))
    scratch = [
        pltpu.VMEM((TL, MAIN_COLS), F32),
        pltpu.VMEM((TL, HG_WIDTH), F32),
        pltpu.VMEM((TL + 2 * SUBLANES, SSD_CONV_DIM), F32),
        pltpu.VMEM((TL, SSD_WIDTH), F32),
        pltpu.VMEM((TL, SSD_WIDTH), F32),
        pltpu.VMEM((HG_HEADS, HG_HEAD_DIM, HG_HEAD_DIM), F32),
        pltpu.VMEM((SSD_GROUPS, SSD_STATE, SSD_WIDTH // SSD_GROUPS), F32),
        pltpu.VMEM((TL, HG_WIDTH), F32),
        pltpu.VMEM((TL, SSD_WIDTH), F32),
    ]
    return pl.pallas_call(
        _mixer_body,
        grid=(batch, nt),
        in_specs=in_specs,
        out_specs=out_specs,
        out_shape=out_shape,
        scratch_shapes=scratch,
        compiler_params=pltpu.CompilerParams(
            dimension_semantics=("arbitrary", "arbitrary"), vmem_limit_bytes=VMEM_LIMIT),
        name="mixer",
    )(x2d, *consts)


def _for_each_run(units_ref, tile, fn):
    def per_expert(e, carry):
        k = tile * N_EXPERTS + e
        n = units_ref[k]

        @pl.when(n > 0)
        def _():
            fn(k, n)
        return carry
    lax.fori_loop(0, N_EXPERTS, per_expert, 0)


def _rows(unit, n_units=1):
    return pl.ds(pl.multiple_of(unit * RUN_ALIGN, RUN_ALIGN), n_units * RUN_ALIGN)


def _dispatch_body(units_ref, g8_ref, l8_ref, tot_ref, tail8_ref, tailn_ref, nv_ref,
                   x1_ref, meta_ref, lst_ref, xs_hbm, sorted_s, zero_s, sems, zsem, bsem):
    tau = pl.program_id(0)
    n = pl.num_programs(0)
    slot = lax.rem(tau, 2)
    TD, S = TILE_ROWS, SORT_ROWS
    n_blocks = xs_hbm.shape[0] // EXPERT_BLOCK

    def unused_block_copy(m):
        rows = pl.ds(pl.multiple_of(m * EXPERT_BLOCK, EXPERT_BLOCK), EXPERT_BLOCK)
        return pltpu.make_async_copy(zero_s, xs_hbm.at[rows, :], bsem)

    def run_copy(s, l_unit, g_unit, n_units):
        return pltpu.make_async_copy(sorted_s.at[s, _rows(l_unit, n_units), :],
                                     xs_hbm.at[_rows(g_unit, n_units), :], sems.at[s])

    def wait_units(s, count):
        run_copy(s, 0, 0, count).wait()

    @pl.when(tau == 0)
    def _():
        zero_s[...] = jnp.zeros_like(zero_s)

        def tail_copy(e, i):
            return pltpu.make_async_copy(zero_s.at[pl.ds(0, RUN_ALIGN), :],
                                         xs_hbm.at[_rows(tail8_ref[e] + i), :], zsem)

        def start_unused(m, carry):
            unused_block_copy(m).start()
            return carry
        lax.fori_loop(nv_ref[0], n_blocks, start_unused, 0)

        def start_e(e, carry):
            def unit(i, c2):
                tail_copy(e, i).start()
                return c2
            lax.fori_loop(0, tailn_ref[e], unit, 0)
            return carry
        lax.fori_loop(0, N_EXPERTS, start_e, 0)

        def wait_e(e, carry):
            def unit(i, c2):
                tail_copy(e, 0).wait()
                return c2
            lax.fori_loop(0, tailn_ref[e], unit, 0)
            return carry
        lax.fori_loop(0, N_EXPERTS, wait_e, 0)

    @pl.when(tau >= 2)
    def _():
        wait_units(slot, tot_ref[jnp.maximum(tau - 2, 0)])

    meta = meta_ref[...]
    lane = lax.broadcasted_iota(jnp.int32, (TD, LANES), 1)
    lane_f = lane.astype(F32)
    u8 = jnp.broadcast_to(lst_ref[0:1, :], (SUBLANES, LANES)).astype(BF16)
    ones8 = jnp.ones((SUBLANES, LANES), BF16)
    r_iota = lax.broadcasted_iota(jnp.int32, (S, TD), 0).astype(F32)
    gh = meta.astype(BF16).astype(F32)
    g1 = meta - gh
    gm = g1.astype(BF16).astype(F32)
    gl = g1 - gm
    nt_dims = (((1,), (1,)), ((), ()))
    conds, gparts = [], []
    for j in range(TOP_K):
        oh = lane_f == meta[:, j:j + 1]
        ohb = jnp.where(oh, 1.0, 0.0).astype(BF16)
        rkb = jnp.where(oh, meta[:, TOP_K + j:TOP_K + j + 1], 0.0).astype(BF16)
        m1 = lax.dot_general(u8, ohb, nt_dims, preferred_element_type=F32)
        m2 = lax.dot_general(ones8, rkb, nt_dims, preferred_element_type=F32)
        lpos = RUN_ALIGN * m1[0:1, :] + m2[0:1, :]
        conds.append(r_iota == lpos)
        gc = 2 * TOP_K + j
        gparts.append(jnp.where(lane == 0, gh[:, gc:gc + 1],
                                jnp.where(lane == 1, gm[:, gc:gc + 1],
                                          jnp.where(lane == 2, gl[:, gc:gc + 1], 0.0))).astype(BF16))
    pcat = jnp.concatenate([jnp.where(c, 1.0, 0.0).astype(BF16) for c in conds], axis=1)
    sorted_g = jnp.dot(pcat, jnp.concatenate(gparts, axis=0), preferred_element_type=F32)
    perm = jnp.where(conds[0], 1.0, jnp.where(conds[1], 1.0, jnp.where(conds[2], 1.0,
                     jnp.where(conds[3], 1.0, 0.0)))).astype(BF16)
    sorted_x = jnp.dot(perm, x1_ref[...].astype(BF16), preferred_element_type=F32)
    sorted_s[slot, :, 0:D_MODEL] = sorted_x
    sorted_s[slot, :, D_MODEL:XS_WIDTH] = sorted_g

    _for_each_run(units_ref, tau, lambda k, cnt: run_copy(slot, l8_ref[k], g8_ref[k], cnt).start())

    @pl.when(tau == n - 1)
    def _():
        @pl.when(tau >= 1)
        def _():
            wait_units(1 - slot, tot_ref[jnp.maximum(tau - 1, 0)])
        wait_units(slot, tot_ref[tau])

        def wait_unused(m, carry):
            unused_block_copy(m).wait()
            return carry
        lax.fori_loop(nv_ref[0], n_blocks, wait_unused, 0)


def _dispatch_call(sched, x1, meta, lst_rows, cap):
    T = x1.shape[0]
    TD = TILE_ROWS
    grid_spec = pltpu.PrefetchScalarGridSpec(
        num_scalar_prefetch=7,
        grid=(T // TD,),
        in_specs=[
            pl.BlockSpec((TD, D_MODEL), lambda i, *_: (i, 0)),
            pl.BlockSpec((TD, LANES), lambda i, *_: (i, 0)),
            pl.BlockSpec((SUBLANES, LANES), lambda i, *_: (i, 0)),
        ],
        out_specs=pl.BlockSpec(memory_space=pl.ANY),
        scratch_shapes=[
            pltpu.VMEM((2, SORT_ROWS, XS_WIDTH), F32),
            pltpu.VMEM((EXPERT_BLOCK, XS_WIDTH), F32),
            pltpu.SemaphoreType.DMA((2,)),
            pltpu.SemaphoreType.DMA(()),
            pltpu.SemaphoreType.DMA(()),
        ],
    )
    return pl.pallas_call(
        _dispatch_body,
        grid_spec=grid_spec,
        out_shape=jax.ShapeDtypeStruct((cap, XS_WIDTH), F32),
        compiler_params=pltpu.CompilerParams(
            dimension_semantics=("arbitrary",), vmem_limit_bytes=VMEM_LIMIT),
        name="dispatch",
    )(*sched, x1, meta, lst_rows)


def _expert_body(be_ref, nv_ref, xs_ref, wg_ref, bg_ref, wu_ref, bu_ref, wd_ref, bd_ref,
                 y_ref, wg_s, wu_s, wd_s):
    m = pl.program_id(0)
    valid = m < nv_ref[0]
    changed = jnp.logical_or(m == 0, be_ref[m] != be_ref[jnp.maximum(m - 1, 0)])

    @pl.when(jnp.logical_and(valid, changed))
    def _():
        wg_s[...] = wg_ref[0].astype(BF16)
        wu_s[...] = wu_ref[0].astype(BF16)
        wd_s[...] = wd_ref[0].astype(BF16)

    @pl.when(valid)
    def _():
        xb = xs_ref[:, 0:D_MODEL].astype(BF16)
        gate = (xs_ref[:, D_MODEL:D_MODEL + 1] + xs_ref[:, D_MODEL + 1:D_MODEL + 2]
                + xs_ref[:, D_MODEL + 2:D_MODEL + 3])
        hg = jnp.minimum(jnp.dot(xb, wg_s[...], preferred_element_type=F32) + bg_ref[0], SWIGLU_LIMIT)
        hu = jnp.clip(jnp.dot(xb, wu_s[...], preferred_element_type=F32) + bu_ref[0],
                      -SWIGLU_LIMIT, SWIGLU_LIMIT)
        hact = (hu + 1.0) * (hg * _sigmoid(SWIGLU_ALPHA * hg))
        y = jnp.dot(hact.astype(BF16), wd_s[...], preferred_element_type=F32) + bd_ref[0]
        y_ref[...] = y * gate

    @pl.when(jnp.logical_not(valid))
    def _():
        y_ref[...] = jnp.zeros_like(y_ref)


def _expert_call(block_expert, n_valid, xs, w_gate, b_gate, w_up, b_up, w_down, b_down):
    n_blocks = xs.shape[0] // EXPERT_BLOCK
    wspec = pl.BlockSpec((1, D_MODEL, D_MODEL), lambda m, be, nv: (be[m], 0, 0))
    bspec = pl.BlockSpec((1, 1, D_MODEL), lambda m, be, nv: (be[m], 0, 0))
    grid_spec = pltpu.PrefetchScalarGridSpec(
        num_scalar_prefetch=2,
        grid=(n_blocks,),
        in_specs=[
            pl.BlockSpec((EXPERT_BLOCK, XS_WIDTH), lambda m, be, nv: (jnp.minimum(m, nv[0] - 1), 0)),
            wspec, bspec, wspec, bspec, wspec, bspec,
        ],
        out_specs=pl.BlockSpec((EXPERT_BLOCK, D_MODEL),
                               lambda m, be, nv: (m, 0)),
        scratch_shapes=[
            pltpu.VMEM((D_MODEL, D_MODEL), BF16),
            pltpu.VMEM((D_MODEL, D_MODEL), BF16),
            pltpu.VMEM((D_MODEL, D_MODEL), BF16),
        ],
    )
    return pl.pallas_call(
        _expert_body,
        grid_spec=grid_spec,
        out_shape=jax.ShapeDtypeStruct((n_blocks * EXPERT_BLOCK, D_MODEL), F32),
        compiler_params=pltpu.CompilerParams(
            dimension_semantics=("arbitrary",), vmem_limit_bytes=VMEM_LIMIT),
        name="experts",
    )(block_expert, n_valid, xs, w_gate, b_gate.reshape(N_EXPERTS, 1, D_MODEL),
      w_up, b_up.reshape(N_EXPERTS, 1, D_MODEL), w_down, b_down.reshape(N_EXPERTS, 1, D_MODEL))


def _combine_body(units_ref, g8_ref, l8_ref, tot_ref,
                  y_hbm, x1_ref, meta_ref, lst_ref, g_ref, b_ref, out_ref, ys_s, sems):
    tau = pl.program_id(0)
    n = pl.num_programs(0)
    slot = lax.rem(tau, 2)
    TD, S = TILE_ROWS, SORT_ROWS

    def run_copy(s, g_unit, l_unit, n_units):
        return pltpu.make_async_copy(y_hbm.at[_rows(g_unit, n_units), :],
                                     ys_s.at[s, _rows(l_unit, n_units), :], sems.at[s])

    def fetch(tile, s):
        _for_each_run(units_ref, tile, lambda k, cnt: run_copy(s, g8_ref[k], l8_ref[k], cnt).start())

    @pl.when(tau == 0)
    def _():
        ys_s[...] = jnp.zeros_like(ys_s)
        fetch(0, 0)

    @pl.when(tau + 1 < n)
    def _():
        fetch(tau + 1, 1 - slot)

    run_copy(slot, 0, 0, tot_ref[tau]).wait()

    meta = meta_ref[...]
    lane_f = lax.broadcasted_iota(jnp.int32, (TD, LANES), 1).astype(F32)
    u_row = lst_ref[0:1, :]
    s_iota = lax.broadcasted_iota(jnp.int32, (TD, S), 1).astype(F32)
    conds = []
    for j in range(TOP_K):
        oh = lane_f == meta[:, j:j + 1]
        start8 = jnp.sum(jnp.where(oh, u_row, 0.0), axis=-1, keepdims=True)
        lpos = RUN_ALIGN * start8 + meta[:, TOP_K + j:TOP_K + j + 1]
        conds.append(s_iota == lpos)
    perm = jnp.where(conds[0], 1.0, jnp.where(conds[1], 1.0, jnp.where(conds[2], 1.0,
                     jnp.where(conds[3], 1.0, 0.0)))).astype(BF16)
    ys = ys_s[slot]
    yh = ys.astype(BF16)
    yl = (ys - yh.astype(F32)).astype(BF16)
    ffn = (jnp.dot(perm, yh, preferred_element_type=F32) + jnp.dot(perm, yl, preferred_element_type=F32))
    acc = DEEPNORM_ALPHA * x1_ref[...] + ffn
    mu = jnp.mean(acc, axis=-1, keepdims=True)
    hc = acc - mu
    var = jnp.mean(hc * hc, axis=-1, keepdims=True)
    out_ref[...] = hc * lax.rsqrt(var + LN_EPS) * g_ref[...] + b_ref[...]


def _combine_call(sched, y_rows, x1, meta, lst_rows, ln2_g, ln2_b):
    T = x1.shape[0]
    TD = TILE_ROWS
    grid_spec = pltpu.PrefetchScalarGridSpec(
        num_scalar_prefetch=4,
        grid=(T // TD,),
        in_specs=[
            pl.BlockSpec(memory_space=pl.ANY),
            pl.BlockSpec((TD, D_MODEL), lambda i, *_: (i, 0)),
            pl.BlockSpec((TD, LANES), lambda i, *_: (i, 0)),
            pl.BlockSpec((SUBLANES, LANES), lambda i, *_: (i, 0)),
            pl.BlockSpec((1, D_MODEL), lambda i, *_: (0, 0)),
            pl.BlockSpec((1, D_MODEL), lambda i, *_: (0, 0)),
        ],
        out_specs=pl.BlockSpec((TD, D_MODEL), lambda i, *_: (i, 0)),
        scratch_shapes=[pltpu.VMEM((2, SORT_ROWS, D_MODEL), F32), pltpu.SemaphoreType.DMA((2,))],
    )
    return pl.pallas_call(
        _combine_body,
        grid_spec=grid_spec,
        out_shape=jax.ShapeDtypeStruct((T, D_MODEL), F32),
        compiler_params=pltpu.CompilerParams(
            dimension_semantics=("arbitrary",), vmem_limit_bytes=VMEM_LIMIT),
        name="combine",
    )(*sched, y_rows, x1, meta, lst_rows, ln2_g, ln2_b)


def _np_consts():
    TL = TILE_ROWS
    r = np.arange(TL)
    same = (r[:, None] // CHUNK) == (r[None, :] // CHUNK)
    tril = (same & (r[None, :] <= r[:, None])).astype(np.float32)
    trils = (r[None, :] < r[:, None]).astype(np.float32)
    e128 = np.zeros((LANES, SSD_WIDTH), np.float32)
    for h in range(SSD_HEADS):
        e128[h, h * SSD_HEAD_DIM:(h + 1) * SSD_HEAD_DIM] = 1.0
    return tril, tril.T.copy(), trils, e128


def kernel(x, w_in, hg_lower_bound, hg_norm_w, conv_w, conv_b, dt_bias, a_log, d_skip, ssd_norm_w, w_out,
           ln1_g, ln1_b, router_w, router_b, w_gate, b_gate, w_up, b_up, w_down, b_down, ln2_g, ln2_b):
    batch, seq, d = x.shape
    assert d == D_MODEL and seq % TILE_ROWS == 0 and w_in.shape[0] == DEPTH
    T = batch * seq
    n_tiles = T // TILE_ROWS
    max_rows = T * TOP_K + n_tiles * N_EXPERTS * (RUN_ALIGN - 1)
    n_blocks = -(-max_rows // EXPERT_BLOCK) + N_EXPERTS
    cap = n_blocks * EXPERT_BLOCK
    assert SORT_ROWS >= TILE_ROWS * TOP_K + N_EXPERTS * (RUN_ALIGN - 1)

    tril, triu, trils, e128 = _np_consts()
    w = w_in[0]
    wdt = w[:, MAIN_COLS:]
    pad_l = LANES - SSD_HEADS
    rw = jnp.pad(router_w[0], ((0, 0), (0, LANES - N_EXPERTS)))
    rwh = rw.astype(BF16)
    rwm = (rw - rwh.astype(F32)).astype(BF16)
    consts = [
        w[:, :MAIN_COLS].astype(BF16),
        jnp.pad(wdt, ((0, 0), (0, pad_l))).astype(BF16),
        wdt.T.astype(BF16),
        hg_lower_bound,
        hg_norm_w[0].reshape(1, HG_HEAD_DIM),
        conv_w[0],
        conv_b[0].reshape(1, SSD_CONV_DIM),
        jnp.pad(dt_bias[0], (0, pad_l)).reshape(1, LANES),
        dt_bias[0].reshape(SSD_HEADS, 1),
        jnp.pad(a_log[0], (0, pad_l)).reshape(1, LANES),
        a_log[0].reshape(SSD_HEADS, 1),
        jnp.repeat(d_skip[0], SSD_HEAD_DIM).reshape(1, SSD_WIDTH),
        ssd_norm_w[0].reshape(1, SSD_WIDTH),
        w_out[0].astype(BF16),
        ln1_g[0].reshape(1, D_MODEL),
        ln1_b[0].reshape(1, D_MODEL),
        jnp.concatenate([rwh, rwm], axis=1),
        rwh,
        jnp.pad(router_b[0], (0, LANES - N_EXPERTS)).reshape(1, LANES),
        jnp.asarray(tril, BF16), jnp.asarray(triu, BF16), jnp.asarray(trils, BF16), jnp.asarray(e128, BF16),
    ]
    x1, meta, cnt = _mixer_call(x.reshape(T, D_MODEL), consts, batch, seq)

    counts = cnt.reshape(n_tiles, SUBLANES, LANES)[:, 0, :N_EXPERTS].astype(jnp.int32)
    c8 = (counts + RUN_ALIGN - 1) // RUN_ALIGN * RUN_ALIGN
    used = jnp.sum(c8, axis=0)
    region = (used + EXPERT_BLOCK - 1) // EXPERT_BLOCK * EXPERT_BLOCK
    region_end = jnp.cumsum(region)
    region_start = region_end - region
    gstart = region_start[None, :] + jnp.cumsum(c8, axis=0) - c8
    lstart = jnp.cumsum(c8, axis=1) - c8
    n_valid = region_end[-1] // EXPERT_BLOCK
    blk_row = jnp.arange(n_blocks, dtype=jnp.int32) * EXPERT_BLOCK
    blk_row = jnp.minimum(blk_row, (n_valid - 1) * EXPERT_BLOCK)
    block_expert = jnp.minimum(jnp.sum(region_end[None, :] <= blk_row[:, None], axis=1),
                               N_EXPERTS - 1).astype(jnp.int32)
    as_units = lambda a: (a // RUN_ALIGN).astype(jnp.int32).reshape(-1)
    units, g8, l8 = as_units(c8), as_units(gstart), as_units(lstart)
    tot = (jnp.sum(c8, axis=1) // RUN_ALIGN).astype(jnp.int32)
    tail8 = as_units(region_start + used)
    tailn = as_units(region - used)
    lst_rows = jnp.repeat(jnp.pad((lstart // RUN_ALIGN).astype(F32), ((0, 0), (0, LANES - N_EXPERTS))),
                          SUBLANES, axis=0)

    nv = n_valid.astype(jnp.int32).reshape(1)
    xs = _dispatch_call((units, g8, l8, tot, tail8, tailn, nv), x1, meta, lst_rows, cap)
    y_rows = _expert_call(block_expert, nv, xs,
                          w_gate[0], b_gate[0], w_up[0], b_up[0], w_down[0], b_down[0])
    out = _combine_call((units, g8, l8, tot), y_rows, x1, meta, lst_rows,
                        ln2_g[0].reshape(1, D_MODEL), ln2_b[0].reshape(1, D_MODEL))
    return out.reshape(batch, seq, D_MODEL)
```

```python
import functools

import jax
import jax.numpy as jnp
import numpy as np
from jax import lax
from jax.experimental import pallas as pl
from jax.experimental.pallas import tpu as pltpu

F32 = jnp.float32
BF16 = jnp.bfloat16

D_MODEL = 1024
CHUNK = 64
HG_WIDTH = 512
HG_HEAD_DIM = 128
HG_HEADS = 4
SSD_WIDTH = 512
SSD_HEAD_DIM = 64
SSD_HEADS = 8
SSD_GROUPS = 2
SSD_STATE = 128
SSD_CONV = 4
SSD_CONV_DIM = SSD_WIDTH + 2 * SSD_GROUPS * SSD_STATE
N_EXPERTS = 32
TOP_K = 4
EXPERT_BLOCK = 256
SWIGLU_LIMIT = 7.0
SWIGLU_ALPHA = 1.702
DEPTH = 1
DEEPNORM_ALPHA = (2 * DEPTH) ** 0.25
LN_EPS = 1e-5
RMS_EPS = 1e-5

LANES = 128
SUBLANES = 8
SUB_CHUNK = 16
EXP_CAP = 60.0
TILE_ROWS = 256
MIX_TILES = 2
STEP_ROWS = MIX_TILES * TILE_ROWS
RUN_ALIGN = SUBLANES
SORT_ROWS = 1280
XS_WIDTH = D_MODEL + LANES
VMEM_LIMIT = 56 * 1024 * 1024

OFF_Q, OFF_F, OFF_I, OFF_G = 0, 512, 1024, 1536
OFF_Z, OFF_XBC = 2048, 2560
OFF_XS, OFF_B, OFF_C = 2560, 3072, 3328
OFF_DT = 3584
PROJ_COLS = OFF_DT + LANES


def _bdot(a, b):
    return jnp.dot(a.astype(BF16), b.astype(BF16), preferred_element_type=F32)


def _bdot_nt(a, b):
    return lax.dot_general(a.astype(BF16), b.astype(BF16), (((1,), (1,)), ((), ())),
                           preferred_element_type=F32)


def _bdot_tn(a, b):
    return lax.dot_general(a.astype(BF16), b.astype(BF16), (((0,), (0,)), ((), ())),
                           preferred_element_type=F32)


def _split3(a):
    hi = a.astype(BF16)
    r1 = a - hi.astype(F32)
    mid = r1.astype(BF16)
    lo = (r1 - mid.astype(F32)).astype(BF16)
    return hi, mid, lo


def _dot01_left(m01, a):
    hi, mid, lo = _split3(a)
    d = functools.partial(jnp.dot, m01, preferred_element_type=F32)
    return d(hi) + d(mid) + d(lo)


def _dot01_right(a, m01):
    hi, mid, lo = _split3(a)
    return (jnp.dot(hi, m01, preferred_element_type=F32) + jnp.dot(mid, m01, preferred_element_type=F32)
            + jnp.dot(lo, m01, preferred_element_type=F32))


def _sigmoid(x):
    return 1.0 / (1.0 + jnp.exp(-x))


def _silu(x):
    return x * _sigmoid(x)


def _softplus(x):
    return jnp.maximum(x, 0.0) + jnp.log(1.0 + jnp.exp(-jnp.abs(x)))


def _mixer_body(x_ref, win_ref, lbp_ref, hgnw_ref, convw_ref, convb_ref, dtb_ref, alog_ref, dskip_ref,
                ssdnw_ref, wout_ref, ln1g_ref, ln1b_ref, rw1_ref, rwh_ref, rb_ref,
                tril_ref, trils_ref, e128_ref,
                x1_ref, meta_ref, cnt_ref,
                proj_s, b_s, xpad_s, xdt_s, cse_s, st_s, pt_s, ohg_s, ossd_s):
    TL = TILE_ROWS
    t = pl.program_id(1)

    @pl.when(t == 0)
    def _():
        xpad_s[0:SUBLANES, :] = jnp.zeros((SUBLANES, SSD_CONV_DIM), F32)
        st_s[...] = jnp.zeros_like(st_s)
        pt_s[...] = jnp.zeros_like(pt_s)

    row64 = lax.broadcasted_iota(jnp.int32, (CHUNK, CHUNK), 0)
    col64 = lax.broadcasted_iota(jnp.int32, (CHUNK, CHUNK), 1)
    causal = row64 >= col64
    gw = SSD_WIDTH // SSD_GROUPS
    hpg = SSD_HEADS // SSD_GROUPS
    lane_head = lax.broadcasted_iota(jnp.int32, (CHUNK, gw), 1) // SSD_HEAD_DIM

    def project(base):
        xb = x_ref[base:base + TL, :].astype(BF16)
        proj_s[base:base + TL, :] = jnp.dot(xb, win_ref[...], preferred_element_type=F32)

    def hgrn2(base):
        a0 = lbp_ref[0:1, :]
        a1 = lbp_ref[1:2, :]
        am = jnp.maximum(a0, a1)
        e0 = jnp.exp(a0 - am)
        e1 = jnp.exp(a1 - am)
        lb = e0 / (e0 + e1)
        f = lb + (1.0 - lb) * _sigmoid(proj_s[base:base + TL, OFF_F:OFF_F + HG_WIDTH])
        b_s[base:base + TL, :] = _dot01_left(tril_ref[...], jnp.log(f))
        proj_s[base:base + TL, OFF_F:OFF_F + HG_WIDTH] = f
        proj_s[base:base + TL, OFF_Q:OFF_Q + HG_WIDTH] = _silu(proj_s[base:base + TL, OFF_Q:OFF_Q + HG_WIDTH])
        hgnw = hgnw_ref[...]
        for c in range(TL // CHUNK):
            r0 = base + c * CHUNK
            for h in range(HG_HEADS):
                h0 = h * HG_HEAD_DIM
                bc = b_s[r0:r0 + CHUNK, h0:h0 + HG_HEAD_DIM]
                qc = proj_s[r0:r0 + CHUNK, OFF_Q + h0:OFF_Q + h0 + HG_HEAD_DIM]
                kc = 1.0 - proj_s[r0:r0 + CHUNK, OFF_F + h0:OFF_F + h0 + HG_HEAD_DIM]
                vc = proj_s[r0:r0 + CHUNK, OFF_I + h0:OFF_I + h0 + HG_HEAD_DIM]
                parts = []
                for i in range(CHUNK // SUB_CHUNK):
                    s0 = i * SUB_CHUNK
                    if i == 0:
                        qi = qc[0:SUB_CHUNK] * jnp.exp(bc[0:SUB_CHUNK])
                        ki = kc * jnp.exp(jnp.minimum(-bc, EXP_CAP))
                    else:
                        ref_i = bc[s0 - 1:s0, :]
                        qi = qc[s0:s0 + SUB_CHUNK] * jnp.exp(bc[s0:s0 + SUB_CHUNK] - ref_i)
                        ki = kc * jnp.exp(jnp.minimum(ref_i - bc, EXP_CAP))
                    parts.append(_bdot_nt(qi, ki))
                sc = jnp.concatenate(parts, axis=0)
                sc = jnp.where(causal, sc, 0.0)
                st = st_s[h]
                o = _bdot(sc, vc) + _bdot_nt(qc * jnp.exp(bc), st)
                b_end = bc[CHUNK - 1:CHUNK, :]
                kdec = kc * jnp.exp(b_end - bc)
                st_s[h] = st * jnp.exp(b_end) + _bdot_tn(vc, kdec)
                ms = jnp.mean(o * o, axis=-1, keepdims=True)
                on = o * lax.rsqrt(ms + RMS_EPS) * hgnw
                gc = proj_s[r0:r0 + CHUNK, OFF_G + h0:OFF_G + h0 + HG_HEAD_DIM]
                ohg_s[r0:r0 + CHUNK, h0:h0 + HG_HEAD_DIM] = on * _silu(gc)

    def ssd(base):
        e128 = e128_ref[...]
        dtc = _softplus(proj_s[base:base + TL, OFF_DT:OFF_DT + LANES] + dtb_ref[...])
        a_row = -jnp.exp(alog_ref[...])
        cs_c = _dot01_left(tril_ref[...], dtc * a_row)
        cs_r = jnp.transpose(cs_c)[0:SSD_HEADS, :]
        cse_s[base:base + TL, :] = _dot01_right(cs_c, e128)
        dt_exp = _dot01_right(dtc, e128)

        xpad_s[SUBLANES + base:SUBLANES + base + TL, :] = proj_s[base:base + TL, OFF_XBC:OFF_XBC + SSD_CONV_DIM]
        acc = jnp.broadcast_to(convb_ref[...], (TL, SSD_CONV_DIM))
        for j in range(SSD_CONV):
            off = base + SUBLANES - (SSD_CONV - 1) + j
            acc = acc + convw_ref[j:j + 1, :] * xpad_s[off:off + TL, :]
        proj_s[base:base + TL, OFF_XBC:OFF_XBC + SSD_CONV_DIM] = _silu(acc)
        xdt_s[base:base + TL, :] = proj_s[base:base + TL, OFF_XS:OFF_XS + SSD_WIDTH] * dt_exp

        for c in range(TL // CHUNK):
            r0 = base + c * CHUNK
            for g in range(SSD_GROUPS):
                g0 = g * gw
                bg = proj_s[r0:r0 + CHUNK, OFF_B + g * SSD_STATE:OFF_B + (g + 1) * SSD_STATE]
                cg = proj_s[r0:r0 + CHUNK, OFF_C + g * SSD_STATE:OFF_C + (g + 1) * SSD_STATE]
                gm = _bdot_nt(cg, bg)
                cse_g = cse_s[r0:r0 + CHUNK, g0:g0 + gw]
                cs_end = cse_s[r0 + CHUNK - 1:r0 + CHUNK, g0:g0 + gw]
                xdt_g = xdt_s[r0:r0 + CHUNK, g0:g0 + gw]
                ydiag = jnp.zeros((CHUNK, gw), F32)
                for hl in range(hpg):
                    hh = g * hpg + hl
                    seg = (cse_g[:, hl * SSD_HEAD_DIM:(hl + 1) * SSD_HEAD_DIM]
                           - cs_r[hh:hh + 1, c * CHUNK:(c + 1) * CHUNK])
                    lm = jnp.where(causal, jnp.exp(jnp.minimum(seg, 0.0)), 0.0)
                    xm = jnp.where(lane_head == hl, xdt_g, 0.0)
                    ydiag = ydiag + _bdot(gm * lm, xm)
                pt = pt_s[g]
                yoff = _bdot(cg, pt) * jnp.exp(cse_g)
                stg = _bdot_tn(bg, xdt_g * jnp.exp(cs_end - cse_g))
                pt_s[g] = pt * jnp.exp(cs_end) + stg
                xs_g = proj_s[r0:r0 + CHUNK, OFF_XS + g0:OFF_XS + g0 + gw]
                ossd_s[r0:r0 + CHUNK, g0:g0 + gw] = ydiag + yoff + xs_g * dskip_ref[:, g0:g0 + gw]

    def post(tile):
        base = tile * TL
        y = ossd_s[base:base + TL, :] * _silu(proj_s[base:base + TL, OFF_Z:OFF_Z + SSD_WIDTH])
        mix = _bdot(ohg_s[base:base + TL, :], wout_ref[0:HG_WIDTH, :])
        for g in range(SSD_GROUPS):
            yg = y[:, g * gw:(g + 1) * gw]
            ms = jnp.mean(yg * yg, axis=-1, keepdims=True)
            yn = yg * lax.rsqrt(ms + RMS_EPS) * ssdnw_ref[:, g * gw:(g + 1) * gw]
            mix = mix + _bdot(yn, wout_ref[HG_WIDTH + g * gw:HG_WIDTH + (g + 1) * gw, :])

        hres = DEEPNORM_ALPHA * x_ref[base:base + TL, :] + mix
        mu = jnp.mean(hres, axis=-1, keepdims=True)
        hc = hres - mu
        var = jnp.mean(hc * hc, axis=-1, keepdims=True)
        x1 = hc * lax.rsqrt(var + LN_EPS) * ln1g_ref[...] + ln1b_ref[...]
        x1_ref[base:base + TL, :] = x1

        xh = x1.astype(BF16)
        xm_ = (x1 - xh.astype(F32)).astype(BF16)
        t1 = jnp.dot(xh, rw1_ref[...], preferred_element_type=F32)
        logits = (t1[:, 0:LANES] + t1[:, LANES:2 * LANES]
                  + jnp.dot(xm_, rwh_ref[...], preferred_element_type=F32) + rb_ref[...])
        lane = lax.broadcasted_iota(jnp.int32, (TL, LANES), 1)
        lane_f = lane.astype(F32)
        neg = jnp.float32(-jnp.inf)
        work = jnp.where(lane < N_EXPERTS, logits, neg)
        onehots, vals, idxs = [], [], []
        for j in range(TOP_K):
            m = jnp.max(work, axis=-1, keepdims=True)
            idx = jnp.min(jnp.where(work == m, lane_f, float(LANES)), axis=-1, keepdims=True)
            oh = lane_f == idx
            onehots.append(oh)
            vals.append(m)
            idxs.append(idx)
            work = jnp.where(oh, neg, work)
        es = [jnp.exp(v - vals[0]) for v in vals]
        den = es[0] + es[1] + es[2] + es[3]
        gates = [e / den for e in es]
        sel = jnp.zeros((TL, LANES), F32)
        for oh in onehots:
            sel = jnp.where(oh, 1.0, sel)
        rankmat = jnp.dot(trils_ref[...], sel.astype(BF16), preferred_element_type=F32)
        cnt_ref[tile * SUBLANES:(tile + 1) * SUBLANES, :] = jnp.broadcast_to(
            jnp.sum(sel, axis=0, keepdims=True), (SUBLANES, LANES))
        meta = jnp.zeros((TL, LANES), F32)
        for j in range(TOP_K):
            rank_j = jnp.sum(jnp.where(onehots[j], rankmat, 0.0), axis=-1, keepdims=True)
            meta = jnp.where(lane == j, idxs[j], meta)
            meta = jnp.where(lane == TOP_K + j, rank_j, meta)
            meta = jnp.where(lane == 2 * TOP_K + j, gates[j], meta)
        meta_ref[base:base + TL, :] = meta

    for tile in range(MIX_TILES):
        project(tile * TL)
    for tile in range(MIX_TILES):
        hgrn2(tile * TL)
        ssd(tile * TL)
        post(tile)
    xpad_s[0:SUBLANES, :] = xpad_s[STEP_ROWS:STEP_ROWS + SUBLANES, :]


def _full(shape):
    nd = len(shape)
    return pl.BlockSpec(shape, lambda *_: (0,) * nd)


def _mixer_call(x2d, consts, batch, seq):
    SR = STEP_ROWS
    nt = seq // SR
    T = batch * seq
    in_specs = [pl.BlockSpec((SR, D_MODEL), lambda b, t: (b * nt + t, 0))]
    in_specs += [_full(c.shape) for c in consts]
    out_shape = (jax.ShapeDtypeStruct((T, D_MODEL), F32),
                 jax.ShapeDtypeStruct((T, LANES), F32),
                 jax.ShapeDtypeStruct((T // TILE_ROWS * SUBLANES, LANES), F32))
    out_specs = (pl.BlockSpec((SR, D_MODEL), lambda b, t: (b * nt + t, 0)),
                 pl.BlockSpec((SR, LANES), lambda b, t: (b * nt + t, 0)),
                 pl.BlockSpec((MIX_TILES * SUBLANES, LANES), lambda b, t: (b * nt + t, 0)))
    scratch = [
        pltpu.VMEM((SR, PROJ_COLS), F32),
        pltpu.VMEM((SR, HG_WIDTH), F32),
        pltpu.VMEM((SR + 2 * SUBLANES, SSD_CONV_DIM), F32),
        pltpu.VMEM((SR, SSD_WIDTH), F32),
        pltpu.VMEM((SR, SSD_WIDTH), F32),
        pltpu.VMEM((HG_HEADS, HG_HEAD_DIM, HG_HEAD_DIM), F32),
        pltpu.VMEM((SSD_GROUPS, SSD_STATE, SSD_WIDTH // SSD_GROUPS), F32),
        pltpu.VMEM((SR, HG_WIDTH), F32),
        pltpu.VMEM((SR, SSD_WIDTH), F32),
    ]
    return pl.pallas_call(
        _mixer_body,
        grid=(batch, nt),
        in_specs=in_specs,
        out_specs=out_specs,
        out_shape=out_shape,
        scratch_shapes=scratch,
        compiler_params=pltpu.CompilerParams(
            dimension_semantics=("arbitrary", "arbitrary"), vmem_limit_bytes=VMEM_LIMIT),
        name="mixer",
    )(x2d, *consts)


def _for_each_run(units_ref, tile, fn):
    def per_expert(e, carry):
        k = tile * N_EXPERTS + e
        n = units_ref[k]

        @pl.when(n > 0)
        def _():
            fn(k, n)
        return carry
    lax.fori_loop(0, N_EXPERTS, per_expert, 0)


def _rows(unit, n_units=1):
    return pl.ds(pl.multiple_of(unit * RUN_ALIGN, RUN_ALIGN), n_units * RUN_ALIGN)


def _dispatch_body(units_ref, g8_ref, l8_ref, tot_ref, tail8_ref, tailn_ref, nv_ref,
                   x1_ref, meta_ref, lst_ref, xs_hbm, sorted_s, zero_s, sems, zsem, bsem):
    tau = pl.program_id(0)
    n = pl.num_programs(0)
    slot = lax.rem(tau, 2)
    TD, S = TILE_ROWS, SORT_ROWS
    n_blocks = xs_hbm.shape[0] // EXPERT_BLOCK

    def unused_block_copy(m):
        rows = pl.ds(pl.multiple_of(m * EXPERT_BLOCK, EXPERT_BLOCK), EXPERT_BLOCK)
        return pltpu.make_async_copy(zero_s, xs_hbm.at[rows, :], bsem)

    def run_copy(s, l_unit, g_unit, n_units):
        return pltpu.make_async_copy(sorted_s.at[s, _rows(l_unit, n_units), :],
                                     xs_hbm.at[_rows(g_unit, n_units), :], sems.at[s])

    def wait_units(s, count):
        run_copy(s, 0, 0, count).wait()

    @pl.when(tau == 0)
    def _():
        zero_s[...] = jnp.zeros_like(zero_s)

        def tail_copy(e, i):
            return pltpu.make_async_copy(zero_s.at[pl.ds(0, RUN_ALIGN), :],
                                         xs_hbm.at[_rows(tail8_ref[e] + i), :], zsem)

        def start_unused(m, carry):
            unused_block_copy(m).start()
            return carry
        lax.fori_loop(nv_ref[0], n_blocks, start_unused, 0)

        def start_e(e, carry):
            def unit(i, c2):
                tail_copy(e, i).start()
                return c2
            lax.fori_loop(0, tailn_ref[e], unit, 0)
            return carry
        lax.fori_loop(0, N_EXPERTS, start_e, 0)

        def wait_e(e, carry):
            def unit(i, c2):
                tail_copy(e, 0).wait()
                return c2
            lax.fori_loop(0, tailn_ref[e], unit, 0)
            return carry
        lax.fori_loop(0, N_EXPERTS, wait_e, 0)

    @pl.when(tau >= 2)
    def _():
        wait_units(slot, tot_ref[jnp.maximum(tau - 2, 0)])

    meta = meta_ref[...]
    lane = lax.broadcasted_iota(jnp.int32, (TD, LANES), 1)
    lane_f = lane.astype(F32)
    u8 = jnp.broadcast_to(lst_ref[0:1, :], (SUBLANES, LANES)).astype(BF16)
    ones8 = jnp.ones((SUBLANES, LANES), BF16)
    r_iota = lax.broadcasted_iota(jnp.int32, (S, TD), 0).astype(F32)
    gh = meta.astype(BF16).astype(F32)
    g1 = meta - gh
    gm = g1.astype(BF16).astype(F32)
    gl = g1 - gm
    nt_dims = (((1,), (1,)), ((), ()))
    conds, gparts = [], []
    for j in range(TOP_K):
        oh = lane_f == meta[:, j:j + 1]
        ohb = jnp.where(oh, 1.0, 0.0).astype(BF16)
        rkb = jnp.where(oh, meta[:, TOP_K + j:TOP_K + j + 1], 0.0).astype(BF16)
        m1 = lax.dot_general(u8, ohb, nt_dims, preferred_element_type=F32)
        m2 = lax.dot_general(ones8, rkb, nt_dims, preferred_element_type=F32)
        lpos = RUN_ALIGN * m1[0:1, :] + m2[0:1, :]
        conds.append(r_iota == lpos)
        gc = 2 * TOP_K + j
        gparts.append(jnp.where(lane == 0, gh[:, gc:gc + 1],
                                jnp.where(lane == 1, gm[:, gc:gc + 1],
                                          jnp.where(lane == 2, gl[:, gc:gc + 1], 0.0))).astype(BF16))
    pcat = jnp.concatenate([jnp.where(c, 1.0, 0.0).astype(BF16) for c in conds], axis=1)
    sorted_g = jnp.dot(pcat, jnp.concatenate(gparts, axis=0), preferred_element_type=F32)
    perm = jnp.where(conds[0], 1.0, jnp.where(conds[1], 1.0, jnp.where(conds[2], 1.0,
                     jnp.where(conds[3], 1.0, 0.0)))).astype(BF16)
    sorted_x = jnp.dot(perm, x1_ref[...].astype(BF16), preferred_element_type=F32)
    sorted_s[slot, :, 0:D_MODEL] = sorted_x
    sorted_s[slot, :, D_MODEL:XS_WIDTH] = sorted_g

    _for_each_run(units_ref, tau, lambda k, cnt: run_copy(slot, l8_ref[k], g8_ref[k], cnt).start())

    @pl.when(tau == n - 1)
    def _():
        @pl.when(tau >= 1)
        def _():
            wait_units(1 - slot, tot_ref[jnp.maximum(tau - 1, 0)])
        wait_units(slot, tot_ref[tau])

        def wait_unused(m, carry):
            unused_block_copy(m).wait()
            return carry
        lax.fori_loop(nv_ref[0], n_blocks, wait_unused, 0)


def _dispatch_call(sched, x1, meta, lst_rows, cap):
    T = x1.shape[0]
    TD = TILE_ROWS
    grid_spec = pltpu.PrefetchScalarGridSpec(
        num_scalar_prefetch=7,
        grid=(T // TD,),
        in_specs=[
            pl.BlockSpec((TD, D_MODEL), lambda i, *_: (i, 0)),
            pl.BlockSpec((TD, LANES), lambda i, *_: (i, 0)),
            pl.BlockSpec((SUBLANES, LANES), lambda i, *_: (i, 0)),
        ],
        out_specs=pl.BlockSpec(memory_space=pl.ANY),
        scratch_shapes=[
            pltpu.VMEM((2, SORT_ROWS, XS_WIDTH), F32),
            pltpu.VMEM((EXPERT_BLOCK, XS_WIDTH), F32),
            pltpu.SemaphoreType.DMA((2,)),
            pltpu.SemaphoreType.DMA(()),
            pltpu.SemaphoreType.DMA(()),
        ],
    )
    return pl.pallas_call(
        _dispatch_body,
        grid_spec=grid_spec,
        out_shape=jax.ShapeDtypeStruct((cap, XS_WIDTH), F32),
        compiler_params=pltpu.CompilerParams(
            dimension_semantics=("arbitrary",), vmem_limit_bytes=VMEM_LIMIT),
        name="dispatch",
    )(*sched, x1, meta, lst_rows)


def _expert_body(be_ref, nv_ref, xs_ref, wg_ref, bg_ref, wu_ref, bu_ref, wd_ref, bd_ref,
                 y_ref, wg_s, wu_s, wd_s):
    m = pl.program_id(0)
    valid = m < nv_ref[0]
    changed = jnp.logical_or(m == 0, be_ref[m] != be_ref[jnp.maximum(m - 1, 0)])

    @pl.when(jnp.logical_and(valid, changed))
    def _():
        wg_s[...] = wg_ref[0].astype(BF16)
        wu_s[...] = wu_ref[0].astype(BF16)
        wd_s[...] = wd_ref[0].astype(BF16)

    @pl.when(valid)
    def _():
        xb = xs_ref[:, 0:D_MODEL].astype(BF16)
        gate = (xs_ref[:, D_MODEL:D_MODEL + 1] + xs_ref[:, D_MODEL + 1:D_MODEL + 2]
                + xs_ref[:, D_MODEL + 2:D_MODEL + 3])
        hg = jnp.minimum(jnp.dot(xb, wg_s[...], preferred_element_type=F32) + bg_ref[0], SWIGLU_LIMIT)
        hu = jnp.clip(jnp.dot(xb, wu_s[...], preferred_element_type=F32) + bu_ref[0],
                      -SWIGLU_LIMIT, SWIGLU_LIMIT)
        hact = (hu + 1.0) * (hg * _sigmoid(SWIGLU_ALPHA * hg))
        y = jnp.dot(hact.astype(BF16), wd_s[...], preferred_element_type=F32) + bd_ref[0]
        y_ref[...] = y * gate

    @pl.when(jnp.logical_not(valid))
    def _():
        y_ref[...] = jnp.zeros_like(y_ref)


def _expert_call(block_expert, n_valid, xs, w_gate, b_gate, w_up, b_up, w_down, b_down):
    n_blocks = xs.shape[0] // EXPERT_BLOCK
    wspec = pl.BlockSpec((1, D_MODEL, D_MODEL), lambda m, be, nv: (be[m], 0, 0))
    bspec = pl.BlockSpec((1, 1, D_MODEL), lambda m, be, nv: (be[m], 0, 0))
    grid_spec = pltpu.PrefetchScalarGridSpec(
        num_scalar_prefetch=2,
        grid=(n_blocks,),
        in_specs=[
            pl.BlockSpec((EXPERT_BLOCK, XS_WIDTH), lambda m, be, nv: (jnp.minimum(m, nv[0] - 1), 0)),
            wspec, bspec, wspec, bspec, wspec, bspec,
        ],
        out_specs=pl.BlockSpec((EXPERT_BLOCK, D_MODEL), lambda m, be, nv: (m, 0)),
        scratch_shapes=[
            pltpu.VMEM((D_MODEL, D_MODEL), BF16),
            pltpu.VMEM((D_MODEL, D_MODEL), BF16),
            pltpu.VMEM((D_MODEL, D_MODEL), BF16),
        ],
    )
    return pl.pallas_call(
        _expert_body,
        grid_spec=grid_spec,
        out_shape=jax.ShapeDtypeStruct((n_blocks * EXPERT_BLOCK, D_MODEL), F32),
        compiler_params=pltpu.CompilerParams(
            dimension_semantics=("arbitrary",), vmem_limit_bytes=VMEM_LIMIT),
        name="experts",
    )(block_expert, n_valid, xs, w_gate, b_gate.reshape(N_EXPERTS, 1, D_MODEL),
      w_up, b_up.reshape(N_EXPERTS, 1, D_MODEL), w_down, b_down.reshape(N_EXPERTS, 1, D_MODEL))


def _combine_body(units_ref, g8_ref, l8_ref, tot_ref,
                  y_hbm, x1_ref, meta_ref, lst_ref, g_ref, b_ref, out_ref, ys_s, sems):
    tau = pl.program_id(0)
    n = pl.num_programs(0)
    slot = lax.rem(tau, 2)
    TD, S = TILE_ROWS, SORT_ROWS

    def run_copy(s, g_unit, l_unit, n_units):
        return pltpu.make_async_copy(y_hbm.at[_rows(g_unit, n_units), :],
                                     ys_s.at[s, _rows(l_unit, n_units), :], sems.at[s])

    def fetch(tile, s):
        _for_each_run(units_ref, tile, lambda k, cnt: run_copy(s, g8_ref[k], l8_ref[k], cnt).start())

    @pl.when(tau == 0)
    def _():
        ys_s[...] = jnp.zeros_like(ys_s)
        fetch(0, 0)

    @pl.when(tau + 1 < n)
    def _():
        fetch(tau + 1, 1 - slot)

    run_copy(slot, 0, 0, tot_ref[tau]).wait()

    meta = meta_ref[...]
    lane_f = lax.broadcasted_iota(jnp.int32, (TD, LANES), 1).astype(F32)
    u_row = lst_ref[0:1, :]
    s_iota = lax.broadcasted_iota(jnp.int32, (TD, S), 1).astype(F32)
    conds = []
    for j in range(TOP_K):
        oh = lane_f == meta[:, j:j + 1]
        start8 = jnp.sum(jnp.where(oh, u_row, 0.0), axis=-1, keepdims=True)
        lpos = RUN_ALIGN * start8 + meta[:, TOP_K + j:TOP_K + j + 1]
        conds.append(s_iota == lpos)
    perm = jnp.where(conds[0], 1.0, jnp.where(conds[1], 1.0, jnp.where(conds[2], 1.0,
                     jnp.where(conds[3], 1.0, 0.0)))).astype(BF16)
    ys = ys_s[slot]
    yh = ys.astype(BF16)
    yl = (ys - yh.astype(F32)).astype(BF16)
    ffn = (jnp.dot(perm, yh, preferred_element_type=F32) + jnp.dot(perm, yl, preferred_element_type=F32))
    acc = DEEPNORM_ALPHA * x1_ref[...] + ffn
    mu = jnp.mean(acc, axis=-1, keepdims=True)
    hc = acc - mu
    var = jnp.mean(hc * hc, axis=-1, keepdims=True)
    out_ref[...] = hc * lax.rsqrt(var + LN_EPS) * g_ref[...] + b_ref[...]


def _combine_call(sched, y_rows, x1, meta, lst_rows, ln2_g, ln2_b):
    T = x1.shape[0]
    TD = TILE_ROWS
    grid_spec = pltpu.PrefetchScalarGridSpec(
        num_scalar_prefetch=4,
        grid=(T // TD,),
        in_specs=[
            pl.BlockSpec(memory_space=pl.ANY),
            pl.BlockSpec((TD, D_MODEL), lambda i, *_: (i, 0)),
            pl.BlockSpec((TD, LANES), lambda i, *_: (i, 0)),
            pl.BlockSpec((SUBLANES, LANES), lambda i, *_: (i, 0)),
            pl.BlockSpec((1, D_MODEL), lambda i, *_: (0, 0)),
            pl.BlockSpec((1, D_MODEL), lambda i, *_: (0, 0)),
        ],
        out_specs=pl.BlockSpec((TD, D_MODEL), lambda i, *_: (i, 0)),
        scratch_shapes=[pltpu.VMEM((2, SORT_ROWS, D_MODEL), F32), pltpu.SemaphoreType.DMA((2,))],
    )
    return pl.pallas_call(
        _combine_body,
        grid_spec=grid_spec,
        out_shape=jax.ShapeDtypeStruct((T, D_MODEL), F32),
        compiler_params=pltpu.CompilerParams(
            dimension_semantics=("arbitrary",), vmem_limit_bytes=VMEM_LIMIT),
        name="combine",
    )(*sched, y_rows, x1, meta, lst_rows, ln2_g, ln2_b)


def _np_consts():
    TL = TILE_ROWS
    r = np.arange(TL)
    same = (r[:, None] // CHUNK) == (r[None, :] // CHUNK)
    tril = (same & (r[None, :] <= r[:, None])).astype(np.float32)
    trils = (r[None, :] < r[:, None]).astype(np.float32)
    e128 = np.zeros((LANES, SSD_WIDTH), np.float32)
    for h in range(SSD_HEADS):
        e128[h, h * SSD_HEAD_DIM:(h + 1) * SSD_HEAD_DIM] = 1.0
    return tril, trils, e128


def kernel(x, w_in, hg_lower_bound, hg_norm_w, conv_w, conv_b, dt_bias, a_log, d_skip, ssd_norm_w, w_out,
           ln1_g, ln1_b, router_w, router_b, w_gate, b_gate, w_up, b_up, w_down, b_down, ln2_g, ln2_b):
    batch, seq, d = x.shape
    assert d == D_MODEL and seq % STEP_ROWS == 0 and w_in.shape[0] == DEPTH
    T = batch * seq
    n_tiles = T // TILE_ROWS
    max_rows = T * TOP_K + n_tiles * N_EXPERTS * (RUN_ALIGN - 1)
    n_blocks = -(-max_rows // EXPERT_BLOCK) + N_EXPERTS
    cap = n_blocks * EXPERT_BLOCK
    assert SORT_ROWS >= TILE_ROWS * TOP_K + N_EXPERTS * (RUN_ALIGN - 1)

    tril, trils, e128 = _np_consts()
    w = w_in[0]
    pad_l = LANES - SSD_HEADS
    rw = jnp.pad(router_w[0], ((0, 0), (0, LANES - N_EXPERTS)))
    rwh = rw.astype(BF16)
    rwm = (rw - rwh.astype(F32)).astype(BF16)
    consts = [
        jnp.pad(w, ((0, 0), (0, pad_l))).astype(BF16),
        hg_lower_bound,
        hg_norm_w[0].reshape(1, HG_HEAD_DIM),
        conv_w[0],
        conv_b[0].reshape(1, SSD_CONV_DIM),
        jnp.pad(dt_bias[0], (0, pad_l)).reshape(1, LANES),
        jnp.pad(a_log[0], (0, pad_l)).reshape(1, LANES),
        jnp.repeat(d_skip[0], SSD_HEAD_DIM).reshape(1, SSD_WIDTH),
        ssd_norm_w[0].reshape(1, SSD_WIDTH),
        w_out[0].astype(BF16),
        ln1_g[0].reshape(1, D_MODEL),
        ln1_b[0].reshape(1, D_MODEL),
        jnp.concatenate([rwh, rwm], axis=1),
        rwh,
        jnp.pad(router_b[0], (0, LANES - N_EXPERTS)).reshape(1, LANES),
        jnp.asarray(tril, BF16), jnp.asarray(trils, BF16), jnp.asarray(e128, BF16),
    ]
    x1, meta, cnt = _mixer_call(x.reshape(T, D_MODEL), consts, batch, seq)

    counts = cnt.reshape(n_tiles, SUBLANES, LANES)[:, 0, :N_EXPERTS].astype(jnp.int32)
    c8 = (counts + RUN_ALIGN - 1) // RUN_ALIGN * RUN_ALIGN
    used = jnp.sum(c8, axis=0)
    region = (used + EXPERT_BLOCK - 1) // EXPERT_BLOCK * EXPERT_BLOCK
    region_end = jnp.cumsum(region)
    region_start = region_end - region
    gstart = region_start[None, :] + jnp.cumsum(c8, axis=0) - c8
    lstart = jnp.cumsum(c8, axis=1) - c8
    n_valid = region_end[-1] // EXPERT_BLOCK
    blk_row = jnp.arange(n_blocks, dtype=jnp.int32) * EXPERT_BLOCK
    blk_row = jnp.minimum(blk_row, (n_valid - 1) * EXPERT_BLOCK)
    block_expert = jnp.minimum(jnp.sum(region_end[None, :] <= blk_row[:, None], axis=1),
                               N_EXPERTS - 1).astype(jnp.int32)
    as_units = lambda a: (a // RUN_ALIGN).astype(jnp.int32).reshape(-1)
    units, g8, l8 = as_units(c8), as_units(gstart), as_units(lstart)
    tot = (jnp.sum(c8, axis=1) // RUN_ALIGN).astype(jnp.int32)
    tail8 = as_units(region_start + used)
    tailn = as_units(region - used)
    lst_rows = jnp.repeat(jnp.pad((lstart // RUN_ALIGN).astype(F32), ((0, 0), (0, LANES - N_EXPERTS))),
                          SUBLANES, axis=0)

    nv = n_valid.astype(jnp.int32).reshape(1)
    xs = _dispatch_call((units, g8, l8, tot, tail8, tailn, nv), x1, meta, lst_rows, cap)
    y_rows = _expert_call(block_expert, nv, xs,
                          w_gate[0], b_gate[0], w_up[0], b_up[0], w_down[0], b_down[0])
    out = _combine_call((units, g8, l8, tot), y_rows, x1, meta, lst_rows,
                        ln2_g[0].reshape(1, D_MODEL), ln2_b[0].reshape(1, D_MODEL))
    return out.reshape(batch, seq, D_MODEL)
```

```python
import functools

import jax
import jax.numpy as jnp
import numpy as np
from jax import lax
from jax.experimental import pallas as pl
from jax.experimental.pallas import tpu as pltpu

F32 = jnp.float32
BF16 = jnp.bfloat16

D_MODEL = 1024
CHUNK = 64
HG_WIDTH = 512
HG_HEAD_DIM = 128
HG_HEADS = 4
SSD_WIDTH = 512
SSD_HEAD_DIM = 64
SSD_HEADS = 8
SSD_GROUPS = 2
SSD_STATE = 128
SSD_CONV = 4
SSD_CONV_DIM = SSD_WIDTH + 2 * SSD_GROUPS * SSD_STATE
N_EXPERTS = 32
TOP_K = 4
EXPERT_BLOCK = 256
SWIGLU_LIMIT = 7.0
SWIGLU_ALPHA = 1.702
DEPTH = 1
DEEPNORM_ALPHA = (2 * DEPTH) ** 0.25
LN_EPS = 1e-5
RMS_EPS = 1e-5

LANES = 128
SUBLANES = 8
SUB_CHUNK = 16
EXP_CAP = 60.0
TILE_ROWS = 256
MIX_TILES = 2
STEP_ROWS = MIX_TILES * TILE_ROWS
RUN_ALIGN = SUBLANES
SORT_ROWS = 1280
XS_WIDTH = D_MODEL + LANES
VMEM_LIMIT = 56 * 1024 * 1024

OFF_Q, OFF_F, OFF_I, OFF_G = 0, 512, 1024, 1536
OFF_Z, OFF_XBC = 2048, 2560
OFF_XS, OFF_B, OFF_C = 2560, 3072, 3328
OFF_DT = 3584
PROJ_COLS = OFF_DT + LANES


def _bdot(a, b):
    return jnp.dot(a.astype(BF16), b.astype(BF16), preferred_element_type=F32)


def _bdot_nt(a, b):
    return lax.dot_general(a.astype(BF16), b.astype(BF16), (((1,), (1,)), ((), ())),
                           preferred_element_type=F32)


def _bdot_tn(a, b):
    return lax.dot_general(a.astype(BF16), b.astype(BF16), (((0,), (0,)), ((), ())),
                           preferred_element_type=F32)


def _split3(a):
    hi = a.astype(BF16)
    r1 = a - hi.astype(F32)
    mid = r1.astype(BF16)
    lo = (r1 - mid.astype(F32)).astype(BF16)
    return hi, mid, lo


def _dot01_left(m01, a):
    hi, mid, lo = _split3(a)
    d = functools.partial(jnp.dot, m01, preferred_element_type=F32)
    return d(hi) + d(mid) + d(lo)


def _dot01_right(a, m01):
    hi, mid, lo = _split3(a)
    return (jnp.dot(hi, m01, preferred_element_type=F32) + jnp.dot(mid, m01, preferred_element_type=F32)
            + jnp.dot(lo, m01, preferred_element_type=F32))


def _sigmoid(x):
    return 1.0 / (1.0 + jnp.exp(-x))


def _silu(x):
    return x * _sigmoid(x)


def _softplus(x):
    return jnp.maximum(x, 0.0) + jnp.log(1.0 + jnp.exp(-jnp.abs(x)))


def _mixer_body(x_ref, win_ref, lbp_ref, hgnw_ref, convw_ref, convb_ref, dtb_ref, alog_ref, dskip_ref,
                ssdnw_ref, wout_ref, ln1g_ref, ln1b_ref, rw1_ref, rwh_ref, rb_ref,
                tril_ref, trils_ref, e128_ref,
                x1_ref, meta_ref, cnt_ref,
                proj_s, b_s, xpad_s, xdt_s, cse_s, st_s, pt_s, ohg_s, ossd_s):
    TL = TILE_ROWS
    t = pl.program_id(1)

    @pl.when(t == 0)
    def _():
        xpad_s[0:SUBLANES, :] = jnp.zeros((SUBLANES, SSD_CONV_DIM), F32)
        st_s[...] = jnp.zeros_like(st_s)
        pt_s[...] = jnp.zeros_like(pt_s)

    row64 = lax.broadcasted_iota(jnp.int32, (CHUNK, CHUNK), 0)
    col64 = lax.broadcasted_iota(jnp.int32, (CHUNK, CHUNK), 1)
    causal = row64 >= col64
    gw = SSD_WIDTH // SSD_GROUPS
    hpg = SSD_HEADS // SSD_GROUPS
    lane_head = lax.broadcasted_iota(jnp.int32, (CHUNK, gw), 1) // SSD_HEAD_DIM

    def project(base):
        xb = x_ref[base:base + TL, :].astype(BF16)
        proj_s[base:base + TL, :] = jnp.dot(xb, win_ref[...], preferred_element_type=F32)

    def hgrn2(base):
        a0 = lbp_ref[0:1, :]
        a1 = lbp_ref[1:2, :]
        am = jnp.maximum(a0, a1)
        e0 = jnp.exp(a0 - am)
        e1 = jnp.exp(a1 - am)
        lb = e0 / (e0 + e1)
        f = lb + (1.0 - lb) * _sigmoid(proj_s[base:base + TL, OFF_F:OFF_F + HG_WIDTH])
        b_s[base:base + TL, :] = _dot01_left(tril_ref[...], jnp.log(f))
        proj_s[base:base + TL, OFF_F:OFF_F + HG_WIDTH] = f
        proj_s[base:base + TL, OFF_Q:OFF_Q + HG_WIDTH] = _silu(proj_s[base:base + TL, OFF_Q:OFF_Q + HG_WIDTH])
        hgnw = hgnw_ref[...]
        for c in range(TL // CHUNK):
            r0 = base + c * CHUNK
            for h in range(HG_HEADS):
                h0 = h * HG_HEAD_DIM
                bc = b_s[r0:r0 + CHUNK, h0:h0 + HG_HEAD_DIM]
                qc = proj_s[r0:r0 + CHUNK, OFF_Q + h0:OFF_Q + h0 + HG_HEAD_DIM]
                kc = 1.0 - proj_s[r0:r0 + CHUNK, OFF_F + h0:OFF_F + h0 + HG_HEAD_DIM]
                vc = proj_s[r0:r0 + CHUNK, OFF_I + h0:OFF_I + h0 + HG_HEAD_DIM]
                parts = []
                for i in range(CHUNK // SUB_CHUNK):
                    s0 = i * SUB_CHUNK
                    if i == 0:
                        qi = qc[0:SUB_CHUNK] * jnp.exp(bc[0:SUB_CHUNK])
                        ki = kc * jnp.exp(jnp.minimum(-bc, EXP_CAP))
                    else:
                        ref_i = bc[s0 - 1:s0, :]
                        qi = qc[s0:s0 + SUB_CHUNK] * jnp.exp(bc[s0:s0 + SUB_CHUNK] - ref_i)
                        ki = kc * jnp.exp(jnp.minimum(ref_i - bc, EXP_CAP))
                    parts.append(_bdot_nt(qi, ki))
                sc = jnp.concatenate(parts, axis=0)
                sc = jnp.where(causal, sc, 0.0)
                st = st_s[h]
                o = _bdot(sc, vc) + _bdot_nt(qc * jnp.exp(bc), st)
                b_end = bc[CHUNK - 1:CHUNK, :]
                kdec = kc * jnp.exp(b_end - bc)
                st_s[h] = st * jnp.exp(b_end) + _bdot_tn(vc, kdec)
                ms = jnp.mean(o * o, axis=-1, keepdims=True)
                on = o * lax.rsqrt(ms + RMS_EPS) * hgnw
                gc = proj_s[r0:r0 + CHUNK, OFF_G + h0:OFF_G + h0 + HG_HEAD_DIM]
                ohg_s[r0:r0 + CHUNK, h0:h0 + HG_HEAD_DIM] = on * _silu(gc)

    def ssd(base):
        e128 = e128_ref[...]
        dtc = _softplus(proj_s[base:base + TL, OFF_DT:OFF_DT + LANES] + dtb_ref[...])
        a_row = -jnp.exp(alog_ref[...])
        cs_c = _dot01_left(tril_ref[...], dtc * a_row)
        cs_r = jnp.transpose(cs_c)[0:SSD_HEADS, :]
        cse_s[base:base + TL, :] = _dot01_right(cs_c, e128)
        dt_exp = _dot01_right(dtc, e128)

        xpad_s[SUBLANES + base:SUBLANES + base + TL, :] = proj_s[base:base + TL, OFF_XBC:OFF_XBC + SSD_CONV_DIM]
        acc = jnp.broadcast_to(convb_ref[...], (TL, SSD_CONV_DIM))
        for j in range(SSD_CONV):
            off = base + SUBLANES - (SSD_CONV - 1) + j
            acc = acc + convw_ref[j:j + 1, :] * xpad_s[off:off + TL, :]
        proj_s[base:base + TL, OFF_XBC:OFF_XBC + SSD_CONV_DIM] = _silu(acc)
        xdt_s[base:base + TL, :] = proj_s[base:base + TL, OFF_XS:OFF_XS + SSD_WIDTH] * dt_exp

        for c in range(TL // CHUNK):
            r0 = base + c * CHUNK
            for g in range(SSD_GROUPS):
                g0 = g * gw
                bg = proj_s[r0:r0 + CHUNK, OFF_B + g * SSD_STATE:OFF_B + (g + 1) * SSD_STATE]
                cg = proj_s[r0:r0 + CHUNK, OFF_C + g * SSD_STATE:OFF_C + (g + 1) * SSD_STATE]
                gm = _bdot_nt(cg, bg)
                cse_g = cse_s[r0:r0 + CHUNK, g0:g0 + gw]
                cs_end = cse_s[r0 + CHUNK - 1:r0 + CHUNK, g0:g0 + gw]
                xdt_g = xdt_s[r0:r0 + CHUNK, g0:g0 + gw]
                ydiag = jnp.zeros((CHUNK, gw), F32)
                for hl in range(hpg):
                    hh = g * hpg + hl
                    seg = (cse_g[:, hl * SSD_HEAD_DIM:(hl + 1) * SSD_HEAD_DIM]
                           - cs_r[hh:hh + 1, c * CHUNK:(c + 1) * CHUNK])
                    lm = jnp.where(causal, jnp.exp(jnp.minimum(seg, 0.0)), 0.0)
                    xm = jnp.where(lane_head == hl, xdt_g, 0.0)
                    ydiag = ydiag + _bdot(gm * lm, xm)
                pt = pt_s[g]
                yoff = _bdot(cg, pt) * jnp.exp(cse_g)
                stg = _bdot_tn(bg, xdt_g * jnp.exp(cs_end - cse_g))
                pt_s[g] = pt * jnp.exp(cs_end) + stg
                xs_g = proj_s[r0:r0 + CHUNK, OFF_XS + g0:OFF_XS + g0 + gw]
                ossd_s[r0:r0 + CHUNK, g0:g0 + gw] = ydiag + yoff + xs_g * dskip_ref[:, g0:g0 + gw]

    def post(tile):
        base = tile * TL
        y = ossd_s[base:base + TL, :] * _silu(proj_s[base:base + TL, OFF_Z:OFF_Z + SSD_WIDTH])
        mix = _bdot(ohg_s[base:base + TL, :], wout_ref[0:HG_WIDTH, :])
        for g in range(SSD_GROUPS):
            yg = y[:, g * gw:(g + 1) * gw]
            ms = jnp.mean(yg * yg, axis=-1, keepdims=True)
            yn = yg * lax.rsqrt(ms + RMS_EPS) * ssdnw_ref[:, g * gw:(g + 1) * gw]
            mix = mix + _bdot(yn, wout_ref[HG_WIDTH + g * gw:HG_WIDTH + (g + 1) * gw, :])

        hres = DEEPNORM_ALPHA * x_ref[base:base + TL, :] + mix
        mu = jnp.mean(hres, axis=-1, keepdims=True)
        hc = hres - mu
        var = jnp.mean(hc * hc, axis=-1, keepdims=True)
        x1 = hc * lax.rsqrt(var + LN_EPS) * ln1g_ref[...] + ln1b_ref[...]
        x1_ref[base:base + TL, :] = x1

        xh = x1.astype(BF16)
        xm_ = (x1 - xh.astype(F32)).astype(BF16)
        t1 = jnp.dot(xh, rw1_ref[...], preferred_element_type=F32)
        logits = (t1[:, 0:LANES] + t1[:, LANES:2 * LANES]
                  + jnp.dot(xm_, rwh_ref[...], preferred_element_type=F32) + rb_ref[...])
        lane = lax.broadcasted_iota(jnp.int32, (TL, LANES), 1)
        lane_f = lane.astype(F32)
        neg = jnp.float32(-jnp.inf)
        work = jnp.where(lane < N_EXPERTS, logits, neg)
        onehots, vals, idxs = [], [], []
        for j in range(TOP_K):
            m = jnp.max(work, axis=-1, keepdims=True)
            idx = jnp.min(jnp.where(work == m, lane_f, float(LANES)), axis=-1, keepdims=True)
            oh = lane_f == idx
            onehots.append(oh)
            vals.append(m)
            idxs.append(idx)
            work = jnp.where(oh, neg, work)
        es = [jnp.exp(v - vals[0]) for v in vals]
        den = es[0] + es[1] + es[2] + es[3]
        gates = [e / den for e in es]
        sel = jnp.zeros((TL, LANES), F32)
        for oh in onehots:
            sel = jnp.where(oh, 1.0, sel)
        rankmat = jnp.dot(trils_ref[...], sel.astype(BF16), preferred_element_type=F32)
        cnt_ref[tile * SUBLANES:(tile + 1) * SUBLANES, :] = jnp.broadcast_to(
            jnp.sum(sel, axis=0, keepdims=True), (SUBLANES, LANES))
        meta = jnp.zeros((TL, LANES), F32)
        for j in range(TOP_K):
            rank_j = jnp.sum(jnp.where(onehots[j], rankmat, 0.0), axis=-1, keepdims=True)
            meta = jnp.where(lane == j, idxs[j], meta)
            meta = jnp.where(lane == TOP_K + j, rank_j, meta)
            meta = jnp.where(lane == 2 * TOP_K + j, gates[j], meta)
        meta_ref[base:base + TL, :] = meta

    for tile in range(MIX_TILES):
        project(tile * TL)
    for tile in range(MIX_TILES):
        hgrn2(tile * TL)
        ssd(tile * TL)
        post(tile)
    xpad_s[0:SUBLANES, :] = xpad_s[STEP_ROWS:STEP_ROWS + SUBLANES, :]


def _full(shape):
    nd = len(shape)
    return pl.BlockSpec(shape, lambda *_: (0,) * nd)


def _mixer_call(x2d, consts, batch, seq):
    SR = STEP_ROWS
    nt = seq // SR
    T = batch * seq
    in_specs = [pl.BlockSpec((SR, D_MODEL), lambda b, t: (b * nt + t, 0))]
    in_specs += [_full(c.shape) for c in consts]
    out_shape = (jax.ShapeDtypeStruct((T, D_MODEL), F32),
                 jax.ShapeDtypeStruct((T, LANES), F32),
                 jax.ShapeDtypeStruct((T // TILE_ROWS * SUBLANES, LANES), F32))
    out_specs = (pl.BlockSpec((SR, D_MODEL), lambda b, t: (b * nt + t, 0)),
                 pl.BlockSpec((SR, LANES), lambda b, t: (b * nt + t, 0)),
                 pl.BlockSpec((MIX_TILES * SUBLANES, LANES), lambda b, t: (b * nt + t, 0)))
    scratch = [
        pltpu.VMEM((SR, PROJ_COLS), F32),
        pltpu.VMEM((SR, HG_WIDTH), F32),
        pltpu.VMEM((SR + 2 * SUBLANES, SSD_CONV_DIM), F32),
        pltpu.VMEM((SR, SSD_WIDTH), F32),
        pltpu.VMEM((SR, SSD_WIDTH), F32),
        pltpu.VMEM((HG_HEADS, HG_HEAD_DIM, HG_HEAD_DIM), F32),
        pltpu.VMEM((SSD_GROUPS, SSD_STATE, SSD_WIDTH // SSD_GROUPS), F32),
        pltpu.VMEM((SR, HG_WIDTH), F32),
        pltpu.VMEM((SR, SSD_WIDTH), F32),
    ]
    return pl.pallas_call(
        _mixer_body,
        grid=(batch, nt),
        in_specs=in_specs,
        out_specs=out_specs,
        out_shape=out_shape,
        scratch_shapes=scratch,
        compiler_params=pltpu.CompilerParams(
            dimension_semantics=("arbitrary", "arbitrary"), vmem_limit_bytes=VMEM_LIMIT),
        name="mixer",
    )(x2d, *consts)


def _for_each_run(units_ref, tile, fn):
    def per_expert(e, carry):
        k = tile * N_EXPERTS + e
        n = units_ref[k]

        @pl.when(n > 0)
        def _():
            fn(k, n)
        return carry
    lax.fori_loop(0, N_EXPERTS, per_expert, 0)


def _rows(unit, n_units=1):
    return pl.ds(pl.multiple_of(unit * RUN_ALIGN, RUN_ALIGN), n_units * RUN_ALIGN)


def _dispatch_body(units_ref, g8_ref, l8_ref, tot_ref, tail8_ref, tailn_ref, nv_ref,
                   x1_ref, meta_ref, lst_ref, xs_hbm, sorted_s, zero_s, sems, zsem, bsem):
    tau = pl.program_id(0)
    n = pl.num_programs(0)
    slot = lax.rem(tau, 2)
    TD, S = TILE_ROWS, SORT_ROWS
    n_blocks = xs_hbm.shape[0] // EXPERT_BLOCK

    def unused_block_copy(m):
        rows = pl.ds(pl.multiple_of(m * EXPERT_BLOCK, EXPERT_BLOCK), EXPERT_BLOCK)
        return pltpu.make_async_copy(zero_s, xs_hbm.at[rows, :], bsem)

    def run_copy(s, l_unit, g_unit, n_units):
        return pltpu.make_async_copy(sorted_s.at[s, _rows(l_unit, n_units), :],
                                     xs_hbm.at[_rows(g_unit, n_units), :], sems.at[s])

    def wait_units(s, count):
        run_copy(s, 0, 0, count).wait()

    @pl.when(tau == 0)
    def _():
        zero_s[...] = jnp.zeros_like(zero_s)

        def tail_copy(e, i):
            return pltpu.make_async_copy(zero_s.at[pl.ds(0, RUN_ALIGN), :],
                                         xs_hbm.at[_rows(tail8_ref[e] + i), :], zsem)

        def start_unused(m, carry):
            unused_block_copy(m).start()
            return carry
        lax.fori_loop(nv_ref[0], n_blocks, start_unused, 0)

        def start_e(e, carry):
            def unit(i, c2):
                tail_copy(e, i).start()
                return c2
            lax.fori_loop(0, tailn_ref[e], unit, 0)
            return carry
        lax.fori_loop(0, N_EXPERTS, start_e, 0)

        def wait_e(e, carry):
            def unit(i, c2):
                tail_copy(e, 0).wait()
                return c2
            lax.fori_loop(0, tailn_ref[e], unit, 0)
            return carry
        lax.fori_loop(0, N_EXPERTS, wait_e, 0)

    @pl.when(tau >= 2)
    def _():
        wait_units(slot, tot_ref[jnp.maximum(tau - 2, 0)])

    meta = meta_ref[...]
    lane = lax.broadcasted_iota(jnp.int32, (TD, LANES), 1)
    lane_f = lane.astype(F32)
    u8 = jnp.broadcast_to(lst_ref[0:1, :], (SUBLANES, LANES)).astype(BF16)
    ones8 = jnp.ones((SUBLANES, LANES), BF16)
    r_iota = lax.broadcasted_iota(jnp.int32, (S, TD), 0).astype(F32)
    gh = meta.astype(BF16).astype(F32)
    g1 = meta - gh
    gm = g1.astype(BF16).astype(F32)
    gl = g1 - gm
    nt_dims = (((1,), (1,)), ((), ()))
    conds, gparts = [], []
    for j in range(TOP_K):
        oh = lane_f == meta[:, j:j + 1]
        ohb = jnp.where(oh, 1.0, 0.0).astype(BF16)
        rkb = jnp.where(oh, meta[:, TOP_K + j:TOP_K + j + 1], 0.0).astype(BF16)
        m1 = lax.dot_general(u8, ohb, nt_dims, preferred_element_type=F32)
        m2 = lax.dot_general(ones8, rkb, nt_dims, preferred_element_type=F32)
        lpos = RUN_ALIGN * m1[0:1, :] + m2[0:1, :]
        conds.append(r_iota == lpos)
        gc = 2 * TOP_K + j
        gparts.append(jnp.where(lane == 0, gh[:, gc:gc + 1],
                                jnp.where(lane == 1, gm[:, gc:gc + 1],
                                          jnp.where(lane == 2, gl[:, gc:gc + 1], 0.0))).astype(BF16))
    pcat = jnp.concatenate([jnp.where(c, 1.0, 0.0).astype(BF16) for c in conds], axis=1)
    sorted_g = jnp.dot(pcat, jnp.concatenate(gparts, axis=0), preferred_element_type=F32)
    perm = jnp.where(conds[0], 1.0, jnp.where(conds[1], 1.0, jnp.where(conds[2], 1.0,
                     jnp.where(conds[3], 1.0, 0.0)))).astype(BF16)
    sorted_x = jnp.dot(perm, x1_ref[...].astype(BF16), preferred_element_type=F32)
    sorted_s[slot, :, 0:D_MODEL] = sorted_x
    sorted_s[slot, :, D_MODEL:XS_WIDTH] = sorted_g

    _for_each_run(units_ref, tau, lambda k, cnt: run_copy(slot, l8_ref[k], g8_ref[k], cnt).start())

    @pl.when(tau == n - 1)
    def _():
        @pl.when(tau >= 1)
        def _():
            wait_units(1 - slot, tot_ref[jnp.maximum(tau - 1, 0)])
        wait_units(slot, tot_ref[tau])

        def wait_unused(m, carry):
            unused_block_copy(m).wait()
            return carry
        lax.fori_loop(nv_ref[0], n_blocks, wait_unused, 0)


def _dispatch_call(sched, x1, meta, lst_rows, cap):
    T = x1.shape[0]
    TD = TILE_ROWS
    grid_spec = pltpu.PrefetchScalarGridSpec(
        num_scalar_prefetch=7,
        grid=(T // TD,),
        in_specs=[
            pl.BlockSpec((TD, D_MODEL), lambda i, *_: (i, 0)),
            pl.BlockSpec((TD, LANES), lambda i, *_: (i, 0)),
            pl.BlockSpec((SUBLANES, LANES), lambda i, *_: (i, 0)),
        ],
        out_specs=pl.BlockSpec(memory_space=pl.ANY),
        scratch_shapes=[
            pltpu.VMEM((2, SORT_ROWS, XS_WIDTH), F32),
            pltpu.VMEM((EXPERT_BLOCK, XS_WIDTH), F32),
            pltpu.SemaphoreType.DMA((2,)),
            pltpu.SemaphoreType.DMA(()),
            pltpu.SemaphoreType.DMA(()),
        ],
    )
    return pl.pallas_call(
        _dispatch_body,
        grid_spec=grid_spec,
        out_shape=jax.ShapeDtypeStruct((cap, XS_WIDTH), F32),
        compiler_params=pltpu.CompilerParams(
            dimension_semantics=("arbitrary",), vmem_limit_bytes=VMEM_LIMIT),
        name="dispatch",
    )(*sched, x1, meta, lst_rows)


def _expert_body(be_ref, nv_ref, nxt_ref, par_ref, xs_ref, wg_hbm, bg_ref, wu_hbm, bu_ref, wd_hbm, bd_ref,
                 y_ref, wbuf, wsem, wg_s, wu_s, wd_s):
    m = pl.program_id(0)
    valid = m < nv_ref[0]
    changed = jnp.logical_or(m == 0, be_ref[m] != be_ref[jnp.maximum(m - 1, 0)])

    def weight_copies(e, s):
        return [pltpu.make_async_copy(w.at[e], wbuf.at[s, k], wsem.at[s])
                for k, w in enumerate((wg_hbm, wu_hbm, wd_hbm))]

    @pl.when(m == 0)
    def _():
        for c in weight_copies(be_ref[0], par_ref[0]):
            c.start()

    @pl.when(jnp.logical_and(valid, changed))
    def _():
        s = par_ref[m]
        for c in weight_copies(be_ref[m], s):
            c.wait()

        @pl.when(nxt_ref[m] != be_ref[m])
        def _():
            for c in weight_copies(nxt_ref[m], 1 - s):
                c.start()
        wg_s[...] = wbuf[s, 0].astype(BF16)
        wu_s[...] = wbuf[s, 1].astype(BF16)
        wd_s[...] = wbuf[s, 2].astype(BF16)

    @pl.when(valid)
    def _():
        xb = xs_ref[:, 0:D_MODEL].astype(BF16)
        gate = (xs_ref[:, D_MODEL:D_MODEL + 1] + xs_ref[:, D_MODEL + 1:D_MODEL + 2]
                + xs_ref[:, D_MODEL + 2:D_MODEL + 3])
        hg = jnp.minimum(jnp.dot(xb, wg_s[...], preferred_element_type=F32) + bg_ref[0], SWIGLU_LIMIT)
        hu = jnp.clip(jnp.dot(xb, wu_s[...], preferred_element_type=F32) + bu_ref[0],
                      -SWIGLU_LIMIT, SWIGLU_LIMIT)
        hact = (hu + 1.0) * (hg * _sigmoid(SWIGLU_ALPHA * hg))
        y = jnp.dot(hact.astype(BF16), wd_s[...], preferred_element_type=F32) + bd_ref[0]
        y_ref[...] = y * gate

    @pl.when(jnp.logical_not(valid))
    def _():
        y_ref[...] = jnp.zeros_like(y_ref)


def _expert_call(block_expert, n_valid, next_expert, parity, xs, w_gate, b_gate, w_up, b_up, w_down, b_down):
    n_blocks = xs.shape[0] // EXPERT_BLOCK
    wspec = pl.BlockSpec(memory_space=pl.ANY)
    bspec = pl.BlockSpec((1, 1, D_MODEL), lambda m, be, *_: (be[m], 0, 0))
    grid_spec = pltpu.PrefetchScalarGridSpec(
        num_scalar_prefetch=4,
        grid=(n_blocks,),
        in_specs=[
            pl.BlockSpec((EXPERT_BLOCK, XS_WIDTH), lambda m, be, nv, *_: (jnp.minimum(m, nv[0] - 1), 0)),
            wspec, bspec, wspec, bspec, wspec, bspec,
        ],
        out_specs=pl.BlockSpec((EXPERT_BLOCK, D_MODEL), lambda m, *_: (m, 0)),
        scratch_shapes=[
            pltpu.VMEM((2, 3, D_MODEL, D_MODEL), F32),
            pltpu.SemaphoreType.DMA((2,)),
            pltpu.VMEM((D_MODEL, D_MODEL), BF16),
            pltpu.VMEM((D_MODEL, D_MODEL), BF16),
            pltpu.VMEM((D_MODEL, D_MODEL), BF16),
        ],
    )
    return pl.pallas_call(
        _expert_body,
        grid_spec=grid_spec,
        out_shape=jax.ShapeDtypeStruct((n_blocks * EXPERT_BLOCK, D_MODEL), F32),
        compiler_params=pltpu.CompilerParams(
            dimension_semantics=("arbitrary",), vmem_limit_bytes=VMEM_LIMIT),
        name="experts",
    )(block_expert, n_valid, next_expert, parity, xs, w_gate, b_gate.reshape(N_EXPERTS, 1, D_MODEL),
      w_up, b_up.reshape(N_EXPERTS, 1, D_MODEL), w_down, b_down.reshape(N_EXPERTS, 1, D_MODEL))


def _combine_body(units_ref, g8_ref, l8_ref, tot_ref,
                  y_hbm, x1_ref, meta_ref, lst_ref, g_ref, b_ref, out_ref, ys_s, sems):
    tau = pl.program_id(0)
    n = pl.num_programs(0)
    slot = lax.rem(tau, 2)
    TD, S = TILE_ROWS, SORT_ROWS

    def run_copy(s, g_unit, l_unit, n_units):
        return pltpu.make_async_copy(y_hbm.at[_rows(g_unit, n_units), :],
                                     ys_s.at[s, _rows(l_unit, n_units), :], sems.at[s])

    def fetch(tile, s):
        _for_each_run(units_ref, tile, lambda k, cnt: run_copy(s, g8_ref[k], l8_ref[k], cnt).start())

    @pl.when(tau == 0)
    def _():
        ys_s[...] = jnp.zeros_like(ys_s)
        fetch(0, 0)

    @pl.when(tau + 1 < n)
    def _():
        fetch(tau + 1, 1 - slot)

    run_copy(slot, 0, 0, tot_ref[tau]).wait()

    meta = meta_ref[...]
    lane_f = lax.broadcasted_iota(jnp.int32, (TD, LANES), 1).astype(F32)
    u_row = lst_ref[0:1, :]
    s_iota = lax.broadcasted_iota(jnp.int32, (TD, S), 1).astype(F32)
    conds = []
    for j in range(TOP_K):
        oh = lane_f == meta[:, j:j + 1]
        start8 = jnp.sum(jnp.where(oh, u_row, 0.0), axis=-1, keepdims=True)
        lpos = RUN_ALIGN * start8 + meta[:, TOP_K + j:TOP_K + j + 1]
        conds.append(s_iota == lpos)
    perm = jnp.where(conds[0], 1.0, jnp.where(conds[1], 1.0, jnp.where(conds[2], 1.0,
                     jnp.where(conds[3], 1.0, 0.0)))).astype(BF16)
    ys = ys_s[slot]
    yh = ys.astype(BF16)
    yl = (ys - yh.astype(F32)).astype(BF16)
    ffn = (jnp.dot(perm, yh, preferred_element_type=F32) + jnp.dot(perm, yl, preferred_element_type=F32))
    acc = DEEPNORM_ALPHA * x1_ref[...] + ffn
    mu = jnp.mean(acc, axis=-1, keepdims=True)
    hc = acc - mu
    var = jnp.mean(hc * hc, axis=-1, keepdims=True)
    out_ref[...] = hc * lax.rsqrt(var + LN_EPS) * g_ref[...] + b_ref[...]


def _combine_call(sched, y_rows, x1, meta, lst_rows, ln2_g, ln2_b):
    T = x1.shape[0]
    TD = TILE_ROWS
    grid_spec = pltpu.PrefetchScalarGridSpec(
        num_scalar_prefetch=4,
        grid=(T // TD,),
        in_specs=[
            pl.BlockSpec(memory_space=pl.ANY),
            pl.BlockSpec((TD, D_MODEL), lambda i, *_: (i, 0)),
            pl.BlockSpec((TD, LANES), lambda i, *_: (i, 0)),
            pl.BlockSpec((SUBLANES, LANES), lambda i, *_: (i, 0)),
            pl.BlockSpec((1, D_MODEL), lambda i, *_: (0, 0)),
            pl.BlockSpec((1, D_MODEL), lambda i, *_: (0, 0)),
        ],
        out_specs=pl.BlockSpec((TD, D_MODEL), lambda i, *_: (i, 0)),
        scratch_shapes=[pltpu.VMEM((2, SORT_ROWS, D_MODEL), F32), pltpu.SemaphoreType.DMA((2,))],
    )
    return pl.pallas_call(
        _combine_body,
        grid_spec=grid_spec,
        out_shape=jax.ShapeDtypeStruct((T, D_MODEL), F32),
        compiler_params=pltpu.CompilerParams(
            dimension_semantics=("arbitrary",), vmem_limit_bytes=VMEM_LIMIT),
        name="combine",
    )(*sched, y_rows, x1, meta, lst_rows, ln2_g, ln2_b)


def _np_consts():
    TL = TILE_ROWS
    r = np.arange(TL)
    same = (r[:, None] // CHUNK) == (r[None, :] // CHUNK)
    tril = (same & (r[None, :] <= r[:, None])).astype(np.float32)
    trils = (r[None, :] < r[:, None]).astype(np.float32)
    e128 = np.zeros((LANES, SSD_WIDTH), np.float32)
    for h in range(SSD_HEADS):
        e128[h, h * SSD_HEAD_DIM:(h + 1) * SSD_HEAD_DIM] = 1.0
    return tril, trils, e128


def kernel(x, w_in, hg_lower_bound, hg_norm_w, conv_w, conv_b, dt_bias, a_log, d_skip, ssd_norm_w, w_out,
           ln1_g, ln1_b, router_w, router_b, w_gate, b_gate, w_up, b_up, w_down, b_down, ln2_g, ln2_b):
    batch, seq, d = x.shape
    assert d == D_MODEL and seq % STEP_ROWS == 0 and w_in.shape[0] == DEPTH
    T = batch * seq
    n_tiles = T // TILE_ROWS
    max_rows = T * TOP_K + n_tiles * N_EXPERTS * (RUN_ALIGN - 1)
    n_blocks = -(-max_rows // EXPERT_BLOCK) + N_EXPERTS
    cap = n_blocks * EXPERT_BLOCK
    assert SORT_ROWS >= TILE_ROWS * TOP_K + N_EXPERTS * (RUN_ALIGN - 1)

    tril, trils, e128 = _np_consts()
    w = w_in[0]
    pad_l = LANES - SSD_HEADS
    rw = jnp.pad(router_w[0], ((0, 0), (0, LANES - N_EXPERTS)))
    rwh = rw.astype(BF16)
    rwm = (rw - rwh.astype(F32)).astype(BF16)
    consts = [
        jnp.pad(w, ((0, 0), (0, pad_l))).astype(BF16),
        hg_lower_bound,
        hg_norm_w[0].reshape(1, HG_HEAD_DIM),
        conv_w[0],
        conv_b[0].reshape(1, SSD_CONV_DIM),
        jnp.pad(dt_bias[0], (0, pad_l)).reshape(1, LANES),
        jnp.pad(a_log[0], (0, pad_l)).reshape(1, LANES),
        jnp.repeat(d_skip[0], SSD_HEAD_DIM).reshape(1, SSD_WIDTH),
        ssd_norm_w[0].reshape(1, SSD_WIDTH),
        w_out[0].astype(BF16),
        ln1_g[0].reshape(1, D_MODEL),
        ln1_b[0].reshape(1, D_MODEL),
        jnp.concatenate([rwh, rwm], axis=1),
        rwh,
        jnp.pad(router_b[0], (0, LANES - N_EXPERTS)).reshape(1, LANES),
        jnp.asarray(tril, BF16), jnp.asarray(trils, BF16), jnp.asarray(e128, BF16),
    ]
    x1, meta, cnt = _mixer_call(x.reshape(T, D_MODEL), consts, batch, seq)

    counts = cnt.reshape(n_tiles, SUBLANES, LANES)[:, 0, :N_EXPERTS].astype(jnp.int32)
    c8 = (counts + RUN_ALIGN - 1) // RUN_ALIGN * RUN_ALIGN
    used = jnp.sum(c8, axis=0)
    region = (used + EXPERT_BLOCK - 1) // EXPERT_BLOCK * EXPERT_BLOCK
    region_end = jnp.cumsum(region)
    region_start = region_end - region
    gstart = region_start[None, :] + jnp.cumsum(c8, axis=0) - c8
    lstart = jnp.cumsum(c8, axis=1) - c8
    n_valid = region_end[-1] // EXPERT_BLOCK
    blk_row = jnp.arange(n_blocks, dtype=jnp.int32) * EXPERT_BLOCK
    blk_row = jnp.minimum(blk_row, (n_valid - 1) * EXPERT_BLOCK)
    block_expert = jnp.minimum(jnp.sum(region_end[None, :] <= blk_row[:, None], axis=1),
                               N_EXPERTS - 1).astype(jnp.int32)
    has = region > 0
    eidx = jnp.arange(N_EXPERTS, dtype=jnp.int32)
    suffix_min = lax.cummin(jnp.where(has, eidx, N_EXPERTS), reverse=True)
    nxt_e = jnp.concatenate([suffix_min[1:], jnp.full((1,), N_EXPERTS, jnp.int32)])
    nxt_e = jnp.where(nxt_e == N_EXPERTS, eidx, nxt_e)
    par_e = (jnp.cumsum(has.astype(jnp.int32)) - 1) % 2
    next_expert = nxt_e[block_expert].astype(jnp.int32)
    parity = par_e[block_expert].astype(jnp.int32)
    as_units = lambda a: (a // RUN_ALIGN).astype(jnp.int32).reshape(-1)
    units, g8, l8 = as_units(c8), as_units(gstart), as_units(lstart)
    tot = (jnp.sum(c8, axis=1) // RUN_ALIGN).astype(jnp.int32)
    tail8 = as_units(region_start + used)
    tailn = as_units(region - used)
    lst_rows = jnp.repeat(jnp.pad((lstart // RUN_ALIGN).astype(F32), ((0, 0), (0, LANES - N_EXPERTS))),
                          SUBLANES, axis=0)

    nv = n_valid.astype(jnp.int32).reshape(1)
    xs = _dispatch_call((units, g8, l8, tot, tail8, tailn, nv), x1, meta, lst_rows, cap)
    y_rows = _expert_call(block_expert, nv, next_expert, parity, xs,
                          w_gate[0], b_gate[0], w_up[0], b_up[0], w_down[0], b_down[0])
    out = _combine_call((units, g8, l8, tot), y_rows, x1, meta, lst_rows,
                        ln2_g[0].reshape(1, D_MODEL), ln2_b[0].reshape(1, D_MODEL))
    return out.reshape(batch, seq, D_MODEL)
```

```python
import functools

import jax
import jax.numpy as jnp
import numpy as np
from jax import lax
from jax.experimental import pallas as pl
from jax.experimental.pallas import tpu as pltpu

F32 = jnp.float32
BF16 = jnp.bfloat16

D_MODEL = 1024
CHUNK = 64
HG_WIDTH = 512
HG_HEAD_DIM = 128
HG_HEADS = 4
SSD_WIDTH = 512
SSD_HEAD_DIM = 64
SSD_HEADS = 8
SSD_GROUPS = 2
SSD_STATE = 128
SSD_CONV = 4
SSD_CONV_DIM = SSD_WIDTH + 2 * SSD_GROUPS * SSD_STATE
N_EXPERTS = 32
TOP_K = 4
EXPERT_BLOCK = 256
SWIGLU_LIMIT = 7.0
SWIGLU_ALPHA = 1.702
DEPTH = 1
DEEPNORM_ALPHA = (2 * DEPTH) ** 0.25
LN_EPS = 1e-5
RMS_EPS = 1e-5

LANES = 128
SUBLANES = 8
SUB_CHUNK = 16
EXP_CAP = 60.0
TILE_ROWS = 256
MIX_TILES = 2
STEP_ROWS = MIX_TILES * TILE_ROWS
RUN_ALIGN = SUBLANES
SORT_ROWS = 1280
XS_WIDTH = D_MODEL + LANES
VMEM_LIMIT = 56 * 1024 * 1024

OFF_Q, OFF_F, OFF_I, OFF_G = 0, 512, 1024, 1536
OFF_Z, OFF_XBC = 2048, 2560
OFF_XS, OFF_B, OFF_C = 2560, 3072, 3328
OFF_DT = 3584
PROJ_COLS = OFF_DT + LANES
PROJ_SLICE = 256
CONV_SLICE = 256
OUT_SLICE = 256


def _bdot(a, b):
    return jnp.dot(a.astype(BF16), b.astype(BF16), preferred_element_type=F32)


def _bdot_nt(a, b):
    return lax.dot_general(a.astype(BF16), b.astype(BF16), (((1,), (1,)), ((), ())),
                           preferred_element_type=F32)


def _bdot_tn(a, b):
    return lax.dot_general(a.astype(BF16), b.astype(BF16), (((0,), (0,)), ((), ())),
                           preferred_element_type=F32)


def _split3(a):
    hi = a.astype(BF16)
    r1 = a - hi.astype(F32)
    mid = r1.astype(BF16)
    lo = (r1 - mid.astype(F32)).astype(BF16)
    return hi, mid, lo


def _dot01_left(m01, a):
    hi, mid, lo = _split3(a)
    d = functools.partial(jnp.dot, m01, preferred_element_type=F32)
    return d(hi) + d(mid) + d(lo)


def _dot01_right(a, m01):
    hi, mid, lo = _split3(a)
    return (jnp.dot(hi, m01, preferred_element_type=F32) + jnp.dot(mid, m01, preferred_element_type=F32)
            + jnp.dot(lo, m01, preferred_element_type=F32))


def _sigmoid(x):
    return 1.0 / (1.0 + jnp.exp(-x))


def _silu(x):
    return x * _sigmoid(x)


def _softplus(x):
    return jnp.maximum(x, 0.0) + jnp.log(1.0 + jnp.exp(-jnp.abs(x)))


def _mixer_body(x_ref, win_ref, lbp_ref, hgnw_ref, convw_ref, convb_ref, dtb_ref, alog_ref, dskip_ref,
                ssdnw_ref, wout_ref, ln1g_ref, ln1b_ref, rw1_ref, rwh_ref, rb_ref,
                tril_ref, trils_ref, e128_ref,
                x1_ref, meta_ref, cnt_ref,
                proj_s, xb_s, b_s, xpad_s, xdt_s, cse_s, st_s, pt_s, cat_s, ossd_s, mix_s):
    TL = TILE_ROWS
    t = pl.program_id(1)

    @pl.when(t == 0)
    def _():
        xpad_s[0:SUBLANES, :] = jnp.zeros((SUBLANES, SSD_CONV_DIM), F32)
        st_s[...] = jnp.zeros_like(st_s)
        pt_s[...] = jnp.zeros_like(pt_s)

    row64 = lax.broadcasted_iota(jnp.int32, (CHUNK, CHUNK), 0)
    col64 = lax.broadcasted_iota(jnp.int32, (CHUNK, CHUNK), 1)
    causal = row64 >= col64
    gw = SSD_WIDTH // SSD_GROUPS
    hpg = SSD_HEADS // SSD_GROUPS
    lane_head = lax.broadcasted_iota(jnp.int32, (CHUNK, gw), 1) // SSD_HEAD_DIM

    pending = []

    def filler():
        if pending:
            pending.pop(0)()

    def project_slices(base):
        def one(c0, c1):
            def run():
                proj_s[base:base + TL, c0:c1] = jnp.dot(xb_s[base:base + TL, :], win_ref[:, c0:c1],
                                                        preferred_element_type=F32)
            return run
        return [one(c0, min(c0 + PROJ_SLICE, PROJ_COLS)) for c0 in range(0, PROJ_COLS, PROJ_SLICE)]

    def hg_front(base):
        a0 = lbp_ref[0:1, :]
        a1 = lbp_ref[1:2, :]
        am = jnp.maximum(a0, a1)
        e0 = jnp.exp(a0 - am)
        e1 = jnp.exp(a1 - am)
        lb = e0 / (e0 + e1)
        f = lb + (1.0 - lb) * _sigmoid(proj_s[base:base + TL, OFF_F:OFF_F + HG_WIDTH])
        b_s[base:base + TL, :] = _dot01_left(tril_ref[...], jnp.log(f))
        proj_s[base:base + TL, OFF_F:OFF_F + HG_WIDTH] = f
        proj_s[base:base + TL, OFF_Q:OFF_Q + HG_WIDTH] = _silu(proj_s[base:base + TL, OFF_Q:OFF_Q + HG_WIDTH])
        units = {}
        for c in range(TL // CHUNK):
            r0 = base + c * CHUNK
            for h in range(HG_HEADS):
                h0 = h * HG_HEAD_DIM
                bc = b_s[r0:r0 + CHUNK, h0:h0 + HG_HEAD_DIM]
                qc = proj_s[r0:r0 + CHUNK, OFF_Q + h0:OFF_Q + h0 + HG_HEAD_DIM]
                kc = 1.0 - proj_s[r0:r0 + CHUNK, OFF_F + h0:OFF_F + h0 + HG_HEAD_DIM]
                vcb = proj_s[r0:r0 + CHUNK, OFF_I + h0:OFF_I + h0 + HG_HEAD_DIM].astype(BF16)
                parts = []
                for i in range(CHUNK // SUB_CHUNK):
                    s0 = i * SUB_CHUNK
                    if i == 0:
                        qi = qc[0:SUB_CHUNK] * jnp.exp(bc[0:SUB_CHUNK])
                        ki = kc * jnp.exp(jnp.minimum(-bc, EXP_CAP))
                    else:
                        ref_i = bc[s0 - 1:s0, :]
                        qi = qc[s0:s0 + SUB_CHUNK] * jnp.exp(bc[s0:s0 + SUB_CHUNK] - ref_i)
                        ki = kc * jnp.exp(jnp.minimum(ref_i - bc, EXP_CAP))
                    parts.append(_bdot_nt(qi, ki))
                b_end = bc[CHUNK - 1:CHUNK, :]
                kdec = kc * jnp.exp(b_end - bc)
                units[c, h] = dict(
                    parts=parts, vcb=vcb, qdec=(qc * jnp.exp(bc)).astype(BF16),
                    local=_bdot_tn(vcb, kdec),
                    decay=jnp.exp(b_end))
                if h % 2 == 1:
                    filler()
        return units

    def hg_back(base, units):
        hgnw = hgnw_ref[...]
        for h in range(HG_HEADS):
            st = st_s[h]
            for c in range(TL // CHUNK):
                u = units[c, h]
                u['state'] = st
                st = st * u['decay'] + u['local']
            st_s[h] = st
        for c in range(TL // CHUNK):
            r0 = base + c * CHUNK
            for h in range(HG_HEADS):
                h0 = h * HG_HEAD_DIM
                u = units[c, h]
                sc = jnp.where(causal, jnp.concatenate(u['parts'], axis=0), 0.0)
                o = _bdot(sc, u['vcb']) + _bdot_nt(u['qdec'], u['state'])
                ms = jnp.mean(o * o, axis=-1, keepdims=True)
                on = o * lax.rsqrt(ms + RMS_EPS) * hgnw
                gc = proj_s[r0:r0 + CHUNK, OFF_G + h0:OFF_G + h0 + HG_HEAD_DIM]
                cat_s[r0:r0 + CHUNK, h0:h0 + HG_HEAD_DIM] = (on * _silu(gc)).astype(BF16)
                filler()

    def ssd_front(base):
        e128 = e128_ref[...]
        dtc = _softplus(proj_s[base:base + TL, OFF_DT:OFF_DT + LANES] + dtb_ref[...])
        a_row = -jnp.exp(alog_ref[...])
        cs_c = _dot01_left(tril_ref[...], dtc * a_row)
        cs_r = jnp.transpose(cs_c)[0:SSD_HEADS, :]
        cse_s[base:base + TL, :] = _dot01_right(cs_c, e128)
        dt_exp = _dot01_right(dtc, e128)

        xpad_s[SUBLANES + base:SUBLANES + base + TL, :] = proj_s[base:base + TL, OFF_XBC:OFF_XBC + SSD_CONV_DIM]
        for c0 in range(0, SSD_CONV_DIM, CONV_SLICE):
            acc = jnp.broadcast_to(convb_ref[:, c0:c0 + CONV_SLICE], (TL, CONV_SLICE))
            for j in range(SSD_CONV):
                off = base + SUBLANES - (SSD_CONV - 1) + j
                acc = acc + convw_ref[j:j + 1, c0:c0 + CONV_SLICE] * xpad_s[off:off + TL, c0:c0 + CONV_SLICE]
            proj_s[base:base + TL, OFF_XBC + c0:OFF_XBC + c0 + CONV_SLICE] = _silu(acc)
            filler()
        xdt_s[base:base + TL, :] = proj_s[base:base + TL, OFF_XS:OFF_XS + SSD_WIDTH] * dt_exp

        units = {}
        for c in range(TL // CHUNK):
            r0 = base + c * CHUNK
            for g in range(SSD_GROUPS):
                g0 = g * gw
                bgb = proj_s[r0:r0 + CHUNK, OFF_B + g * SSD_STATE:OFF_B + (g + 1) * SSD_STATE].astype(BF16)
                cgb = proj_s[r0:r0 + CHUNK, OFF_C + g * SSD_STATE:OFF_C + (g + 1) * SSD_STATE].astype(BF16)
                cse_g = cse_s[r0:r0 + CHUNK, g0:g0 + gw]
                cs_end = cse_s[r0 + CHUNK - 1:r0 + CHUNK, g0:g0 + gw]
                xdt_g = xdt_s[r0:r0 + CHUNK, g0:g0 + gw]
                units[c, g] = dict(
                    cgb=cgb, gm=_bdot_nt(cgb, bgb),
                    local=_bdot_tn(bgb, xdt_g * jnp.exp(cs_end - cse_g)),
                    decay=jnp.exp(cs_end))
                filler()
        return units, cs_r

    def ssd_back(base, units, cs_r):
        for g in range(SSD_GROUPS):
            pt = pt_s[g]
            for c in range(TL // CHUNK):
                u = units[c, g]
                u['state'] = pt
                pt = pt * u['decay'] + u['local']
            pt_s[g] = pt
        for c in range(TL // CHUNK):
            r0 = base + c * CHUNK
            for g in range(SSD_GROUPS):
                g0 = g * gw
                u = units[c, g]
                cse_g = cse_s[r0:r0 + CHUNK, g0:g0 + gw]
                xdt_g = xdt_s[r0:r0 + CHUNK, g0:g0 + gw]
                ydiag = jnp.zeros((CHUNK, gw), F32)
                for hl in range(hpg):
                    hh = g * hpg + hl
                    seg = (cse_g[:, hl * SSD_HEAD_DIM:(hl + 1) * SSD_HEAD_DIM]
                           - cs_r[hh:hh + 1, c * CHUNK:(c + 1) * CHUNK])
                    lm = jnp.where(causal, jnp.exp(jnp.minimum(seg, 0.0)), 0.0)
                    xm = jnp.where(lane_head == hl, xdt_g, 0.0)
                    ydiag = ydiag + _bdot(u['gm'] * lm, xm)
                yoff = _bdot(u['cgb'], u['state']) * jnp.exp(cse_g)
                xs_g = proj_s[r0:r0 + CHUNK, OFF_XS + g0:OFF_XS + g0 + gw]
                ossd_s[r0:r0 + CHUNK, g0:g0 + gw] = ydiag + yoff + xs_g * dskip_ref[:, g0:g0 + gw]
                filler()

    def ssd_gate_norm(tile):
        base = tile * TL
        y = ossd_s[base:base + TL, :] * _silu(proj_s[base:base + TL, OFF_Z:OFF_Z + SSD_WIDTH])
        for g in range(SSD_GROUPS):
            yg = y[:, g * gw:(g + 1) * gw]
            ms = jnp.mean(yg * yg, axis=-1, keepdims=True)
            yn = yg * lax.rsqrt(ms + RMS_EPS) * ssdnw_ref[:, g * gw:(g + 1) * gw]
            cat_s[base:base + TL, HG_WIDTH + g * gw:HG_WIDTH + (g + 1) * gw] = yn.astype(BF16)

    def out_slice(tile, c0):
        base = tile * TL
        mix_s[base:base + TL, c0:c0 + OUT_SLICE] = jnp.dot(
            cat_s[base:base + TL, :], wout_ref[:, c0:c0 + OUT_SLICE], preferred_element_type=F32)

    def norm_route(tile):
        base = tile * TL
        hres = DEEPNORM_ALPHA * x_ref[base:base + TL, :] + mix_s[base:base + TL, :]
        mu = jnp.mean(hres, axis=-1, keepdims=True)
        hc = hres - mu
        var = jnp.mean(hc * hc, axis=-1, keepdims=True)
        x1 = hc * lax.rsqrt(var + LN_EPS) * ln1g_ref[...] + ln1b_ref[...]
        x1_ref[base:base + TL, :] = x1

        xh = x1.astype(BF16)
        xm_ = (x1 - xh.astype(F32)).astype(BF16)
        t1 = jnp.dot(xh, rw1_ref[...], preferred_element_type=F32)
        logits = (t1[:, 0:LANES] + t1[:, LANES:2 * LANES]
                  + jnp.dot(xm_, rwh_ref[...], preferred_element_type=F32) + rb_ref[...])
        lane = lax.broadcasted_iota(jnp.int32, (TL, LANES), 1)
        lane_f = lane.astype(F32)
        neg = jnp.float32(-jnp.inf)
        work = jnp.where(lane < N_EXPERTS, logits, neg)
        onehots, vals, idxs = [], [], []
        for j in range(TOP_K):
            m = jnp.max(work, axis=-1, keepdims=True)
            idx = jnp.min(jnp.where(work == m, lane_f, float(LANES)), axis=-1, keepdims=True)
            oh = lane_f == idx
            onehots.append(oh)
            vals.append(m)
            idxs.append(idx)
            work = jnp.where(oh, neg, work)
        es = [jnp.exp(v - vals[0]) for v in vals]
        den = es[0] + es[1] + es[2] + es[3]
        gates = [e / den for e in es]
        sel = jnp.zeros((TL, LANES), F32)
        for oh in onehots:
            sel = jnp.where(oh, 1.0, sel)
        rankmat = jnp.dot(trils_ref[...], sel.astype(BF16), preferred_element_type=F32)
        cnt_ref[tile * SUBLANES:(tile + 1) * SUBLANES, :] = jnp.broadcast_to(
            jnp.sum(sel, axis=0, keepdims=True), (SUBLANES, LANES))
        meta = jnp.zeros((TL, LANES), F32)
        for j in range(TOP_K):
            rank_j = jnp.sum(jnp.where(onehots[j], rankmat, 0.0), axis=-1, keepdims=True)
            meta = jnp.where(lane == j, idxs[j], meta)
            meta = jnp.where(lane == TOP_K + j, rank_j, meta)
            meta = jnp.where(lane == 2 * TOP_K + j, gates[j], meta)
        meta_ref[base:base + TL, :] = meta

    def post_thunks(tile):
        return ([functools.partial(ssd_gate_norm, tile)]
                + [functools.partial(out_slice, tile, c0) for c0 in range(0, D_MODEL, OUT_SLICE)]
                + [functools.partial(norm_route, tile)])

    xb_s[...] = x_ref[...].astype(BF16)
    for run in project_slices(0):
        run()
    for tile in range(MIX_TILES):
        base = tile * TL
        if tile + 1 < MIX_TILES:
            pending.extend(project_slices((tile + 1) * TL))
        hg_units = hg_front(base)
        ssd_units, cs_r = ssd_front(base)
        while pending:
            filler()
        hg_back(base, hg_units)
        ssd_back(base, ssd_units, cs_r)
        pending.extend(post_thunks(tile))
    while pending:
        filler()
    xpad_s[0:SUBLANES, :] = xpad_s[STEP_ROWS:STEP_ROWS + SUBLANES, :]


def _full(shape):
    nd = len(shape)
    return pl.BlockSpec(shape, lambda *_: (0,) * nd)


def _mixer_call(x2d, consts, batch, seq):
    SR = STEP_ROWS
    nt = seq // SR
    T = batch * seq
    in_specs = [pl.BlockSpec((SR, D_MODEL), lambda b, t: (b * nt + t, 0))]
    in_specs += [_full(c.shape) for c in consts]
    out_shape = (jax.ShapeDtypeStruct((T, D_MODEL), F32),
                 jax.ShapeDtypeStruct((T, LANES), F32),
                 jax.ShapeDtypeStruct((T // TILE_ROWS * SUBLANES, LANES), F32))
    out_specs = (pl.BlockSpec((SR, D_MODEL), lambda b, t: (b * nt + t, 0)),
                 pl.BlockSpec((SR, LANES), lambda b, t: (b * nt + t, 0)),
                 pl.BlockSpec((MIX_TILES * SUBLANES, LANES), lambda b, t: (b * nt + t, 0)))
    scratch = [
        pltpu.VMEM((SR, PROJ_COLS), F32),
        pltpu.VMEM((SR, D_MODEL), BF16),
        pltpu.VMEM((SR, HG_WIDTH), F32),
        pltpu.VMEM((SR + 2 * SUBLANES, SSD_CONV_DIM), F32),
        pltpu.VMEM((SR, SSD_WIDTH), F32),
        pltpu.VMEM((SR, SSD_WIDTH), F32),
        pltpu.VMEM((HG_HEADS, HG_HEAD_DIM, HG_HEAD_DIM), F32),
        pltpu.VMEM((SSD_GROUPS, SSD_STATE, SSD_WIDTH // SSD_GROUPS), F32),
        pltpu.VMEM((SR, D_MODEL), BF16),
        pltpu.VMEM((SR, SSD_WIDTH), F32),
        pltpu.VMEM((SR, D_MODEL), F32),
    ]
    return pl.pallas_call(
        _mixer_body,
        grid=(batch, nt),
        in_specs=in_specs,
        out_specs=out_specs,
        out_shape=out_shape,
        scratch_shapes=scratch,
        compiler_params=pltpu.CompilerParams(
            dimension_semantics=("arbitrary", "arbitrary"), vmem_limit_bytes=VMEM_LIMIT),
        name="mixer",
    )(x2d, *consts)


def _for_each_run(units_ref, tile, fn):
    def per_expert(e, carry):
        k = tile * N_EXPERTS + e
        n = units_ref[k]

        @pl.when(n > 0)
        def _():
            fn(k, n)
        return carry
    lax.fori_loop(0, N_EXPERTS, per_expert, 0)


def _rows(unit, n_units=1):
    return pl.ds(pl.multiple_of(unit * RUN_ALIGN, RUN_ALIGN), n_units * RUN_ALIGN)


def _dispatch_body(units_ref, g8_ref, l8_ref, tot_ref, tail8_ref, tailn_ref, nv_ref,
                   x1_ref, meta_ref, lst_ref, xs_hbm, sorted_s, zero_s, sems, zsem, bsem):
    tau = pl.program_id(0)
    n = pl.num_programs(0)
    slot = lax.rem(tau, 2)
    TD, S = TILE_ROWS, SORT_ROWS
    n_blocks = xs_hbm.shape[0] // EXPERT_BLOCK

    def unused_block_copy(m):
        rows = pl.ds(pl.multiple_of(m * EXPERT_BLOCK, EXPERT_BLOCK), EXPERT_BLOCK)
        return pltpu.make_async_copy(zero_s, xs_hbm.at[rows, :], bsem)

    def run_copy(s, l_unit, g_unit, n_units):
        return pltpu.make_async_copy(sorted_s.at[s, _rows(l_unit, n_units), :],
                                     xs_hbm.at[_rows(g_unit, n_units), :], sems.at[s])

    def wait_units(s, count):
        run_copy(s, 0, 0, count).wait()

    @pl.when(tau == 0)
    def _():
        zero_s[...] = jnp.zeros_like(zero_s)

        def tail_copy(e, i):
            return pltpu.make_async_copy(zero_s.at[pl.ds(0, RUN_ALIGN), :],
                                         xs_hbm.at[_rows(tail8_ref[e] + i), :], zsem)

        def start_unused(m, carry):
            unused_block_copy(m).start()
            return carry
        lax.fori_loop(nv_ref[0], n_blocks, start_unused, 0)

        def start_e(e, carry):
            def unit(i, c2):
                tail_copy(e, i).start()
                return c2
            lax.fori_loop(0, tailn_ref[e], unit, 0)
            return carry
        lax.fori_loop(0, N_EXPERTS, start_e, 0)

        def wait_e(e, carry):
            def unit(i, c2):
                tail_copy(e, 0).wait()
                return c2
            lax.fori_loop(0, tailn_ref[e], unit, 0)
            return carry
        lax.fori_loop(0, N_EXPERTS, wait_e, 0)

    @pl.when(tau >= 2)
    def _():
        wait_units(slot, tot_ref[jnp.maximum(tau - 2, 0)])

    meta = meta_ref[...]
    lane = lax.broadcasted_iota(jnp.int32, (TD, LANES), 1)
    lane_f = lane.astype(F32)
    u8 = jnp.broadcast_to(lst_ref[0:1, :], (SUBLANES, LANES)).astype(BF16)
    ones8 = jnp.ones((SUBLANES, LANES), BF16)
    r_iota = lax.broadcasted_iota(jnp.int32, (S, TD), 0).astype(F32)
    gh = meta.astype(BF16).astype(F32)
    g1 = meta - gh
    gm = g1.astype(BF16).astype(F32)
    gl = g1 - gm
    nt_dims = (((1,), (1,)), ((), ()))
    conds, gparts = [], []
    for j in range(TOP_K):
        oh = lane_f == meta[:, j:j + 1]
        ohb = jnp.where(oh, 1.0, 0.0).astype(BF16)
        rkb = jnp.where(oh, meta[:, TOP_K + j:TOP_K + j + 1], 0.0).astype(BF16)
        m1 = lax.dot_general(u8, ohb, nt_dims, preferred_element_type=F32)
        m2 = lax.dot_general(ones8, rkb, nt_dims, preferred_element_type=F32)
        lpos = RUN_ALIGN * m1[0:1, :] + m2[0:1, :]
        conds.append(r_iota == lpos)
        gc = 2 * TOP_K + j
        gparts.append(jnp.where(lane == 0, gh[:, gc:gc + 1],
                                jnp.where(lane == 1, gm[:, gc:gc + 1],
                                          jnp.where(lane == 2, gl[:, gc:gc + 1], 0.0))).astype(BF16))
    pcat = jnp.concatenate([jnp.where(c, 1.0, 0.0).astype(BF16) for c in conds], axis=1)
    sorted_g = jnp.dot(pcat, jnp.concatenate(gparts, axis=0), preferred_element_type=F32)
    perm = jnp.where(conds[0], 1.0, jnp.where(conds[1], 1.0, jnp.where(conds[2], 1.0,
                     jnp.where(conds[3], 1.0, 0.0)))).astype(BF16)
    sorted_x = jnp.dot(perm, x1_ref[...].astype(BF16), preferred_element_type=F32)
    sorted_s[slot, :, 0:D_MODEL] = sorted_x
    sorted_s[slot, :, D_MODEL:XS_WIDTH] = sorted_g

    _for_each_run(units_ref, tau, lambda k, cnt: run_copy(slot, l8_ref[k], g8_ref[k], cnt).start())

    @pl.when(tau == n - 1)
    def _():
        @pl.when(tau >= 1)
        def _():
            wait_units(1 - slot, tot_ref[jnp.maximum(tau - 1, 0)])
        wait_units(slot, tot_ref[tau])

        def wait_unused(m, carry):
            unused_block_copy(m).wait()
            return carry
        lax.fori_loop(nv_ref[0], n_blocks, wait_unused, 0)


def _dispatch_call(sched, x1, meta, lst_rows, cap):
    T = x1.shape[0]
    TD = TILE_ROWS
    grid_spec = pltpu.PrefetchScalarGridSpec(
        num_scalar_prefetch=7,
        grid=(T // TD,),
        in_specs=[
            pl.BlockSpec((TD, D_MODEL), lambda i, *_: (i, 0)),
            pl.BlockSpec((TD, LANES), lambda i, *_: (i, 0)),
            pl.BlockSpec((SUBLANES, LANES), lambda i, *_: (i, 0)),
        ],
        out_specs=pl.BlockSpec(memory_space=pl.ANY),
        scratch_shapes=[
            pltpu.VMEM((2, SORT_ROWS, XS_WIDTH), F32),
            pltpu.VMEM((EXPERT_BLOCK, XS_WIDTH), F32),
            pltpu.SemaphoreType.DMA((2,)),
            pltpu.SemaphoreType.DMA(()),
            pltpu.SemaphoreType.DMA(()),
        ],
    )
    return pl.pallas_call(
        _dispatch_body,
        grid_spec=grid_spec,
        out_shape=jax.ShapeDtypeStruct((cap, XS_WIDTH), F32),
        compiler_params=pltpu.CompilerParams(
            dimension_semantics=("arbitrary",), vmem_limit_bytes=VMEM_LIMIT),
        name="dispatch",
    )(*sched, x1, meta, lst_rows)


def _expert_body(be_ref, nv_ref, nxt_ref, par_ref, xs_ref, wg_hbm, bg_ref, wu_hbm, bu_ref, wd_hbm, bd_ref,
                 y_ref, wbuf, wsem, wg_s, wu_s, wd_s):
    m = pl.program_id(0)
    valid = m < nv_ref[0]
    changed = jnp.logical_or(m == 0, be_ref[m] != be_ref[jnp.maximum(m - 1, 0)])

    def weight_copies(e, s):
        return [pltpu.make_async_copy(w.at[e], wbuf.at[s, k], wsem.at[s])
                for k, w in enumerate((wg_hbm, wu_hbm, wd_hbm))]

    @pl.when(m == 0)
    def _():
        for c in weight_copies(be_ref[0], par_ref[0]):
            c.start()

    @pl.when(jnp.logical_and(valid, changed))
    def _():
        s = par_ref[m]
        for c in weight_copies(be_ref[m], s):
            c.wait()

        @pl.when(nxt_ref[m] != be_ref[m])
        def _():
            for c in weight_copies(nxt_ref[m], 1 - s):
                c.start()
        wg_s[...] = wbuf[s, 0].astype(BF16)
        wu_s[...] = wbuf[s, 1].astype(BF16)
        wd_s[...] = wbuf[s, 2].astype(BF16)

    @pl.when(valid)
    def _():
        xb = xs_ref[:, 0:D_MODEL].astype(BF16)
        gate = (xs_ref[:, D_MODEL:D_MODEL + 1] + xs_ref[:, D_MODEL + 1:D_MODEL + 2]
                + xs_ref[:, D_MODEL + 2:D_MODEL + 3])
        hg = jnp.minimum(jnp.dot(xb, wg_s[...], preferred_element_type=F32) + bg_ref[0], SWIGLU_LIMIT)
        hu = jnp.clip(jnp.dot(xb, wu_s[...], preferred_element_type=F32) + bu_ref[0],
                      -SWIGLU_LIMIT, SWIGLU_LIMIT)
        hact = (hu + 1.0) * (hg * _sigmoid(SWIGLU_ALPHA * hg))
        y = jnp.dot(hact.astype(BF16), wd_s[...], preferred_element_type=F32) + bd_ref[0]
        y_ref[...] = y * gate

    @pl.when(jnp.logical_not(valid))
    def _():
        y_ref[...] = jnp.zeros_like(y_ref)


def _expert_call(block_expert, n_valid, next_expert, parity, xs, w_gate, b_gate, w_up, b_up, w_down, b_down):
    n_blocks = xs.shape[0] // EXPERT_BLOCK
    wspec = pl.BlockSpec(memory_space=pl.ANY)
    bspec = pl.BlockSpec((1, 1, D_MODEL), lambda m, be, *_: (be[m], 0, 0))
    grid_spec = pltpu.PrefetchScalarGridSpec(
        num_scalar_prefetch=4,
        grid=(n_blocks,),
        in_specs=[
            pl.BlockSpec((EXPERT_BLOCK, XS_WIDTH), lambda m, be, nv, *_: (jnp.minimum(m, nv[0] - 1), 0)),
            wspec, bspec, wspec, bspec, wspec, bspec,
        ],
        out_specs=pl.BlockSpec((EXPERT_BLOCK, D_MODEL), lambda m, *_: (m, 0)),
        scratch_shapes=[
            pltpu.VMEM((2, 3, D_MODEL, D_MODEL), F32),
            pltpu.SemaphoreType.DMA((2,)),
            pltpu.VMEM((D_MODEL, D_MODEL), BF16),
            pltpu.VMEM((D_MODEL, D_MODEL), BF16),
            pltpu.VMEM((D_MODEL, D_MODEL), BF16),
        ],
    )
    return pl.pallas_call(
        _expert_body,
        grid_spec=grid_spec,
        out_shape=jax.ShapeDtypeStruct((n_blocks * EXPERT_BLOCK, D_MODEL), F32),
        compiler_params=pltpu.CompilerParams(
            dimension_semantics=("arbitrary",), vmem_limit_bytes=VMEM_LIMIT),
        name="experts",
    )(block_expert, n_valid, next_expert, parity, xs, w_gate, b_gate.reshape(N_EXPERTS, 1, D_MODEL),
      w_up, b_up.reshape(N_EXPERTS, 1, D_MODEL), w_down, b_down.reshape(N_EXPERTS, 1, D_MODEL))


def _combine_body(units_ref, g8_ref, l8_ref, tot_ref,
                  y_hbm, x1_ref, meta_ref, lst_ref, g_ref, b_ref, out_ref, ys_s, sems):
    tau = pl.program_id(0)
    n = pl.num_programs(0)
    slot = lax.rem(tau, 2)
    TD, S = TILE_ROWS, SORT_ROWS

    def run_copy(s, g_unit, l_unit, n_units):
        return pltpu.make_async_copy(y_hbm.at[_rows(g_unit, n_units), :],
                                     ys_s.at[s, _rows(l_unit, n_units), :], sems.at[s])

    def fetch(tile, s):
        _for_each_run(units_ref, tile, lambda k, cnt: run_copy(s, g8_ref[k], l8_ref[k], cnt).start())

    @pl.when(tau == 0)
    def _():
        ys_s[...] = jnp.zeros_like(ys_s)
        fetch(0, 0)

    @pl.when(tau + 1 < n)
    def _():
        fetch(tau + 1, 1 - slot)

    run_copy(slot, 0, 0, tot_ref[tau]).wait()

    meta = meta_ref[...]
    lane_f = lax.broadcasted_iota(jnp.int32, (TD, LANES), 1).astype(F32)
    u_row = lst_ref[0:1, :]
    s_iota = lax.broadcasted_iota(jnp.int32, (TD, S), 1).astype(F32)
    conds = []
    for j in range(TOP_K):
        oh = lane_f == meta[:, j:j + 1]
        start8 = jnp.sum(jnp.where(oh, u_row, 0.0), axis=-1, keepdims=True)
        lpos = RUN_ALIGN * start8 + meta[:, TOP_K + j:TOP_K + j + 1]
        conds.append(s_iota == lpos)
    perm = jnp.where(conds[0], 1.0, jnp.where(conds[1], 1.0, jnp.where(conds[2], 1.0,
                     jnp.where(conds[3], 1.0, 0.0)))).astype(BF16)
    ys = ys_s[slot]
    yh = ys.astype(BF16)
    yl = (ys - yh.astype(F32)).astype(BF16)
    ffn = (jnp.dot(perm, yh, preferred_element_type=F32) + jnp.dot(perm, yl, preferred_element_type=F32))
    acc = DEEPNORM_ALPHA * x1_ref[...] + ffn
    mu = jnp.mean(acc, axis=-1, keepdims=True)
    hc = acc - mu
    var = jnp.mean(hc * hc, axis=-1, keepdims=True)
    out_ref[...] = hc * lax.rsqrt(var + LN_EPS) * g_ref[...] + b_ref[...]


def _combine_call(sched, y_rows, x1, meta, lst_rows, ln2_g, ln2_b):
    T = x1.shape[0]
    TD = TILE_ROWS
    grid_spec = pltpu.PrefetchScalarGridSpec(
        num_scalar_prefetch=4,
        grid=(T // TD,),
        in_specs=[
            pl.BlockSpec(memory_space=pl.ANY),
            pl.BlockSpec((TD, D_MODEL), lambda i, *_: (i, 0)),
            pl.BlockSpec((TD, LANES), lambda i, *_: (i, 0)),
            pl.BlockSpec((SUBLANES, LANES), lambda i, *_: (i, 0)),
            pl.BlockSpec((1, D_MODEL), lambda i, *_: (0, 0)),
            pl.BlockSpec((1, D_MODEL), lambda i, *_: (0, 0)),
        ],
        out_specs=pl.BlockSpec((TD, D_MODEL), lambda i, *_: (i, 0)),
        scratch_shapes=[pltpu.VMEM((2, SORT_ROWS, D_MODEL), F32), pltpu.SemaphoreType.DMA((2,))],
    )
    return pl.pallas_call(
        _combine_body,
        grid_spec=grid_spec,
        out_shape=jax.ShapeDtypeStruct((T, D_MODEL), F32),
        compiler_params=pltpu.CompilerParams(
            dimension_semantics=("arbitrary",), vmem_limit_bytes=VMEM_LIMIT),
        name="combine",
    )(*sched, y_rows, x1, meta, lst_rows, ln2_g, ln2_b)


def _np_consts():
    TL = TILE_ROWS
    r = np.arange(TL)
    same = (r[:, None] // CHUNK) == (r[None, :] // CHUNK)
    tril = (same & (r[None, :] <= r[:, None])).astype(np.float32)
    trils = (r[None, :] < r[:, None]).astype(np.float32)
    e128 = np.zeros((LANES, SSD_WIDTH), np.float32)
    for h in range(SSD_HEADS):
        e128[h, h * SSD_HEAD_DIM:(h + 1) * SSD_HEAD_DIM] = 1.0
    return tril, trils, e128


def kernel(x, w_in, hg_lower_bound, hg_norm_w, conv_w, conv_b, dt_bias, a_log, d_skip, ssd_norm_w, w_out,
           ln1_g, ln1_b, router_w, router_b, w_gate, b_gate, w_up, b_up, w_down, b_down, ln2_g, ln2_b):
    batch, seq, d = x.shape
    assert d == D_MODEL and seq % STEP_ROWS == 0 and w_in.shape[0] == DEPTH
    T = batch * seq
    n_tiles = T // TILE_ROWS
    max_rows = T * TOP_K + n_tiles * N_EXPERTS * (RUN_ALIGN - 1)
    n_blocks = -(-max_rows // EXPERT_BLOCK) + N_EXPERTS
    cap = n_blocks * EXPERT_BLOCK
    assert SORT_ROWS >= TILE_ROWS * TOP_K + N_EXPERTS * (RUN_ALIGN - 1)

    tril, trils, e128 = _np_consts()
    w = w_in[0]
    pad_l = LANES - SSD_HEADS
    rw = jnp.pad(router_w[0], ((0, 0), (0, LANES - N_EXPERTS)))
    rwh = rw.astype(BF16)
    rwm = (rw - rwh.astype(F32)).astype(BF16)
    consts = [
        jnp.pad(w, ((0, 0), (0, pad_l))).astype(BF16),
        hg_lower_bound,
        hg_norm_w[0].reshape(1, HG_HEAD_DIM),
        conv_w[0],
        conv_b[0].reshape(1, SSD_CONV_DIM),
        jnp.pad(dt_bias[0], (0, pad_l)).reshape(1, LANES),
        jnp.pad(a_log[0], (0, pad_l)).reshape(1, LANES),
        jnp.repeat(d_skip[0], SSD_HEAD_DIM).reshape(1, SSD_WIDTH),
        ssd_norm_w[0].reshape(1, SSD_WIDTH),
        w_out[0].astype(BF16),
        ln1_g[0].reshape(1, D_MODEL),
        ln1_b[0].reshape(1, D_MODEL),
        jnp.concatenate([rwh, rwm], axis=1),
        rwh,
        jnp.pad(router_b[0], (0, LANES - N_EXPERTS)).reshape(1, LANES),
        jnp.asarray(tril, BF16), jnp.asarray(trils, BF16), jnp.asarray(e128, BF16),
    ]
    x1, meta, cnt = _mixer_call(x.reshape(T, D_MODEL), consts, batch, seq)

    counts = cnt.reshape(n_tiles, SUBLANES, LANES)[:, 0, :N_EXPERTS].astype(jnp.int32)
    c8 = (counts + RUN_ALIGN - 1) // RUN_ALIGN * RUN_ALIGN
    used = jnp.sum(c8, axis=0)
    region = (used + EXPERT_BLOCK - 1) // EXPERT_BLOCK * EXPERT_BLOCK
    region_end = jnp.cumsum(region)
    region_start = region_end - region
    gstart = region_start[None, :] + jnp.cumsum(c8, axis=0) - c8
    lstart = jnp.cumsum(c8, axis=1) - c8
    n_valid = region_end[-1] // EXPERT_BLOCK
    blk_row = jnp.arange(n_blocks, dtype=jnp.int32) * EXPERT_BLOCK
    blk_row = jnp.minimum(blk_row, (n_valid - 1) * EXPERT_BLOCK)
    block_expert = jnp.minimum(jnp.sum(region_end[None, :] <= blk_row[:, None], axis=1),
                               N_EXPERTS - 1).astype(jnp.int32)
    has = region > 0
    eidx = jnp.arange(N_EXPERTS, dtype=jnp.int32)
    suffix_min = lax.cummin(jnp.where(has, eidx, N_EXPERTS), reverse=True)
    nxt_e = jnp.concatenate([suffix_min[1:], jnp.full((1,), N_EXPERTS, jnp.int32)])
    nxt_e = jnp.where(nxt_e == N_EXPERTS, eidx, nxt_e)
    par_e = (jnp.cumsum(has.astype(jnp.int32)) - 1) % 2
    blk_onehot = block_expert[:, None] == eidx[None, :]
    next_expert = jnp.sum(jnp.where(blk_onehot, nxt_e[None, :], 0), axis=1).astype(jnp.int32)
    parity = jnp.sum(jnp.where(blk_onehot, par_e[None, :], 0), axis=1).astype(jnp.int32)
    as_units = lambda a: (a // RUN_ALIGN).astype(jnp.int32).reshape(-1)
    units, g8, l8 = as_units(c8), as_units(gstart), as_units(lstart)
    tot = (jnp.sum(c8, axis=1) // RUN_ALIGN).astype(jnp.int32)
    tail8 = as_units(region_start + used)
    tailn = as_units(region - used)
    lst_rows = jnp.repeat(jnp.pad((lstart // RUN_ALIGN).astype(F32), ((0, 0), (0, LANES - N_EXPERTS))),
                          SUBLANES, axis=0)

    nv = n_valid.astype(jnp.int32).reshape(1)
    xs = _dispatch_call((units, g8, l8, tot, tail8, tailn, nv), x1, meta, lst_rows, cap)
    y_rows = _expert_call(block_expert, nv, next_expert, parity, xs,
                          w_gate[0], b_gate[0], w_up[0], b_up[0], w_down[0], b_down[0])
    out = _combine_call((units, g8, l8, tot), y_rows, x1, meta, lst_rows,
                        ln2_g[0].reshape(1, D_MODEL), ln2_b[0].reshape(1, D_MODEL))
    return out.reshape(batch, seq, D_MODEL)
```

```python
import functools

import jax
import jax.numpy as jnp
import numpy as np
from jax import lax
from jax.experimental import pallas as pl
from jax.experimental.pallas import tpu as pltpu

F32 = jnp.float32
BF16 = jnp.bfloat16

D_MODEL = 1024
CHUNK = 64
HG_WIDTH = 512
HG_HEAD_DIM = 128
HG_HEADS = 4
SSD_WIDTH = 512
SSD_HEAD_DIM = 64
SSD_HEADS = 8
SSD_GROUPS = 2
SSD_STATE = 128
SSD_CONV = 4
SSD_CONV_DIM = SSD_WIDTH + 2 * SSD_GROUPS * SSD_STATE
N_EXPERTS = 32
TOP_K = 4
EXPERT_BLOCK = 512
SWIGLU_LIMIT = 7.0
SWIGLU_ALPHA = 1.702
DEPTH = 1
DEEPNORM_ALPHA = (2 * DEPTH) ** 0.25
LN_EPS = 1e-5
RMS_EPS = 1e-5

LANES = 128
SUBLANES = 8
SUB_CHUNK = 16
EXP_CAP = 60.0
TILE_ROWS = 256
MIX_TILES = 2
STEP_ROWS = MIX_TILES * TILE_ROWS
RUN_ALIGN = SUBLANES
SORT_ROWS = 1280
LST_SPLIT = 32
XS_WIDTH = D_MODEL + LANES
VMEM_LIMIT = 56 * 1024 * 1024

OFF_Q, OFF_F, OFF_I, OFF_G = 0, 512, 1024, 1536
OFF_Z, OFF_XBC = 2048, 2560
OFF_XS, OFF_B, OFF_C = 2560, 3072, 3328
OFF_DT = 3584
PROJ_COLS = OFF_DT + LANES
PROJ_SLICE = 256
CONV_SLICE = 256
OUT_SLICE = 256


def _bdot(a, b):
    return jnp.dot(a.astype(BF16), b.astype(BF16), preferred_element_type=F32)


def _bdot_nt(a, b):
    return lax.dot_general(a.astype(BF16), b.astype(BF16), (((1,), (1,)), ((), ())),
                           preferred_element_type=F32)


def _bdot_tn(a, b):
    return lax.dot_general(a.astype(BF16), b.astype(BF16), (((0,), (0,)), ((), ())),
                           preferred_element_type=F32)


def _split3(a):
    hi = a.astype(BF16)
    r1 = a - hi.astype(F32)
    mid = r1.astype(BF16)
    lo = (r1 - mid.astype(F32)).astype(BF16)
    return hi, mid, lo


def _dot01_left(m01, a):
    hi, mid, lo = _split3(a)
    d = functools.partial(jnp.dot, m01, preferred_element_type=F32)
    return d(hi) + d(mid) + d(lo)


def _dot01_right(a, m01):
    hi, mid, lo = _split3(a)
    return (jnp.dot(hi, m01, preferred_element_type=F32) + jnp.dot(mid, m01, preferred_element_type=F32)
            + jnp.dot(lo, m01, preferred_element_type=F32))


def _sigmoid(x):
    return 1.0 / (1.0 + jnp.exp(-x))


def _silu(x):
    return x * _sigmoid(x)


def _softplus(x):
    return jnp.maximum(x, 0.0) + jnp.log(1.0 + jnp.exp(-jnp.abs(x)))


def _mixer_body(x_ref, win_ref, lbp_ref, hgnw_ref, convw_ref, convb_ref, dtb_ref, alog_ref, dskip_ref,
                ssdnw_ref, wout_ref, ln1g_ref, ln1b_ref, rw1_ref, rwh_ref, rb_ref,
                tril_ref, trils_ref, e128_ref,
                x1_ref, meta_ref, cnt_ref,
                proj_s, xb_s, b_s, xpad_s, xdt_s, cse_s, st_s, pt_s, cat_s, ossd_s, mix_s):
    TL = TILE_ROWS
    t = pl.program_id(1)

    @pl.when(t == 0)
    def _():
        xpad_s[0:SUBLANES, :] = jnp.zeros((SUBLANES, SSD_CONV_DIM), F32)
        st_s[...] = jnp.zeros_like(st_s)
        pt_s[...] = jnp.zeros_like(pt_s)

    row64 = lax.broadcasted_iota(jnp.int32, (CHUNK, CHUNK), 0)
    col64 = lax.broadcasted_iota(jnp.int32, (CHUNK, CHUNK), 1)
    causal = row64 >= col64
    gw = SSD_WIDTH // SSD_GROUPS
    hpg = SSD_HEADS // SSD_GROUPS
    lane_head = lax.broadcasted_iota(jnp.int32, (CHUNK, gw), 1) // SSD_HEAD_DIM

    pending = []

    def filler():
        if pending:
            pending.pop(0)()

    def project_slices(base):
        def one(c0, c1):
            def run():
                proj_s[base:base + TL, c0:c1] = jnp.dot(xb_s[base:base + TL, :], win_ref[:, c0:c1],
                                                        preferred_element_type=F32)
            return run
        return [one(c0, min(c0 + PROJ_SLICE, PROJ_COLS)) for c0 in range(0, PROJ_COLS, PROJ_SLICE)]

    def hg_front(base):
        a0 = lbp_ref[0:1, :]
        a1 = lbp_ref[1:2, :]
        am = jnp.maximum(a0, a1)
        e0 = jnp.exp(a0 - am)
        e1 = jnp.exp(a1 - am)
        lb = e0 / (e0 + e1)
        f = lb + (1.0 - lb) * _sigmoid(proj_s[base:base + TL, OFF_F:OFF_F + HG_WIDTH])
        b_s[base:base + TL, :] = _dot01_left(tril_ref[...], jnp.log(f))
        proj_s[base:base + TL, OFF_F:OFF_F + HG_WIDTH] = f
        proj_s[base:base + TL, OFF_Q:OFF_Q + HG_WIDTH] = _silu(proj_s[base:base + TL, OFF_Q:OFF_Q + HG_WIDTH])
        units = {}
        for c in range(TL // CHUNK):
            r0 = base + c * CHUNK
            for h in range(HG_HEADS):
                h0 = h * HG_HEAD_DIM
                bc = b_s[r0:r0 + CHUNK, h0:h0 + HG_HEAD_DIM]
                qc = proj_s[r0:r0 + CHUNK, OFF_Q + h0:OFF_Q + h0 + HG_HEAD_DIM]
                kc = 1.0 - proj_s[r0:r0 + CHUNK, OFF_F + h0:OFF_F + h0 + HG_HEAD_DIM]
                vcb = proj_s[r0:r0 + CHUNK, OFF_I + h0:OFF_I + h0 + HG_HEAD_DIM].astype(BF16)
                parts = []
                for i in range(CHUNK // SUB_CHUNK):
                    s0 = i * SUB_CHUNK
                    if i == 0:
                        qi = qc[0:SUB_CHUNK] * jnp.exp(bc[0:SUB_CHUNK])
                        ki = kc * jnp.exp(jnp.minimum(-bc, EXP_CAP))
                    else:
                        ref_i = bc[s0 - 1:s0, :]
                        qi = qc[s0:s0 + SUB_CHUNK] * jnp.exp(bc[s0:s0 + SUB_CHUNK] - ref_i)
                        ki = kc * jnp.exp(jnp.minimum(ref_i - bc, EXP_CAP))
                    parts.append(_bdot_nt(qi, ki))
                b_end = bc[CHUNK - 1:CHUNK, :]
                kdec = kc * jnp.exp(b_end - bc)
                units[c, h] = dict(
                    parts=parts, vcb=vcb, qdec=(qc * jnp.exp(bc)).astype(BF16),
                    local=_bdot_tn(vcb, kdec),
                    decay=jnp.exp(b_end))
                if h % 2 == 1:
                    filler()
        return units

    def hg_back(base, units):
        hgnw = hgnw_ref[...]
        for h in range(HG_HEADS):
            st = st_s[h]
            for c in range(TL // CHUNK):
                u = units[c, h]
                u['state'] = st
                st = st * u['decay'] + u['local']
            st_s[h] = st
        for c in range(TL // CHUNK):
            r0 = base + c * CHUNK
            for h in range(HG_HEADS):
                h0 = h * HG_HEAD_DIM
                u = units[c, h]
                sc = jnp.where(causal, jnp.concatenate(u['parts'], axis=0), 0.0)
                o = _bdot(sc, u['vcb']) + _bdot_nt(u['qdec'], u['state'])
                ms = jnp.mean(o * o, axis=-1, keepdims=True)
                on = o * lax.rsqrt(ms + RMS_EPS) * hgnw
                gc = proj_s[r0:r0 + CHUNK, OFF_G + h0:OFF_G + h0 + HG_HEAD_DIM]
                cat_s[r0:r0 + CHUNK, h0:h0 + HG_HEAD_DIM] = (on * _silu(gc)).astype(BF16)
                filler()

    def ssd_front(base):
        e128 = e128_ref[...]
        dtc = _softplus(proj_s[base:base + TL, OFF_DT:OFF_DT + LANES] + dtb_ref[...])
        a_row = -jnp.exp(alog_ref[...])
        cs_c = _dot01_left(tril_ref[...], dtc * a_row)
        cs_r = jnp.transpose(cs_c)[0:SSD_HEADS, :]
        cse_s[base:base + TL, :] = _dot01_right(cs_c, e128)
        dt_exp = _dot01_right(dtc, e128)

        xpad_s[SUBLANES + base:SUBLANES + base + TL, :] = proj_s[base:base + TL, OFF_XBC:OFF_XBC + SSD_CONV_DIM]
        for c0 in range(0, SSD_CONV_DIM, CONV_SLICE):
            acc = jnp.broadcast_to(convb_ref[:, c0:c0 + CONV_SLICE], (TL, CONV_SLICE))
            for j in range(SSD_CONV):
                off = base + SUBLANES - (SSD_CONV - 1) + j
                acc = acc + convw_ref[j:j + 1, c0:c0 + CONV_SLICE] * xpad_s[off:off + TL, c0:c0 + CONV_SLICE]
            proj_s[base:base + TL, OFF_XBC + c0:OFF_XBC + c0 + CONV_SLICE] = _silu(acc)
            filler()
        xdt_s[base:base + TL, :] = proj_s[base:base + TL, OFF_XS:OFF_XS + SSD_WIDTH] * dt_exp

        units = {}
        for c in range(TL // CHUNK):
            r0 = base + c * CHUNK
            for g in range(SSD_GROUPS):
                g0 = g * gw
                bgb = proj_s[r0:r0 + CHUNK, OFF_B + g * SSD_STATE:OFF_B + (g + 1) * SSD_STATE].astype(BF16)
                cgb = proj_s[r0:r0 + CHUNK, OFF_C + g * SSD_STATE:OFF_C + (g + 1) * SSD_STATE].astype(BF16)
                cse_g = cse_s[r0:r0 + CHUNK, g0:g0 + gw]
                cs_end = cse_s[r0 + CHUNK - 1:r0 + CHUNK, g0:g0 + gw]
                xdt_g = xdt_s[r0:r0 + CHUNK, g0:g0 + gw]
                units[c, g] = dict(
                    cgb=cgb, gm=_bdot_nt(cgb, bgb),
                    local=_bdot_tn(bgb, xdt_g * jnp.exp(cs_end - cse_g)),
                    decay=jnp.exp(cs_end))
                filler()
        return units, cs_r

    def ssd_back(base, units, cs_r):
        for g in range(SSD_GROUPS):
            pt = pt_s[g]
            for c in range(TL // CHUNK):
                u = units[c, g]
                u['state'] = pt
                pt = pt * u['decay'] + u['local']
            pt_s[g] = pt
        for c in range(TL // CHUNK):
            r0 = base + c * CHUNK
            for g in range(SSD_GROUPS):
                g0 = g * gw
                u = units[c, g]
                cse_g = cse_s[r0:r0 + CHUNK, g0:g0 + gw]
                xdt_g = xdt_s[r0:r0 + CHUNK, g0:g0 + gw]
                ydiag = jnp.zeros((CHUNK, gw), F32)
                for hl in range(hpg):
                    hh = g * hpg + hl
                    seg = (cse_g[:, hl * SSD_HEAD_DIM:(hl + 1) * SSD_HEAD_DIM]
                           - cs_r[hh:hh + 1, c * CHUNK:(c + 1) * CHUNK])
                    lm = jnp.where(causal, jnp.exp(jnp.minimum(seg, 0.0)), 0.0)
                    xm = jnp.where(lane_head == hl, xdt_g, 0.0)
                    ydiag = ydiag + _bdot(u['gm'] * lm, xm)
                yoff = _bdot(u['cgb'], u['state']) * jnp.exp(cse_g)
                xs_g = proj_s[r0:r0 + CHUNK, OFF_XS + g0:OFF_XS + g0 + gw]
                ossd_s[r0:r0 + CHUNK, g0:g0 + gw] = ydiag + yoff + xs_g * dskip_ref[:, g0:g0 + gw]
                filler()

    def ssd_gate_norm(tile):
        base = tile * TL
        y = ossd_s[base:base + TL, :] * _silu(proj_s[base:base + TL, OFF_Z:OFF_Z + SSD_WIDTH])
        for g in range(SSD_GROUPS):
            yg = y[:, g * gw:(g + 1) * gw]
            ms = jnp.mean(yg * yg, axis=-1, keepdims=True)
            yn = yg * lax.rsqrt(ms + RMS_EPS) * ssdnw_ref[:, g * gw:(g + 1) * gw]
            cat_s[base:base + TL, HG_WIDTH + g * gw:HG_WIDTH + (g + 1) * gw] = yn.astype(BF16)

    def out_slice(tile, c0):
        base = tile * TL
        mix_s[base:base + TL, c0:c0 + OUT_SLICE] = jnp.dot(
            cat_s[base:base + TL, :], wout_ref[:, c0:c0 + OUT_SLICE], preferred_element_type=F32)

    def norm_route(tile):
        base = tile * TL
        hres = DEEPNORM_ALPHA * x_ref[base:base + TL, :] + mix_s[base:base + TL, :]
        mu = jnp.mean(hres, axis=-1, keepdims=True)
        hc = hres - mu
        var = jnp.mean(hc * hc, axis=-1, keepdims=True)
        x1 = hc * lax.rsqrt(var + LN_EPS) * ln1g_ref[...] + ln1b_ref[...]
        x1_ref[base:base + TL, :] = x1

        xh = x1.astype(BF16)
        xm_ = (x1 - xh.astype(F32)).astype(BF16)
        t1 = jnp.dot(xh, rw1_ref[...], preferred_element_type=F32)
        logits = (t1[:, 0:LANES] + t1[:, LANES:2 * LANES]
                  + jnp.dot(xm_, rwh_ref[...], preferred_element_type=F32) + rb_ref[...])
        lane = lax.broadcasted_iota(jnp.int32, (TL, LANES), 1)
        lane_f = lane.astype(F32)
        neg = jnp.float32(-jnp.inf)
        work = jnp.where(lane < N_EXPERTS, logits, neg)
        onehots, vals, idxs = [], [], []
        for j in range(TOP_K):
            m = jnp.max(work, axis=-1, keepdims=True)
            idx = jnp.min(jnp.where(work == m, lane_f, float(LANES)), axis=-1, keepdims=True)
            oh = lane_f == idx
            onehots.append(oh)
            vals.append(m)
            idxs.append(idx)
            work = jnp.where(oh, neg, work)
        es = [jnp.exp(v - vals[0]) for v in vals]
        den = es[0] + es[1] + es[2] + es[3]
        gates = [e / den for e in es]
        sel = jnp.zeros((TL, LANES), F32)
        for oh in onehots:
            sel = jnp.where(oh, 1.0, sel)
        rankmat = jnp.dot(trils_ref[...], sel.astype(BF16), preferred_element_type=F32)
        cnt_ref[tile * SUBLANES:(tile + 1) * SUBLANES, :] = jnp.broadcast_to(
            jnp.sum(sel, axis=0, keepdims=True), (SUBLANES, LANES))
        meta = jnp.zeros((TL, LANES), F32)
        for j in range(TOP_K):
            rank_j = jnp.sum(jnp.where(onehots[j], rankmat, 0.0), axis=-1, keepdims=True)
            meta = jnp.where(lane == j, idxs[j], meta)
            meta = jnp.where(lane == TOP_K + j, rank_j, meta)
            meta = jnp.where(lane == 2 * TOP_K + j, gates[j], meta)
        meta_ref[base:base + TL, :] = meta

    def post_thunks(tile):
        return ([functools.partial(ssd_gate_norm, tile)]
                + [functools.partial(out_slice, tile, c0) for c0 in range(0, D_MODEL, OUT_SLICE)]
                + [functools.partial(norm_route, tile)])

    xb_s[...] = x_ref[...].astype(BF16)
    for run in project_slices(0):
        run()
    for tile in range(MIX_TILES):
        base = tile * TL
        if tile + 1 < MIX_TILES:
            pending.extend(project_slices((tile + 1) * TL))
        hg_units = hg_front(base)
        ssd_units, cs_r = ssd_front(base)
        while pending:
            filler()
        hg_back(base, hg_units)
        ssd_back(base, ssd_units, cs_r)
        pending.extend(post_thunks(tile))
    while pending:
        filler()
    xpad_s[0:SUBLANES, :] = xpad_s[STEP_ROWS:STEP_ROWS + SUBLANES, :]


def _full(shape):
    nd = len(shape)
    return pl.BlockSpec(shape, lambda *_: (0,) * nd)


def _mixer_call(x2d, consts, batch, seq):
    SR = STEP_ROWS
    nt = seq // SR
    T = batch * seq
    in_specs = [pl.BlockSpec((SR, D_MODEL), lambda b, t: (b * nt + t, 0))]
    in_specs += [_full(c.shape) for c in consts]
    out_shape = (jax.ShapeDtypeStruct((T, D_MODEL), F32),
                 jax.ShapeDtypeStruct((T, LANES), F32),
                 jax.ShapeDtypeStruct((T // TILE_ROWS * SUBLANES, LANES), F32))
    out_specs = (pl.BlockSpec((SR, D_MODEL), lambda b, t: (b * nt + t, 0)),
                 pl.BlockSpec((SR, LANES), lambda b, t: (b * nt + t, 0)),
                 pl.BlockSpec((MIX_TILES * SUBLANES, LANES), lambda b, t: (b * nt + t, 0)))
    scratch = [
        pltpu.VMEM((SR, PROJ_COLS), F32),
        pltpu.VMEM((SR, D_MODEL), BF16),
        pltpu.VMEM((SR, HG_WIDTH), F32),
        pltpu.VMEM((SR + 2 * SUBLANES, SSD_CONV_DIM), F32),
        pltpu.VMEM((SR, SSD_WIDTH), F32),
        pltpu.VMEM((SR, SSD_WIDTH), F32),
        pltpu.VMEM((HG_HEADS, HG_HEAD_DIM, HG_HEAD_DIM), F32),
        pltpu.VMEM((SSD_GROUPS, SSD_STATE, SSD_WIDTH // SSD_GROUPS), F32),
        pltpu.VMEM((SR, D_MODEL), BF16),
        pltpu.VMEM((SR, SSD_WIDTH), F32),
        pltpu.VMEM((SR, D_MODEL), F32),
    ]
    return pl.pallas_call(
        _mixer_body,
        grid=(batch, nt),
        in_specs=in_specs,
        out_specs=out_specs,
        out_shape=out_shape,
        scratch_shapes=scratch,
        compiler_params=pltpu.CompilerParams(
            dimension_semantics=("arbitrary", "arbitrary"), vmem_limit_bytes=VMEM_LIMIT),
        name="mixer",
    )(x2d, *consts)


def _for_each_run(units_ref, tile, fn):
    def per_expert(e, carry):
        k = tile * N_EXPERTS + e
        n = units_ref[k]

        @pl.when(n > 0)
        def _():
            fn(k, n)
        return carry
    lax.fori_loop(0, N_EXPERTS, per_expert, 0)


def _rows(unit, n_units=1):
    start = unit * RUN_ALIGN
    if RUN_ALIGN > 1:
        start = pl.multiple_of(start, RUN_ALIGN)
    return pl.ds(start, n_units * RUN_ALIGN)


def _dispatch_body(units_ref, g8_ref, l8_ref, tot_ref, tail8_ref, tailn_ref, nv_ref,
                   x1_ref, meta_ref, lst_ref, xs_hbm, sorted_s, zero_s, sems, zsem, bsem):
    tau = pl.program_id(0)
    n = pl.num_programs(0)
    slot = lax.rem(tau, 2)
    TD, S = TILE_ROWS, SORT_ROWS
    n_blocks = xs_hbm.shape[0] // EXPERT_BLOCK

    def unused_block_copy(m):
        rows = pl.ds(pl.multiple_of(m * EXPERT_BLOCK, EXPERT_BLOCK), EXPERT_BLOCK)
        return pltpu.make_async_copy(zero_s, xs_hbm.at[rows, :], bsem)

    def run_copy(s, l_unit, g_unit, n_units):
        return pltpu.make_async_copy(sorted_s.at[s, _rows(l_unit, n_units), :],
                                     xs_hbm.at[_rows(g_unit, n_units), :], sems.at[s])

    def wait_units(s, count):
        run_copy(s, 0, 0, count).wait()

    @pl.when(tau == 0)
    def _():
        zero_s[...] = jnp.zeros_like(zero_s)

        def tail_copy(e):
            n_units = tailn_ref[e]
            return pltpu.make_async_copy(zero_s.at[pl.ds(0, n_units * RUN_ALIGN), :],
                                         xs_hbm.at[_rows(tail8_ref[e], n_units), :], zsem)

        def start_unused(m, carry):
            unused_block_copy(m).start()
            return carry
        lax.fori_loop(nv_ref[0], n_blocks, start_unused, 0)

        def start_e(e, carry):
            @pl.when(tailn_ref[e] > 0)
            def _():
                tail_copy(e).start()
            return carry
        lax.fori_loop(0, N_EXPERTS, start_e, 0)

        def wait_e(e, carry):
            @pl.when(tailn_ref[e] > 0)
            def _():
                tail_copy(e).wait()
            return carry
        lax.fori_loop(0, N_EXPERTS, wait_e, 0)

    @pl.when(tau >= 2)
    def _():
        wait_units(slot, tot_ref[jnp.maximum(tau - 2, 0)])

    meta = meta_ref[...]
    lane = lax.broadcasted_iota(jnp.int32, (TD, LANES), 1)
    lane_f = lane.astype(F32)
    u8 = lst_ref[...].astype(BF16)
    ones8 = jnp.ones((SUBLANES, LANES), BF16)
    r_iota = lax.broadcasted_iota(jnp.int32, (S, TD), 0).astype(F32)
    gh = meta.astype(BF16).astype(F32)
    g1 = meta - gh
    gm = g1.astype(BF16).astype(F32)
    gl = g1 - gm
    nt_dims = (((1,), (1,)), ((), ()))
    conds, gparts = [], []
    for j in range(TOP_K):
        oh = lane_f == meta[:, j:j + 1]
        ohb = jnp.where(oh, 1.0, 0.0).astype(BF16)
        rkb = jnp.where(oh, meta[:, TOP_K + j:TOP_K + j + 1], 0.0).astype(BF16)
        m1 = lax.dot_general(u8, ohb, nt_dims, preferred_element_type=F32)
        m2 = lax.dot_general(ones8, rkb, nt_dims, preferred_element_type=F32)
        lpos = RUN_ALIGN * (LST_SPLIT * m1[0:1, :] + m1[1:2, :]) + m2[0:1, :]
        conds.append(r_iota == lpos)
        gc = 2 * TOP_K + j
        gparts.append(jnp.where(lane == 0, gh[:, gc:gc + 1],
                                jnp.where(lane == 1, gm[:, gc:gc + 1],
                                          jnp.where(lane == 2, gl[:, gc:gc + 1], 0.0))).astype(BF16))
    pcat = jnp.concatenate([jnp.where(c, 1.0, 0.0).astype(BF16) for c in conds], axis=1)
    sorted_g = jnp.dot(pcat, jnp.concatenate(gparts, axis=0), preferred_element_type=F32)
    perm = jnp.where(conds[0], 1.0, jnp.where(conds[1], 1.0, jnp.where(conds[2], 1.0,
                     jnp.where(conds[3], 1.0, 0.0)))).astype(BF16)
    sorted_x = jnp.dot(perm, x1_ref[...].astype(BF16), preferred_element_type=F32)
    sorted_s[slot, :, 0:D_MODEL] = sorted_x
    sorted_s[slot, :, D_MODEL:XS_WIDTH] = sorted_g

    _for_each_run(units_ref, tau, lambda k, cnt: run_copy(slot, l8_ref[k], g8_ref[k], cnt).start())

    @pl.when(tau == n - 1)
    def _():
        @pl.when(tau >= 1)
        def _():
            wait_units(1 - slot, tot_ref[jnp.maximum(tau - 1, 0)])
        wait_units(slot, tot_ref[tau])

        def wait_unused(m, carry):
            unused_block_copy(m).wait()
            return carry
        lax.fori_loop(nv_ref[0], n_blocks, wait_unused, 0)


def _dispatch_call(sched, x1, meta, lst_rows, cap):
    T = x1.shape[0]
    TD = TILE_ROWS
    grid_spec = pltpu.PrefetchScalarGridSpec(
        num_scalar_prefetch=7,
        grid=(T // TD,),
        in_specs=[
            pl.BlockSpec((TD, D_MODEL), lambda i, *_: (i, 0)),
            pl.BlockSpec((TD, LANES), lambda i, *_: (i, 0)),
            pl.BlockSpec((SUBLANES, LANES), lambda i, *_: (i, 0)),
        ],
        out_specs=pl.BlockSpec(memory_space=pl.ANY),
        scratch_shapes=[
            pltpu.VMEM((2, SORT_ROWS, XS_WIDTH), F32),
            pltpu.VMEM((EXPERT_BLOCK, XS_WIDTH), F32),
            pltpu.SemaphoreType.DMA((2,)),
            pltpu.SemaphoreType.DMA(()),
            pltpu.SemaphoreType.DMA(()),
        ],
    )
    return pl.pallas_call(
        _dispatch_body,
        grid_spec=grid_spec,
        out_shape=jax.ShapeDtypeStruct((cap, XS_WIDTH), F32),
        compiler_params=pltpu.CompilerParams(
            dimension_semantics=("arbitrary",), vmem_limit_bytes=VMEM_LIMIT),
        name="dispatch",
    )(*sched, x1, meta, lst_rows)


def _expert_body(be_ref, nv_ref, nxt_ref, par_ref, xs_ref, wg_hbm, bg_ref, wu_hbm, bu_ref, wd_hbm, bd_ref,
                 y_ref, wbuf, wsem, wg_s, wu_s, wd_s):
    m = pl.program_id(0)
    valid = m < nv_ref[0]
    changed = jnp.logical_or(m == 0, be_ref[m] != be_ref[jnp.maximum(m - 1, 0)])

    def weight_copies(e, s):
        return [pltpu.make_async_copy(w.at[e], wbuf.at[s, k], wsem.at[s])
                for k, w in enumerate((wg_hbm, wu_hbm, wd_hbm))]

    @pl.when(m == 0)
    def _():
        for c in weight_copies(be_ref[0], par_ref[0]):
            c.start()

    @pl.when(jnp.logical_and(valid, changed))
    def _():
        s = par_ref[m]
        for c in weight_copies(be_ref[m], s):
            c.wait()

        @pl.when(nxt_ref[m] != be_ref[m])
        def _():
            for c in weight_copies(nxt_ref[m], 1 - s):
                c.start()
        wg_s[...] = wbuf[s, 0].astype(BF16)
        wu_s[...] = wbuf[s, 1].astype(BF16)
        wd_s[...] = wbuf[s, 2].astype(BF16)

    @pl.when(valid)
    def _():
        xb = xs_ref[:, 0:D_MODEL].astype(BF16)
        gate = (xs_ref[:, D_MODEL:D_MODEL + 1] + xs_ref[:, D_MODEL + 1:D_MODEL + 2]
                + xs_ref[:, D_MODEL + 2:D_MODEL + 3])
        hg = jnp.minimum(jnp.dot(xb, wg_s[...], preferred_element_type=F32) + bg_ref[0], SWIGLU_LIMIT)
        hu = jnp.clip(jnp.dot(xb, wu_s[...], preferred_element_type=F32) + bu_ref[0],
                      -SWIGLU_LIMIT, SWIGLU_LIMIT)
        hact = (hu + 1.0) * (hg * _sigmoid(SWIGLU_ALPHA * hg))
        y = jnp.dot(hact.astype(BF16), wd_s[...], preferred_element_type=F32) + bd_ref[0]
        y_ref[...] = y * gate

    @pl.when(jnp.logical_not(valid))
    def _():
        y_ref[...] = jnp.zeros_like(y_ref)


def _expert_call(block_expert, n_valid, next_expert, parity, xs, w_gate, b_gate, w_up, b_up, w_down, b_down):
    n_blocks = xs.shape[0] // EXPERT_BLOCK
    wspec = pl.BlockSpec(memory_space=pl.ANY)
    bspec = pl.BlockSpec((1, 1, D_MODEL), lambda m, be, *_: (be[m], 0, 0))
    grid_spec = pltpu.PrefetchScalarGridSpec(
        num_scalar_prefetch=4,
        grid=(n_blocks,),
        in_specs=[
            pl.BlockSpec((EXPERT_BLOCK, XS_WIDTH), lambda m, be, nv, *_: (jnp.minimum(m, nv[0] - 1), 0)),
            wspec, bspec, wspec, bspec, wspec, bspec,
        ],
        out_specs=pl.BlockSpec((EXPERT_BLOCK, D_MODEL), lambda m, *_: (m, 0)),
        scratch_shapes=[
            pltpu.VMEM((2, 3, D_MODEL, D_MODEL), F32),
            pltpu.SemaphoreType.DMA((2,)),
            pltpu.VMEM((D_MODEL, D_MODEL), BF16),
            pltpu.VMEM((D_MODEL, D_MODEL), BF16),
            pltpu.VMEM((D_MODEL, D_MODEL), BF16),
        ],
    )
    return pl.pallas_call(
        _expert_body,
        grid_spec=grid_spec,
        out_shape=jax.ShapeDtypeStruct((n_blocks * EXPERT_BLOCK, D_MODEL), F32),
        compiler_params=pltpu.CompilerParams(
            dimension_semantics=("arbitrary",), vmem_limit_bytes=VMEM_LIMIT),
        name="experts",
    )(block_expert, n_valid, next_expert, parity, xs, w_gate, b_gate.reshape(N_EXPERTS, 1, D_MODEL),
      w_up, b_up.reshape(N_EXPERTS, 1, D_MODEL), w_down, b_down.reshape(N_EXPERTS, 1, D_MODEL))


def _combine_body(units_ref, g8_ref, l8_ref, tot_ref,
                  y_hbm, x1_ref, meta_ref, lst_ref, g_ref, b_ref, out_ref, ys_s, sems):
    tau = pl.program_id(0)
    n = pl.num_programs(0)
    slot = lax.rem(tau, 2)
    TD, S = TILE_ROWS, SORT_ROWS

    def run_copy(s, g_unit, l_unit, n_units):
        return pltpu.make_async_copy(y_hbm.at[_rows(g_unit, n_units), :],
                                     ys_s.at[s, _rows(l_unit, n_units), :], sems.at[s])

    def fetch(tile, s):
        _for_each_run(units_ref, tile, lambda k, cnt: run_copy(s, g8_ref[k], l8_ref[k], cnt).start())

    @pl.when(tau == 0)
    def _():
        ys_s[...] = jnp.zeros_like(ys_s)
        fetch(0, 0)

    @pl.when(tau + 1 < n)
    def _():
        fetch(tau + 1, 1 - slot)

    run_copy(slot, 0, 0, tot_ref[tau]).wait()

    meta = meta_ref[...]
    lane_f = lax.broadcasted_iota(jnp.int32, (TD, LANES), 1).astype(F32)
    u_row = lst_ref[2:3, :]
    s_iota = lax.broadcasted_iota(jnp.int32, (TD, S), 1).astype(F32)
    conds = []
    for j in range(TOP_K):
        oh = lane_f == meta[:, j:j + 1]
        start8 = jnp.sum(jnp.where(oh, u_row, 0.0), axis=-1, keepdims=True)
        lpos = RUN_ALIGN * start8 + meta[:, TOP_K + j:TOP_K + j + 1]
        conds.append(s_iota == lpos)
    perm = jnp.where(conds[0], 1.0, jnp.where(conds[1], 1.0, jnp.where(conds[2], 1.0,
                     jnp.where(conds[3], 1.0, 0.0)))).astype(BF16)
    ys = ys_s[slot]
    yh = ys.astype(BF16)
    yl = (ys - yh.astype(F32)).astype(BF16)
    ffn = (jnp.dot(perm, yh, preferred_element_type=F32) + jnp.dot(perm, yl, preferred_element_type=F32))
    acc = DEEPNORM_ALPHA * x1_ref[...] + ffn
    mu = jnp.mean(acc, axis=-1, keepdims=True)
    hc = acc - mu
    var = jnp.mean(hc * hc, axis=-1, keepdims=True)
    out_ref[...] = hc * lax.rsqrt(var + LN_EPS) * g_ref[...] + b_ref[...]


def _combine_call(sched, y_rows, x1, meta, lst_rows, ln2_g, ln2_b):
    T = x1.shape[0]
    TD = TILE_ROWS
    grid_spec = pltpu.PrefetchScalarGridSpec(
        num_scalar_prefetch=4,
        grid=(T // TD,),
        in_specs=[
            pl.BlockSpec(memory_space=pl.ANY),
            pl.BlockSpec((TD, D_MODEL), lambda i, *_: (i, 0)),
            pl.BlockSpec((TD, LANES), lambda i, *_: (i, 0)),
            pl.BlockSpec((SUBLANES, LANES), lambda i, *_: (i, 0)),
            pl.BlockSpec((1, D_MODEL), lambda i, *_: (0, 0)),
            pl.BlockSpec((1, D_MODEL), lambda i, *_: (0, 0)),
        ],
        out_specs=pl.BlockSpec((TD, D_MODEL), lambda i, *_: (i, 0)),
        scratch_shapes=[pltpu.VMEM((2, SORT_ROWS, D_MODEL), F32), pltpu.SemaphoreType.DMA((2,))],
    )
    return pl.pallas_call(
        _combine_body,
        grid_spec=grid_spec,
        out_shape=jax.ShapeDtypeStruct((T, D_MODEL), F32),
        compiler_params=pltpu.CompilerParams(
            dimension_semantics=("arbitrary",), vmem_limit_bytes=VMEM_LIMIT),
        name="combine",
    )(*sched, y_rows, x1, meta, lst_rows, ln2_g, ln2_b)


def _np_consts():
    TL = TILE_ROWS
    r = np.arange(TL)
    same = (r[:, None] // CHUNK) == (r[None, :] // CHUNK)
    tril = (same & (r[None, :] <= r[:, None])).astype(np.float32)
    trils = (r[None, :] < r[:, None]).astype(np.float32)
    e128 = np.zeros((LANES, SSD_WIDTH), np.float32)
    for h in range(SSD_HEADS):
        e128[h, h * SSD_HEAD_DIM:(h + 1) * SSD_HEAD_DIM] = 1.0
    return tril, trils, e128


def kernel(x, w_in, hg_lower_bound, hg_norm_w, conv_w, conv_b, dt_bias, a_log, d_skip, ssd_norm_w, w_out,
           ln1_g, ln1_b, router_w, router_b, w_gate, b_gate, w_up, b_up, w_down, b_down, ln2_g, ln2_b):
    batch, seq, d = x.shape
    assert d == D_MODEL and seq % STEP_ROWS == 0 and w_in.shape[0] == DEPTH
    T = batch * seq
    n_tiles = T // TILE_ROWS
    max_rows = T * TOP_K + n_tiles * N_EXPERTS * (RUN_ALIGN - 1)
    n_blocks = -(-max_rows // EXPERT_BLOCK) + N_EXPERTS
    cap = n_blocks * EXPERT_BLOCK
    assert SORT_ROWS >= TILE_ROWS * TOP_K + N_EXPERTS * (RUN_ALIGN - 1)

    tril, trils, e128 = _np_consts()
    w = w_in[0]
    pad_l = LANES - SSD_HEADS
    rw = jnp.pad(router_w[0], ((0, 0), (0, LANES - N_EXPERTS)))
    rwh = rw.astype(BF16)
    rwm = (rw - rwh.astype(F32)).astype(BF16)
    consts = [
        jnp.pad(w, ((0, 0), (0, pad_l))).astype(BF16),
        hg_lower_bound,
        hg_norm_w[0].reshape(1, HG_HEAD_DIM),
        conv_w[0],
        conv_b[0].reshape(1, SSD_CONV_DIM),
        jnp.pad(dt_bias[0], (0, pad_l)).reshape(1, LANES),
        jnp.pad(a_log[0], (0, pad_l)).reshape(1, LANES),
        jnp.repeat(d_skip[0], SSD_HEAD_DIM).reshape(1, SSD_WIDTH),
        ssd_norm_w[0].reshape(1, SSD_WIDTH),
        w_out[0].astype(BF16),
        ln1_g[0].reshape(1, D_MODEL),
        ln1_b[0].reshape(1, D_MODEL),
        jnp.concatenate([rwh, rwm], axis=1),
        rwh,
        jnp.pad(router_b[0], (0, LANES - N_EXPERTS)).reshape(1, LANES),
        jnp.asarray(tril, BF16), jnp.asarray(trils, BF16), jnp.asarray(e128, BF16),
    ]
    x1, meta, cnt = _mixer_call(x.reshape(T, D_MODEL), consts, batch, seq)

    counts = cnt.reshape(n_tiles, SUBLANES, LANES)[:, 0, :N_EXPERTS].astype(jnp.int32)
    c8 = (counts + RUN_ALIGN - 1) // RUN_ALIGN * RUN_ALIGN
    used = jnp.sum(c8, axis=0)
    region = (used + EXPERT_BLOCK - 1) // EXPERT_BLOCK * EXPERT_BLOCK
    region_end = jnp.cumsum(region)
    region_start = region_end - region
    gstart = region_start[None, :] + jnp.cumsum(c8, axis=0) - c8
    lstart = jnp.cumsum(c8, axis=1) - c8
    n_valid = region_end[-1] // EXPERT_BLOCK
    blk_row = jnp.arange(n_blocks, dtype=jnp.int32) * EXPERT_BLOCK
    blk_row = jnp.minimum(blk_row, (n_valid - 1) * EXPERT_BLOCK)
    block_expert = jnp.minimum(jnp.sum(region_end[None, :] <= blk_row[:, None], axis=1),
                               N_EXPERTS - 1).astype(jnp.int32)
    has = region > 0
    eidx = jnp.arange(N_EXPERTS, dtype=jnp.int32)
    suffix_min = lax.cummin(jnp.where(has, eidx, N_EXPERTS), reverse=True)
    nxt_e = jnp.concatenate([suffix_min[1:], jnp.full((1,), N_EXPERTS, jnp.int32)])
    nxt_e = jnp.where(nxt_e == N_EXPERTS, eidx, nxt_e)
    par_e = (jnp.cumsum(has.astype(jnp.int32)) - 1) % 2
    blk_onehot = block_expert[:, None] == eidx[None, :]
    next_expert = jnp.sum(jnp.where(blk_onehot, nxt_e[None, :], 0), axis=1).astype(jnp.int32)
    parity = jnp.sum(jnp.where(blk_onehot, par_e[None, :], 0), axis=1).astype(jnp.int32)
    as_units = lambda a: (a // RUN_ALIGN).astype(jnp.int32).reshape(-1)
    units, g8, l8 = as_units(c8), as_units(gstart), as_units(lstart)
    tot = (jnp.sum(c8, axis=1) // RUN_ALIGN).astype(jnp.int32)
    tail8 = as_units(region_start + used)
    tailn = as_units(region - used)
    lu = lstart // RUN_ALIGN
    lst3 = jnp.stack([lu // LST_SPLIT, lu % LST_SPLIT, lu], axis=1).astype(F32)
    lst_rows = jnp.pad(lst3, ((0, 0), (0, SUBLANES - 3), (0, LANES - N_EXPERTS))).reshape(
        n_tiles * SUBLANES, LANES)

    nv = n_valid.astype(jnp.int32).reshape(1)
    xs = _dispatch_call((units, g8, l8, tot, tail8, tailn, nv), x1, meta, lst_rows, cap)
    y_rows = _expert_call(block_expert, nv, next_expert, parity, xs,
                          w_gate[0], b_gate[0], w_up[0], b_up[0], w_down[0], b_down[0])
    out = _combine_call((units, g8, l8, tot), y_rows, x1, meta, lst_rows,
                        ln2_g[0].reshape(1, D_MODEL), ln2_b[0].reshape(1, D_MODEL))
    return out.reshape(batch, seq, D_MODEL)
```

```python
import functools

import jax
import jax.numpy as jnp
import numpy as np
from jax import lax
from jax.experimental import pallas as pl
from jax.experimental.pallas import tpu as pltpu

F32 = jnp.float32
BF16 = jnp.bfloat16

D_MODEL = 1024
CHUNK = 64
HG_WIDTH = 512
HG_HEAD_DIM = 128
HG_HEADS = 4
SSD_WIDTH = 512
SSD_HEAD_DIM = 64
SSD_HEADS = 8
SSD_GROUPS = 2
SSD_STATE = 128
SSD_CONV = 4
SSD_CONV_DIM = SSD_WIDTH + 2 * SSD_GROUPS * SSD_STATE
N_EXPERTS = 32
TOP_K = 4
EXPERT_BLOCK = 256
SWIGLU_LIMIT = 7.0
SWIGLU_ALPHA = 1.702
DEPTH = 1
DEEPNORM_ALPHA = (2 * DEPTH) ** 0.25
LN_EPS = 1e-5
RMS_EPS = 1e-5

LANES = 128
SUBLANES = 8
SUB_CHUNK = 16
EXP_CAP = 60.0
TILE_ROWS = 256
MIX_TILES = 2
STEP_ROWS = MIX_TILES * TILE_ROWS
RUN_ALIGN = SUBLANES
SORT_ROWS = 1280
LST_SPLIT = 32
XS_WIDTH = D_MODEL + LANES
VMEM_LIMIT = 56 * 1024 * 1024

OFF_Q, OFF_F, OFF_I, OFF_G = 0, 512, 1024, 1536
OFF_Z, OFF_XBC = 2048, 2560
OFF_XS, OFF_B, OFF_C = 2560, 3072, 3328
OFF_DT = 3584
PROJ_COLS = OFF_DT + LANES
PROJ_SLICE = 256
CONV_SLICE = 256
OUT_SLICE = 256


def _bdot(a, b):
    return jnp.dot(a.astype(BF16), b.astype(BF16), preferred_element_type=F32)


def _bdot_nt(a, b):
    return lax.dot_general(a.astype(BF16), b.astype(BF16), (((1,), (1,)), ((), ())),
                           preferred_element_type=F32)


def _bdot_tn(a, b):
    return lax.dot_general(a.astype(BF16), b.astype(BF16), (((0,), (0,)), ((), ())),
                           preferred_element_type=F32)


def _split3(a):
    hi = a.astype(BF16)
    r1 = a - hi.astype(F32)
    mid = r1.astype(BF16)
    lo = (r1 - mid.astype(F32)).astype(BF16)
    return hi, mid, lo


def _dot01_left(m01, a):
    hi, mid, lo = _split3(a)
    d = functools.partial(jnp.dot, m01, preferred_element_type=F32)
    return d(hi) + d(mid) + d(lo)


def _dot01_right(a, m01):
    hi, mid, lo = _split3(a)
    return (jnp.dot(hi, m01, preferred_element_type=F32) + jnp.dot(mid, m01, preferred_element_type=F32)
            + jnp.dot(lo, m01, preferred_element_type=F32))


def _sigmoid(x):
    return 1.0 / (1.0 + jnp.exp(-x))


def _silu(x):
    return x * _sigmoid(x)


def _softplus(x):
    return jnp.maximum(x, 0.0) + jnp.log(1.0 + jnp.exp(-jnp.abs(x)))


def _mixer_body(x_ref, win_ref, lbp_ref, hgnw_ref, convw_ref, convb_ref, dtb_ref, alog_ref, dskip_ref,
                ssdnw_ref, wout_ref, ln1g_ref, ln1b_ref, rw1_ref, rwh_ref, rb_ref,
                tril_ref, trils_ref, e128_ref,
                x1_ref, meta_ref, cnt_ref,
                proj_s, xb_s, b_s, xpad_s, xdt_s, cse_s, st_s, pt_s, cat_s, ossd_s, mix_s):
    TL = TILE_ROWS
    t = pl.program_id(1)

    @pl.when(t == 0)
    def _():
        xpad_s[0:SUBLANES, :] = jnp.zeros((SUBLANES, SSD_CONV_DIM), F32)
        st_s[...] = jnp.zeros_like(st_s)
        pt_s[...] = jnp.zeros_like(pt_s)

    row64 = lax.broadcasted_iota(jnp.int32, (CHUNK, CHUNK), 0)
    col64 = lax.broadcasted_iota(jnp.int32, (CHUNK, CHUNK), 1)
    causal = row64 >= col64
    gw = SSD_WIDTH // SSD_GROUPS
    hpg = SSD_HEADS // SSD_GROUPS
    lane_head = lax.broadcasted_iota(jnp.int32, (CHUNK, gw), 1) // SSD_HEAD_DIM

    pending = []

    def filler():
        if pending:
            pending.pop(0)()

    def project_slices(base):
        def one(c0, c1):
            def run():
                proj_s[base:base + TL, c0:c1] = jnp.dot(xb_s[base:base + TL, :], win_ref[:, c0:c1],
                                                        preferred_element_type=F32)
            return run
        return [one(c0, min(c0 + PROJ_SLICE, PROJ_COLS)) for c0 in range(0, PROJ_COLS, PROJ_SLICE)]

    def hg_front(base):
        a0 = lbp_ref[0:1, :]
        a1 = lbp_ref[1:2, :]
        am = jnp.maximum(a0, a1)
        e0 = jnp.exp(a0 - am)
        e1 = jnp.exp(a1 - am)
        lb = e0 / (e0 + e1)
        f = lb + (1.0 - lb) * _sigmoid(proj_s[base:base + TL, OFF_F:OFF_F + HG_WIDTH])
        b_s[base:base + TL, :] = _dot01_left(tril_ref[...], jnp.log(f))
        proj_s[base:base + TL, OFF_F:OFF_F + HG_WIDTH] = f
        proj_s[base:base + TL, OFF_Q:OFF_Q + HG_WIDTH] = _silu(proj_s[base:base + TL, OFF_Q:OFF_Q + HG_WIDTH])
        units = {}
        for c in range(TL // CHUNK):
            r0 = base + c * CHUNK
            for h in range(HG_HEADS):
                h0 = h * HG_HEAD_DIM
                bc = b_s[r0:r0 + CHUNK, h0:h0 + HG_HEAD_DIM]
                qc = proj_s[r0:r0 + CHUNK, OFF_Q + h0:OFF_Q + h0 + HG_HEAD_DIM]
                kc = 1.0 - proj_s[r0:r0 + CHUNK, OFF_F + h0:OFF_F + h0 + HG_HEAD_DIM]
                vcb = proj_s[r0:r0 + CHUNK, OFF_I + h0:OFF_I + h0 + HG_HEAD_DIM].astype(BF16)
                parts = []
                for i in range(CHUNK // SUB_CHUNK):
                    s0 = i * SUB_CHUNK
                    if i == 0:
                        qi = qc[0:SUB_CHUNK] * jnp.exp(bc[0:SUB_CHUNK])
                        ki = kc * jnp.exp(jnp.minimum(-bc, EXP_CAP))
                    else:
                        ref_i = bc[s0 - 1:s0, :]
                        qi = qc[s0:s0 + SUB_CHUNK] * jnp.exp(bc[s0:s0 + SUB_CHUNK] - ref_i)
                        ki = kc * jnp.exp(jnp.minimum(ref_i - bc, EXP_CAP))
                    parts.append(_bdot_nt(qi, ki))
                b_end = bc[CHUNK - 1:CHUNK, :]
                kdec = kc * jnp.exp(b_end - bc)
                units[c, h] = dict(
                    parts=parts, vcb=vcb, qdec=(qc * jnp.exp(bc)).astype(BF16),
                    local=_bdot_tn(vcb, kdec),
                    decay=jnp.exp(b_end))
                if h % 2 == 1:
                    filler()
        return units

    def hg_back(base, units):
        hgnw = hgnw_ref[...]
        for h in range(HG_HEADS):
            st = st_s[h]
            for c in range(TL // CHUNK):
                u = units[c, h]
                u['state'] = st
                st = st * u['decay'] + u['local']
            st_s[h] = st
        for c in range(TL // CHUNK):
            r0 = base + c * CHUNK
            for h in range(HG_HEADS):
                h0 = h * HG_HEAD_DIM
                u = units[c, h]
                sc = jnp.where(causal, jnp.concatenate(u['parts'], axis=0), 0.0)
                o = _bdot(sc, u['vcb']) + _bdot_nt(u['qdec'], u['state'])
                ms = jnp.mean(o * o, axis=-1, keepdims=True)
                on = o * lax.rsqrt(ms + RMS_EPS) * hgnw
                gc = proj_s[r0:r0 + CHUNK, OFF_G + h0:OFF_G + h0 + HG_HEAD_DIM]
                cat_s[r0:r0 + CHUNK, h0:h0 + HG_HEAD_DIM] = (on * _silu(gc)).astype(BF16)
                filler()

    def ssd_front(base):
        e128 = e128_ref[...]
        dtc = _softplus(proj_s[base:base + TL, OFF_DT:OFF_DT + LANES] + dtb_ref[...])
        a_row = -jnp.exp(alog_ref[...])
        cs_c = _dot01_left(tril_ref[...], dtc * a_row)
        cs_r = jnp.transpose(cs_c)[0:SSD_HEADS, :]
        cse_s[base:base + TL, :] = _dot01_right(cs_c, e128)
        dt_exp = _dot01_right(dtc, e128)

        xpad_s[SUBLANES + base:SUBLANES + base + TL, :] = proj_s[base:base + TL, OFF_XBC:OFF_XBC + SSD_CONV_DIM]
        for c0 in range(0, SSD_CONV_DIM, CONV_SLICE):
            acc = jnp.broadcast_to(convb_ref[:, c0:c0 + CONV_SLICE], (TL, CONV_SLICE))
            for j in range(SSD_CONV):
                off = base + SUBLANES - (SSD_CONV - 1) + j
                acc = acc + convw_ref[j:j + 1, c0:c0 + CONV_SLICE] * xpad_s[off:off + TL, c0:c0 + CONV_SLICE]
            proj_s[base:base + TL, OFF_XBC + c0:OFF_XBC + c0 + CONV_SLICE] = _silu(acc)
            filler()
        xdt_s[base:base + TL, :] = proj_s[base:base + TL, OFF_XS:OFF_XS + SSD_WIDTH] * dt_exp

        units = {}
        for c in range(TL // CHUNK):
            r0 = base + c * CHUNK
            for g in range(SSD_GROUPS):
                g0 = g * gw
                bgb = proj_s[r0:r0 + CHUNK, OFF_B + g * SSD_STATE:OFF_B + (g + 1) * SSD_STATE].astype(BF16)
                cgb = proj_s[r0:r0 + CHUNK, OFF_C + g * SSD_STATE:OFF_C + (g + 1) * SSD_STATE].astype(BF16)
                cse_g = cse_s[r0:r0 + CHUNK, g0:g0 + gw]
                cs_end = cse_s[r0 + CHUNK - 1:r0 + CHUNK, g0:g0 + gw]
                xdt_g = xdt_s[r0:r0 + CHUNK, g0:g0 + gw]
                units[c, g] = dict(
                    cgb=cgb, gm=_bdot_nt(cgb, bgb),
                    local=_bdot_tn(bgb, xdt_g * jnp.exp(cs_end - cse_g)),
                    decay=jnp.exp(cs_end))
                filler()
        return units, cs_r

    def ssd_back(base, units, cs_r):
        for g in range(SSD_GROUPS):
            pt = pt_s[g]
            for c in range(TL // CHUNK):
                u = units[c, g]
                u['state'] = pt
                pt = pt * u['decay'] + u['local']
            pt_s[g] = pt
        for c in range(TL // CHUNK):
            r0 = base + c * CHUNK
            for g in range(SSD_GROUPS):
                g0 = g * gw
                u = units[c, g]
                cse_g = cse_s[r0:r0 + CHUNK, g0:g0 + gw]
                xdt_g = xdt_s[r0:r0 + CHUNK, g0:g0 + gw]
                ydiag = jnp.zeros((CHUNK, gw), F32)
                for hl in range(hpg):
                    hh = g * hpg + hl
                    seg = (cse_g[:, hl * SSD_HEAD_DIM:(hl + 1) * SSD_HEAD_DIM]
                           - cs_r[hh:hh + 1, c * CHUNK:(c + 1) * CHUNK])
                    lm = jnp.where(causal, jnp.exp(jnp.minimum(seg, 0.0)), 0.0)
                    xm = jnp.where(lane_head == hl, xdt_g, 0.0)
                    ydiag = ydiag + _bdot(u['gm'] * lm, xm)
                yoff = _bdot(u['cgb'], u['state']) * jnp.exp(cse_g)
                xs_g = proj_s[r0:r0 + CHUNK, OFF_XS + g0:OFF_XS + g0 + gw]
                ossd_s[r0:r0 + CHUNK, g0:g0 + gw] = ydiag + yoff + xs_g * dskip_ref[:, g0:g0 + gw]
                filler()

    def ssd_gate_norm(tile):
        base = tile * TL
        y = ossd_s[base:base + TL, :] * _silu(proj_s[base:base + TL, OFF_Z:OFF_Z + SSD_WIDTH])
        for g in range(SSD_GROUPS):
            yg = y[:, g * gw:(g + 1) * gw]
            ms = jnp.mean(yg * yg, axis=-1, keepdims=True)
            yn = yg * lax.rsqrt(ms + RMS_EPS) * ssdnw_ref[:, g * gw:(g + 1) * gw]
            cat_s[base:base + TL, HG_WIDTH + g * gw:HG_WIDTH + (g + 1) * gw] = yn.astype(BF16)

    def out_slice(tile, c0):
        base = tile * TL
        mix_s[base:base + TL, c0:c0 + OUT_SLICE] = jnp.dot(
            cat_s[base:base + TL, :], wout_ref[:, c0:c0 + OUT_SLICE], preferred_element_type=F32)

    def norm_route(tile):
        base = tile * TL
        hres = DEEPNORM_ALPHA * x_ref[base:base + TL, :] + mix_s[base:base + TL, :]
        mu = jnp.mean(hres, axis=-1, keepdims=True)
        hc = hres - mu
        var = jnp.mean(hc * hc, axis=-1, keepdims=True)
        x1 = hc * lax.rsqrt(var + LN_EPS) * ln1g_ref[...] + ln1b_ref[...]
        x1_ref[base:base + TL, :] = x1

        xh = x1.astype(BF16)
        xm_ = (x1 - xh.astype(F32)).astype(BF16)
        t1 = jnp.dot(xh, rw1_ref[...], preferred_element_type=F32)
        logits = (t1[:, 0:LANES] + t1[:, LANES:2 * LANES]
                  + jnp.dot(xm_, rwh_ref[...], preferred_element_type=F32) + rb_ref[...])
        lane = lax.broadcasted_iota(jnp.int32, (TL, LANES), 1)
        lane_f = lane.astype(F32)
        neg = jnp.float32(-jnp.inf)
        work = jnp.where(lane < N_EXPERTS, logits, neg)
        onehots, vals, idxs = [], [], []
        for j in range(TOP_K):
            m = jnp.max(work, axis=-1, keepdims=True)
            idx = jnp.min(jnp.where(work == m, lane_f, float(LANES)), axis=-1, keepdims=True)
            oh = lane_f == idx
            onehots.append(oh)
            vals.append(m)
            idxs.append(idx)
            work = jnp.where(oh, neg, work)
        es = [jnp.exp(v - vals[0]) for v in vals]
        den = es[0] + es[1] + es[2] + es[3]
        gates = [e / den for e in es]
        sel = jnp.zeros((TL, LANES), F32)
        for oh in onehots:
            sel = jnp.where(oh, 1.0, sel)
        rankmat = jnp.dot(trils_ref[...], sel.astype(BF16), preferred_element_type=F32)
        cnt_ref[tile * SUBLANES:(tile + 1) * SUBLANES, :] = jnp.broadcast_to(
            jnp.sum(sel, axis=0, keepdims=True), (SUBLANES, LANES))
        meta = jnp.zeros((TL, LANES), F32)
        for j in range(TOP_K):
            rank_j = jnp.sum(jnp.where(onehots[j], rankmat, 0.0), axis=-1, keepdims=True)
            meta = jnp.where(lane == j, idxs[j], meta)
            meta = jnp.where(lane == TOP_K + j, rank_j, meta)
            meta = jnp.where(lane == 2 * TOP_K + j, gates[j], meta)
        meta_ref[base:base + TL, :] = meta

    def post_thunks(tile):
        return ([functools.partial(ssd_gate_norm, tile)]
                + [functools.partial(out_slice, tile, c0) for c0 in range(0, D_MODEL, OUT_SLICE)]
                + [functools.partial(norm_route, tile)])

    xb_s[...] = x_ref[...].astype(BF16)
    for run in project_slices(0):
        run()
    for tile in range(MIX_TILES):
        base = tile * TL
        if tile + 1 < MIX_TILES:
            pending.extend(project_slices((tile + 1) * TL))
        hg_units = hg_front(base)
        ssd_units, cs_r = ssd_front(base)
        while pending:
            filler()
        hg_back(base, hg_units)
        ssd_back(base, ssd_units, cs_r)
        pending.extend(post_thunks(tile))
    while pending:
        filler()
    xpad_s[0:SUBLANES, :] = xpad_s[STEP_ROWS:STEP_ROWS + SUBLANES, :]


def _full(shape):
    nd = len(shape)
    return pl.BlockSpec(shape, lambda *_: (0,) * nd)


def _mixer_call(x2d, consts, batch, seq):
    SR = STEP_ROWS
    nt = seq // SR
    T = batch * seq
    in_specs = [pl.BlockSpec((SR, D_MODEL), lambda b, t: (b * nt + t, 0))]
    in_specs += [_full(c.shape) for c in consts]
    out_shape = (jax.ShapeDtypeStruct((T, D_MODEL), F32),
                 jax.ShapeDtypeStruct((T, LANES), F32),
                 jax.ShapeDtypeStruct((T // TILE_ROWS * SUBLANES, LANES), F32))
    out_specs = (pl.BlockSpec((SR, D_MODEL), lambda b, t: (b * nt + t, 0)),
                 pl.BlockSpec((SR, LANES), lambda b, t: (b * nt + t, 0)),
                 pl.BlockSpec((MIX_TILES * SUBLANES, LANES), lambda b, t: (b * nt + t, 0)))
    scratch = [
        pltpu.VMEM((SR, PROJ_COLS), F32),
        pltpu.VMEM((SR, D_MODEL), BF16),
        pltpu.VMEM((SR, HG_WIDTH), F32),
        pltpu.VMEM((SR + 2 * SUBLANES, SSD_CONV_DIM), F32),
        pltpu.VMEM((SR, SSD_WIDTH), F32),
        pltpu.VMEM((SR, SSD_WIDTH), F32),
        pltpu.VMEM((HG_HEADS, HG_HEAD_DIM, HG_HEAD_DIM), F32),
        pltpu.VMEM((SSD_GROUPS, SSD_STATE, SSD_WIDTH // SSD_GROUPS), F32),
        pltpu.VMEM((SR, D_MODEL), BF16),
        pltpu.VMEM((SR, SSD_WIDTH), F32),
        pltpu.VMEM((SR, D_MODEL), F32),
    ]
    return pl.pallas_call(
        _mixer_body,
        grid=(batch, nt),
        in_specs=in_specs,
        out_specs=out_specs,
        out_shape=out_shape,
        scratch_shapes=scratch,
        compiler_params=pltpu.CompilerParams(
            dimension_semantics=("arbitrary", "arbitrary"), vmem_limit_bytes=VMEM_LIMIT),
        name="mixer",
    )(x2d, *consts)


def _for_each_run(units_ref, tile, fn):
    def per_expert(e, carry):
        k = tile * N_EXPERTS + e
        n = units_ref[k]

        @pl.when(n > 0)
        def _():
            fn(k, n)
        return carry
    lax.fori_loop(0, N_EXPERTS, per_expert, 0)


def _rows(unit, n_units=1):
    start = unit * RUN_ALIGN
    if RUN_ALIGN > 1:
        start = pl.multiple_of(start, RUN_ALIGN)
    return pl.ds(start, n_units * RUN_ALIGN)


def _dispatch_body(units_ref, g8_ref, l8_ref, tot_ref, tail8_ref, tailn_ref, nv_ref,
                   x1_ref, meta_ref, lst_ref, xs_hbm, sorted_s, zero_s, sems, zsem, bsem):
    tau = pl.program_id(0)
    n = pl.num_programs(0)
    slot = lax.rem(tau, 2)
    TD, S = TILE_ROWS, SORT_ROWS
    n_blocks = xs_hbm.shape[0] // EXPERT_BLOCK

    def unused_block_copy(m):
        rows = pl.ds(pl.multiple_of(m * EXPERT_BLOCK, EXPERT_BLOCK), EXPERT_BLOCK)
        return pltpu.make_async_copy(zero_s, xs_hbm.at[rows, :], bsem)

    def run_copy(s, l_unit, g_unit, n_units):
        return pltpu.make_async_copy(sorted_s.at[s, _rows(l_unit, n_units), :],
                                     xs_hbm.at[_rows(g_unit, n_units), :], sems.at[s])

    def wait_units(s, count):
        run_copy(s, 0, 0, count).wait()

    @pl.when(tau == 0)
    def _():
        zero_s[...] = jnp.zeros_like(zero_s)

        def tail_copy(e):
            n_units = tailn_ref[e]
            return pltpu.make_async_copy(zero_s.at[pl.ds(0, n_units * RUN_ALIGN), :],
                                         xs_hbm.at[_rows(tail8_ref[e], n_units), :], zsem)

        def start_unused(m, carry):
            unused_block_copy(m).start()
            return carry
        lax.fori_loop(nv_ref[0], n_blocks, start_unused, 0)

        def start_e(e, carry):
            @pl.when(tailn_ref[e] > 0)
            def _():
                tail_copy(e).start()
            return carry
        lax.fori_loop(0, N_EXPERTS, start_e, 0)

        def wait_e(e, carry):
            @pl.when(tailn_ref[e] > 0)
            def _():
                tail_copy(e).wait()
            return carry
        lax.fori_loop(0, N_EXPERTS, wait_e, 0)

    @pl.when(tau >= 2)
    def _():
        wait_units(slot, tot_ref[jnp.maximum(tau - 2, 0)])

    meta = meta_ref[...]
    lane = lax.broadcasted_iota(jnp.int32, (TD, LANES), 1)
    lane_f = lane.astype(F32)
    u8 = lst_ref[...].astype(BF16)
    ones8 = jnp.ones((SUBLANES, LANES), BF16)
    r_iota = lax.broadcasted_iota(jnp.int32, (S, TD), 0).astype(F32)
    gh = meta.astype(BF16).astype(F32)
    g1 = meta - gh
    gm = g1.astype(BF16).astype(F32)
    gl = g1 - gm
    nt_dims = (((1,), (1,)), ((), ()))
    conds, gparts = [], []
    for j in range(TOP_K):
        oh = lane_f == meta[:, j:j + 1]
        ohb = jnp.where(oh, 1.0, 0.0).astype(BF16)
        rkb = jnp.where(oh, meta[:, TOP_K + j:TOP_K + j + 1], 0.0).astype(BF16)
        m1 = lax.dot_general(u8, ohb, nt_dims, preferred_element_type=F32)
        m2 = lax.dot_general(ones8, rkb, nt_dims, preferred_element_type=F32)
        lpos = RUN_ALIGN * (LST_SPLIT * m1[0:1, :] + m1[1:2, :]) + m2[0:1, :]
        conds.append(r_iota == lpos)
        gc = 2 * TOP_K + j
        gparts.append(jnp.where(lane == 0, gh[:, gc:gc + 1],
                                jnp.where(lane == 1, gm[:, gc:gc + 1],
                                          jnp.where(lane == 2, gl[:, gc:gc + 1], 0.0))).astype(BF16))
    pcat = jnp.concatenate([jnp.where(c, 1.0, 0.0).astype(BF16) for c in conds], axis=1)
    sorted_g = jnp.dot(pcat, jnp.concatenate(gparts, axis=0), preferred_element_type=F32)
    perm = jnp.where(conds[0], 1.0, jnp.where(conds[1], 1.0, jnp.where(conds[2], 1.0,
                     jnp.where(conds[3], 1.0, 0.0)))).astype(BF16)
    sorted_x = jnp.dot(perm, x1_ref[...].astype(BF16), preferred_element_type=F32)
    sorted_s[slot, :, 0:D_MODEL] = sorted_x
    sorted_s[slot, :, D_MODEL:XS_WIDTH] = sorted_g

    _for_each_run(units_ref, tau, lambda k, cnt: run_copy(slot, l8_ref[k], g8_ref[k], cnt).start())

    @pl.when(tau == n - 1)
    def _():
        @pl.when(tau >= 1)
        def _():
            wait_units(1 - slot, tot_ref[jnp.maximum(tau - 1, 0)])
        wait_units(slot, tot_ref[tau])

        def wait_unused(m, carry):
            unused_block_copy(m).wait()
            return carry
        lax.fori_loop(nv_ref[0], n_blocks, wait_unused, 0)


def _dispatch_call(sched, x1, meta, lst_rows, cap):
    T = x1.shape[0]
    TD = TILE_ROWS
    grid_spec = pltpu.PrefetchScalarGridSpec(
        num_scalar_prefetch=7,
        grid=(T // TD,),
        in_specs=[
            pl.BlockSpec((TD, D_MODEL), lambda i, *_: (i, 0)),
            pl.BlockSpec((TD, LANES), lambda i, *_: (i, 0)),
            pl.BlockSpec((SUBLANES, LANES), lambda i, *_: (i, 0)),
        ],
        out_specs=pl.BlockSpec(memory_space=pl.ANY),
        scratch_shapes=[
            pltpu.VMEM((2, SORT_ROWS, XS_WIDTH), F32),
            pltpu.VMEM((EXPERT_BLOCK, XS_WIDTH), F32),
            pltpu.SemaphoreType.DMA((2,)),
            pltpu.SemaphoreType.DMA(()),
            pltpu.SemaphoreType.DMA(()),
        ],
    )
    return pl.pallas_call(
        _dispatch_body,
        grid_spec=grid_spec,
        out_shape=jax.ShapeDtypeStruct((cap, XS_WIDTH), F32),
        compiler_params=pltpu.CompilerParams(
            dimension_semantics=("arbitrary",), vmem_limit_bytes=VMEM_LIMIT),
        name="dispatch",
    )(*sched, x1, meta, lst_rows)


def _expert_body(be_ref, nv_ref, nxt_ref, par_ref, xs_hbm, wg_hbm, bg_ref, wu_hbm, bu_ref, wd_hbm, bd_ref,
                 y_hbm, xbuf, ybuf, zbuf, wbuf, xsem, ysem, zsem, wsem, wg_s, wu_s, wd_s):
    nv = nv_ref[0]
    n_blocks = xs_hbm.shape[0] // EXPERT_BLOCK

    def blk(m):
        return pl.ds(pl.multiple_of(m * EXPERT_BLOCK, EXPERT_BLOCK), EXPERT_BLOCK)

    def x_copy(m, s):
        return pltpu.make_async_copy(xs_hbm.at[blk(m), :], xbuf.at[s], xsem.at[s])

    def y_copy(m, s):
        return pltpu.make_async_copy(ybuf.at[s], y_hbm.at[blk(m), :], ysem.at[s])

    def zero_copy(m):
        return pltpu.make_async_copy(zbuf, y_hbm.at[blk(m), :], zsem)

    def weight_copies(e, s):
        return [pltpu.make_async_copy(w.at[e], wbuf.at[s, k], wsem.at[s])
                for k, w in enumerate((wg_hbm, wu_hbm, wd_hbm))]

    for c in weight_copies(be_ref[0], par_ref[0]):
        c.start()
    x_copy(0, 0).start()

    zbuf[...] = jnp.zeros_like(zbuf)

    def start_zero(m, carry):
        zero_copy(m).start()
        return carry
    lax.fori_loop(nv, n_blocks, start_zero, 0)

    def body(m, carry):
        s = lax.rem(m, 2)
        e = be_ref[m]
        x_copy(m, s).wait()

        @pl.when(m + 1 < nv)
        def _():
            x_copy(m + 1, 1 - s).start()

        @pl.when(jnp.logical_or(m == 0, e != be_ref[jnp.maximum(m - 1, 0)]))
        def _():
            ws = par_ref[m]
            for c in weight_copies(e, ws):
                c.wait()

            @pl.when(nxt_ref[m] != e)
            def _():
                for c in weight_copies(nxt_ref[m], 1 - ws):
                    c.start()
            wg_s[...] = wbuf[ws, 0].astype(BF16)
            wu_s[...] = wbuf[ws, 1].astype(BF16)
            wd_s[...] = wbuf[ws, 2].astype(BF16)

        @pl.when(m >= 2)
        def _():
            y_copy(m - 2, s).wait()

        xb = xbuf[s, :, 0:D_MODEL].astype(BF16)
        gate = (xbuf[s, :, D_MODEL:D_MODEL + 1] + xbuf[s, :, D_MODEL + 1:D_MODEL + 2]
                + xbuf[s, :, D_MODEL + 2:D_MODEL + 3])
        hg = jnp.minimum(jnp.dot(xb, wg_s[...], preferred_element_type=F32) + bg_ref[e], SWIGLU_LIMIT)
        hu = jnp.clip(jnp.dot(xb, wu_s[...], preferred_element_type=F32) + bu_ref[e],
                      -SWIGLU_LIMIT, SWIGLU_LIMIT)
        hact = (hu + 1.0) * (hg * _sigmoid(SWIGLU_ALPHA * hg))
        y = jnp.dot(hact.astype(BF16), wd_s[...], preferred_element_type=F32) + bd_ref[e]
        ybuf[s] = y * gate
        y_copy(m, s).start()
        return carry
    lax.fori_loop(0, nv, body, 0)

    @pl.when(nv >= 2)
    def _():
        y_copy(nv - 2, lax.rem(nv, 2)).wait()
    y_copy(nv - 1, lax.rem(nv - 1, 2)).wait()

    def wait_zero(m, carry):
        zero_copy(m).wait()
        return carry
    lax.fori_loop(nv, n_blocks, wait_zero, 0)


def _expert_call(block_expert, n_valid, next_expert, parity, xs, w_gate, b_gate, w_up, b_up, w_down, b_down):
    n_blocks = xs.shape[0] // EXPERT_BLOCK
    anyspec = pl.BlockSpec(memory_space=pl.ANY)
    bspec = pl.BlockSpec((N_EXPERTS, 1, D_MODEL), lambda i, *_: (0, 0, 0))
    grid_spec = pltpu.PrefetchScalarGridSpec(
        num_scalar_prefetch=4,
        grid=(1,),
        in_specs=[anyspec, anyspec, bspec, anyspec, bspec, anyspec, bspec],
        out_specs=anyspec,
        scratch_shapes=[
            pltpu.VMEM((2, EXPERT_BLOCK, XS_WIDTH), F32),
            pltpu.VMEM((2, EXPERT_BLOCK, D_MODEL), F32),
            pltpu.VMEM((EXPERT_BLOCK, D_MODEL), F32),
            pltpu.VMEM((2, 3, D_MODEL, D_MODEL), F32),
            pltpu.SemaphoreType.DMA((2,)),
            pltpu.SemaphoreType.DMA((2,)),
            pltpu.SemaphoreType.DMA(()),
            pltpu.SemaphoreType.DMA((2,)),
            pltpu.VMEM((D_MODEL, D_MODEL), BF16),
            pltpu.VMEM((D_MODEL, D_MODEL), BF16),
            pltpu.VMEM((D_MODEL, D_MODEL), BF16),
        ],
    )
    return pl.pallas_call(
        _expert_body,
        grid_spec=grid_spec,
        out_shape=jax.ShapeDtypeStruct((n_blocks * EXPERT_BLOCK, D_MODEL), F32),
        compiler_params=pltpu.CompilerParams(
            dimension_semantics=("arbitrary",), vmem_limit_bytes=VMEM_LIMIT),
        name="experts",
    )(block_expert, n_valid, next_expert, parity, xs, w_gate, b_gate.reshape(N_EXPERTS, 1, D_MODEL),
      w_up, b_up.reshape(N_EXPERTS, 1, D_MODEL), w_down, b_down.reshape(N_EXPERTS, 1, D_MODEL))


def _combine_body(units_ref, g8_ref, l8_ref, tot_ref,
                  y_hbm, x1_ref, meta_ref, lst_ref, g_ref, b_ref, out_ref, ys_s, sems):
    tau = pl.program_id(0)
    n = pl.num_programs(0)
    slot = lax.rem(tau, 2)
    TD, S = TILE_ROWS, SORT_ROWS

    def run_copy(s, g_unit, l_unit, n_units):
        return pltpu.make_async_copy(y_hbm.at[_rows(g_unit, n_units), :],
                                     ys_s.at[s, _rows(l_unit, n_units), :], sems.at[s])

    def fetch(tile, s):
        _for_each_run(units_ref, tile, lambda k, cnt: run_copy(s, g8_ref[k], l8_ref[k], cnt).start())

    @pl.when(tau == 0)
    def _():
        ys_s[...] = jnp.zeros_like(ys_s)
        fetch(0, 0)

    @pl.when(tau + 1 < n)
    def _():
        fetch(tau + 1, 1 - slot)

    run_copy(slot, 0, 0, tot_ref[tau]).wait()

    meta = meta_ref[...]
    lane_f = lax.broadcasted_iota(jnp.int32, (TD, LANES), 1).astype(F32)
    u_row = lst_ref[2:3, :]
    s_iota = lax.broadcasted_iota(jnp.int32, (TD, S), 1).astype(F32)
    conds = []
    for j in range(TOP_K):
        oh = lane_f == meta[:, j:j + 1]
        start8 = jnp.sum(jnp.where(oh, u_row, 0.0), axis=-1, keepdims=True)
        lpos = RUN_ALIGN * start8 + meta[:, TOP_K + j:TOP_K + j + 1]
        conds.append(s_iota == lpos)
    perm = jnp.where(conds[0], 1.0, jnp.where(conds[1], 1.0, jnp.where(conds[2], 1.0,
                     jnp.where(conds[3], 1.0, 0.0)))).astype(BF16)
    ys = ys_s[slot]
    yh = ys.astype(BF16)
    yl = (ys - yh.astype(F32)).astype(BF16)
    ffn = (jnp.dot(perm, yh, preferred_element_type=F32) + jnp.dot(perm, yl, preferred_element_type=F32))
    acc = DEEPNORM_ALPHA * x1_ref[...] + ffn
    mu = jnp.mean(acc, axis=-1, keepdims=True)
    hc = acc - mu
    var = jnp.mean(hc * hc, axis=-1, keepdims=True)
    out_ref[...] = hc * lax.rsqrt(var + LN_EPS) * g_ref[...] + b_ref[...]


def _combine_call(sched, y_rows, x1, meta, lst_rows, ln2_g, ln2_b):
    T = x1.shape[0]
    TD = TILE_ROWS
    grid_spec = pltpu.PrefetchScalarGridSpec(
        num_scalar_prefetch=4,
        grid=(T // TD,),
        in_specs=[
            pl.BlockSpec(memory_space=pl.ANY),
            pl.BlockSpec((TD, D_MODEL), lambda i, *_: (i, 0)),
            pl.BlockSpec((TD, LANES), lambda i, *_: (i, 0)),
            pl.BlockSpec((SUBLANES, LANES), lambda i, *_: (i, 0)),
            pl.BlockSpec((1, D_MODEL), lambda i, *_: (0, 0)),
            pl.BlockSpec((1, D_MODEL), lambda i, *_: (0, 0)),
        ],
        out_specs=pl.BlockSpec((TD, D_MODEL), lambda i, *_: (i, 0)),
        scratch_shapes=[pltpu.VMEM((2, SORT_ROWS, D_MODEL), F32), pltpu.SemaphoreType.DMA((2,))],
    )
    return pl.pallas_call(
        _combine_body,
        grid_spec=grid_spec,
        out_shape=jax.ShapeDtypeStruct((T, D_MODEL), F32),
        compiler_params=pltpu.CompilerParams(
            dimension_semantics=("arbitrary",), vmem_limit_bytes=VMEM_LIMIT),
        name="combine",
    )(*sched, y_rows, x1, meta, lst_rows, ln2_g, ln2_b)


def _np_consts():
    TL = TILE_ROWS
    r = np.arange(TL)
    same = (r[:, None] // CHUNK) == (r[None, :] // CHUNK)
    tril = (same & (r[None, :] <= r[:, None])).astype(np.float32)
    trils = (r[None, :] < r[:, None]).astype(np.float32)
    e128 = np.zeros((LANES, SSD_WIDTH), np.float32)
    for h in range(SSD_HEADS):
        e128[h, h * SSD_HEAD_DIM:(h + 1) * SSD_HEAD_DIM] = 1.0
    return tril, trils, e128


def kernel(x, w_in, hg_lower_bound, hg_norm_w, conv_w, conv_b, dt_bias, a_log, d_skip, ssd_norm_w, w_out,
           ln1_g, ln1_b, router_w, router_b, w_gate, b_gate, w_up, b_up, w_down, b_down, ln2_g, ln2_b):
    batch, seq, d = x.shape
    assert d == D_MODEL and seq % STEP_ROWS == 0 and w_in.shape[0] == DEPTH
    T = batch * seq
    n_tiles = T // TILE_ROWS
    max_rows = T * TOP_K + n_tiles * N_EXPERTS * (RUN_ALIGN - 1)
    n_blocks = -(-max_rows // EXPERT_BLOCK) + N_EXPERTS
    cap = n_blocks * EXPERT_BLOCK
    assert SORT_ROWS >= TILE_ROWS * TOP_K + N_EXPERTS * (RUN_ALIGN - 1)

    tril, trils, e128 = _np_consts()
    w = w_in[0]
    pad_l = LANES - SSD_HEADS
    rw = jnp.pad(router_w[0], ((0, 0), (0, LANES - N_EXPERTS)))
    rwh = rw.astype(BF16)
    rwm = (rw - rwh.astype(F32)).astype(BF16)
    consts = [
        jnp.pad(w, ((0, 0), (0, pad_l))).astype(BF16),
        hg_lower_bound,
        hg_norm_w[0].reshape(1, HG_HEAD_DIM),
        conv_w[0],
        conv_b[0].reshape(1, SSD_CONV_DIM),
        jnp.pad(dt_bias[0], (0, pad_l)).reshape(1, LANES),
        jnp.pad(a_log[0], (0, pad_l)).reshape(1, LANES),
        jnp.repeat(d_skip[0], SSD_HEAD_DIM).reshape(1, SSD_WIDTH),
        ssd_norm_w[0].reshape(1, SSD_WIDTH),
        w_out[0].astype(BF16),
        ln1_g[0].reshape(1, D_MODEL),
        ln1_b[0].reshape(1, D_MODEL),
        jnp.concatenate([rwh, rwm], axis=1),
        rwh,
        jnp.pad(router_b[0], (0, LANES - N_EXPERTS)).reshape(1, LANES),
        jnp.asarray(tril, BF16), jnp.asarray(trils, BF16), jnp.asarray(e128, BF16),
    ]
    x1, meta, cnt = _mixer_call(x.reshape(T, D_MODEL), consts, batch, seq)

    counts = cnt.reshape(n_tiles, SUBLANES, LANES)[:, 0, :N_EXPERTS].astype(jnp.int32)
    c8 = (counts + RUN_ALIGN - 1) // RUN_ALIGN * RUN_ALIGN
    used = jnp.sum(c8, axis=0)
    region = (used + EXPERT_BLOCK - 1) // EXPERT_BLOCK * EXPERT_BLOCK
    region_end = jnp.cumsum(region)
    region_start = region_end - region
    gstart = region_start[None, :] + jnp.cumsum(c8, axis=0) - c8
    lstart = jnp.cumsum(c8, axis=1) - c8
    n_valid = region_end[-1] // EXPERT_BLOCK
    blk_row = jnp.arange(n_blocks, dtype=jnp.int32) * EXPERT_BLOCK
    blk_row = jnp.minimum(blk_row, (n_valid - 1) * EXPERT_BLOCK)
    block_expert = jnp.minimum(jnp.sum(region_end[None, :] <= blk_row[:, None], axis=1),
                               N_EXPERTS - 1).astype(jnp.int32)
    has = region > 0
    eidx = jnp.arange(N_EXPERTS, dtype=jnp.int32)
    suffix_min = lax.cummin(jnp.where(has, eidx, N_EXPERTS), reverse=True)
    nxt_e = jnp.concatenate([suffix_min[1:], jnp.full((1,), N_EXPERTS, jnp.int32)])
    nxt_e = jnp.where(nxt_e == N_EXPERTS, eidx, nxt_e)
    par_e = (jnp.cumsum(has.astype(jnp.int32)) - 1) % 2
    blk_onehot = block_expert[:, None] == eidx[None, :]
    next_expert = jnp.sum(jnp.where(blk_onehot, nxt_e[None, :], 0), axis=1).astype(jnp.int32)
    parity = jnp.sum(jnp.where(blk_onehot, par_e[None, :], 0), axis=1).astype(jnp.int32)
    as_units = lambda a: (a // RUN_ALIGN).astype(jnp.int32).reshape(-1)
    units, g8, l8 = as_units(c8), as_units(gstart), as_units(lstart)
    tot = (jnp.sum(c8, axis=1) // RUN_ALIGN).astype(jnp.int32)
    tail8 = as_units(region_start + used)
    tailn = as_units(region - used)
    lu = lstart // RUN_ALIGN
    lst3 = jnp.stack([lu // LST_SPLIT, lu % LST_SPLIT, lu], axis=1).astype(F32)
    lst_rows = jnp.pad(lst3, ((0, 0), (0, SUBLANES - 3), (0, LANES - N_EXPERTS))).reshape(
        n_tiles * SUBLANES, LANES)

    nv = n_valid.astype(jnp.int32).reshape(1)
    xs = _dispatch_call((units, g8, l8, tot, tail8, tailn, nv), x1, meta, lst_rows, cap)
    y_rows = _expert_call(block_expert, nv, next_expert, parity, xs,
                          w_gate[0], b_gate[0], w_up[0], b_up[0], w_down[0], b_down[0])
    out = _combine_call((units, g8, l8, tot), y_rows, x1, meta, lst_rows,
                        ln2_g[0].reshape(1, D_MODEL), ln2_b[0].reshape(1, D_MODEL))
    return out.reshape(batch, seq, D_MODEL)
```

```python
import functools

import jax
import jax.numpy as jnp
import numpy as np
from jax import lax
from jax.experimental import pallas as pl
from jax.experimental.pallas import tpu as pltpu

F32 = jnp.float32
BF16 = jnp.bfloat16

D_MODEL = 1024
CHUNK = 64
HG_WIDTH = 512
HG_HEAD_DIM = 128
HG_HEADS = 4
SSD_WIDTH = 512
SSD_HEAD_DIM = 64
SSD_HEADS = 8
SSD_GROUPS = 2
SSD_STATE = 128
SSD_CONV = 4
SSD_CONV_DIM = SSD_WIDTH + 2 * SSD_GROUPS * SSD_STATE
N_EXPERTS = 32
TOP_K = 4
EXPERT_BLOCK = 512
SWIGLU_LIMIT = 7.0
SWIGLU_ALPHA = 1.702
DEPTH = 1
DEEPNORM_ALPHA = (2 * DEPTH) ** 0.25
LN_EPS = 1e-5
RMS_EPS = 1e-5

LANES = 128
SUBLANES = 8
SUB_CHUNK = 16
EXP_CAP = 60.0
TILE_ROWS = 256
MIX_TILES = 2
STEP_ROWS = MIX_TILES * TILE_ROWS
RUN_ALIGN = SUBLANES
SORT_ROWS = 1280
LST_SPLIT = 32
XS_WIDTH = D_MODEL + LANES
VMEM_LIMIT = 56 * 1024 * 1024

OFF_Q, OFF_F, OFF_I, OFF_G = 0, 512, 1024, 1536
OFF_Z, OFF_XBC = 2048, 2560
OFF_XS, OFF_B, OFF_C = 2560, 3072, 3328
OFF_DT = 3584
PROJ_COLS = OFF_DT + LANES
PROJ_SLICE = 256
CONV_SLICE = 256
OUT_SLICE = 256


def _bdot(a, b):
    return jnp.dot(a.astype(BF16), b.astype(BF16), preferred_element_type=F32)


def _bdot_nt(a, b):
    return lax.dot_general(a.astype(BF16), b.astype(BF16), (((1,), (1,)), ((), ())),
                           preferred_element_type=F32)


def _bdot_tn(a, b):
    return lax.dot_general(a.astype(BF16), b.astype(BF16), (((0,), (0,)), ((), ())),
                           preferred_element_type=F32)


def _split3(a):
    hi = a.astype(BF16)
    r1 = a - hi.astype(F32)
    mid = r1.astype(BF16)
    lo = (r1 - mid.astype(F32)).astype(BF16)
    return hi, mid, lo


def _dot01_left(m01, a):
    hi, mid, lo = _split3(a)
    d = functools.partial(jnp.dot, m01, preferred_element_type=F32)
    return d(hi) + d(mid) + d(lo)


def _dot01_right(a, m01):
    hi, mid, lo = _split3(a)
    return (jnp.dot(hi, m01, preferred_element_type=F32) + jnp.dot(mid, m01, preferred_element_type=F32)
            + jnp.dot(lo, m01, preferred_element_type=F32))


def _sigmoid(x):
    return 1.0 / (1.0 + jnp.exp(-x))


def _silu(x):
    return x * _sigmoid(x)


def _softplus(x):
    return jnp.maximum(x, 0.0) + jnp.log(1.0 + jnp.exp(-jnp.abs(x)))


def _mixer_body(x_ref, win_ref, lbp_ref, hgnw_ref, convw_ref, convb_ref, dtb_ref, alog_ref, dskip_ref,
                ssdnw_ref, wout_ref, ln1g_ref, ln1b_ref, rw1_ref, rwh_ref, rb_ref,
                tril_ref, trils_ref, e128_ref,
                x1_ref, meta_ref, cnt_ref,
                proj_s, xb_s, b_s, xpad_s, xdt_s, cse_s, st_s, pt_s, cat_s, ossd_s, mix_s):
    TL = TILE_ROWS
    t = pl.program_id(1)

    @pl.when(t == 0)
    def _():
        xpad_s[0:SUBLANES, :] = jnp.zeros((SUBLANES, SSD_CONV_DIM), F32)
        st_s[...] = jnp.zeros_like(st_s)
        pt_s[...] = jnp.zeros_like(pt_s)

    row64 = lax.broadcasted_iota(jnp.int32, (CHUNK, CHUNK), 0)
    col64 = lax.broadcasted_iota(jnp.int32, (CHUNK, CHUNK), 1)
    causal = row64 >= col64
    gw = SSD_WIDTH // SSD_GROUPS
    hpg = SSD_HEADS // SSD_GROUPS
    lane_head = lax.broadcasted_iota(jnp.int32, (CHUNK, gw), 1) // SSD_HEAD_DIM

    pending = []

    def filler():
        if pending:
            pending.pop(0)()

    def project_slices(base):
        def one(c0, c1):
            def run():
                proj_s[base:base + TL, c0:c1] = jnp.dot(xb_s[base:base + TL, :], win_ref[:, c0:c1],
                                                        preferred_element_type=F32)
            return run
        return [one(c0, min(c0 + PROJ_SLICE, PROJ_COLS)) for c0 in range(0, PROJ_COLS, PROJ_SLICE)]

    def hg_front(base):
        a0 = lbp_ref[0:1, :]
        a1 = lbp_ref[1:2, :]
        am = jnp.maximum(a0, a1)
        e0 = jnp.exp(a0 - am)
        e1 = jnp.exp(a1 - am)
        lb = e0 / (e0 + e1)
        f = lb + (1.0 - lb) * _sigmoid(proj_s[base:base + TL, OFF_F:OFF_F + HG_WIDTH])
        b_s[base:base + TL, :] = _dot01_left(tril_ref[...], jnp.log(f))
        proj_s[base:base + TL, OFF_F:OFF_F + HG_WIDTH] = f
        proj_s[base:base + TL, OFF_Q:OFF_Q + HG_WIDTH] = _silu(proj_s[base:base + TL, OFF_Q:OFF_Q + HG_WIDTH])
        units = {}
        for c in range(TL // CHUNK):
            r0 = base + c * CHUNK
            for h in range(HG_HEADS):
                h0 = h * HG_HEAD_DIM
                bc = b_s[r0:r0 + CHUNK, h0:h0 + HG_HEAD_DIM]
                qc = proj_s[r0:r0 + CHUNK, OFF_Q + h0:OFF_Q + h0 + HG_HEAD_DIM]
                kc = 1.0 - proj_s[r0:r0 + CHUNK, OFF_F + h0:OFF_F + h0 + HG_HEAD_DIM]
                vcb = proj_s[r0:r0 + CHUNK, OFF_I + h0:OFF_I + h0 + HG_HEAD_DIM].astype(BF16)
                parts = []
                for i in range(CHUNK // SUB_CHUNK):
                    s0 = i * SUB_CHUNK
                    if i == 0:
                        qi = qc[0:SUB_CHUNK] * jnp.exp(bc[0:SUB_CHUNK])
                        ki = kc * jnp.exp(jnp.minimum(-bc, EXP_CAP))
                    else:
                        ref_i = bc[s0 - 1:s0, :]
                        qi = qc[s0:s0 + SUB_CHUNK] * jnp.exp(bc[s0:s0 + SUB_CHUNK] - ref_i)
                        ki = kc * jnp.exp(jnp.minimum(ref_i - bc, EXP_CAP))
                    parts.append(_bdot_nt(qi, ki))
                b_end = bc[CHUNK - 1:CHUNK, :]
                kdec = kc * jnp.exp(b_end - bc)
                units[c, h] = dict(
                    parts=parts, vcb=vcb, qdec=(qc * jnp.exp(bc)).astype(BF16),
                    local=_bdot_tn(vcb, kdec),
                    decay=jnp.exp(b_end))
                if h % 2 == 1:
                    filler()
        return units

    def hg_back(base, units):
        hgnw = hgnw_ref[...]
        for h in range(HG_HEADS):
            st = st_s[h]
            for c in range(TL // CHUNK):
                u = units[c, h]
                u['state'] = st
                st = st * u['decay'] + u['local']
            st_s[h] = st
        for c in range(TL // CHUNK):
            r0 = base + c * CHUNK
            for h in range(HG_HEADS):
                h0 = h * HG_HEAD_DIM
                u = units[c, h]
                sc = jnp.where(causal, jnp.concatenate(u['parts'], axis=0), 0.0)
                o = _bdot(sc, u['vcb']) + _bdot_nt(u['qdec'], u['state'])
                ms = jnp.mean(o * o, axis=-1, keepdims=True)
                on = o * lax.rsqrt(ms + RMS_EPS) * hgnw
                gc = proj_s[r0:r0 + CHUNK, OFF_G + h0:OFF_G + h0 + HG_HEAD_DIM]
                cat_s[r0:r0 + CHUNK, h0:h0 + HG_HEAD_DIM] = (on * _silu(gc)).astype(BF16)
                filler()

    def ssd_front(base):
        e128 = e128_ref[...]
        dtc = _softplus(proj_s[base:base + TL, OFF_DT:OFF_DT + LANES] + dtb_ref[...])
        a_row = -jnp.exp(alog_ref[...])
        cs_c = _dot01_left(tril_ref[...], dtc * a_row)
        cs_r = jnp.transpose(cs_c)[0:SSD_HEADS, :]
        cse_s[base:base + TL, :] = _dot01_right(cs_c, e128)
        dt_exp = _dot01_right(dtc, e128)

        xpad_s[SUBLANES + base:SUBLANES + base + TL, :] = proj_s[base:base + TL, OFF_XBC:OFF_XBC + SSD_CONV_DIM]
        for c0 in range(0, SSD_CONV_DIM, CONV_SLICE):
            acc = jnp.broadcast_to(convb_ref[:, c0:c0 + CONV_SLICE], (TL, CONV_SLICE))
            for j in range(SSD_CONV):
                off = base + SUBLANES - (SSD_CONV - 1) + j
                acc = acc + convw_ref[j:j + 1, c0:c0 + CONV_SLICE] * xpad_s[off:off + TL, c0:c0 + CONV_SLICE]
            proj_s[base:base + TL, OFF_XBC + c0:OFF_XBC + c0 + CONV_SLICE] = _silu(acc)
            filler()
        xdt_s[base:base + TL, :] = proj_s[base:base + TL, OFF_XS:OFF_XS + SSD_WIDTH] * dt_exp

        units = {}
        for c in range(TL // CHUNK):
            r0 = base + c * CHUNK
            for g in range(SSD_GROUPS):
                g0 = g * gw
                bgb = proj_s[r0:r0 + CHUNK, OFF_B + g * SSD_STATE:OFF_B + (g + 1) * SSD_STATE].astype(BF16)
                cgb = proj_s[r0:r0 + CHUNK, OFF_C + g * SSD_STATE:OFF_C + (g + 1) * SSD_STATE].astype(BF16)
                cse_g = cse_s[r0:r0 + CHUNK, g0:g0 + gw]
                cs_end = cse_s[r0 + CHUNK - 1:r0 + CHUNK, g0:g0 + gw]
                xdt_g = xdt_s[r0:r0 + CHUNK, g0:g0 + gw]
                units[c, g] = dict(
                    cgb=cgb, gm=_bdot_nt(cgb, bgb),
                    local=_bdot_tn(bgb, xdt_g * jnp.exp(cs_end - cse_g)),
                    decay=jnp.exp(cs_end))
                filler()
        return units, cs_r

    def ssd_back(base, units, cs_r):
        for g in range(SSD_GROUPS):
            pt = pt_s[g]
            for c in range(TL // CHUNK):
                u = units[c, g]
                u['state'] = pt
                pt = pt * u['decay'] + u['local']
            pt_s[g] = pt
        for c in range(TL // CHUNK):
            r0 = base + c * CHUNK
            for g in range(SSD_GROUPS):
                g0 = g * gw
                u = units[c, g]
                cse_g = cse_s[r0:r0 + CHUNK, g0:g0 + gw]
                xdt_g = xdt_s[r0:r0 + CHUNK, g0:g0 + gw]
                ydiag = jnp.zeros((CHUNK, gw), F32)
                for hl in range(hpg):
                    hh = g * hpg + hl
                    seg = (cse_g[:, hl * SSD_HEAD_DIM:(hl + 1) * SSD_HEAD_DIM]
                           - cs_r[hh:hh + 1, c * CHUNK:(c + 1) * CHUNK])
                    lm = jnp.where(causal, jnp.exp(jnp.minimum(seg, 0.0)), 0.0)
                    xm = jnp.where(lane_head == hl, xdt_g, 0.0)
                    ydiag = ydiag + _bdot(u['gm'] * lm, xm)
                yoff = _bdot(u['cgb'], u['state']) * jnp.exp(cse_g)
                xs_g = proj_s[r0:r0 + CHUNK, OFF_XS + g0:OFF_XS + g0 + gw]
                ossd_s[r0:r0 + CHUNK, g0:g0 + gw] = ydiag + yoff + xs_g * dskip_ref[:, g0:g0 + gw]
                filler()

    def ssd_gate_norm(tile):
        base = tile * TL
        y = ossd_s[base:base + TL, :] * _silu(proj_s[base:base + TL, OFF_Z:OFF_Z + SSD_WIDTH])
        for g in range(SSD_GROUPS):
            yg = y[:, g * gw:(g + 1) * gw]
            ms = jnp.mean(yg * yg, axis=-1, keepdims=True)
            yn = yg * lax.rsqrt(ms + RMS_EPS) * ssdnw_ref[:, g * gw:(g + 1) * gw]
            cat_s[base:base + TL, HG_WIDTH + g * gw:HG_WIDTH + (g + 1) * gw] = yn.astype(BF16)

    def out_slice(tile, c0):
        base = tile * TL
        mix_s[base:base + TL, c0:c0 + OUT_SLICE] = jnp.dot(
            cat_s[base:base + TL, :], wout_ref[:, c0:c0 + OUT_SLICE], preferred_element_type=F32)

    def norm_route(tile):
        base = tile * TL
        hres = DEEPNORM_ALPHA * x_ref[base:base + TL, :] + mix_s[base:base + TL, :]
        mu = jnp.mean(hres, axis=-1, keepdims=True)
        hc = hres - mu
        var = jnp.mean(hc * hc, axis=-1, keepdims=True)
        x1 = hc * lax.rsqrt(var + LN_EPS) * ln1g_ref[...] + ln1b_ref[...]
        x1_ref[base:base + TL, :] = x1

        xh = x1.astype(BF16)
        xm_ = (x1 - xh.astype(F32)).astype(BF16)
        t1 = jnp.dot(xh, rw1_ref[...], preferred_element_type=F32)
        logits = (t1[:, 0:LANES] + t1[:, LANES:2 * LANES]
                  + jnp.dot(xm_, rwh_ref[...], preferred_element_type=F32) + rb_ref[...])
        lane = lax.broadcasted_iota(jnp.int32, (TL, LANES), 1)
        lane_f = lane.astype(F32)
        neg = jnp.float32(-jnp.inf)
        work = jnp.where(lane < N_EXPERTS, logits, neg)
        onehots, vals, idxs = [], [], []
        for j in range(TOP_K):
            m = jnp.max(work, axis=-1, keepdims=True)
            idx = jnp.min(jnp.where(work == m, lane_f, float(LANES)), axis=-1, keepdims=True)
            oh = lane_f == idx
            onehots.append(oh)
            vals.append(m)
            idxs.append(idx)
            work = jnp.where(oh, neg, work)
        es = [jnp.exp(v - vals[0]) for v in vals]
        den = es[0] + es[1] + es[2] + es[3]
        gates = [e / den for e in es]
        sel = jnp.zeros((TL, LANES), F32)
        for oh in onehots:
            sel = jnp.where(oh, 1.0, sel)
        rankmat = jnp.dot(trils_ref[...], sel.astype(BF16), preferred_element_type=F32)
        cnt_ref[tile * SUBLANES:(tile + 1) * SUBLANES, :] = jnp.broadcast_to(
            jnp.sum(sel, axis=0, keepdims=True), (SUBLANES, LANES))
        meta = jnp.zeros((TL, LANES), F32)
        for j in range(TOP_K):
            rank_j = jnp.sum(jnp.where(onehots[j], rankmat, 0.0), axis=-1, keepdims=True)
            meta = jnp.where(lane == j, idxs[j], meta)
            meta = jnp.where(lane == TOP_K + j, rank_j, meta)
            meta = jnp.where(lane == 2 * TOP_K + j, gates[j], meta)
        meta_ref[base:base + TL, :] = meta

    def post_thunks(tile):
        return ([functools.partial(ssd_gate_norm, tile)]
                + [functools.partial(out_slice, tile, c0) for c0 in range(0, D_MODEL, OUT_SLICE)]
                + [functools.partial(norm_route, tile)])

    xb_s[...] = x_ref[...].astype(BF16)
    for run in project_slices(0):
        run()
    for tile in range(MIX_TILES):
        base = tile * TL
        if tile + 1 < MIX_TILES:
            pending.extend(project_slices((tile + 1) * TL))
        hg_units = hg_front(base)
        ssd_units, cs_r = ssd_front(base)
        while pending:
            filler()
        hg_back(base, hg_units)
        ssd_back(base, ssd_units, cs_r)
        pending.extend(post_thunks(tile))
    while pending:
        filler()
    xpad_s[0:SUBLANES, :] = xpad_s[STEP_ROWS:STEP_ROWS + SUBLANES, :]


def _full(shape):
    nd = len(shape)
    return pl.BlockSpec(shape, lambda *_: (0,) * nd)


def _mixer_call(x2d, consts, batch, seq):
    SR = STEP_ROWS
    nt = seq // SR
    T = batch * seq
    in_specs = [pl.BlockSpec((SR, D_MODEL), lambda b, t: (b * nt + t, 0))]
    in_specs += [_full(c.shape) for c in consts]
    out_shape = (jax.ShapeDtypeStruct((T, D_MODEL), F32),
                 jax.ShapeDtypeStruct((T, LANES), F32),
                 jax.ShapeDtypeStruct((T // TILE_ROWS * SUBLANES, LANES), F32))
    out_specs = (pl.BlockSpec((SR, D_MODEL), lambda b, t: (b * nt + t, 0)),
                 pl.BlockSpec((SR, LANES), lambda b, t: (b * nt + t, 0)),
                 pl.BlockSpec((MIX_TILES * SUBLANES, LANES), lambda b, t: (b * nt + t, 0)))
    scratch = [
        pltpu.VMEM((SR, PROJ_COLS), F32),
        pltpu.VMEM((SR, D_MODEL), BF16),
        pltpu.VMEM((SR, HG_WIDTH), F32),
        pltpu.VMEM((SR + 2 * SUBLANES, SSD_CONV_DIM), F32),
        pltpu.VMEM((SR, SSD_WIDTH), F32),
        pltpu.VMEM((SR, SSD_WIDTH), F32),
        pltpu.VMEM((HG_HEADS, HG_HEAD_DIM, HG_HEAD_DIM), F32),
        pltpu.VMEM((SSD_GROUPS, SSD_STATE, SSD_WIDTH // SSD_GROUPS), F32),
        pltpu.VMEM((SR, D_MODEL), BF16),
        pltpu.VMEM((SR, SSD_WIDTH), F32),
        pltpu.VMEM((SR, D_MODEL), F32),
    ]
    return pl.pallas_call(
        _mixer_body,
        grid=(batch, nt),
        in_specs=in_specs,
        out_specs=out_specs,
        out_shape=out_shape,
        scratch_shapes=scratch,
        compiler_params=pltpu.CompilerParams(
            dimension_semantics=("arbitrary", "arbitrary"), vmem_limit_bytes=VMEM_LIMIT),
        name="mixer",
    )(x2d, *consts)


def _for_each_run(units_ref, tile, fn):
    def per_expert(e, carry):
        k = tile * N_EXPERTS + e
        n = units_ref[k]

        @pl.when(n > 0)
        def _():
            fn(k, n)
        return carry
    lax.fori_loop(0, N_EXPERTS, per_expert, 0)


def _rows(unit, n_units=1):
    start = unit * RUN_ALIGN
    if RUN_ALIGN > 1:
        start = pl.multiple_of(start, RUN_ALIGN)
    return pl.ds(start, n_units * RUN_ALIGN)


def _dispatch_body(units_ref, g8_ref, l8_ref, tot_ref, tail8_ref, tailn_ref, nv_ref,
                   x1_ref, meta_ref, lst_ref, xs_hbm, sorted_s, zero_s, sems, zsem, bsem):
    tau = pl.program_id(0)
    n = pl.num_programs(0)
    slot = lax.rem(tau, 2)
    TD, S = TILE_ROWS, SORT_ROWS
    n_blocks = xs_hbm.shape[0] // EXPERT_BLOCK

    def unused_block_copy(m):
        rows = pl.ds(pl.multiple_of(m * EXPERT_BLOCK, EXPERT_BLOCK), EXPERT_BLOCK)
        return pltpu.make_async_copy(zero_s, xs_hbm.at[rows, :], bsem)

    def run_copy(s, l_unit, g_unit, n_units):
        return pltpu.make_async_copy(sorted_s.at[s, _rows(l_unit, n_units), :],
                                     xs_hbm.at[_rows(g_unit, n_units), :], sems.at[s])

    def wait_units(s, count):
        run_copy(s, 0, 0, count).wait()

    @pl.when(tau == 0)
    def _():
        zero_s[...] = jnp.zeros_like(zero_s)

        def tail_copy(e):
            n_units = tailn_ref[e]
            return pltpu.make_async_copy(zero_s.at[pl.ds(0, n_units * RUN_ALIGN), :],
                                         xs_hbm.at[_rows(tail8_ref[e], n_units), :], zsem)

        def start_unused(m, carry):
            unused_block_copy(m).start()
            return carry
        lax.fori_loop(nv_ref[0], n_blocks, start_unused, 0)

        def start_e(e, carry):
            @pl.when(tailn_ref[e] > 0)
            def _():
                tail_copy(e).start()
            return carry
        lax.fori_loop(0, N_EXPERTS, start_e, 0)

        def wait_e(e, carry):
            @pl.when(tailn_ref[e] > 0)
            def _():
                tail_copy(e).wait()
            return carry
        lax.fori_loop(0, N_EXPERTS, wait_e, 0)

    @pl.when(tau >= 2)
    def _():
        wait_units(slot, tot_ref[jnp.maximum(tau - 2, 0)])

    meta = meta_ref[...]
    lane = lax.broadcasted_iota(jnp.int32, (TD, LANES), 1)
    lane_f = lane.astype(F32)
    u8 = lst_ref[...].astype(BF16)
    ones8 = jnp.ones((SUBLANES, LANES), BF16)
    r_iota = lax.broadcasted_iota(jnp.int32, (S, TD), 0).astype(F32)
    gh = meta.astype(BF16).astype(F32)
    g1 = meta - gh
    gm = g1.astype(BF16).astype(F32)
    gl = g1 - gm
    nt_dims = (((1,), (1,)), ((), ()))
    conds, gparts = [], []
    for j in range(TOP_K):
        oh = lane_f == meta[:, j:j + 1]
        ohb = jnp.where(oh, 1.0, 0.0).astype(BF16)
        rkb = jnp.where(oh, meta[:, TOP_K + j:TOP_K + j + 1], 0.0).astype(BF16)
        m1 = lax.dot_general(u8, ohb, nt_dims, preferred_element_type=F32)
        m2 = lax.dot_general(ones8, rkb, nt_dims, preferred_element_type=F32)
        lpos = RUN_ALIGN * (LST_SPLIT * m1[0:1, :] + m1[1:2, :]) + m2[0:1, :]
        conds.append(r_iota == lpos)
        gc = 2 * TOP_K + j
        gparts.append(jnp.where(lane == 0, gh[:, gc:gc + 1],
                                jnp.where(lane == 1, gm[:, gc:gc + 1],
                                          jnp.where(lane == 2, gl[:, gc:gc + 1], 0.0))).astype(BF16))
    pcat = jnp.concatenate([jnp.where(c, 1.0, 0.0).astype(BF16) for c in conds], axis=1)
    sorted_g = jnp.dot(pcat, jnp.concatenate(gparts, axis=0), preferred_element_type=F32)
    perm = jnp.where(conds[0], 1.0, jnp.where(conds[1], 1.0, jnp.where(conds[2], 1.0,
                     jnp.where(conds[3], 1.0, 0.0)))).astype(BF16)
    sorted_x = jnp.dot(perm, x1_ref[...].astype(BF16), preferred_element_type=F32)
    sorted_s[slot, :, 0:D_MODEL] = sorted_x
    sorted_s[slot, :, D_MODEL:XS_WIDTH] = sorted_g

    _for_each_run(units_ref, tau, lambda k, cnt: run_copy(slot, l8_ref[k], g8_ref[k], cnt).start())

    @pl.when(tau == n - 1)
    def _():
        @pl.when(tau >= 1)
        def _():
            wait_units(1 - slot, tot_ref[jnp.maximum(tau - 1, 0)])
        wait_units(slot, tot_ref[tau])

        def wait_unused(m, carry):
            unused_block_copy(m).wait()
            return carry
        lax.fori_loop(nv_ref[0], n_blocks, wait_unused, 0)


def _dispatch_call(sched, x1, meta, lst_rows, cap):
    T = x1.shape[0]
    TD = TILE_ROWS
    grid_spec = pltpu.PrefetchScalarGridSpec(
        num_scalar_prefetch=7,
        grid=(T // TD,),
        in_specs=[
            pl.BlockSpec((TD, D_MODEL), lambda i, *_: (i, 0)),
            pl.BlockSpec((TD, LANES), lambda i, *_: (i, 0)),
            pl.BlockSpec((SUBLANES, LANES), lambda i, *_: (i, 0)),
        ],
        out_specs=pl.BlockSpec(memory_space=pl.ANY),
        scratch_shapes=[
            pltpu.VMEM((2, SORT_ROWS, XS_WIDTH), F32),
            pltpu.VMEM((EXPERT_BLOCK, XS_WIDTH), F32),
            pltpu.SemaphoreType.DMA((2,)),
            pltpu.SemaphoreType.DMA(()),
            pltpu.SemaphoreType.DMA(()),
        ],
    )
    return pl.pallas_call(
        _dispatch_body,
        grid_spec=grid_spec,
        out_shape=jax.ShapeDtypeStruct((cap, XS_WIDTH), F32),
        compiler_params=pltpu.CompilerParams(
            dimension_semantics=("arbitrary",), vmem_limit_bytes=VMEM_LIMIT),
        name="dispatch",
    )(*sched, x1, meta, lst_rows)


def _expert_body(be_ref, nv_ref, nxt_ref, par_ref, xs_hbm, wg_hbm, bg_ref, wu_hbm, bu_ref, wd_hbm, bd_ref,
                 y_hbm, xbuf, ybuf, zbuf, wbuf, xsem, ysem, zsem, wsem, wg_s, wu_s, wd_s):
    nv = nv_ref[0]
    n_blocks = xs_hbm.shape[0] // EXPERT_BLOCK

    def blk(m):
        return pl.ds(pl.multiple_of(m * EXPERT_BLOCK, EXPERT_BLOCK), EXPERT_BLOCK)

    def x_copy(m, s):
        return pltpu.make_async_copy(xs_hbm.at[blk(m), :], xbuf.at[s], xsem.at[s])

    def y_copy(m, s):
        return pltpu.make_async_copy(ybuf.at[s], y_hbm.at[blk(m), :], ysem.at[s])

    def zero_copy(m):
        return pltpu.make_async_copy(zbuf, y_hbm.at[blk(m), :], zsem)

    def weight_copies(e, s):
        return [pltpu.make_async_copy(w.at[e], wbuf.at[s, k], wsem.at[s])
                for k, w in enumerate((wg_hbm, wu_hbm, wd_hbm))]

    for c in weight_copies(be_ref[0], par_ref[0]):
        c.start()
    x_copy(0, 0).start()

    zbuf[...] = jnp.zeros_like(zbuf)

    def start_zero(m, carry):
        zero_copy(m).start()
        return carry
    lax.fori_loop(nv, n_blocks, start_zero, 0)

    def body(m, carry):
        s = lax.rem(m, 2)
        e = be_ref[m]
        x_copy(m, s).wait()

        @pl.when(m + 1 < nv)
        def _():
            x_copy(m + 1, 1 - s).start()

        @pl.when(jnp.logical_or(m == 0, e != be_ref[jnp.maximum(m - 1, 0)]))
        def _():
            ws = par_ref[m]
            for c in weight_copies(e, ws):
                c.wait()

            @pl.when(nxt_ref[m] != e)
            def _():
                for c in weight_copies(nxt_ref[m], 1 - ws):
                    c.start()
            wg_s[...] = wbuf[ws, 0].astype(BF16)
            wu_s[...] = wbuf[ws, 1].astype(BF16)
            wd_s[...] = wbuf[ws, 2].astype(BF16)

        @pl.when(m >= 2)
        def _():
            y_copy(m - 2, s).wait()

        xb = xbuf[s, :, 0:D_MODEL].astype(BF16)
        gate = (xbuf[s, :, D_MODEL:D_MODEL + 1] + xbuf[s, :, D_MODEL + 1:D_MODEL + 2]
                + xbuf[s, :, D_MODEL + 2:D_MODEL + 3])
        hg = jnp.minimum(jnp.dot(xb, wg_s[...], preferred_element_type=F32) + bg_ref[e], SWIGLU_LIMIT)
        hu = jnp.clip(jnp.dot(xb, wu_s[...], preferred_element_type=F32) + bu_ref[e],
                      -SWIGLU_LIMIT, SWIGLU_LIMIT)
        hact = (hu + 1.0) * (hg * _sigmoid(SWIGLU_ALPHA * hg))
        y = jnp.dot(hact.astype(BF16), wd_s[...], preferred_element_type=F32) + bd_ref[e]
        ybuf[s] = y * gate
        y_copy(m, s).start()
        return carry
    lax.fori_loop(0, nv, body, 0)

    @pl.when(nv >= 2)
    def _():
        y_copy(nv - 2, lax.rem(nv, 2)).wait()
    y_copy(nv - 1, lax.rem(nv - 1, 2)).wait()

    def wait_zero(m, carry):
        zero_copy(m).wait()
        return carry
    lax.fori_loop(nv, n_blocks, wait_zero, 0)


def _expert_call(block_expert, n_valid, next_expert, parity, xs, w_gate, b_gate, w_up, b_up, w_down, b_down):
    n_blocks = xs.shape[0] // EXPERT_BLOCK
    anyspec = pl.BlockSpec(memory_space=pl.ANY)
    bspec = pl.BlockSpec((N_EXPERTS, 1, D_MODEL), lambda i, *_: (0, 0, 0))
    grid_spec = pltpu.PrefetchScalarGridSpec(
        num_scalar_prefetch=4,
        grid=(1,),
        in_specs=[anyspec, anyspec, bspec, anyspec, bspec, anyspec, bspec],
        out_specs=anyspec,
        scratch_shapes=[
            pltpu.VMEM((2, EXPERT_BLOCK, XS_WIDTH), F32),
            pltpu.VMEM((2, EXPERT_BLOCK, D_MODEL), F32),
            pltpu.VMEM((EXPERT_BLOCK, D_MODEL), F32),
            pltpu.VMEM((2, 3, D_MODEL, D_MODEL), F32),
            pltpu.SemaphoreType.DMA((2,)),
            pltpu.SemaphoreType.DMA((2,)),
            pltpu.SemaphoreType.DMA(()),
            pltpu.SemaphoreType.DMA((2,)),
            pltpu.VMEM((D_MODEL, D_MODEL), BF16),
            pltpu.VMEM((D_MODEL, D_MODEL), BF16),
            pltpu.VMEM((D_MODEL, D_MODEL), BF16),
        ],
    )
    return pl.pallas_call(
        _expert_body,
        grid_spec=grid_spec,
        out_shape=jax.ShapeDtypeStruct((n_blocks * EXPERT_BLOCK, D_MODEL), F32),
        compiler_params=pltpu.CompilerParams(
            dimension_semantics=("arbitrary",), vmem_limit_bytes=VMEM_LIMIT),
        name="experts",
    )(block_expert, n_valid, next_expert, parity, xs, w_gate, b_gate.reshape(N_EXPERTS, 1, D_MODEL),
      w_up, b_up.reshape(N_EXPERTS, 1, D_MODEL), w_down, b_down.reshape(N_EXPERTS, 1, D_MODEL))


def _combine_body(units_ref, g8_ref, l8_ref, tot_ref,
                  y_hbm, x1_ref, meta_ref, lst_ref, g_ref, b_ref, out_ref, ys_s, sems):
    tau = pl.program_id(0)
    n = pl.num_programs(0)
    slot = lax.rem(tau, 2)
    TD, S = TILE_ROWS, SORT_ROWS

    def run_copy(s, g_unit, l_unit, n_units):
        return pltpu.make_async_copy(y_hbm.at[_rows(g_unit, n_units), :],
                                     ys_s.at[s, _rows(l_unit, n_units), :], sems.at[s])

    def fetch(tile, s):
        _for_each_run(units_ref, tile, lambda k, cnt: run_copy(s, g8_ref[k], l8_ref[k], cnt).start())

    @pl.when(tau == 0)
    def _():
        ys_s[...] = jnp.zeros_like(ys_s)
        fetch(0, 0)

    @pl.when(tau + 1 < n)
    def _():
        fetch(tau + 1, 1 - slot)

    run_copy(slot, 0, 0, tot_ref[tau]).wait()

    meta = meta_ref[...]
    lane_f = lax.broadcasted_iota(jnp.int32, (TD, LANES), 1).astype(F32)
    u_row = lst_ref[2:3, :]
    s_iota = lax.broadcasted_iota(jnp.int32, (TD, S), 1).astype(F32)
    conds = []
    for j in range(TOP_K):
        oh = lane_f == meta[:, j:j + 1]
        start8 = jnp.sum(jnp.where(oh, u_row, 0.0), axis=-1, keepdims=True)
        lpos = RUN_ALIGN * start8 + meta[:, TOP_K + j:TOP_K + j + 1]
        conds.append(s_iota == lpos)
    perm = jnp.where(conds[0], 1.0, jnp.where(conds[1], 1.0, jnp.where(conds[2], 1.0,
                     jnp.where(conds[3], 1.0, 0.0)))).astype(BF16)
    ys = ys_s[slot]
    yh = ys.astype(BF16)
    yl = (ys - yh.astype(F32)).astype(BF16)
    ffn = (jnp.dot(perm, yh, preferred_element_type=F32) + jnp.dot(perm, yl, preferred_element_type=F32))
    acc = DEEPNORM_ALPHA * x1_ref[...] + ffn
    mu = jnp.mean(acc, axis=-1, keepdims=True)
    hc = acc - mu
    var = jnp.mean(hc * hc, axis=-1, keepdims=True)
    out_ref[...] = hc * lax.rsqrt(var + LN_EPS) * g_ref[...] + b_ref[...]


def _combine_call(sched, y_rows, x1, meta, lst_rows, ln2_g, ln2_b):
    T = x1.shape[0]
    TD = TILE_ROWS
    grid_spec = pltpu.PrefetchScalarGridSpec(
        num_scalar_prefetch=4,
        grid=(T // TD,),
        in_specs=[
            pl.BlockSpec(memory_space=pl.ANY),
            pl.BlockSpec((TD, D_MODEL), lambda i, *_: (i, 0)),
            pl.BlockSpec((TD, LANES), lambda i, *_: (i, 0)),
            pl.BlockSpec((SUBLANES, LANES), lambda i, *_: (i, 0)),
            pl.BlockSpec((1, D_MODEL), lambda i, *_: (0, 0)),
            pl.BlockSpec((1, D_MODEL), lambda i, *_: (0, 0)),
        ],
        out_specs=pl.BlockSpec((TD, D_MODEL), lambda i, *_: (i, 0)),
        scratch_shapes=[pltpu.VMEM((2, SORT_ROWS, D_MODEL), F32), pltpu.SemaphoreType.DMA((2,))],
    )
    return pl.pallas_call(
        _combine_body,
        grid_spec=grid_spec,
        out_shape=jax.ShapeDtypeStruct((T, D_MODEL), F32),
        compiler_params=pltpu.CompilerParams(
            dimension_semantics=("arbitrary",), vmem_limit_bytes=VMEM_LIMIT),
        name="combine",
    )(*sched, y_rows, x1, meta, lst_rows, ln2_g, ln2_b)


def _np_consts():
    TL = TILE_ROWS
    r = np.arange(TL)
    same = (r[:, None] // CHUNK) == (r[None, :] // CHUNK)
    tril = (same & (r[None, :] <= r[:, None])).astype(np.float32)
    trils = (r[None, :] < r[:, None]).astype(np.float32)
    e128 = np.zeros((LANES, SSD_WIDTH), np.float32)
    for h in range(SSD_HEADS):
        e128[h, h * SSD_HEAD_DIM:(h + 1) * SSD_HEAD_DIM] = 1.0
    return tril, trils, e128


def kernel(x, w_in, hg_lower_bound, hg_norm_w, conv_w, conv_b, dt_bias, a_log, d_skip, ssd_norm_w, w_out,
           ln1_g, ln1_b, router_w, router_b, w_gate, b_gate, w_up, b_up, w_down, b_down, ln2_g, ln2_b):
    batch, seq, d = x.shape
    assert d == D_MODEL and seq % STEP_ROWS == 0 and w_in.shape[0] == DEPTH
    T = batch * seq
    n_tiles = T // TILE_ROWS
    max_rows = T * TOP_K + n_tiles * N_EXPERTS * (RUN_ALIGN - 1)
    n_blocks = -(-max_rows // EXPERT_BLOCK) + N_EXPERTS
    cap = n_blocks * EXPERT_BLOCK
    assert SORT_ROWS >= TILE_ROWS * TOP_K + N_EXPERTS * (RUN_ALIGN - 1)

    tril, trils, e128 = _np_consts()
    w = w_in[0]
    pad_l = LANES - SSD_HEADS
    rw = jnp.pad(router_w[0], ((0, 0), (0, LANES - N_EXPERTS)))
    rwh = rw.astype(BF16)
    rwm = (rw - rwh.astype(F32)).astype(BF16)
    consts = [
        jnp.pad(w, ((0, 0), (0, pad_l))).astype(BF16),
        hg_lower_bound,
        hg_norm_w[0].reshape(1, HG_HEAD_DIM),
        conv_w[0],
        conv_b[0].reshape(1, SSD_CONV_DIM),
        jnp.pad(dt_bias[0], (0, pad_l)).reshape(1, LANES),
        jnp.pad(a_log[0], (0, pad_l)).reshape(1, LANES),
        jnp.repeat(d_skip[0], SSD_HEAD_DIM).reshape(1, SSD_WIDTH),
        ssd_norm_w[0].reshape(1, SSD_WIDTH),
        w_out[0].astype(BF16),
        ln1_g[0].reshape(1, D_MODEL),
        ln1_b[0].reshape(1, D_MODEL),
        jnp.concatenate([rwh, rwm], axis=1),
        rwh,
        jnp.pad(router_b[0], (0, LANES - N_EXPERTS)).reshape(1, LANES),
        jnp.asarray(tril, BF16), jnp.asarray(trils, BF16), jnp.asarray(e128, BF16),
    ]
    x1, meta, cnt = _mixer_call(x.reshape(T, D_MODEL), consts, batch, seq)

    counts = cnt.reshape(n_tiles, SUBLANES, LANES)[:, 0, :N_EXPERTS].astype(jnp.int32)
    c8 = (counts + RUN_ALIGN - 1) // RUN_ALIGN * RUN_ALIGN
    used = jnp.sum(c8, axis=0)
    region = (used + EXPERT_BLOCK - 1) // EXPERT_BLOCK * EXPERT_BLOCK
    region_end = jnp.cumsum(region)
    region_start = region_end - region
    gstart = region_start[None, :] + jnp.cumsum(c8, axis=0) - c8
    lstart = jnp.cumsum(c8, axis=1) - c8
    n_valid = region_end[-1] // EXPERT_BLOCK
    blk_row = jnp.arange(n_blocks, dtype=jnp.int32) * EXPERT_BLOCK
    blk_row = jnp.minimum(blk_row, (n_valid - 1) * EXPERT_BLOCK)
    block_expert = jnp.minimum(jnp.sum(region_end[None, :] <= blk_row[:, None], axis=1),
                               N_EXPERTS - 1).astype(jnp.int32)
    has = region > 0
    eidx = jnp.arange(N_EXPERTS, dtype=jnp.int32)
    suffix_min = lax.cummin(jnp.where(has, eidx, N_EXPERTS), reverse=True)
    nxt_e = jnp.concatenate([suffix_min[1:], jnp.full((1,), N_EXPERTS, jnp.int32)])
    nxt_e = jnp.where(nxt_e == N_EXPERTS, eidx, nxt_e)
    par_e = (jnp.cumsum(has.astype(jnp.int32)) - 1) % 2
    blk_onehot = block_expert[:, None] == eidx[None, :]
    next_expert = jnp.sum(jnp.where(blk_onehot, nxt_e[None, :], 0), axis=1).astype(jnp.int32)
    parity = jnp.sum(jnp.where(blk_onehot, par_e[None, :], 0), axis=1).astype(jnp.int32)
    as_units = lambda a: (a // RUN_ALIGN).astype(jnp.int32).reshape(-1)
    units, g8, l8 = as_units(c8), as_units(gstart), as_units(lstart)
    tot = (jnp.sum(c8, axis=1) // RUN_ALIGN).astype(jnp.int32)
    tail8 = as_units(region_start + used)
    tailn = as_units(region - used)
    lu = lstart // RUN_ALIGN
    lst3 = jnp.stack([lu // LST_SPLIT, lu % LST_SPLIT, lu], axis=1).astype(F32)
    lst_rows = jnp.pad(lst3, ((0, 0), (0, SUBLANES - 3), (0, LANES - N_EXPERTS))).reshape(
        n_tiles * SUBLANES, LANES)

    nv = n_valid.astype(jnp.int32).reshape(1)
    xs = _dispatch_call((units, g8, l8, tot, tail8, tailn, nv), x1, meta, lst_rows, cap)
    y_rows = _expert_call(block_expert, nv, next_expert, parity, xs,
                          w_gate[0], b_gate[0], w_up[0], b_up[0], w_down[0], b_down[0])
    out = _combine_call((units, g8, l8, tot), y_rows, x1, meta, lst_rows,
                        ln2_g[0].reshape(1, D_MODEL), ln2_b[0].reshape(1, D_MODEL))
    return out.reshape(batch, seq, D_MODEL)
```

```python
import functools

import jax
import jax.numpy as jnp
import numpy as np
from jax import lax
from jax.experimental import pallas as pl
from jax.experimental.pallas import tpu as pltpu

F32 = jnp.float32
BF16 = jnp.bfloat16

D_MODEL = 1024
CHUNK = 64
HG_WIDTH = 512
HG_HEAD_DIM = 128
HG_HEADS = 4
SSD_WIDTH = 512
SSD_HEAD_DIM = 64
SSD_HEADS = 8
SSD_GROUPS = 2
SSD_STATE = 128
SSD_CONV = 4
SSD_CONV_DIM = SSD_WIDTH + 2 * SSD_GROUPS * SSD_STATE
N_EXPERTS = 32
TOP_K = 4
EXPERT_BLOCK = 256
EXPERT_BIG = 512
SWIGLU_LIMIT = 7.0
SWIGLU_ALPHA = 1.702
DEPTH = 1
DEEPNORM_ALPHA = (2 * DEPTH) ** 0.25
LN_EPS = 1e-5
RMS_EPS = 1e-5

LANES = 128
SUBLANES = 8
SUB_CHUNK = 16
EXP_CAP = 60.0
TILE_ROWS = 256
MIX_TILES = 2
STEP_ROWS = MIX_TILES * TILE_ROWS
RUN_ALIGN = SUBLANES
SORT_ROWS = 1280
LST_SPLIT = 32
XS_WIDTH = D_MODEL + LANES
VMEM_LIMIT = 56 * 1024 * 1024

OFF_Q, OFF_F, OFF_I, OFF_G = 0, 512, 1024, 1536
OFF_Z, OFF_XBC = 2048, 2560
OFF_XS, OFF_B, OFF_C = 2560, 3072, 3328
OFF_DT = 3584
PROJ_COLS = OFF_DT + LANES
PROJ_SLICE = 256
CONV_SLICE = 256
OUT_SLICE = 256


def _bdot(a, b):
    return jnp.dot(a.astype(BF16), b.astype(BF16), preferred_element_type=F32)


def _bdot_nt(a, b):
    return lax.dot_general(a.astype(BF16), b.astype(BF16), (((1,), (1,)), ((), ())),
                           preferred_element_type=F32)


def _bdot_tn(a, b):
    return lax.dot_general(a.astype(BF16), b.astype(BF16), (((0,), (0,)), ((), ())),
                           preferred_element_type=F32)


def _split3(a):
    hi = a.astype(BF16)
    r1 = a - hi.astype(F32)
    mid = r1.astype(BF16)
    lo = (r1 - mid.astype(F32)).astype(BF16)
    return hi, mid, lo


def _dot01_left(m01, a):
    hi, mid, lo = _split3(a)
    d = functools.partial(jnp.dot, m01, preferred_element_type=F32)
    return d(hi) + d(mid) + d(lo)


def _dot01_right(a, m01):
    hi, mid, lo = _split3(a)
    return (jnp.dot(hi, m01, preferred_element_type=F32) + jnp.dot(mid, m01, preferred_element_type=F32)
            + jnp.dot(lo, m01, preferred_element_type=F32))


def _sigmoid(x):
    return 1.0 / (1.0 + jnp.exp(-x))


def _silu(x):
    return x * _sigmoid(x)


def _softplus(x):
    return jnp.maximum(x, 0.0) + jnp.log(1.0 + jnp.exp(-jnp.abs(x)))


def _mixer_body(x_ref, win_ref, lbp_ref, hgnw_ref, convw_ref, convb_ref, dtb_ref, alog_ref, dskip_ref,
                ssdnw_ref, wout_ref, ln1g_ref, ln1b_ref, rw1_ref, rwh_ref, rb_ref,
                tril_ref, trils_ref, e128_ref,
                x1_ref, meta_ref, cnt_ref,
                proj_s, xb_s, b_s, xpad_s, xdt_s, cse_s, st_s, pt_s, cat_s, ossd_s, mix_s):
    TL = TILE_ROWS
    t = pl.program_id(1)

    @pl.when(t == 0)
    def _():
        xpad_s[0:SUBLANES, :] = jnp.zeros((SUBLANES, SSD_CONV_DIM), F32)
        st_s[...] = jnp.zeros_like(st_s)
        pt_s[...] = jnp.zeros_like(pt_s)

    row64 = lax.broadcasted_iota(jnp.int32, (CHUNK, CHUNK), 0)
    col64 = lax.broadcasted_iota(jnp.int32, (CHUNK, CHUNK), 1)
    causal = row64 >= col64
    gw = SSD_WIDTH // SSD_GROUPS
    hpg = SSD_HEADS // SSD_GROUPS
    lane_head = lax.broadcasted_iota(jnp.int32, (CHUNK, gw), 1) // SSD_HEAD_DIM

    pending = []

    def filler():
        if pending:
            pending.pop(0)()

    def project_slices(base):
        def one(c0, c1):
            def run():
                proj_s[base:base + TL, c0:c1] = jnp.dot(xb_s[base:base + TL, :], win_ref[:, c0:c1],
                                                        preferred_element_type=F32)
            return run
        return [one(c0, min(c0 + PROJ_SLICE, PROJ_COLS)) for c0 in range(0, PROJ_COLS, PROJ_SLICE)]

    def hg_front(base):
        a0 = lbp_ref[0:1, :]
        a1 = lbp_ref[1:2, :]
        am = jnp.maximum(a0, a1)
        e0 = jnp.exp(a0 - am)
        e1 = jnp.exp(a1 - am)
        lb = e0 / (e0 + e1)
        f = lb + (1.0 - lb) * _sigmoid(proj_s[base:base + TL, OFF_F:OFF_F + HG_WIDTH])
        b_s[base:base + TL, :] = _dot01_left(tril_ref[...], jnp.log(f))
        proj_s[base:base + TL, OFF_F:OFF_F + HG_WIDTH] = f
        proj_s[base:base + TL, OFF_Q:OFF_Q + HG_WIDTH] = _silu(proj_s[base:base + TL, OFF_Q:OFF_Q + HG_WIDTH])
        units = {}
        for c in range(TL // CHUNK):
            r0 = base + c * CHUNK
            for h in range(HG_HEADS):
                h0 = h * HG_HEAD_DIM
                bc = b_s[r0:r0 + CHUNK, h0:h0 + HG_HEAD_DIM]
                qc = proj_s[r0:r0 + CHUNK, OFF_Q + h0:OFF_Q + h0 + HG_HEAD_DIM]
                kc = 1.0 - proj_s[r0:r0 + CHUNK, OFF_F + h0:OFF_F + h0 + HG_HEAD_DIM]
                vcb = proj_s[r0:r0 + CHUNK, OFF_I + h0:OFF_I + h0 + HG_HEAD_DIM].astype(BF16)
                parts = []
                for i in range(CHUNK // SUB_CHUNK):
                    s0 = i * SUB_CHUNK
                    if i == 0:
                        qi = qc[0:SUB_CHUNK] * jnp.exp(bc[0:SUB_CHUNK])
                        ki = kc * jnp.exp(jnp.minimum(-bc, EXP_CAP))
                    else:
                        ref_i = bc[s0 - 1:s0, :]
                        qi = qc[s0:s0 + SUB_CHUNK] * jnp.exp(bc[s0:s0 + SUB_CHUNK] - ref_i)
                        ki = kc * jnp.exp(jnp.minimum(ref_i - bc, EXP_CAP))
                    parts.append(_bdot_nt(qi, ki))
                b_end = bc[CHUNK - 1:CHUNK, :]
                kdec = kc * jnp.exp(b_end - bc)
                units[c, h] = dict(
                    parts=parts, vcb=vcb, qdec=(qc * jnp.exp(bc)).astype(BF16),
                    local=_bdot_tn(vcb, kdec),
                    decay=jnp.exp(b_end))
                if h % 2 == 1:
                    filler()
        return units

    def hg_back(base, units):
        hgnw = hgnw_ref[...]
        for h in range(HG_HEADS):
            st = st_s[h]
            for c in range(TL // CHUNK):
                u = units[c, h]
                u['state'] = st
                st = st * u['decay'] + u['local']
            st_s[h] = st
        for c in range(TL // CHUNK):
            r0 = base + c * CHUNK
            for h in range(HG_HEADS):
                h0 = h * HG_HEAD_DIM
                u = units[c, h]
                sc = jnp.where(causal, jnp.concatenate(u['parts'], axis=0), 0.0)
                o = _bdot(sc, u['vcb']) + _bdot_nt(u['qdec'], u['state'])
                ms = jnp.mean(o * o, axis=-1, keepdims=True)
                on = o * lax.rsqrt(ms + RMS_EPS) * hgnw
                gc = proj_s[r0:r0 + CHUNK, OFF_G + h0:OFF_G + h0 + HG_HEAD_DIM]
                cat_s[r0:r0 + CHUNK, h0:h0 + HG_HEAD_DIM] = (on * _silu(gc)).astype(BF16)
                filler()

    def ssd_front(base):
        e128 = e128_ref[...]
        dtc = _softplus(proj_s[base:base + TL, OFF_DT:OFF_DT + LANES] + dtb_ref[...])
        a_row = -jnp.exp(alog_ref[...])
        cs_c = _dot01_left(tril_ref[...], dtc * a_row)
        cs_r = jnp.transpose(cs_c)[0:SSD_HEADS, :]
        cse_s[base:base + TL, :] = _dot01_right(cs_c, e128)
        dt_exp = _dot01_right(dtc, e128)

        xpad_s[SUBLANES + base:SUBLANES + base + TL, :] = proj_s[base:base + TL, OFF_XBC:OFF_XBC + SSD_CONV_DIM]
        for c0 in range(0, SSD_CONV_DIM, CONV_SLICE):
            acc = jnp.broadcast_to(convb_ref[:, c0:c0 + CONV_SLICE], (TL, CONV_SLICE))
            for j in range(SSD_CONV):
                off = base + SUBLANES - (SSD_CONV - 1) + j
                acc = acc + convw_ref[j:j + 1, c0:c0 + CONV_SLICE] * xpad_s[off:off + TL, c0:c0 + CONV_SLICE]
            proj_s[base:base + TL, OFF_XBC + c0:OFF_XBC + c0 + CONV_SLICE] = _silu(acc)
            filler()
        xdt_s[base:base + TL, :] = proj_s[base:base + TL, OFF_XS:OFF_XS + SSD_WIDTH] * dt_exp

        units = {}
        for c in range(TL // CHUNK):
            r0 = base + c * CHUNK
            for g in range(SSD_GROUPS):
                g0 = g * gw
                bgb = proj_s[r0:r0 + CHUNK, OFF_B + g * SSD_STATE:OFF_B + (g + 1) * SSD_STATE].astype(BF16)
                cgb = proj_s[r0:r0 + CHUNK, OFF_C + g * SSD_STATE:OFF_C + (g + 1) * SSD_STATE].astype(BF16)
                cse_g = cse_s[r0:r0 + CHUNK, g0:g0 + gw]
                cs_end = cse_s[r0 + CHUNK - 1:r0 + CHUNK, g0:g0 + gw]
                xdt_g = xdt_s[r0:r0 + CHUNK, g0:g0 + gw]
                units[c, g] = dict(
                    cgb=cgb, gm=_bdot_nt(cgb, bgb),
                    local=_bdot_tn(bgb, xdt_g * jnp.exp(cs_end - cse_g)),
                    decay=jnp.exp(cs_end))
                filler()
        return units, cs_r

    def ssd_back(base, units, cs_r):
        for g in range(SSD_GROUPS):
            pt = pt_s[g]
            for c in range(TL // CHUNK):
                u = units[c, g]
                u['state'] = pt
                pt = pt * u['decay'] + u['local']
            pt_s[g] = pt
        for c in range(TL // CHUNK):
            r0 = base + c * CHUNK
            for g in range(SSD_GROUPS):
                g0 = g * gw
                u = units[c, g]
                cse_g = cse_s[r0:r0 + CHUNK, g0:g0 + gw]
                xdt_g = xdt_s[r0:r0 + CHUNK, g0:g0 + gw]
                ydiag = jnp.zeros((CHUNK, gw), F32)
                for hl in range(hpg):
                    hh = g * hpg + hl
                    seg = (cse_g[:, hl * SSD_HEAD_DIM:(hl + 1) * SSD_HEAD_DIM]
                           - cs_r[hh:hh + 1, c * CHUNK:(c + 1) * CHUNK])
                    lm = jnp.where(causal, jnp.exp(jnp.minimum(seg, 0.0)), 0.0)
                    xm = jnp.where(lane_head == hl, xdt_g, 0.0)
                    ydiag = ydiag + _bdot(u['gm'] * lm, xm)
                yoff = _bdot(u['cgb'], u['state']) * jnp.exp(cse_g)
                xs_g = proj_s[r0:r0 + CHUNK, OFF_XS + g0:OFF_XS + g0 + gw]
                ossd_s[r0:r0 + CHUNK, g0:g0 + gw] = ydiag + yoff + xs_g * dskip_ref[:, g0:g0 + gw]
                filler()

    def ssd_gate_norm(tile):
        base = tile * TL
        y = ossd_s[base:base + TL, :] * _silu(proj_s[base:base + TL, OFF_Z:OFF_Z + SSD_WIDTH])
        for g in range(SSD_GROUPS):
            yg = y[:, g * gw:(g + 1) * gw]
            ms = jnp.mean(yg * yg, axis=-1, keepdims=True)
            yn = yg * lax.rsqrt(ms + RMS_EPS) * ssdnw_ref[:, g * gw:(g + 1) * gw]
            cat_s[base:base + TL, HG_WIDTH + g * gw:HG_WIDTH + (g + 1) * gw] = yn.astype(BF16)

    def out_slice(tile, c0):
        base = tile * TL
        mix_s[base:base + TL, c0:c0 + OUT_SLICE] = jnp.dot(
            cat_s[base:base + TL, :], wout_ref[:, c0:c0 + OUT_SLICE], preferred_element_type=F32)

    def norm_route(tile):
        base = tile * TL
        hres = DEEPNORM_ALPHA * x_ref[base:base + TL, :] + mix_s[base:base + TL, :]
        mu = jnp.mean(hres, axis=-1, keepdims=True)
        hc = hres - mu
        var = jnp.mean(hc * hc, axis=-1, keepdims=True)
        x1 = hc * lax.rsqrt(var + LN_EPS) * ln1g_ref[...] + ln1b_ref[...]
        x1_ref[base:base + TL, :] = x1

        xh = x1.astype(BF16)
        xm_ = (x1 - xh.astype(F32)).astype(BF16)
        t1 = jnp.dot(xh, rw1_ref[...], preferred_element_type=F32)
        logits = (t1[:, 0:LANES] + t1[:, LANES:2 * LANES]
                  + jnp.dot(xm_, rwh_ref[...], preferred_element_type=F32) + rb_ref[...])
        lane = lax.broadcasted_iota(jnp.int32, (TL, LANES), 1)
        lane_f = lane.astype(F32)
        neg = jnp.float32(-jnp.inf)
        work = jnp.where(lane < N_EXPERTS, logits, neg)
        onehots, vals, idxs = [], [], []
        for j in range(TOP_K):
            m = jnp.max(work, axis=-1, keepdims=True)
            idx = jnp.min(jnp.where(work == m, lane_f, float(LANES)), axis=-1, keepdims=True)
            oh = lane_f == idx
            onehots.append(oh)
            vals.append(m)
            idxs.append(idx)
            work = jnp.where(oh, neg, work)
        es = [jnp.exp(v - vals[0]) for v in vals]
        den = es[0] + es[1] + es[2] + es[3]
        gates = [e / den for e in es]
        sel = jnp.zeros((TL, LANES), F32)
        for oh in onehots:
            sel = jnp.where(oh, 1.0, sel)
        rankmat = jnp.dot(trils_ref[...], sel.astype(BF16), preferred_element_type=F32)
        cnt_ref[tile * SUBLANES:(tile + 1) * SUBLANES, :] = jnp.broadcast_to(
            jnp.sum(sel, axis=0, keepdims=True), (SUBLANES, LANES))
        meta = jnp.zeros((TL, LANES), F32)
        for j in range(TOP_K):
            rank_j = jnp.sum(jnp.where(onehots[j], rankmat, 0.0), axis=-1, keepdims=True)
            meta = jnp.where(lane == j, idxs[j], meta)
            meta = jnp.where(lane == TOP_K + j, rank_j, meta)
            meta = jnp.where(lane == 2 * TOP_K + j, gates[j], meta)
        meta_ref[base:base + TL, :] = meta

    def post_thunks(tile):
        return ([functools.partial(ssd_gate_norm, tile)]
                + [functools.partial(out_slice, tile, c0) for c0 in range(0, D_MODEL, OUT_SLICE)]
                + [functools.partial(norm_route, tile)])

    xb_s[...] = x_ref[...].astype(BF16)
    for run in project_slices(0):
        run()
    for tile in range(MIX_TILES):
        base = tile * TL
        if tile + 1 < MIX_TILES:
            pending.extend(project_slices((tile + 1) * TL))
        hg_units = hg_front(base)
        ssd_units, cs_r = ssd_front(base)
        while pending:
            filler()
        hg_back(base, hg_units)
        ssd_back(base, ssd_units, cs_r)
        pending.extend(post_thunks(tile))
    while pending:
        filler()
    xpad_s[0:SUBLANES, :] = xpad_s[STEP_ROWS:STEP_ROWS + SUBLANES, :]


def _full(shape):
    nd = len(shape)
    return pl.BlockSpec(shape, lambda *_: (0,) * nd)


def _mixer_call(x2d, consts, batch, seq):
    SR = STEP_ROWS
    nt = seq // SR
    T = batch * seq
    in_specs = [pl.BlockSpec((SR, D_MODEL), lambda b, t: (b * nt + t, 0))]
    in_specs += [_full(c.shape) for c in consts]
    out_shape = (jax.ShapeDtypeStruct((T, D_MODEL), F32),
                 jax.ShapeDtypeStruct((T, LANES), F32),
                 jax.ShapeDtypeStruct((T // TILE_ROWS * SUBLANES, LANES), F32))
    out_specs = (pl.BlockSpec((SR, D_MODEL), lambda b, t: (b * nt + t, 0)),
                 pl.BlockSpec((SR, LANES), lambda b, t: (b * nt + t, 0)),
                 pl.BlockSpec((MIX_TILES * SUBLANES, LANES), lambda b, t: (b * nt + t, 0)))
    scratch = [
        pltpu.VMEM((SR, PROJ_COLS), F32),
        pltpu.VMEM((SR, D_MODEL), BF16),
        pltpu.VMEM((SR, HG_WIDTH), F32),
        pltpu.VMEM((SR + 2 * SUBLANES, SSD_CONV_DIM), F32),
        pltpu.VMEM((SR, SSD_WIDTH), F32),
        pltpu.VMEM((SR, SSD_WIDTH), F32),
        pltpu.VMEM((HG_HEADS, HG_HEAD_DIM, HG_HEAD_DIM), F32),
        pltpu.VMEM((SSD_GROUPS, SSD_STATE, SSD_WIDTH // SSD_GROUPS), F32),
        pltpu.VMEM((SR, D_MODEL), BF16),
        pltpu.VMEM((SR, SSD_WIDTH), F32),
        pltpu.VMEM((SR, D_MODEL), F32),
    ]
    return pl.pallas_call(
        _mixer_body,
        grid=(batch, nt),
        in_specs=in_specs,
        out_specs=out_specs,
        out_shape=out_shape,
        scratch_shapes=scratch,
        compiler_params=pltpu.CompilerParams(
            dimension_semantics=("arbitrary", "arbitrary"), vmem_limit_bytes=VMEM_LIMIT),
        name="mixer",
    )(x2d, *consts)


def _for_each_run(units_ref, tile, fn):
    def per_expert(e, carry):
        k = tile * N_EXPERTS + e
        n = units_ref[k]

        @pl.when(n > 0)
        def _():
            fn(k, n)
        return carry
    lax.fori_loop(0, N_EXPERTS, per_expert, 0)


def _rows(unit, n_units=1):
    start = unit * RUN_ALIGN
    if RUN_ALIGN > 1:
        start = pl.multiple_of(start, RUN_ALIGN)
    return pl.ds(start, n_units * RUN_ALIGN)


def _dispatch_body(units_ref, g8_ref, l8_ref, tot_ref, tail8_ref, tailn_ref, nv_ref,
                   x1_ref, meta_ref, lst_ref, xs_hbm, sorted_s, zero_s, sems, zsem, bsem):
    tau = pl.program_id(0)
    n = pl.num_programs(0)
    slot = lax.rem(tau, 2)
    TD, S = TILE_ROWS, SORT_ROWS
    n_blocks = xs_hbm.shape[0] // EXPERT_BLOCK

    def unused_block_copy(m):
        rows = pl.ds(pl.multiple_of(m * EXPERT_BLOCK, EXPERT_BLOCK), EXPERT_BLOCK)
        return pltpu.make_async_copy(zero_s, xs_hbm.at[rows, :], bsem)

    def run_copy(s, l_unit, g_unit, n_units):
        return pltpu.make_async_copy(sorted_s.at[s, _rows(l_unit, n_units), :],
                                     xs_hbm.at[_rows(g_unit, n_units), :], sems.at[s])

    def wait_units(s, count):
        run_copy(s, 0, 0, count).wait()

    @pl.when(tau == 0)
    def _():
        zero_s[...] = jnp.zeros_like(zero_s)

        def tail_copy(e):
            n_units = tailn_ref[e]
            return pltpu.make_async_copy(zero_s.at[pl.ds(0, n_units * RUN_ALIGN), :],
                                         xs_hbm.at[_rows(tail8_ref[e], n_units), :], zsem)

        def start_unused(m, carry):
            unused_block_copy(m).start()
            return carry
        lax.fori_loop(nv_ref[0], n_blocks, start_unused, 0)

        def start_e(e, carry):
            @pl.when(tailn_ref[e] > 0)
            def _():
                tail_copy(e).start()
            return carry
        lax.fori_loop(0, N_EXPERTS, start_e, 0)

        def wait_e(e, carry):
            @pl.when(tailn_ref[e] > 0)
            def _():
                tail_copy(e).wait()
            return carry
        lax.fori_loop(0, N_EXPERTS, wait_e, 0)

    @pl.when(tau >= 2)
    def _():
        wait_units(slot, tot_ref[jnp.maximum(tau - 2, 0)])

    meta = meta_ref[...]
    lane = lax.broadcasted_iota(jnp.int32, (TD, LANES), 1)
    lane_f = lane.astype(F32)
    u8 = lst_ref[...].astype(BF16)
    ones8 = jnp.ones((SUBLANES, LANES), BF16)
    r_iota = lax.broadcasted_iota(jnp.int32, (S, TD), 0).astype(F32)
    gh = meta.astype(BF16).astype(F32)
    g1 = meta - gh
    gm = g1.astype(BF16).astype(F32)
    gl = g1 - gm
    nt_dims = (((1,), (1,)), ((), ()))
    conds, gparts = [], []
    for j in range(TOP_K):
        oh = lane_f == meta[:, j:j + 1]
        ohb = jnp.where(oh, 1.0, 0.0).astype(BF16)
        rkb = jnp.where(oh, meta[:, TOP_K + j:TOP_K + j + 1], 0.0).astype(BF16)
        m1 = lax.dot_general(u8, ohb, nt_dims, preferred_element_type=F32)
        m2 = lax.dot_general(ones8, rkb, nt_dims, preferred_element_type=F32)
        lpos = RUN_ALIGN * (LST_SPLIT * m1[0:1, :] + m1[1:2, :]) + m2[0:1, :]
        conds.append(r_iota == lpos)
        gc = 2 * TOP_K + j
        gparts.append(jnp.where(lane == 0, gh[:, gc:gc + 1],
                                jnp.where(lane == 1, gm[:, gc:gc + 1],
                                          jnp.where(lane == 2, gl[:, gc:gc + 1], 0.0))).astype(BF16))
    pcat = jnp.concatenate([jnp.where(c, 1.0, 0.0).astype(BF16) for c in conds], axis=1)
    sorted_g = jnp.dot(pcat, jnp.concatenate(gparts, axis=0), preferred_element_type=F32)
    perm = jnp.where(conds[0], 1.0, jnp.where(conds[1], 1.0, jnp.where(conds[2], 1.0,
                     jnp.where(conds[3], 1.0, 0.0)))).astype(BF16)
    sorted_x = jnp.dot(perm, x1_ref[...].astype(BF16), preferred_element_type=F32)
    sorted_s[slot, :, 0:D_MODEL] = sorted_x
    sorted_s[slot, :, D_MODEL:XS_WIDTH] = sorted_g

    _for_each_run(units_ref, tau, lambda k, cnt: run_copy(slot, l8_ref[k], g8_ref[k], cnt).start())

    @pl.when(tau == n - 1)
    def _():
        @pl.when(tau >= 1)
        def _():
            wait_units(1 - slot, tot_ref[jnp.maximum(tau - 1, 0)])
        wait_units(slot, tot_ref[tau])

        def wait_unused(m, carry):
            unused_block_copy(m).wait()
            return carry
        lax.fori_loop(nv_ref[0], n_blocks, wait_unused, 0)


def _dispatch_call(sched, x1, meta, lst_rows, cap):
    T = x1.shape[0]
    TD = TILE_ROWS
    grid_spec = pltpu.PrefetchScalarGridSpec(
        num_scalar_prefetch=7,
        grid=(T // TD,),
        in_specs=[
            pl.BlockSpec((TD, D_MODEL), lambda i, *_: (i, 0)),
            pl.BlockSpec((TD, LANES), lambda i, *_: (i, 0)),
            pl.BlockSpec((SUBLANES, LANES), lambda i, *_: (i, 0)),
        ],
        out_specs=pl.BlockSpec(memory_space=pl.ANY),
        scratch_shapes=[
            pltpu.VMEM((2, SORT_ROWS, XS_WIDTH), F32),
            pltpu.VMEM((EXPERT_BLOCK, XS_WIDTH), F32),
            pltpu.SemaphoreType.DMA((2,)),
            pltpu.SemaphoreType.DMA(()),
            pltpu.SemaphoreType.DMA(()),
        ],
    )
    return pl.pallas_call(
        _dispatch_body,
        grid_spec=grid_spec,
        out_shape=jax.ShapeDtypeStruct((cap, XS_WIDTH), F32),
        compiler_params=pltpu.CompilerParams(
            dimension_semantics=("arbitrary",), vmem_limit_bytes=VMEM_LIMIT),
        name="dispatch",
    )(*sched, x1, meta, lst_rows)


def _expert_body(ite_ref, itb_ref, big_ref, nxt_ref, par_ref, nit_ref, nv_ref,
                 xs_hbm, wg_hbm, bg_ref, wu_hbm, bu_ref, wd_hbm, bd_ref,
                 y_hbm, xbuf, ybuf, zbuf, wbuf, xsem, ysem, zsem, wsem, wg_s, wu_s, wd_s):
    n_iter = nit_ref[0]
    n_blocks = xs_hbm.shape[0] // EXPERT_BLOCK
    sizes = (EXPERT_BIG, EXPERT_BLOCK)

    def rows_of(i, rows):
        return pl.ds(pl.multiple_of(itb_ref[i] * EXPERT_BLOCK, EXPERT_BLOCK), rows)

    def x_copy(i, s, rows):
        return pltpu.make_async_copy(xs_hbm.at[rows_of(i, rows), :], xbuf.at[s, pl.ds(0, rows), :], xsem.at[s])

    def y_copy(i, s, rows):
        return pltpu.make_async_copy(ybuf.at[s, pl.ds(0, rows), :], y_hbm.at[rows_of(i, rows), :], ysem.at[s])

    def by_size(i, fn):
        for flag, rows in zip((1, 0), sizes):
            @pl.when(big_ref[i] == flag)
            def _():
                fn(rows)

    def zero_copy(m):
        rows = pl.ds(pl.multiple_of(m * EXPERT_BLOCK, EXPERT_BLOCK), EXPERT_BLOCK)
        return pltpu.make_async_copy(zbuf, y_hbm.at[rows, :], zsem)

    def weight_copies(e, s):
        return [pltpu.make_async_copy(w.at[e], wbuf.at[s, k], wsem.at[s])
                for k, w in enumerate((wg_hbm, wu_hbm, wd_hbm))]

    for c in weight_copies(ite_ref[0], par_ref[0]):
        c.start()
    by_size(0, lambda rows: x_copy(0, 0, rows).start())

    zbuf[...] = jnp.zeros_like(zbuf)

    def start_zero(m, carry):
        zero_copy(m).start()
        return carry
    lax.fori_loop(nv_ref[0], n_blocks, start_zero, 0)

    def ffn(s, e, rows):
        xb = xbuf[s, 0:rows, 0:D_MODEL].astype(BF16)
        gate = (xbuf[s, 0:rows, D_MODEL:D_MODEL + 1] + xbuf[s, 0:rows, D_MODEL + 1:D_MODEL + 2]
                + xbuf[s, 0:rows, D_MODEL + 2:D_MODEL + 3])
        hg = jnp.minimum(jnp.dot(xb, wg_s[...], preferred_element_type=F32) + bg_ref[e], SWIGLU_LIMIT)
        hu = jnp.clip(jnp.dot(xb, wu_s[...], preferred_element_type=F32) + bu_ref[e],
                      -SWIGLU_LIMIT, SWIGLU_LIMIT)
        hact = (hu + 1.0) * (hg * _sigmoid(SWIGLU_ALPHA * hg))
        y = jnp.dot(hact.astype(BF16), wd_s[...], preferred_element_type=F32) + bd_ref[e]
        ybuf[s, 0:rows, :] = y * gate

    def body(i, carry):
        s = lax.rem(i, 2)
        e = ite_ref[i]
        by_size(i, lambda rows: x_copy(i, s, rows).wait())

        @pl.when(i + 1 < n_iter)
        def _():
            by_size(i + 1, lambda rows: x_copy(i + 1, 1 - s, rows).start())

        @pl.when(jnp.logical_or(i == 0, e != ite_ref[jnp.maximum(i - 1, 0)]))
        def _():
            ws = par_ref[i]
            for c in weight_copies(e, ws):
                c.wait()

            @pl.when(nxt_ref[i] != e)
            def _():
                for c in weight_copies(nxt_ref[i], 1 - ws):
                    c.start()
            wg_s[...] = wbuf[ws, 0].astype(BF16)
            wu_s[...] = wbuf[ws, 1].astype(BF16)
            wd_s[...] = wbuf[ws, 2].astype(BF16)

        @pl.when(i >= 2)
        def _():
            by_size(i - 2, lambda rows: y_copy(i - 2, s, rows).wait())

        def chunk(rows):
            ffn(s, e, rows)
            y_copy(i, s, rows).start()
        by_size(i, chunk)
        return carry
    lax.fori_loop(0, n_iter, body, 0)

    @pl.when(n_iter >= 2)
    def _():
        by_size(n_iter - 2, lambda rows: y_copy(n_iter - 2, lax.rem(n_iter, 2), rows).wait())
    by_size(n_iter - 1, lambda rows: y_copy(n_iter - 1, lax.rem(n_iter - 1, 2), rows).wait())

    def wait_zero(m, carry):
        zero_copy(m).wait()
        return carry
    lax.fori_loop(nv_ref[0], n_blocks, wait_zero, 0)


def _expert_call(sched, xs, w_gate, b_gate, w_up, b_up, w_down, b_down):
    n_blocks = xs.shape[0] // EXPERT_BLOCK
    anyspec = pl.BlockSpec(memory_space=pl.ANY)
    bspec = pl.BlockSpec((N_EXPERTS, 1, D_MODEL), lambda i, *_: (0, 0, 0))
    grid_spec = pltpu.PrefetchScalarGridSpec(
        num_scalar_prefetch=len(sched),
        grid=(1,),
        in_specs=[anyspec, anyspec, bspec, anyspec, bspec, anyspec, bspec],
        out_specs=anyspec,
        scratch_shapes=[
            pltpu.VMEM((2, EXPERT_BIG, XS_WIDTH), F32),
            pltpu.VMEM((2, EXPERT_BIG, D_MODEL), F32),
            pltpu.VMEM((EXPERT_BLOCK, D_MODEL), F32),
            pltpu.VMEM((2, 3, D_MODEL, D_MODEL), F32),
            pltpu.SemaphoreType.DMA((2,)),
            pltpu.SemaphoreType.DMA((2,)),
            pltpu.SemaphoreType.DMA(()),
            pltpu.SemaphoreType.DMA((2,)),
            pltpu.VMEM((D_MODEL, D_MODEL), BF16),
            pltpu.VMEM((D_MODEL, D_MODEL), BF16),
            pltpu.VMEM((D_MODEL, D_MODEL), BF16),
        ],
    )
    return pl.pallas_call(
        _expert_body,
        grid_spec=grid_spec,
        out_shape=jax.ShapeDtypeStruct((n_blocks * EXPERT_BLOCK, D_MODEL), F32),
        compiler_params=pltpu.CompilerParams(
            dimension_semantics=("arbitrary",), vmem_limit_bytes=VMEM_LIMIT),
        name="experts",
    )(*sched, xs, w_gate, b_gate.reshape(N_EXPERTS, 1, D_MODEL),
      w_up, b_up.reshape(N_EXPERTS, 1, D_MODEL), w_down, b_down.reshape(N_EXPERTS, 1, D_MODEL))


def _combine_body(units_ref, g8_ref, l8_ref, tot_ref,
                  y_hbm, x1_ref, meta_ref, lst_ref, g_ref, b_ref, out_ref, ys_s, sems):
    tau = pl.program_id(0)
    n = pl.num_programs(0)
    slot = lax.rem(tau, 2)
    TD, S = TILE_ROWS, SORT_ROWS

    def run_copy(s, g_unit, l_unit, n_units):
        return pltpu.make_async_copy(y_hbm.at[_rows(g_unit, n_units), :],
                                     ys_s.at[s, _rows(l_unit, n_units), :], sems.at[s])

    def fetch(tile, s):
        _for_each_run(units_ref, tile, lambda k, cnt: run_copy(s, g8_ref[k], l8_ref[k], cnt).start())

    @pl.when(tau == 0)
    def _():
        ys_s[...] = jnp.zeros_like(ys_s)
        fetch(0, 0)

    @pl.when(tau + 1 < n)
    def _():
        fetch(tau + 1, 1 - slot)

    run_copy(slot, 0, 0, tot_ref[tau]).wait()

    meta = meta_ref[...]
    lane_f = lax.broadcasted_iota(jnp.int32, (TD, LANES), 1).astype(F32)
    u_row = lst_ref[2:3, :]
    s_iota = lax.broadcasted_iota(jnp.int32, (TD, S), 1).astype(F32)
    conds = []
    for j in range(TOP_K):
        oh = lane_f == meta[:, j:j + 1]
        start8 = jnp.sum(jnp.where(oh, u_row, 0.0), axis=-1, keepdims=True)
        lpos = RUN_ALIGN * start8 + meta[:, TOP_K + j:TOP_K + j + 1]
        conds.append(s_iota == lpos)
    perm = jnp.where(conds[0], 1.0, jnp.where(conds[1], 1.0, jnp.where(conds[2], 1.0,
                     jnp.where(conds[3], 1.0, 0.0)))).astype(BF16)
    ys = ys_s[slot]
    yh = ys.astype(BF16)
    yl = (ys - yh.astype(F32)).astype(BF16)
    ffn = (jnp.dot(perm, yh, preferred_element_type=F32) + jnp.dot(perm, yl, preferred_element_type=F32))
    acc = DEEPNORM_ALPHA * x1_ref[...] + ffn
    mu = jnp.mean(acc, axis=-1, keepdims=True)
    hc = acc - mu
    var = jnp.mean(hc * hc, axis=-1, keepdims=True)
    out_ref[...] = hc * lax.rsqrt(var + LN_EPS) * g_ref[...] + b_ref[...]


def _combine_call(sched, y_rows, x1, meta, lst_rows, ln2_g, ln2_b):
    T = x1.shape[0]
    TD = TILE_ROWS
    grid_spec = pltpu.PrefetchScalarGridSpec(
        num_scalar_prefetch=4,
        grid=(T // TD,),
        in_specs=[
            pl.BlockSpec(memory_space=pl.ANY),
            pl.BlockSpec((TD, D_MODEL), lambda i, *_: (i, 0)),
            pl.BlockSpec((TD, LANES), lambda i, *_: (i, 0)),
            pl.BlockSpec((SUBLANES, LANES), lambda i, *_: (i, 0)),
            pl.BlockSpec((1, D_MODEL), lambda i, *_: (0, 0)),
            pl.BlockSpec((1, D_MODEL), lambda i, *_: (0, 0)),
        ],
        out_specs=pl.BlockSpec((TD, D_MODEL), lambda i, *_: (i, 0)),
        scratch_shapes=[pltpu.VMEM((2, SORT_ROWS, D_MODEL), F32), pltpu.SemaphoreType.DMA((2,))],
    )
    return pl.pallas_call(
        _combine_body,
        grid_spec=grid_spec,
        out_shape=jax.ShapeDtypeStruct((T, D_MODEL), F32),
        compiler_params=pltpu.CompilerParams(
            dimension_semantics=("arbitrary",), vmem_limit_bytes=VMEM_LIMIT),
        name="combine",
    )(*sched, y_rows, x1, meta, lst_rows, ln2_g, ln2_b)


def _np_consts():
    TL = TILE_ROWS
    r = np.arange(TL)
    same = (r[:, None] // CHUNK) == (r[None, :] // CHUNK)
    tril = (same & (r[None, :] <= r[:, None])).astype(np.float32)
    trils = (r[None, :] < r[:, None]).astype(np.float32)
    e128 = np.zeros((LANES, SSD_WIDTH), np.float32)
    for h in range(SSD_HEADS):
        e128[h, h * SSD_HEAD_DIM:(h + 1) * SSD_HEAD_DIM] = 1.0
    return tril, trils, e128


def kernel(x, w_in, hg_lower_bound, hg_norm_w, conv_w, conv_b, dt_bias, a_log, d_skip, ssd_norm_w, w_out,
           ln1_g, ln1_b, router_w, router_b, w_gate, b_gate, w_up, b_up, w_down, b_down, ln2_g, ln2_b):
    batch, seq, d = x.shape
    assert d == D_MODEL and seq % STEP_ROWS == 0 and w_in.shape[0] == DEPTH
    T = batch * seq
    n_tiles = T // TILE_ROWS
    max_rows = T * TOP_K + n_tiles * N_EXPERTS * (RUN_ALIGN - 1)
    n_blocks = -(-max_rows // EXPERT_BLOCK) + N_EXPERTS
    cap = n_blocks * EXPERT_BLOCK
    assert SORT_ROWS >= TILE_ROWS * TOP_K + N_EXPERTS * (RUN_ALIGN - 1)

    tril, trils, e128 = _np_consts()
    w = w_in[0]
    pad_l = LANES - SSD_HEADS
    rw = jnp.pad(router_w[0], ((0, 0), (0, LANES - N_EXPERTS)))
    rwh = rw.astype(BF16)
    rwm = (rw - rwh.astype(F32)).astype(BF16)
    consts = [
        jnp.pad(w, ((0, 0), (0, pad_l))).astype(BF16),
        hg_lower_bound,
        hg_norm_w[0].reshape(1, HG_HEAD_DIM),
        conv_w[0],
        conv_b[0].reshape(1, SSD_CONV_DIM),
        jnp.pad(dt_bias[0], (0, pad_l)).reshape(1, LANES),
        jnp.pad(a_log[0], (0, pad_l)).reshape(1, LANES),
        jnp.repeat(d_skip[0], SSD_HEAD_DIM).reshape(1, SSD_WIDTH),
        ssd_norm_w[0].reshape(1, SSD_WIDTH),
        w_out[0].astype(BF16),
        ln1_g[0].reshape(1, D_MODEL),
        ln1_b[0].reshape(1, D_MODEL),
        jnp.concatenate([rwh, rwm], axis=1),
        rwh,
        jnp.pad(router_b[0], (0, LANES - N_EXPERTS)).reshape(1, LANES),
        jnp.asarray(tril, BF16), jnp.asarray(trils, BF16), jnp.asarray(e128, BF16),
    ]
    x1, meta, cnt = _mixer_call(x.reshape(T, D_MODEL), consts, batch, seq)

    counts = cnt.reshape(n_tiles, SUBLANES, LANES)[:, 0, :N_EXPERTS].astype(jnp.int32)
    c8 = (counts + RUN_ALIGN - 1) // RUN_ALIGN * RUN_ALIGN
    used = jnp.sum(c8, axis=0)
    region = (used + EXPERT_BLOCK - 1) // EXPERT_BLOCK * EXPERT_BLOCK
    region_end = jnp.cumsum(region)
    region_start = region_end - region
    gstart = region_start[None, :] + jnp.cumsum(c8, axis=0) - c8
    lstart = jnp.cumsum(c8, axis=1) - c8
    n_valid = region_end[-1] // EXPERT_BLOCK
    has = region > 0
    eidx = jnp.arange(N_EXPERTS, dtype=jnp.int32)
    suffix_min = lax.cummin(jnp.where(has, eidx, N_EXPERTS), reverse=True)
    nxt_e = jnp.concatenate([suffix_min[1:], jnp.full((1,), N_EXPERTS, jnp.int32)])
    nxt_e = jnp.where(nxt_e == N_EXPERTS, eidx, nxt_e)
    par_e = (jnp.cumsum(has.astype(jnp.int32)) - 1) % 2
    per_big = EXPERT_BIG // EXPERT_BLOCK
    nblk_e = region // EXPERT_BLOCK
    nbig_e = nblk_e // per_big
    it_cnt = nbig_e + nblk_e % per_big
    it_end = jnp.cumsum(it_cnt)
    n_iter = it_end[-1]
    it = jnp.minimum(jnp.arange(n_blocks // per_big + N_EXPERTS, dtype=jnp.int32), n_iter - 1)
    it_e = jnp.minimum(jnp.sum(it_end[None, :] <= it[:, None], axis=1), N_EXPERTS - 1).astype(jnp.int32)
    it_onehot = it_e[:, None] == eidx[None, :]
    lookup = lambda tab: jnp.sum(jnp.where(it_onehot, tab[None, :].astype(jnp.int32), 0), axis=1)
    k = it - lookup(it_end - it_cnt)
    nb = lookup(nbig_e)
    it_big = (k < nb).astype(jnp.int32)
    it_blk = (lookup(region_start // EXPERT_BLOCK) + jnp.where(k < nb, per_big * k, per_big * nb + (k - nb))
              ).astype(jnp.int32)
    expert_sched = (it_e, it_blk, it_big, lookup(nxt_e).astype(jnp.int32), lookup(par_e).astype(jnp.int32),
                    n_iter.astype(jnp.int32).reshape(1), n_valid.astype(jnp.int32).reshape(1))
    as_units = lambda a: (a // RUN_ALIGN).astype(jnp.int32).reshape(-1)
    units, g8, l8 = as_units(c8), as_units(gstart), as_units(lstart)
    tot = (jnp.sum(c8, axis=1) // RUN_ALIGN).astype(jnp.int32)
    tail8 = as_units(region_start + used)
    tailn = as_units(region - used)
    lu = lstart // RUN_ALIGN
    lst3 = jnp.stack([lu // LST_SPLIT, lu % LST_SPLIT, lu], axis=1).astype(F32)
    lst_rows = jnp.pad(lst3, ((0, 0), (0, SUBLANES - 3), (0, LANES - N_EXPERTS))).reshape(
        n_tiles * SUBLANES, LANES)

    nv = n_valid.astype(jnp.int32).reshape(1)
    xs = _dispatch_call((units, g8, l8, tot, tail8, tailn, nv), x1, meta, lst_rows, cap)
    y_rows = _expert_call(expert_sched, xs, w_gate[0], b_gate[0], w_up[0], b_up[0], w_down[0], b_down[0])
    out = _combine_call((units, g8, l8, tot), y_rows, x1, meta, lst_rows,
                        ln2_g[0].reshape(1, D_MODEL), ln2_b[0].reshape(1, D_MODEL))
    return out.reshape(batch, seq, D_MODEL)
```

```python
import functools

import jax
import jax.numpy as jnp
import numpy as np
from jax import lax
from jax.experimental import pallas as pl
from jax.experimental.pallas import tpu as pltpu

F32 = jnp.float32
BF16 = jnp.bfloat16

D_MODEL = 1024
CHUNK = 64
HG_WIDTH = 512
HG_HEAD_DIM = 128
HG_HEADS = 4
SSD_WIDTH = 512
SSD_HEAD_DIM = 64
SSD_HEADS = 8
SSD_GROUPS = 2
SSD_STATE = 128
SSD_CONV = 4
SSD_CONV_DIM = SSD_WIDTH + 2 * SSD_GROUPS * SSD_STATE
N_EXPERTS = 32
TOP_K = 4
EXPERT_BLOCK = 256
CHUNK_BLOCKS = (4, 2, 1)
SWIGLU_LIMIT = 7.0
SWIGLU_ALPHA = 1.702
DEPTH = 1
DEEPNORM_ALPHA = (2 * DEPTH) ** 0.25
LN_EPS = 1e-5
RMS_EPS = 1e-5

LANES = 128
SUBLANES = 8
SUB_CHUNK = 16
EXP_CAP = 60.0
TILE_ROWS = 256
MIX_TILES = 2
STEP_ROWS = MIX_TILES * TILE_ROWS
RUN_ALIGN = SUBLANES
SORT_ROWS = 1280
LST_SPLIT = 32
XS_WIDTH = D_MODEL + LANES
VMEM_LIMIT = 56 * 1024 * 1024

OFF_Q, OFF_F, OFF_I, OFF_G = 0, 512, 1024, 1536
OFF_Z, OFF_XBC = 2048, 2560
OFF_XS, OFF_B, OFF_C = 2560, 3072, 3328
OFF_DT = 3584
PROJ_COLS = OFF_DT + LANES
PROJ_SLICE = 256
CONV_SLICE = 256
OUT_SLICE = 256


def _bdot(a, b):
    return jnp.dot(a.astype(BF16), b.astype(BF16), preferred_element_type=F32)


def _bdot_nt(a, b):
    return lax.dot_general(a.astype(BF16), b.astype(BF16), (((1,), (1,)), ((), ())),
                           preferred_element_type=F32)


def _bdot_tn(a, b):
    return lax.dot_general(a.astype(BF16), b.astype(BF16), (((0,), (0,)), ((), ())),
                           preferred_element_type=F32)


def _split3(a):
    hi = a.astype(BF16)
    r1 = a - hi.astype(F32)
    mid = r1.astype(BF16)
    lo = (r1 - mid.astype(F32)).astype(BF16)
    return hi, mid, lo


def _dot01_left(m01, a):
    hi, mid, lo = _split3(a)
    d = functools.partial(jnp.dot, m01, preferred_element_type=F32)
    return d(hi) + d(mid) + d(lo)


def _dot01_right(a, m01):
    hi, mid, lo = _split3(a)
    return (jnp.dot(hi, m01, preferred_element_type=F32) + jnp.dot(mid, m01, preferred_element_type=F32)
            + jnp.dot(lo, m01, preferred_element_type=F32))


def _sigmoid(x):
    return 1.0 / (1.0 + jnp.exp(-x))


def _silu(x):
    return x * _sigmoid(x)


def _softplus(x):
    return jnp.maximum(x, 0.0) + jnp.log(1.0 + jnp.exp(-jnp.abs(x)))


def _mixer_body(x_ref, win_ref, lbp_ref, hgnw_ref, convw_ref, convb_ref, dtb_ref, alog_ref, dskip_ref,
                ssdnw_ref, wout_ref, ln1g_ref, ln1b_ref, rw1_ref, rwh_ref, rb_ref,
                tril_ref, trils_ref, e128_ref,
                x1_ref, meta_ref, cnt_ref,
                proj_s, xb_s, b_s, xpad_s, xdt_s, cse_s, st_s, pt_s, cat_s, ossd_s, mix_s):
    TL = TILE_ROWS
    t = pl.program_id(1)

    @pl.when(t == 0)
    def _():
        xpad_s[0:SUBLANES, :] = jnp.zeros((SUBLANES, SSD_CONV_DIM), F32)
        st_s[...] = jnp.zeros_like(st_s)
        pt_s[...] = jnp.zeros_like(pt_s)

    row64 = lax.broadcasted_iota(jnp.int32, (CHUNK, CHUNK), 0)
    col64 = lax.broadcasted_iota(jnp.int32, (CHUNK, CHUNK), 1)
    causal = row64 >= col64
    gw = SSD_WIDTH // SSD_GROUPS
    hpg = SSD_HEADS // SSD_GROUPS
    lane_head = lax.broadcasted_iota(jnp.int32, (CHUNK, gw), 1) // SSD_HEAD_DIM

    pending = []

    def filler():
        if pending:
            pending.pop(0)()

    def project_slices(base):
        def one(c0, c1):
            def run():
                proj_s[base:base + TL, c0:c1] = jnp.dot(xb_s[base:base + TL, :], win_ref[:, c0:c1],
                                                        preferred_element_type=F32)
            return run
        return [one(c0, min(c0 + PROJ_SLICE, PROJ_COLS)) for c0 in range(0, PROJ_COLS, PROJ_SLICE)]

    def hg_front(base):
        a0 = lbp_ref[0:1, :]
        a1 = lbp_ref[1:2, :]
        am = jnp.maximum(a0, a1)
        e0 = jnp.exp(a0 - am)
        e1 = jnp.exp(a1 - am)
        lb = e0 / (e0 + e1)
        f = lb + (1.0 - lb) * _sigmoid(proj_s[base:base + TL, OFF_F:OFF_F + HG_WIDTH])
        b_s[base:base + TL, :] = _dot01_left(tril_ref[...], jnp.log(f))
        proj_s[base:base + TL, OFF_F:OFF_F + HG_WIDTH] = f
        proj_s[base:base + TL, OFF_Q:OFF_Q + HG_WIDTH] = _silu(proj_s[base:base + TL, OFF_Q:OFF_Q + HG_WIDTH])
        units = {}
        for c in range(TL // CHUNK):
            r0 = base + c * CHUNK
            for h in range(HG_HEADS):
                h0 = h * HG_HEAD_DIM
                bc = b_s[r0:r0 + CHUNK, h0:h0 + HG_HEAD_DIM]
                qc = proj_s[r0:r0 + CHUNK, OFF_Q + h0:OFF_Q + h0 + HG_HEAD_DIM]
                kc = 1.0 - proj_s[r0:r0 + CHUNK, OFF_F + h0:OFF_F + h0 + HG_HEAD_DIM]
                vcb = proj_s[r0:r0 + CHUNK, OFF_I + h0:OFF_I + h0 + HG_HEAD_DIM].astype(BF16)
                parts = []
                for i in range(CHUNK // SUB_CHUNK):
                    s0 = i * SUB_CHUNK
                    if i == 0:
                        qi = qc[0:SUB_CHUNK] * jnp.exp(bc[0:SUB_CHUNK])
                        ki = kc * jnp.exp(jnp.minimum(-bc, EXP_CAP))
                    else:
                        ref_i = bc[s0 - 1:s0, :]
                        qi = qc[s0:s0 + SUB_CHUNK] * jnp.exp(bc[s0:s0 + SUB_CHUNK] - ref_i)
                        ki = kc * jnp.exp(jnp.minimum(ref_i - bc, EXP_CAP))
                    parts.append(_bdot_nt(qi, ki))
                b_end = bc[CHUNK - 1:CHUNK, :]
                kdec = kc * jnp.exp(b_end - bc)
                units[c, h] = dict(
                    parts=parts, vcb=vcb, qdec=(qc * jnp.exp(bc)).astype(BF16),
                    local=_bdot_tn(vcb, kdec),
                    decay=jnp.exp(b_end))
                if h % 2 == 1:
                    filler()
        return units

    def hg_back(base, units):
        hgnw = hgnw_ref[...]
        for h in range(HG_HEADS):
            st = st_s[h]
            for c in range(TL // CHUNK):
                u = units[c, h]
                u['state'] = st
                st = st * u['decay'] + u['local']
            st_s[h] = st
        for c in range(TL // CHUNK):
            r0 = base + c * CHUNK
            for h in range(HG_HEADS):
                h0 = h * HG_HEAD_DIM
                u = units[c, h]
                sc = jnp.where(causal, jnp.concatenate(u['parts'], axis=0), 0.0)
                o = _bdot(sc, u['vcb']) + _bdot_nt(u['qdec'], u['state'])
                ms = jnp.mean(o * o, axis=-1, keepdims=True)
                on = o * lax.rsqrt(ms + RMS_EPS) * hgnw
                gc = proj_s[r0:r0 + CHUNK, OFF_G + h0:OFF_G + h0 + HG_HEAD_DIM]
                cat_s[r0:r0 + CHUNK, h0:h0 + HG_HEAD_DIM] = (on * _silu(gc)).astype(BF16)
                filler()

    def ssd_front(base):
        e128 = e128_ref[...]
        dtc = _softplus(proj_s[base:base + TL, OFF_DT:OFF_DT + LANES] + dtb_ref[...])
        a_row = -jnp.exp(alog_ref[...])
        cs_c = _dot01_left(tril_ref[...], dtc * a_row)
        cs_r = jnp.transpose(cs_c)[0:SSD_HEADS, :]
        cse_s[base:base + TL, :] = _dot01_right(cs_c, e128)
        dt_exp = _dot01_right(dtc, e128)

        xpad_s[SUBLANES + base:SUBLANES + base + TL, :] = proj_s[base:base + TL, OFF_XBC:OFF_XBC + SSD_CONV_DIM]
        for c0 in range(0, SSD_CONV_DIM, CONV_SLICE):
            acc = jnp.broadcast_to(convb_ref[:, c0:c0 + CONV_SLICE], (TL, CONV_SLICE))
            for j in range(SSD_CONV):
                off = base + SUBLANES - (SSD_CONV - 1) + j
                acc = acc + convw_ref[j:j + 1, c0:c0 + CONV_SLICE] * xpad_s[off:off + TL, c0:c0 + CONV_SLICE]
            proj_s[base:base + TL, OFF_XBC + c0:OFF_XBC + c0 + CONV_SLICE] = _silu(acc)
            filler()
        xdt_s[base:base + TL, :] = proj_s[base:base + TL, OFF_XS:OFF_XS + SSD_WIDTH] * dt_exp

        units = {}
        for c in range(TL // CHUNK):
            r0 = base + c * CHUNK
            for g in range(SSD_GROUPS):
                g0 = g * gw
                bgb = proj_s[r0:r0 + CHUNK, OFF_B + g * SSD_STATE:OFF_B + (g + 1) * SSD_STATE].astype(BF16)
                cgb = proj_s[r0:r0 + CHUNK, OFF_C + g * SSD_STATE:OFF_C + (g + 1) * SSD_STATE].astype(BF16)
                cse_g = cse_s[r0:r0 + CHUNK, g0:g0 + gw]
                cs_end = cse_s[r0 + CHUNK - 1:r0 + CHUNK, g0:g0 + gw]
                xdt_g = xdt_s[r0:r0 + CHUNK, g0:g0 + gw]
                units[c, g] = dict(
                    cgb=cgb, gm=_bdot_nt(cgb, bgb),
                    local=_bdot_tn(bgb, xdt_g * jnp.exp(cs_end - cse_g)),
                    decay=jnp.exp(cs_end))
                filler()
        return units, cs_r

    def ssd_back(base, units, cs_r):
        for g in range(SSD_GROUPS):
            pt = pt_s[g]
            for c in range(TL // CHUNK):
                u = units[c, g]
                u['state'] = pt
                pt = pt * u['decay'] + u['local']
            pt_s[g] = pt
        for c in range(TL // CHUNK):
            r0 = base + c * CHUNK
            for g in range(SSD_GROUPS):
                g0 = g * gw
                u = units[c, g]
                cse_g = cse_s[r0:r0 + CHUNK, g0:g0 + gw]
                xdt_g = xdt_s[r0:r0 + CHUNK, g0:g0 + gw]
                ydiag = jnp.zeros((CHUNK, gw), F32)
                for hl in range(hpg):
                    hh = g * hpg + hl
                    seg = (cse_g[:, hl * SSD_HEAD_DIM:(hl + 1) * SSD_HEAD_DIM]
                           - cs_r[hh:hh + 1, c * CHUNK:(c + 1) * CHUNK])
                    lm = jnp.where(causal, jnp.exp(jnp.minimum(seg, 0.0)), 0.0)
                    xm = jnp.where(lane_head == hl, xdt_g, 0.0)
                    ydiag = ydiag + _bdot(u['gm'] * lm, xm)
                yoff = _bdot(u['cgb'], u['state']) * jnp.exp(cse_g)
                xs_g = proj_s[r0:r0 + CHUNK, OFF_XS + g0:OFF_XS + g0 + gw]
                ossd_s[r0:r0 + CHUNK, g0:g0 + gw] = ydiag + yoff + xs_g * dskip_ref[:, g0:g0 + gw]
                filler()

    def ssd_gate_norm(tile):
        base = tile * TL
        y = ossd_s[base:base + TL, :] * _silu(proj_s[base:base + TL, OFF_Z:OFF_Z + SSD_WIDTH])
        for g in range(SSD_GROUPS):
            yg = y[:, g * gw:(g + 1) * gw]
            ms = jnp.mean(yg * yg, axis=-1, keepdims=True)
            yn = yg * lax.rsqrt(ms + RMS_EPS) * ssdnw_ref[:, g * gw:(g + 1) * gw]
            cat_s[base:base + TL, HG_WIDTH + g * gw:HG_WIDTH + (g + 1) * gw] = yn.astype(BF16)

    def out_slice(tile, c0):
        base = tile * TL
        mix_s[base:base + TL, c0:c0 + OUT_SLICE] = jnp.dot(
            cat_s[base:base + TL, :], wout_ref[:, c0:c0 + OUT_SLICE], preferred_element_type=F32)

    def norm_route(tile):
        base = tile * TL
        hres = DEEPNORM_ALPHA * x_ref[base:base + TL, :] + mix_s[base:base + TL, :]
        mu = jnp.mean(hres, axis=-1, keepdims=True)
        hc = hres - mu
        var = jnp.mean(hc * hc, axis=-1, keepdims=True)
        x1 = hc * lax.rsqrt(var + LN_EPS) * ln1g_ref[...] + ln1b_ref[...]
        x1_ref[base:base + TL, :] = x1

        xh = x1.astype(BF16)
        xm_ = (x1 - xh.astype(F32)).astype(BF16)
        t1 = jnp.dot(xh, rw1_ref[...], preferred_element_type=F32)
        logits = (t1[:, 0:LANES] + t1[:, LANES:2 * LANES]
                  + jnp.dot(xm_, rwh_ref[...], preferred_element_type=F32) + rb_ref[...])
        lane = lax.broadcasted_iota(jnp.int32, (TL, LANES), 1)
        lane_f = lane.astype(F32)
        neg = jnp.float32(-jnp.inf)
        work = jnp.where(lane < N_EXPERTS, logits, neg)
        onehots, vals, idxs = [], [], []
        for j in range(TOP_K):
            m = jnp.max(work, axis=-1, keepdims=True)
            idx = jnp.min(jnp.where(work == m, lane_f, float(LANES)), axis=-1, keepdims=True)
            oh = lane_f == idx
            onehots.append(oh)
            vals.append(m)
            idxs.append(idx)
            work = jnp.where(oh, neg, work)
        es = [jnp.exp(v - vals[0]) for v in vals]
        den = es[0] + es[1] + es[2] + es[3]
        gates = [e / den for e in es]
        sel = jnp.zeros((TL, LANES), F32)
        for oh in onehots:
            sel = jnp.where(oh, 1.0, sel)
        rankmat = jnp.dot(trils_ref[...], sel.astype(BF16), preferred_element_type=F32)
        cnt_ref[tile * SUBLANES:(tile + 1) * SUBLANES, :] = jnp.broadcast_to(
            jnp.sum(sel, axis=0, keepdims=True), (SUBLANES, LANES))
        meta = jnp.zeros((TL, LANES), F32)
        for j in range(TOP_K):
            rank_j = jnp.sum(jnp.where(onehots[j], rankmat, 0.0), axis=-1, keepdims=True)
            meta = jnp.where(lane == j, idxs[j], meta)
            meta = jnp.where(lane == TOP_K + j, rank_j, meta)
            meta = jnp.where(lane == 2 * TOP_K + j, gates[j], meta)
        meta_ref[base:base + TL, :] = meta

    def post_thunks(tile):
        return ([functools.partial(ssd_gate_norm, tile)]
                + [functools.partial(out_slice, tile, c0) for c0 in range(0, D_MODEL, OUT_SLICE)]
                + [functools.partial(norm_route, tile)])

    xb_s[...] = x_ref[...].astype(BF16)
    for run in project_slices(0):
        run()
    for tile in range(MIX_TILES):
        base = tile * TL
        if tile + 1 < MIX_TILES:
            pending.extend(project_slices((tile + 1) * TL))
        hg_units = hg_front(base)
        ssd_units, cs_r = ssd_front(base)
        while pending:
            filler()
        hg_back(base, hg_units)
        ssd_back(base, ssd_units, cs_r)
        pending.extend(post_thunks(tile))
    while pending:
        filler()
    xpad_s[0:SUBLANES, :] = xpad_s[STEP_ROWS:STEP_ROWS + SUBLANES, :]


def _full(shape):
    nd = len(shape)
    return pl.BlockSpec(shape, lambda *_: (0,) * nd)


def _mixer_call(x2d, consts, batch, seq):
    SR = STEP_ROWS
    nt = seq // SR
    T = batch * seq
    in_specs = [pl.BlockSpec((SR, D_MODEL), lambda b, t: (b * nt + t, 0))]
    in_specs += [_full(c.shape) for c in consts]
    out_shape = (jax.ShapeDtypeStruct((T, D_MODEL), F32),
                 jax.ShapeDtypeStruct((T, LANES), F32),
                 jax.ShapeDtypeStruct((T // TILE_ROWS * SUBLANES, LANES), F32))
    out_specs = (pl.BlockSpec((SR, D_MODEL), lambda b, t: (b * nt + t, 0)),
                 pl.BlockSpec((SR, LANES), lambda b, t: (b * nt + t, 0)),
                 pl.BlockSpec((MIX_TILES * SUBLANES, LANES), lambda b, t: (b * nt + t, 0)))
    scratch = [
        pltpu.VMEM((SR, PROJ_COLS), F32),
        pltpu.VMEM((SR, D_MODEL), BF16),
        pltpu.VMEM((SR, HG_WIDTH), F32),
        pltpu.VMEM((SR + 2 * SUBLANES, SSD_CONV_DIM), F32),
        pltpu.VMEM((SR, SSD_WIDTH), F32),
        pltpu.VMEM((SR, SSD_WIDTH), F32),
        pltpu.VMEM((HG_HEADS, HG_HEAD_DIM, HG_HEAD_DIM), F32),
        pltpu.VMEM((SSD_GROUPS, SSD_STATE, SSD_WIDTH // SSD_GROUPS), F32),
        pltpu.VMEM((SR, D_MODEL), BF16),
        pltpu.VMEM((SR, SSD_WIDTH), F32),
        pltpu.VMEM((SR, D_MODEL), F32),
    ]
    return pl.pallas_call(
        _mixer_body,
        grid=(batch, nt),
        in_specs=in_specs,
        out_specs=out_specs,
        out_shape=out_shape,
        scratch_shapes=scratch,
        compiler_params=pltpu.CompilerParams(
            dimension_semantics=("arbitrary", "arbitrary"), vmem_limit_bytes=VMEM_LIMIT),
        name="mixer",
    )(x2d, *consts)


def _for_each_run(units_ref, tile, fn):
    def per_expert(e, carry):
        k = tile * N_EXPERTS + e
        n = units_ref[k]

        @pl.when(n > 0)
        def _():
            fn(k, n)
        return carry
    lax.fori_loop(0, N_EXPERTS, per_expert, 0)


def _rows(unit, n_units=1):
    start = unit * RUN_ALIGN
    if RUN_ALIGN > 1:
        start = pl.multiple_of(start, RUN_ALIGN)
    return pl.ds(start, n_units * RUN_ALIGN)


def _dispatch_body(units_ref, g8_ref, l8_ref, tot_ref, tail8_ref, tailn_ref, nv_ref,
                   x1_ref, meta_ref, lst_ref, xs_hbm, sorted_s, zero_s, sems, zsem, bsem):
    tau = pl.program_id(0)
    n = pl.num_programs(0)
    slot = lax.rem(tau, 2)
    TD, S = TILE_ROWS, SORT_ROWS
    n_blocks = xs_hbm.shape[0] // EXPERT_BLOCK

    def unused_block_copy(m):
        rows = pl.ds(pl.multiple_of(m * EXPERT_BLOCK, EXPERT_BLOCK), EXPERT_BLOCK)
        return pltpu.make_async_copy(zero_s, xs_hbm.at[rows, :], bsem)

    def run_copy(s, l_unit, g_unit, n_units):
        return pltpu.make_async_copy(sorted_s.at[s, _rows(l_unit, n_units), :],
                                     xs_hbm.at[_rows(g_unit, n_units), :], sems.at[s])

    def wait_units(s, count):
        run_copy(s, 0, 0, count).wait()

    @pl.when(tau == 0)
    def _():
        zero_s[...] = jnp.zeros_like(zero_s)

        def tail_copy(e):
            n_units = tailn_ref[e]
            return pltpu.make_async_copy(zero_s.at[pl.ds(0, n_units * RUN_ALIGN), :],
                                         xs_hbm.at[_rows(tail8_ref[e], n_units), :], zsem)

        def start_unused(m, carry):
            unused_block_copy(m).start()
            return carry
        lax.fori_loop(nv_ref[0], n_blocks, start_unused, 0)

        def start_e(e, carry):
            @pl.when(tailn_ref[e] > 0)
            def _():
                tail_copy(e).start()
            return carry
        lax.fori_loop(0, N_EXPERTS, start_e, 0)

        def wait_e(e, carry):
            @pl.when(tailn_ref[e] > 0)
            def _():
                tail_copy(e).wait()
            return carry
        lax.fori_loop(0, N_EXPERTS, wait_e, 0)

    @pl.when(tau >= 2)
    def _():
        wait_units(slot, tot_ref[jnp.maximum(tau - 2, 0)])

    meta = meta_ref[...]
    lane = lax.broadcasted_iota(jnp.int32, (TD, LANES), 1)
    lane_f = lane.astype(F32)
    u8 = lst_ref[...].astype(BF16)
    ones8 = jnp.ones((SUBLANES, LANES), BF16)
    r_iota = lax.broadcasted_iota(jnp.int32, (S, TD), 0).astype(F32)
    gh = meta.astype(BF16).astype(F32)
    g1 = meta - gh
    gm = g1.astype(BF16).astype(F32)
    gl = g1 - gm
    nt_dims = (((1,), (1,)), ((), ()))
    conds, gparts = [], []
    for j in range(TOP_K):
        oh = lane_f == meta[:, j:j + 1]
        ohb = jnp.where(oh, 1.0, 0.0).astype(BF16)
        rkb = jnp.where(oh, meta[:, TOP_K + j:TOP_K + j + 1], 0.0).astype(BF16)
        m1 = lax.dot_general(u8, ohb, nt_dims, preferred_element_type=F32)
        m2 = lax.dot_general(ones8, rkb, nt_dims, preferred_element_type=F32)
        lpos = RUN_ALIGN * (LST_SPLIT * m1[0:1, :] + m1[1:2, :]) + m2[0:1, :]
        conds.append(r_iota == lpos)
        gc = 2 * TOP_K + j
        gparts.append(jnp.where(lane == 0, gh[:, gc:gc + 1],
                                jnp.where(lane == 1, gm[:, gc:gc + 1],
                                          jnp.where(lane == 2, gl[:, gc:gc + 1], 0.0))).astype(BF16))
    pcat = jnp.concatenate([jnp.where(c, 1.0, 0.0).astype(BF16) for c in conds], axis=1)
    sorted_g = jnp.dot(pcat, jnp.concatenate(gparts, axis=0), preferred_element_type=F32)
    perm = jnp.where(conds[0], 1.0, jnp.where(conds[1], 1.0, jnp.where(conds[2], 1.0,
                     jnp.where(conds[3], 1.0, 0.0)))).astype(BF16)
    sorted_x = jnp.dot(perm, x1_ref[...].astype(BF16), preferred_element_type=F32)
    sorted_s[slot, :, 0:D_MODEL] = sorted_x
    sorted_s[slot, :, D_MODEL:XS_WIDTH] = sorted_g

    _for_each_run(units_ref, tau, lambda k, cnt: run_copy(slot, l8_ref[k], g8_ref[k], cnt).start())

    @pl.when(tau == n - 1)
    def _():
        @pl.when(tau >= 1)
        def _():
            wait_units(1 - slot, tot_ref[jnp.maximum(tau - 1, 0)])
        wait_units(slot, tot_ref[tau])

        def wait_unused(m, carry):
            unused_block_copy(m).wait()
            return carry
        lax.fori_loop(nv_ref[0], n_blocks, wait_unused, 0)


def _dispatch_call(sched, x1, meta, lst_rows, cap):
    T = x1.shape[0]
    TD = TILE_ROWS
    grid_spec = pltpu.PrefetchScalarGridSpec(
        num_scalar_prefetch=7,
        grid=(T // TD,),
        in_specs=[
            pl.BlockSpec((TD, D_MODEL), lambda i, *_: (i, 0)),
            pl.BlockSpec((TD, LANES), lambda i, *_: (i, 0)),
            pl.BlockSpec((SUBLANES, LANES), lambda i, *_: (i, 0)),
        ],
        out_specs=pl.BlockSpec(memory_space=pl.ANY),
        scratch_shapes=[
            pltpu.VMEM((2, SORT_ROWS, XS_WIDTH), F32),
            pltpu.VMEM((EXPERT_BLOCK, XS_WIDTH), F32),
            pltpu.SemaphoreType.DMA((2,)),
            pltpu.SemaphoreType.DMA(()),
            pltpu.SemaphoreType.DMA(()),
        ],
    )
    return pl.pallas_call(
        _dispatch_body,
        grid_spec=grid_spec,
        out_shape=jax.ShapeDtypeStruct((cap, XS_WIDTH), F32),
        compiler_params=pltpu.CompilerParams(
            dimension_semantics=("arbitrary",), vmem_limit_bytes=VMEM_LIMIT),
        name="dispatch",
    )(*sched, x1, meta, lst_rows)


def _expert_body(ite_ref, itb_ref, sz_ref, nxt_ref, nit_ref, nv_ref,
                 xs_hbm, wg_hbm, bg_ref, wu_hbm, bu_ref, wd_hbm, bd_ref,
                 y_hbm, xbuf, ybuf, zbuf, wbuf, xsem, ysem, zsem, wsem, wg_s, wu_s, wd_s):
    n_iter = nit_ref[0]
    n_blocks = xs_hbm.shape[0] // EXPERT_BLOCK
    sizes = tuple(n * EXPERT_BLOCK for n in CHUNK_BLOCKS)

    def rows_of(i, rows):
        return pl.ds(pl.multiple_of(itb_ref[i] * EXPERT_BLOCK, EXPERT_BLOCK), rows)

    def x_copy(i, s, rows):
        return pltpu.make_async_copy(xs_hbm.at[rows_of(i, rows), :], xbuf.at[s, pl.ds(0, rows), :], xsem.at[s])

    def y_copy(i, s, rows):
        return pltpu.make_async_copy(ybuf.at[s, pl.ds(0, rows), :], y_hbm.at[rows_of(i, rows), :], ysem.at[s])

    def by_size(i, fn):
        for k, rows in enumerate(sizes):
            @pl.when(sz_ref[i] == k)
            def _():
                fn(rows)

    def zero_copy(m):
        rows = pl.ds(pl.multiple_of(m * EXPERT_BLOCK, EXPERT_BLOCK), EXPERT_BLOCK)
        return pltpu.make_async_copy(zbuf, y_hbm.at[rows, :], zsem)

    def weight_copies(e):
        return [pltpu.make_async_copy(w.at[e], wbuf.at[k], wsem)
                for k, w in enumerate((wg_hbm, wu_hbm, wd_hbm))]

    for c in weight_copies(ite_ref[0]):
        c.start()
    by_size(0, lambda rows: x_copy(0, 0, rows).start())

    zbuf[...] = jnp.zeros_like(zbuf)

    def start_zero(m, carry):
        zero_copy(m).start()
        return carry
    lax.fori_loop(nv_ref[0], n_blocks, start_zero, 0)

    def ffn(s, e, rows):
        xb = xbuf[s, 0:rows, 0:D_MODEL].astype(BF16)
        gate = (xbuf[s, 0:rows, D_MODEL:D_MODEL + 1] + xbuf[s, 0:rows, D_MODEL + 1:D_MODEL + 2]
                + xbuf[s, 0:rows, D_MODEL + 2:D_MODEL + 3])
        hg = jnp.minimum(jnp.dot(xb, wg_s[...], preferred_element_type=F32) + bg_ref[e], SWIGLU_LIMIT)
        hu = jnp.clip(jnp.dot(xb, wu_s[...], preferred_element_type=F32) + bu_ref[e],
                      -SWIGLU_LIMIT, SWIGLU_LIMIT)
        hact = (hu + 1.0) * (hg * _sigmoid(SWIGLU_ALPHA * hg))
        y = jnp.dot(hact.astype(BF16), wd_s[...], preferred_element_type=F32) + bd_ref[e]
        ybuf[s, 0:rows, :] = y * gate

    def body(i, carry):
        s = lax.rem(i, 2)
        e = ite_ref[i]
        by_size(i, lambda rows: x_copy(i, s, rows).wait())

        @pl.when(i + 1 < n_iter)
        def _():
            by_size(i + 1, lambda rows: x_copy(i + 1, 1 - s, rows).start())

        @pl.when(jnp.logical_or(i == 0, e != ite_ref[jnp.maximum(i - 1, 0)]))
        def _():
            for c in weight_copies(e):
                c.wait()
            wg_s[...] = wbuf[0].astype(BF16)
            wu_s[...] = wbuf[1].astype(BF16)
            wd_s[...] = wbuf[2].astype(BF16)

            @pl.when(nxt_ref[i] != e)
            def _():
                for c in weight_copies(nxt_ref[i]):
                    c.start()

        @pl.when(i >= 2)
        def _():
            by_size(i - 2, lambda rows: y_copy(i - 2, s, rows).wait())

        def chunk(rows):
            ffn(s, e, rows)
            y_copy(i, s, rows).start()
        by_size(i, chunk)
        return carry
    lax.fori_loop(0, n_iter, body, 0)

    @pl.when(n_iter >= 2)
    def _():
        by_size(n_iter - 2, lambda rows: y_copy(n_iter - 2, lax.rem(n_iter, 2), rows).wait())
    by_size(n_iter - 1, lambda rows: y_copy(n_iter - 1, lax.rem(n_iter - 1, 2), rows).wait())

    def wait_zero(m, carry):
        zero_copy(m).wait()
        return carry
    lax.fori_loop(nv_ref[0], n_blocks, wait_zero, 0)


def _expert_call(sched, xs, w_gate, b_gate, w_up, b_up, w_down, b_down):
    n_blocks = xs.shape[0] // EXPERT_BLOCK
    anyspec = pl.BlockSpec(memory_space=pl.ANY)
    bspec = pl.BlockSpec((N_EXPERTS, 1, D_MODEL), lambda i, *_: (0, 0, 0))
    grid_spec = pltpu.PrefetchScalarGridSpec(
        num_scalar_prefetch=len(sched),
        grid=(1,),
        in_specs=[anyspec, anyspec, bspec, anyspec, bspec, anyspec, bspec],
        out_specs=anyspec,
        scratch_shapes=[
            pltpu.VMEM((2, max(CHUNK_BLOCKS) * EXPERT_BLOCK, XS_WIDTH), F32),
            pltpu.VMEM((2, max(CHUNK_BLOCKS) * EXPERT_BLOCK, D_MODEL), F32),
            pltpu.VMEM((EXPERT_BLOCK, D_MODEL), F32),
            pltpu.VMEM((3, D_MODEL, D_MODEL), F32),
            pltpu.SemaphoreType.DMA((2,)),
            pltpu.SemaphoreType.DMA((2,)),
            pltpu.SemaphoreType.DMA(()),
            pltpu.SemaphoreType.DMA(()),
            pltpu.VMEM((D_MODEL, D_MODEL), BF16),
            pltpu.VMEM((D_MODEL, D_MODEL), BF16),
            pltpu.VMEM((D_MODEL, D_MODEL), BF16),
        ],
    )
    return pl.pallas_call(
        _expert_body,
        grid_spec=grid_spec,
        out_shape=jax.ShapeDtypeStruct((n_blocks * EXPERT_BLOCK, D_MODEL), F32),
        compiler_params=pltpu.CompilerParams(
            dimension_semantics=("arbitrary",), vmem_limit_bytes=VMEM_LIMIT),
        name="experts",
    )(*sched, xs, w_gate, b_gate.reshape(N_EXPERTS, 1, D_MODEL),
      w_up, b_up.reshape(N_EXPERTS, 1, D_MODEL), w_down, b_down.reshape(N_EXPERTS, 1, D_MODEL))


def _combine_body(units_ref, g8_ref, l8_ref, tot_ref,
                  y_hbm, x1_ref, meta_ref, lst_ref, g_ref, b_ref, out_ref, ys_s, sems):
    tau = pl.program_id(0)
    n = pl.num_programs(0)
    slot = lax.rem(tau, 2)
    TD, S = TILE_ROWS, SORT_ROWS

    def run_copy(s, g_unit, l_unit, n_units):
        return pltpu.make_async_copy(y_hbm.at[_rows(g_unit, n_units), :],
                                     ys_s.at[s, _rows(l_unit, n_units), :], sems.at[s])

    def fetch(tile, s):
        _for_each_run(units_ref, tile, lambda k, cnt: run_copy(s, g8_ref[k], l8_ref[k], cnt).start())

    @pl.when(tau == 0)
    def _():
        ys_s[...] = jnp.zeros_like(ys_s)
        fetch(0, 0)

    @pl.when(tau + 1 < n)
    def _():
        fetch(tau + 1, 1 - slot)

    run_copy(slot, 0, 0, tot_ref[tau]).wait()

    meta = meta_ref[...]
    lane_f = lax.broadcasted_iota(jnp.int32, (TD, LANES), 1).astype(F32)
    u_row = lst_ref[2:3, :]
    s_iota = lax.broadcasted_iota(jnp.int32, (TD, S), 1).astype(F32)
    conds = []
    for j in range(TOP_K):
        oh = lane_f == meta[:, j:j + 1]
        start8 = jnp.sum(jnp.where(oh, u_row, 0.0), axis=-1, keepdims=True)
        lpos = RUN_ALIGN * start8 + meta[:, TOP_K + j:TOP_K + j + 1]
        conds.append(s_iota == lpos)
    perm = jnp.where(conds[0], 1.0, jnp.where(conds[1], 1.0, jnp.where(conds[2], 1.0,
                     jnp.where(conds[3], 1.0, 0.0)))).astype(BF16)
    ys = ys_s[slot]
    yh = ys.astype(BF16)
    yl = (ys - yh.astype(F32)).astype(BF16)
    ffn = (jnp.dot(perm, yh, preferred_element_type=F32) + jnp.dot(perm, yl, preferred_element_type=F32))
    acc = DEEPNORM_ALPHA * x1_ref[...] + ffn
    mu = jnp.mean(acc, axis=-1, keepdims=True)
    hc = acc - mu
    var = jnp.mean(hc * hc, axis=-1, keepdims=True)
    out_ref[...] = hc * lax.rsqrt(var + LN_EPS) * g_ref[...] + b_ref[...]


def _combine_call(sched, y_rows, x1, meta, lst_rows, ln2_g, ln2_b):
    T = x1.shape[0]
    TD = TILE_ROWS
    grid_spec = pltpu.PrefetchScalarGridSpec(
        num_scalar_prefetch=4,
        grid=(T // TD,),
        in_specs=[
            pl.BlockSpec(memory_space=pl.ANY),
            pl.BlockSpec((TD, D_MODEL), lambda i, *_: (i, 0)),
            pl.BlockSpec((TD, LANES), lambda i, *_: (i, 0)),
            pl.BlockSpec((SUBLANES, LANES), lambda i, *_: (i, 0)),
            pl.BlockSpec((1, D_MODEL), lambda i, *_: (0, 0)),
            pl.BlockSpec((1, D_MODEL), lambda i, *_: (0, 0)),
        ],
        out_specs=pl.BlockSpec((TD, D_MODEL), lambda i, *_: (i, 0)),
        scratch_shapes=[pltpu.VMEM((2, SORT_ROWS, D_MODEL), F32), pltpu.SemaphoreType.DMA((2,))],
    )
    return pl.pallas_call(
        _combine_body,
        grid_spec=grid_spec,
        out_shape=jax.ShapeDtypeStruct((T, D_MODEL), F32),
        compiler_params=pltpu.CompilerParams(
            dimension_semantics=("arbitrary",), vmem_limit_bytes=VMEM_LIMIT),
        name="combine",
    )(*sched, y_rows, x1, meta, lst_rows, ln2_g, ln2_b)


def _np_consts():
    TL = TILE_ROWS
    r = np.arange(TL)
    same = (r[:, None] // CHUNK) == (r[None, :] // CHUNK)
    tril = (same & (r[None, :] <= r[:, None])).astype(np.float32)
    trils = (r[None, :] < r[:, None]).astype(np.float32)
    e128 = np.zeros((LANES, SSD_WIDTH), np.float32)
    for h in range(SSD_HEADS):
        e128[h, h * SSD_HEAD_DIM:(h + 1) * SSD_HEAD_DIM] = 1.0
    return tril, trils, e128


def kernel(x, w_in, hg_lower_bound, hg_norm_w, conv_w, conv_b, dt_bias, a_log, d_skip, ssd_norm_w, w_out,
           ln1_g, ln1_b, router_w, router_b, w_gate, b_gate, w_up, b_up, w_down, b_down, ln2_g, ln2_b):
    batch, seq, d = x.shape
    assert d == D_MODEL and seq % STEP_ROWS == 0 and w_in.shape[0] == DEPTH
    T = batch * seq
    n_tiles = T // TILE_ROWS
    max_rows = T * TOP_K + n_tiles * N_EXPERTS * (RUN_ALIGN - 1)
    n_blocks = -(-max_rows // EXPERT_BLOCK) + N_EXPERTS
    cap = n_blocks * EXPERT_BLOCK
    assert SORT_ROWS >= TILE_ROWS * TOP_K + N_EXPERTS * (RUN_ALIGN - 1)

    tril, trils, e128 = _np_consts()
    w = w_in[0]
    pad_l = LANES - SSD_HEADS
    rw = jnp.pad(router_w[0], ((0, 0), (0, LANES - N_EXPERTS)))
    rwh = rw.astype(BF16)
    rwm = (rw - rwh.astype(F32)).astype(BF16)
    consts = [
        jnp.pad(w, ((0, 0), (0, pad_l))).astype(BF16),
        hg_lower_bound,
        hg_norm_w[0].reshape(1, HG_HEAD_DIM),
        conv_w[0],
        conv_b[0].reshape(1, SSD_CONV_DIM),
        jnp.pad(dt_bias[0], (0, pad_l)).reshape(1, LANES),
        jnp.pad(a_log[0], (0, pad_l)).reshape(1, LANES),
        jnp.repeat(d_skip[0], SSD_HEAD_DIM).reshape(1, SSD_WIDTH),
        ssd_norm_w[0].reshape(1, SSD_WIDTH),
        w_out[0].astype(BF16),
        ln1_g[0].reshape(1, D_MODEL),
        ln1_b[0].reshape(1, D_MODEL),
        jnp.concatenate([rwh, rwm], axis=1),
        rwh,
        jnp.pad(router_b[0], (0, LANES - N_EXPERTS)).reshape(1, LANES),
        jnp.asarray(tril, BF16), jnp.asarray(trils, BF16), jnp.asarray(e128, BF16),
    ]
    x1, meta, cnt = _mixer_call(x.reshape(T, D_MODEL), consts, batch, seq)

    counts = cnt.reshape(n_tiles, SUBLANES, LANES)[:, 0, :N_EXPERTS].astype(jnp.int32)
    c8 = (counts + RUN_ALIGN - 1) // RUN_ALIGN * RUN_ALIGN
    used = jnp.sum(c8, axis=0)
    region = (used + EXPERT_BLOCK - 1) // EXPERT_BLOCK * EXPERT_BLOCK
    region_end = jnp.cumsum(region)
    region_start = region_end - region
    gstart = region_start[None, :] + jnp.cumsum(c8, axis=0) - c8
    lstart = jnp.cumsum(c8, axis=1) - c8
    n_valid = region_end[-1] // EXPERT_BLOCK
    has = region > 0
    eidx = jnp.arange(N_EXPERTS, dtype=jnp.int32)
    suffix_min = lax.cummin(jnp.where(has, eidx, N_EXPERTS), reverse=True)
    nxt_e = jnp.concatenate([suffix_min[1:], jnp.full((1,), N_EXPERTS, jnp.int32)])
    nxt_e = jnp.where(nxt_e == N_EXPERTS, eidx, nxt_e)
    left = region // EXPERT_BLOCK
    n_of = []
    for nb in CHUNK_BLOCKS:
        n_of.append(left // nb)
        left = left % nb
    it_cnt = sum(n_of)
    it_end = jnp.cumsum(it_cnt)
    n_iter = it_end[-1]
    max_iter = n_blocks // max(CHUNK_BLOCKS) + (len(CHUNK_BLOCKS) - 1) * N_EXPERTS
    it = jnp.minimum(jnp.arange(max_iter, dtype=jnp.int32), n_iter - 1)
    it_e = jnp.minimum(jnp.sum(it_end[None, :] <= it[:, None], axis=1), N_EXPERTS - 1).astype(jnp.int32)
    it_onehot = it_e[:, None] == eidx[None, :]
    lookup = lambda tab: jnp.sum(jnp.where(it_onehot, tab[None, :].astype(jnp.int32), 0), axis=1)
    k = it - lookup(it_end - it_cnt)
    it_sz = jnp.zeros_like(it)
    it_blk = lookup(region_start // EXPERT_BLOCK)
    for idx, nb in enumerate(CHUNK_BLOCKS):
        n_here = lookup(n_of[idx])
        inside = jnp.logical_and(k >= 0, k < n_here)
        it_sz = jnp.where(inside, idx, it_sz)
        it_blk = it_blk + nb * jnp.clip(k, 0, n_here)
        k = k - n_here
    expert_sched = (it_e, it_blk.astype(jnp.int32), it_sz.astype(jnp.int32), lookup(nxt_e).astype(jnp.int32),
                    n_iter.astype(jnp.int32).reshape(1), n_valid.astype(jnp.int32).reshape(1))
    as_units = lambda a: (a // RUN_ALIGN).astype(jnp.int32).reshape(-1)
    units, g8, l8 = as_units(c8), as_units(gstart), as_units(lstart)
    tot = (jnp.sum(c8, axis=1) // RUN_ALIGN).astype(jnp.int32)
    tail8 = as_units(region_start + used)
    tailn = as_units(region - used)
    lu = lstart // RUN_ALIGN
    lst3 = jnp.stack([lu // LST_SPLIT, lu % LST_SPLIT, lu], axis=1).astype(F32)
    lst_rows = jnp.pad(lst3, ((0, 0), (0, SUBLANES - 3), (0, LANES - N_EXPERTS))).reshape(
        n_tiles * SUBLANES, LANES)

    nv = n_valid.astype(jnp.int32).reshape(1)
    xs = _dispatch_call((units, g8, l8, tot, tail8, tailn, nv), x1, meta, lst_rows, cap)
    y_rows = _expert_call(expert_sched, xs, w_gate[0], b_gate[0], w_up[0], b_up[0], w_down[0], b_down[0])
    out = _combine_call((units, g8, l8, tot), y_rows, x1, meta, lst_rows,
                        ln2_g[0].reshape(1, D_MODEL), ln2_b[0].reshape(1, D_MODEL))
    return out.reshape(batch, seq, D_MODEL)
```

```python
import functools

import jax
import jax.numpy as jnp
import numpy as np
from jax import lax
from jax.experimental import pallas as pl
from jax.experimental.pallas import tpu as pltpu

F32 = jnp.float32
BF16 = jnp.bfloat16

D_MODEL = 1024
CHUNK = 64
HG_WIDTH = 512
HG_HEAD_DIM = 128
HG_HEADS = 4
SSD_WIDTH = 512
SSD_HEAD_DIM = 64
SSD_HEADS = 8
SSD_GROUPS = 2
SSD_STATE = 128
SSD_CONV = 4
SSD_CONV_DIM = SSD_WIDTH + 2 * SSD_GROUPS * SSD_STATE
N_EXPERTS = 32
TOP_K = 4
EXPERT_BLOCK = 256
CHUNK_BLOCKS = (2, 1)
SWIGLU_LIMIT = 7.0
SWIGLU_ALPHA = 1.702
DEPTH = 1
DEEPNORM_ALPHA = (2 * DEPTH) ** 0.25
LN_EPS = 1e-5
RMS_EPS = 1e-5

LANES = 128
SUBLANES = 8
SUB_CHUNK = 16
EXP_CAP = 60.0
TILE_ROWS = 256
MIX_TILES = 2
ROUTE_TILES = 2
STEP_ROWS = MIX_TILES * TILE_ROWS
RUN_ALIGN = SUBLANES
SORT_ROWS = 1280
LST_SPLIT = 32
XS_WIDTH = D_MODEL + LANES
VMEM_LIMIT = 56 * 1024 * 1024

OFF_Q, OFF_F, OFF_I, OFF_G = 0, 512, 1024, 1536
OFF_Z, OFF_XBC = 2048, 2560
OFF_XS, OFF_B, OFF_C = 2560, 3072, 3328
OFF_DT = 3584
PROJ_COLS = OFF_DT + LANES
PROJ_SLICE = 256
CONV_SLICE = 256
OUT_SLICE = 256


def _bdot(a, b):
    return jnp.dot(a.astype(BF16), b.astype(BF16), preferred_element_type=F32)


def _bdot_nt(a, b):
    return lax.dot_general(a.astype(BF16), b.astype(BF16), (((1,), (1,)), ((), ())),
                           preferred_element_type=F32)


def _bdot_tn(a, b):
    return lax.dot_general(a.astype(BF16), b.astype(BF16), (((0,), (0,)), ((), ())),
                           preferred_element_type=F32)


def _split3(a):
    hi = a.astype(BF16)
    r1 = a - hi.astype(F32)
    mid = r1.astype(BF16)
    lo = (r1 - mid.astype(F32)).astype(BF16)
    return hi, mid, lo


def _dot01_left(m01, a):
    hi, mid, lo = _split3(a)
    d = functools.partial(jnp.dot, m01, preferred_element_type=F32)
    return d(hi) + d(mid) + d(lo)


def _dot01_right(a, m01):
    hi, mid, lo = _split3(a)
    return (jnp.dot(hi, m01, preferred_element_type=F32) + jnp.dot(mid, m01, preferred_element_type=F32)
            + jnp.dot(lo, m01, preferred_element_type=F32))


def _sigmoid(x):
    return 1.0 / (1.0 + jnp.exp(-x))


def _silu(x):
    return x * _sigmoid(x)


def _softplus(x):
    return jnp.maximum(x, 0.0) + jnp.log(1.0 + jnp.exp(-jnp.abs(x)))


def _mixer_body(x_ref, win_ref, lbp_ref, hgnw_ref, convw_ref, convb_ref, dtb_ref, alog_ref, dskip_ref,
                ssdnw_ref, wout_ref, ln1g_ref, ln1b_ref, rw1_ref, rwh_ref, rb_ref,
                tril_ref, trils_ref, e128_ref,
                x1_ref, meta_ref, cnt_ref,
                proj_s, xb_s, b_s, xpad_s, xdt_s, cse_s, st_s, pt_s, cat_s, ossd_s, mix_s):
    TL = TILE_ROWS
    t = pl.program_id(1)

    @pl.when(t == 0)
    def _():
        xpad_s[0:SUBLANES, :] = jnp.zeros((SUBLANES, SSD_CONV_DIM), F32)
        st_s[...] = jnp.zeros_like(st_s)
        pt_s[...] = jnp.zeros_like(pt_s)

    row64 = lax.broadcasted_iota(jnp.int32, (CHUNK, CHUNK), 0)
    col64 = lax.broadcasted_iota(jnp.int32, (CHUNK, CHUNK), 1)
    causal = row64 >= col64
    gw = SSD_WIDTH // SSD_GROUPS
    hpg = SSD_HEADS // SSD_GROUPS
    lane_head = lax.broadcasted_iota(jnp.int32, (CHUNK, gw), 1) // SSD_HEAD_DIM

    pending = []

    def filler():
        if pending:
            pending.pop(0)()

    def project_slices(base):
        def one(c0, c1):
            def run():
                proj_s[base:base + TL, c0:c1] = jnp.dot(xb_s[base:base + TL, :], win_ref[:, c0:c1],
                                                        preferred_element_type=F32)
            return run
        return [one(c0, min(c0 + PROJ_SLICE, PROJ_COLS)) for c0 in range(0, PROJ_COLS, PROJ_SLICE)]

    def hg_front(base):
        a0 = lbp_ref[0:1, :]
        a1 = lbp_ref[1:2, :]
        am = jnp.maximum(a0, a1)
        e0 = jnp.exp(a0 - am)
        e1 = jnp.exp(a1 - am)
        lb = e0 / (e0 + e1)
        f = lb + (1.0 - lb) * _sigmoid(proj_s[base:base + TL, OFF_F:OFF_F + HG_WIDTH])
        b_s[base:base + TL, :] = _dot01_left(tril_ref[...], jnp.log(f))
        proj_s[base:base + TL, OFF_F:OFF_F + HG_WIDTH] = f
        proj_s[base:base + TL, OFF_Q:OFF_Q + HG_WIDTH] = _silu(proj_s[base:base + TL, OFF_Q:OFF_Q + HG_WIDTH])
        units = {}
        for c in range(TL // CHUNK):
            r0 = base + c * CHUNK
            for h in range(HG_HEADS):
                h0 = h * HG_HEAD_DIM
                bc = b_s[r0:r0 + CHUNK, h0:h0 + HG_HEAD_DIM]
                qc = proj_s[r0:r0 + CHUNK, OFF_Q + h0:OFF_Q + h0 + HG_HEAD_DIM]
                kc = 1.0 - proj_s[r0:r0 + CHUNK, OFF_F + h0:OFF_F + h0 + HG_HEAD_DIM]
                vcb = proj_s[r0:r0 + CHUNK, OFF_I + h0:OFF_I + h0 + HG_HEAD_DIM].astype(BF16)
                parts = []
                for i in range(CHUNK // SUB_CHUNK):
                    s0 = i * SUB_CHUNK
                    if i == 0:
                        qi = qc[0:SUB_CHUNK] * jnp.exp(bc[0:SUB_CHUNK])
                        ki = kc * jnp.exp(jnp.minimum(-bc, EXP_CAP))
                    else:
                        ref_i = bc[s0 - 1:s0, :]
                        qi = qc[s0:s0 + SUB_CHUNK] * jnp.exp(bc[s0:s0 + SUB_CHUNK] - ref_i)
                        ki = kc * jnp.exp(jnp.minimum(ref_i - bc, EXP_CAP))
                    parts.append(_bdot_nt(qi, ki))
                b_end = bc[CHUNK - 1:CHUNK, :]
                kdec = kc * jnp.exp(b_end - bc)
                units[c, h] = dict(
                    parts=parts, vcb=vcb, qdec=(qc * jnp.exp(bc)).astype(BF16),
                    local=_bdot_tn(vcb, kdec),
                    decay=jnp.exp(b_end))
                if h % 2 == 1:
                    filler()
        return units

    def hg_back(base, units):
        hgnw = hgnw_ref[...]
        for h in range(HG_HEADS):
            st = st_s[h]
            for c in range(TL // CHUNK):
                u = units[c, h]
                u['state'] = st
                st = st * u['decay'] + u['local']
            st_s[h] = st
        for c in range(TL // CHUNK):
            r0 = base + c * CHUNK
            for h in range(HG_HEADS):
                h0 = h * HG_HEAD_DIM
                u = units[c, h]
                sc = jnp.where(causal, jnp.concatenate(u['parts'], axis=0), 0.0)
                o = _bdot(sc, u['vcb']) + _bdot_nt(u['qdec'], u['state'])
                ms = jnp.mean(o * o, axis=-1, keepdims=True)
                on = o * lax.rsqrt(ms + RMS_EPS) * hgnw
                gc = proj_s[r0:r0 + CHUNK, OFF_G + h0:OFF_G + h0 + HG_HEAD_DIM]
                cat_s[r0:r0 + CHUNK, h0:h0 + HG_HEAD_DIM] = (on * _silu(gc)).astype(BF16)
                filler()

    def ssd_front(base):
        e128 = e128_ref[...]
        dtc = _softplus(proj_s[base:base + TL, OFF_DT:OFF_DT + LANES] + dtb_ref[...])
        a_row = -jnp.exp(alog_ref[...])
        cs_c = _dot01_left(tril_ref[...], dtc * a_row)
        cs_r = jnp.transpose(cs_c)[0:SSD_HEADS, :]
        cse_s[base:base + TL, :] = _dot01_right(cs_c, e128)
        dt_exp = _dot01_right(dtc, e128)

        xpad_s[SUBLANES + base:SUBLANES + base + TL, :] = proj_s[base:base + TL, OFF_XBC:OFF_XBC + SSD_CONV_DIM]
        for c0 in range(0, SSD_CONV_DIM, CONV_SLICE):
            acc = jnp.broadcast_to(convb_ref[:, c0:c0 + CONV_SLICE], (TL, CONV_SLICE))
            for j in range(SSD_CONV):
                off = base + SUBLANES - (SSD_CONV - 1) + j
                acc = acc + convw_ref[j:j + 1, c0:c0 + CONV_SLICE] * xpad_s[off:off + TL, c0:c0 + CONV_SLICE]
            proj_s[base:base + TL, OFF_XBC + c0:OFF_XBC + c0 + CONV_SLICE] = _silu(acc)
            filler()
        xdt_s[base:base + TL, :] = proj_s[base:base + TL, OFF_XS:OFF_XS + SSD_WIDTH] * dt_exp

        units = {}
        for c in range(TL // CHUNK):
            r0 = base + c * CHUNK
            for g in range(SSD_GROUPS):
                g0 = g * gw
                bgb = proj_s[r0:r0 + CHUNK, OFF_B + g * SSD_STATE:OFF_B + (g + 1) * SSD_STATE].astype(BF16)
                cgb = proj_s[r0:r0 + CHUNK, OFF_C + g * SSD_STATE:OFF_C + (g + 1) * SSD_STATE].astype(BF16)
                cse_g = cse_s[r0:r0 + CHUNK, g0:g0 + gw]
                cs_end = cse_s[r0 + CHUNK - 1:r0 + CHUNK, g0:g0 + gw]
                xdt_g = xdt_s[r0:r0 + CHUNK, g0:g0 + gw]
                units[c, g] = dict(
                    cgb=cgb, gm=_bdot_nt(cgb, bgb),
                    local=_bdot_tn(bgb, xdt_g * jnp.exp(cs_end - cse_g)),
                    decay=jnp.exp(cs_end))
                filler()
        return units, cs_r

    def ssd_back(base, units, cs_r):
        for g in range(SSD_GROUPS):
            pt = pt_s[g]
            for c in range(TL // CHUNK):
                u = units[c, g]
                u['state'] = pt
                pt = pt * u['decay'] + u['local']
            pt_s[g] = pt
        for c in range(TL // CHUNK):
            r0 = base + c * CHUNK
            for g in range(SSD_GROUPS):
                g0 = g * gw
                u = units[c, g]
                cse_g = cse_s[r0:r0 + CHUNK, g0:g0 + gw]
                xdt_g = xdt_s[r0:r0 + CHUNK, g0:g0 + gw]
                ydiag = jnp.zeros((CHUNK, gw), F32)
                for hl in range(hpg):
                    hh = g * hpg + hl
                    seg = (cse_g[:, hl * SSD_HEAD_DIM:(hl + 1) * SSD_HEAD_DIM]
                           - cs_r[hh:hh + 1, c * CHUNK:(c + 1) * CHUNK])
                    lm = jnp.where(causal, jnp.exp(jnp.minimum(seg, 0.0)), 0.0)
                    xm = jnp.where(lane_head == hl, xdt_g, 0.0)
                    ydiag = ydiag + _bdot(u['gm'] * lm, xm)
                yoff = _bdot(u['cgb'], u['state']) * jnp.exp(cse_g)
                xs_g = proj_s[r0:r0 + CHUNK, OFF_XS + g0:OFF_XS + g0 + gw]
                ossd_s[r0:r0 + CHUNK, g0:g0 + gw] = ydiag + yoff + xs_g * dskip_ref[:, g0:g0 + gw]
                filler()

    def ssd_gate_norm(tile):
        base = tile * TL
        y = ossd_s[base:base + TL, :] * _silu(proj_s[base:base + TL, OFF_Z:OFF_Z + SSD_WIDTH])
        for g in range(SSD_GROUPS):
            yg = y[:, g * gw:(g + 1) * gw]
            ms = jnp.mean(yg * yg, axis=-1, keepdims=True)
            yn = yg * lax.rsqrt(ms + RMS_EPS) * ssdnw_ref[:, g * gw:(g + 1) * gw]
            cat_s[base:base + TL, HG_WIDTH + g * gw:HG_WIDTH + (g + 1) * gw] = yn.astype(BF16)

    def out_slice(tile, c0):
        base = tile * TL
        mix_s[base:base + TL, c0:c0 + OUT_SLICE] = jnp.dot(
            cat_s[base:base + TL, :], wout_ref[:, c0:c0 + OUT_SLICE], preferred_element_type=F32)

    def norm_route(tile):
        base = tile * TL
        hres = DEEPNORM_ALPHA * x_ref[base:base + TL, :] + mix_s[base:base + TL, :]
        mu = jnp.mean(hres, axis=-1, keepdims=True)
        hc = hres - mu
        var = jnp.mean(hc * hc, axis=-1, keepdims=True)
        x1 = hc * lax.rsqrt(var + LN_EPS) * ln1g_ref[...] + ln1b_ref[...]
        x1_ref[base:base + TL, :] = x1

        xh = x1.astype(BF16)
        xm_ = (x1 - xh.astype(F32)).astype(BF16)
        t1 = jnp.dot(xh, rw1_ref[...], preferred_element_type=F32)
        logits = (t1[:, 0:LANES] + t1[:, LANES:2 * LANES]
                  + jnp.dot(xm_, rwh_ref[...], preferred_element_type=F32) + rb_ref[...])
        lane = lax.broadcasted_iota(jnp.int32, (TL, LANES), 1)
        lane_f = lane.astype(F32)
        neg = jnp.float32(-jnp.inf)
        work = jnp.where(lane < N_EXPERTS, logits, neg)
        onehots, vals, idxs = [], [], []
        for j in range(TOP_K):
            m = jnp.max(work, axis=-1, keepdims=True)
            idx = jnp.min(jnp.where(work == m, lane_f, float(LANES)), axis=-1, keepdims=True)
            oh = lane_f == idx
            onehots.append(oh)
            vals.append(m)
            idxs.append(idx)
            work = jnp.where(oh, neg, work)
        es = [jnp.exp(v - vals[0]) for v in vals]
        den = es[0] + es[1] + es[2] + es[3]
        gates = [e / den for e in es]
        sel = jnp.zeros((TL, LANES), F32)
        for oh in onehots:
            sel = jnp.where(oh, 1.0, sel)
        rankmat = jnp.dot(trils_ref[...], sel.astype(BF16), preferred_element_type=F32)
        cnt_ref[tile * SUBLANES:(tile + 1) * SUBLANES, :] = jnp.broadcast_to(
            jnp.sum(sel, axis=0, keepdims=True), (SUBLANES, LANES))
        meta = jnp.zeros((TL, LANES), F32)
        for j in range(TOP_K):
            rank_j = jnp.sum(jnp.where(onehots[j], rankmat, 0.0), axis=-1, keepdims=True)
            meta = jnp.where(lane == j, idxs[j], meta)
            meta = jnp.where(lane == TOP_K + j, rank_j, meta)
            meta = jnp.where(lane == 2 * TOP_K + j, gates[j], meta)
        meta_ref[base:base + TL, :] = meta

    def post_thunks(tile):
        return ([functools.partial(ssd_gate_norm, tile)]
                + [functools.partial(out_slice, tile, c0) for c0 in range(0, D_MODEL, OUT_SLICE)]
                + [functools.partial(norm_route, tile)])

    xb_s[...] = x_ref[...].astype(BF16)
    for run in project_slices(0):
        run()
    for tile in range(MIX_TILES):
        base = tile * TL
        if tile + 1 < MIX_TILES:
            pending.extend(project_slices((tile + 1) * TL))
        hg_units = hg_front(base)
        ssd_units, cs_r = ssd_front(base)
        while pending:
            filler()
        hg_back(base, hg_units)
        ssd_back(base, ssd_units, cs_r)
        pending.extend(post_thunks(tile))
    while pending:
        filler()
    xpad_s[0:SUBLANES, :] = xpad_s[STEP_ROWS:STEP_ROWS + SUBLANES, :]


def _full(shape):
    nd = len(shape)
    return pl.BlockSpec(shape, lambda *_: (0,) * nd)


def _mixer_call(x2d, consts, batch, seq):
    SR = STEP_ROWS
    nt = seq // SR
    T = batch * seq
    in_specs = [pl.BlockSpec((SR, D_MODEL), lambda b, t: (b * nt + t, 0))]
    in_specs += [_full(c.shape) for c in consts]
    out_shape = (jax.ShapeDtypeStruct((T, D_MODEL), F32),
                 jax.ShapeDtypeStruct((T, LANES), F32),
                 jax.ShapeDtypeStruct((T // TILE_ROWS * SUBLANES, LANES), F32))
    out_specs = (pl.BlockSpec((SR, D_MODEL), lambda b, t: (b * nt + t, 0)),
                 pl.BlockSpec((SR, LANES), lambda b, t: (b * nt + t, 0)),
                 pl.BlockSpec((MIX_TILES * SUBLANES, LANES), lambda b, t: (b * nt + t, 0)))
    scratch = [
        pltpu.VMEM((SR, PROJ_COLS), F32),
        pltpu.VMEM((SR, D_MODEL), BF16),
        pltpu.VMEM((SR, HG_WIDTH), F32),
        pltpu.VMEM((SR + 2 * SUBLANES, SSD_CONV_DIM), F32),
        pltpu.VMEM((SR, SSD_WIDTH), F32),
        pltpu.VMEM((SR, SSD_WIDTH), F32),
        pltpu.VMEM((HG_HEADS, HG_HEAD_DIM, HG_HEAD_DIM), F32),
        pltpu.VMEM((SSD_GROUPS, SSD_STATE, SSD_WIDTH // SSD_GROUPS), F32),
        pltpu.VMEM((SR, D_MODEL), BF16),
        pltpu.VMEM((SR, SSD_WIDTH), F32),
        pltpu.VMEM((SR, D_MODEL), F32),
    ]
    return pl.pallas_call(
        _mixer_body,
        grid=(batch, nt),
        in_specs=in_specs,
        out_specs=out_specs,
        out_shape=out_shape,
        scratch_shapes=scratch,
        compiler_params=pltpu.CompilerParams(
            dimension_semantics=("arbitrary", "arbitrary"), vmem_limit_bytes=VMEM_LIMIT),
        name="mixer",
    )(x2d, *consts)


def _for_each_run(units_ref, tile, fn):
    def per_expert(e, carry):
        k = tile * N_EXPERTS + e
        n = units_ref[k]

        @pl.when(n > 0)
        def _():
            fn(k, n)
        return carry
    lax.fori_loop(0, N_EXPERTS, per_expert, 0)


def _rows(unit, n_units=1):
    start = unit * RUN_ALIGN
    if RUN_ALIGN > 1:
        start = pl.multiple_of(start, RUN_ALIGN)
    return pl.ds(start, n_units * RUN_ALIGN)


def _dispatch_body(units_ref, g8_ref, l8_ref, tot_ref, tail8_ref, tailn_ref, nv_ref,
                   x1_ref, meta_ref, lst_ref, xs_hbm, sorted_s, zero_s, sems, zsem, bsem):
    step = pl.program_id(0)
    n = pl.num_programs(0)
    TD, S = TILE_ROWS, SORT_ROWS
    n_blocks = xs_hbm.shape[0] // EXPERT_BLOCK

    def unused_block_copy(m):
        rows = pl.ds(pl.multiple_of(m * EXPERT_BLOCK, EXPERT_BLOCK), EXPERT_BLOCK)
        return pltpu.make_async_copy(zero_s, xs_hbm.at[rows, :], bsem)

    def run_copy(s, l_unit, g_unit, n_units):
        return pltpu.make_async_copy(sorted_s.at[s, _rows(l_unit, n_units), :],
                                     xs_hbm.at[_rows(g_unit, n_units), :], sems.at[s])

    def wait_units(s, count):
        run_copy(s, 0, 0, count).wait()

    @pl.when(step == 0)
    def _():
        zero_s[...] = jnp.zeros_like(zero_s)

        def tail_copy(e):
            n_units = tailn_ref[e]
            return pltpu.make_async_copy(zero_s.at[pl.ds(0, n_units * RUN_ALIGN), :],
                                         xs_hbm.at[_rows(tail8_ref[e], n_units), :], zsem)

        def start_unused(m, carry):
            unused_block_copy(m).start()
            return carry
        lax.fori_loop(nv_ref[0], n_blocks, start_unused, 0)

        def start_e(e, carry):
            @pl.when(tailn_ref[e] > 0)
            def _():
                tail_copy(e).start()
            return carry
        lax.fori_loop(0, N_EXPERTS, start_e, 0)

        def wait_e(e, carry):
            @pl.when(tailn_ref[e] > 0)
            def _():
                tail_copy(e).wait()
            return carry
        lax.fori_loop(0, N_EXPERTS, wait_e, 0)

    lane = lax.broadcasted_iota(jnp.int32, (TD, LANES), 1)
    lane_f = lane.astype(F32)
    ones8 = jnp.ones((SUBLANES, LANES), BF16)
    r_iota = lax.broadcasted_iota(jnp.int32, (S, TD), 0).astype(F32)
    nt_dims = (((1,), (1,)), ((), ()))
    for slot in range(ROUTE_TILES):
        _dispatch_tile(step * ROUTE_TILES + slot, slot, step, units_ref, g8_ref, l8_ref, tot_ref,
                       x1_ref, meta_ref, lst_ref, sorted_s, run_copy, wait_units,
                       lane, lane_f, ones8, r_iota, nt_dims)

    @pl.when(step == n - 1)
    def _():
        for slot in range(ROUTE_TILES):
            wait_units(slot, tot_ref[step * ROUTE_TILES + slot])

        def wait_unused(m, carry):
            unused_block_copy(m).wait()
            return carry
        lax.fori_loop(nv_ref[0], n_blocks, wait_unused, 0)


def _dispatch_tile(tau, slot, step, units_ref, g8_ref, l8_ref, tot_ref, x1_ref, meta_ref, lst_ref, sorted_s,
                   run_copy, wait_units, lane, lane_f, ones8, r_iota, nt_dims):
    TD = TILE_ROWS
    rows = slice(slot * TD, (slot + 1) * TD)

    @pl.when(step >= 1)
    def _():
        wait_units(slot, tot_ref[jnp.maximum(tau - ROUTE_TILES, 0)])

    meta = meta_ref[rows, :]
    u8 = lst_ref[slot * SUBLANES:(slot + 1) * SUBLANES, :].astype(BF16)
    gh = meta.astype(BF16).astype(F32)
    g1 = meta - gh
    gm = g1.astype(BF16).astype(F32)
    gl = g1 - gm
    conds, gparts = [], []
    for j in range(TOP_K):
        oh = lane_f == meta[:, j:j + 1]
        ohb = jnp.where(oh, 1.0, 0.0).astype(BF16)
        rkb = jnp.where(oh, meta[:, TOP_K + j:TOP_K + j + 1], 0.0).astype(BF16)
        m1 = lax.dot_general(u8, ohb, nt_dims, preferred_element_type=F32)
        m2 = lax.dot_general(ones8, rkb, nt_dims, preferred_element_type=F32)
        lpos = RUN_ALIGN * (LST_SPLIT * m1[0:1, :] + m1[1:2, :]) + m2[0:1, :]
        conds.append(r_iota == lpos)
        gc = 2 * TOP_K + j
        gparts.append(jnp.where(lane == 0, gh[:, gc:gc + 1],
                                jnp.where(lane == 1, gm[:, gc:gc + 1],
                                          jnp.where(lane == 2, gl[:, gc:gc + 1], 0.0))).astype(BF16))
    pcat = jnp.concatenate([jnp.where(c, 1.0, 0.0).astype(BF16) for c in conds], axis=1)
    sorted_g = jnp.dot(pcat, jnp.concatenate(gparts, axis=0), preferred_element_type=F32)
    perm = jnp.where(conds[0], 1.0, jnp.where(conds[1], 1.0, jnp.where(conds[2], 1.0,
                     jnp.where(conds[3], 1.0, 0.0)))).astype(BF16)
    sorted_x = jnp.dot(perm, x1_ref[rows, :].astype(BF16), preferred_element_type=F32)
    sorted_s[slot, :, 0:D_MODEL] = sorted_x
    sorted_s[slot, :, D_MODEL:XS_WIDTH] = sorted_g

    _for_each_run(units_ref, tau, lambda k, cnt: run_copy(slot, l8_ref[k], g8_ref[k], cnt).start())


def _dispatch_call(sched, x1, meta, lst_rows, cap):
    T = x1.shape[0]
    RR = ROUTE_TILES * TILE_ROWS
    grid_spec = pltpu.PrefetchScalarGridSpec(
        num_scalar_prefetch=7,
        grid=(T // RR,),
        in_specs=[
            pl.BlockSpec((RR, D_MODEL), lambda i, *_: (i, 0)),
            pl.BlockSpec((RR, LANES), lambda i, *_: (i, 0)),
            pl.BlockSpec((ROUTE_TILES * SUBLANES, LANES), lambda i, *_: (i, 0)),
        ],
        out_specs=pl.BlockSpec(memory_space=pl.ANY),
        scratch_shapes=[
            pltpu.VMEM((2, SORT_ROWS, XS_WIDTH), F32),
            pltpu.VMEM((EXPERT_BLOCK, XS_WIDTH), F32),
            pltpu.SemaphoreType.DMA((2,)),
            pltpu.SemaphoreType.DMA(()),
            pltpu.SemaphoreType.DMA(()),
        ],
    )
    return pl.pallas_call(
        _dispatch_body,
        grid_spec=grid_spec,
        out_shape=jax.ShapeDtypeStruct((cap, XS_WIDTH), F32),
        compiler_params=pltpu.CompilerParams(
            dimension_semantics=("arbitrary",), vmem_limit_bytes=VMEM_LIMIT),
        name="dispatch",
    )(*sched, x1, meta, lst_rows)


def _expert_body(ite_ref, itb_ref, sz_ref, nxt_ref, par_ref, nit_ref, nv_ref,
                 xs_hbm, wg_hbm, bg_ref, wu_hbm, bu_ref, wd_hbm, bd_ref,
                 y_hbm, xbuf, ybuf, zbuf, wbuf, xsem, ysem, zsem, wsem, wg_s, wu_s, wd_s):
    n_iter = nit_ref[0]
    n_blocks = xs_hbm.shape[0] // EXPERT_BLOCK
    sizes = tuple(n * EXPERT_BLOCK for n in CHUNK_BLOCKS)

    def rows_of(i, rows):
        return pl.ds(pl.multiple_of(itb_ref[i] * EXPERT_BLOCK, EXPERT_BLOCK), rows)

    def x_copy(i, s, rows):
        return pltpu.make_async_copy(xs_hbm.at[rows_of(i, rows), :], xbuf.at[s, pl.ds(0, rows), :], xsem.at[s])

    def y_copy(i, s, rows):
        return pltpu.make_async_copy(ybuf.at[s, pl.ds(0, rows), :], y_hbm.at[rows_of(i, rows), :], ysem.at[s])

    def by_size(i, fn):
        for k, rows in enumerate(sizes):
            @pl.when(sz_ref[i] == k)
            def _():
                fn(rows)

    def zero_copy(m):
        rows = pl.ds(pl.multiple_of(m * EXPERT_BLOCK, EXPERT_BLOCK), EXPERT_BLOCK)
        return pltpu.make_async_copy(zbuf, y_hbm.at[rows, :], zsem)

    def weight_copies(e, s):
        return [pltpu.make_async_copy(w.at[e], wbuf.at[s, k], wsem.at[s])
                for k, w in enumerate((wg_hbm, wu_hbm, wd_hbm))]

    for c in weight_copies(ite_ref[0], par_ref[0]):
        c.start()
    by_size(0, lambda rows: x_copy(0, 0, rows).start())

    zbuf[...] = jnp.zeros_like(zbuf)

    def start_zero(m, carry):
        zero_copy(m).start()
        return carry
    lax.fori_loop(nv_ref[0], n_blocks, start_zero, 0)

    def ffn(s, e, rows):
        xb = xbuf[s, 0:rows, 0:D_MODEL].astype(BF16)
        gate = (xbuf[s, 0:rows, D_MODEL:D_MODEL + 1] + xbuf[s, 0:rows, D_MODEL + 1:D_MODEL + 2]
                + xbuf[s, 0:rows, D_MODEL + 2:D_MODEL + 3])
        hg = jnp.minimum(jnp.dot(xb, wg_s[...], preferred_element_type=F32) + bg_ref[e], SWIGLU_LIMIT)
        hu = jnp.clip(jnp.dot(xb, wu_s[...], preferred_element_type=F32) + bu_ref[e],
                      -SWIGLU_LIMIT, SWIGLU_LIMIT)
        hact = (hu + 1.0) * (hg * _sigmoid(SWIGLU_ALPHA * hg))
        y = jnp.dot(hact.astype(BF16), wd_s[...], preferred_element_type=F32) + bd_ref[e]
        ybuf[s, 0:rows, :] = y * gate

    def body(i, carry):
        s = lax.rem(i, 2)
        e = ite_ref[i]
        by_size(i, lambda rows: x_copy(i, s, rows).wait())

        @pl.when(i + 1 < n_iter)
        def _():
            by_size(i + 1, lambda rows: x_copy(i + 1, 1 - s, rows).start())

        @pl.when(jnp.logical_or(i == 0, e != ite_ref[jnp.maximum(i - 1, 0)]))
        def _():
            ws = par_ref[i]
            for c in weight_copies(e, ws):
                c.wait()

            @pl.when(nxt_ref[i] != e)
            def _():
                for c in weight_copies(nxt_ref[i], 1 - ws):
                    c.start()
            wg_s[...] = wbuf[ws, 0].astype(BF16)
            wu_s[...] = wbuf[ws, 1].astype(BF16)
            wd_s[...] = wbuf[ws, 2].astype(BF16)

        @pl.when(i >= 2)
        def _():
            by_size(i - 2, lambda rows: y_copy(i - 2, s, rows).wait())

        def chunk(rows):
            ffn(s, e, rows)
            y_copy(i, s, rows).start()
        by_size(i, chunk)
        return carry
    lax.fori_loop(0, n_iter, body, 0)

    @pl.when(n_iter >= 2)
    def _():
        by_size(n_iter - 2, lambda rows: y_copy(n_iter - 2, lax.rem(n_iter, 2), rows).wait())
    by_size(n_iter - 1, lambda rows: y_copy(n_iter - 1, lax.rem(n_iter - 1, 2), rows).wait())

    def wait_zero(m, carry):
        zero_copy(m).wait()
        return carry
    lax.fori_loop(nv_ref[0], n_blocks, wait_zero, 0)


def _expert_call(sched, xs, w_gate, b_gate, w_up, b_up, w_down, b_down):
    n_blocks = xs.shape[0] // EXPERT_BLOCK
    anyspec = pl.BlockSpec(memory_space=pl.ANY)
    bspec = pl.BlockSpec((N_EXPERTS, 1, D_MODEL), lambda i, *_: (0, 0, 0))
    grid_spec = pltpu.PrefetchScalarGridSpec(
        num_scalar_prefetch=len(sched),
        grid=(1,),
        in_specs=[anyspec, anyspec, bspec, anyspec, bspec, anyspec, bspec],
        out_specs=anyspec,
        scratch_shapes=[
            pltpu.VMEM((2, max(CHUNK_BLOCKS) * EXPERT_BLOCK, XS_WIDTH), F32),
            pltpu.VMEM((2, max(CHUNK_BLOCKS) * EXPERT_BLOCK, D_MODEL), F32),
            pltpu.VMEM((EXPERT_BLOCK, D_MODEL), F32),
            pltpu.VMEM((2, 3, D_MODEL, D_MODEL), F32),
            pltpu.SemaphoreType.DMA((2,)),
            pltpu.SemaphoreType.DMA((2,)),
            pltpu.SemaphoreType.DMA(()),
            pltpu.SemaphoreType.DMA((2,)),
            pltpu.VMEM((D_MODEL, D_MODEL), BF16),
            pltpu.VMEM((D_MODEL, D_MODEL), BF16),
            pltpu.VMEM((D_MODEL, D_MODEL), BF16),
        ],
    )
    return pl.pallas_call(
        _expert_body,
        grid_spec=grid_spec,
        out_shape=jax.ShapeDtypeStruct((n_blocks * EXPERT_BLOCK, D_MODEL), F32),
        compiler_params=pltpu.CompilerParams(
            dimension_semantics=("arbitrary",), vmem_limit_bytes=VMEM_LIMIT),
        name="experts",
    )(*sched, xs, w_gate, b_gate.reshape(N_EXPERTS, 1, D_MODEL),
      w_up, b_up.reshape(N_EXPERTS, 1, D_MODEL), w_down, b_down.reshape(N_EXPERTS, 1, D_MODEL))


def _combine_body(units_ref, g8_ref, l8_ref, tot_ref,
                  y_hbm, x1_ref, meta_ref, lst_ref, g_ref, b_ref, out_ref, ys_s, sems):
    step = pl.program_id(0)
    n_tiles = pl.num_programs(0) * ROUTE_TILES
    TD, S = TILE_ROWS, SORT_ROWS

    def run_copy(s, g_unit, l_unit, n_units):
        return pltpu.make_async_copy(y_hbm.at[_rows(g_unit, n_units), :],
                                     ys_s.at[s, _rows(l_unit, n_units), :], sems.at[s])

    def fetch(tile, s):
        _for_each_run(units_ref, tile, lambda k, cnt: run_copy(s, g8_ref[k], l8_ref[k], cnt).start())

    @pl.when(step == 0)
    def _():
        ys_s[...] = jnp.zeros_like(ys_s)
        fetch(0, 0)

    lane_f = lax.broadcasted_iota(jnp.int32, (TD, LANES), 1).astype(F32)
    s_iota = lax.broadcasted_iota(jnp.int32, (TD, S), 1).astype(F32)
    for h in range(ROUTE_TILES):
        tau = step * ROUTE_TILES + h
        slot = h % 2
        rows = slice(h * TD, (h + 1) * TD)

        @pl.when(tau + 1 < n_tiles)
        def _():
            fetch(tau + 1, 1 - slot)

        run_copy(slot, 0, 0, tot_ref[tau]).wait()

        meta = meta_ref[rows, :]
        u_row = lst_ref[h * SUBLANES + 2:h * SUBLANES + 3, :]
        conds = []
        for j in range(TOP_K):
            oh = lane_f == meta[:, j:j + 1]
            start8 = jnp.sum(jnp.where(oh, u_row, 0.0), axis=-1, keepdims=True)
            lpos = RUN_ALIGN * start8 + meta[:, TOP_K + j:TOP_K + j + 1]
            conds.append(s_iota == lpos)
        perm = jnp.where(conds[0], 1.0, jnp.where(conds[1], 1.0, jnp.where(conds[2], 1.0,
                         jnp.where(conds[3], 1.0, 0.0)))).astype(BF16)
        ys = ys_s[slot]
        yh = ys.astype(BF16)
        yl = (ys - yh.astype(F32)).astype(BF16)
        ffn = (jnp.dot(perm, yh, preferred_element_type=F32) + jnp.dot(perm, yl, preferred_element_type=F32))
        acc = DEEPNORM_ALPHA * x1_ref[rows, :] + ffn
        mu = jnp.mean(acc, axis=-1, keepdims=True)
        hc = acc - mu
        var = jnp.mean(hc * hc, axis=-1, keepdims=True)
        out_ref[rows, :] = hc * lax.rsqrt(var + LN_EPS) * g_ref[...] + b_ref[...]


def _combine_call(sched, y_rows, x1, meta, lst_rows, ln2_g, ln2_b):
    T = x1.shape[0]
    RR = ROUTE_TILES * TILE_ROWS
    grid_spec = pltpu.PrefetchScalarGridSpec(
        num_scalar_prefetch=4,
        grid=(T // RR,),
        in_specs=[
            pl.BlockSpec(memory_space=pl.ANY),
            pl.BlockSpec((RR, D_MODEL), lambda i, *_: (i, 0)),
            pl.BlockSpec((RR, LANES), lambda i, *_: (i, 0)),
            pl.BlockSpec((ROUTE_TILES * SUBLANES, LANES), lambda i, *_: (i, 0)),
            pl.BlockSpec((1, D_MODEL), lambda i, *_: (0, 0)),
            pl.BlockSpec((1, D_MODEL), lambda i, *_: (0, 0)),
        ],
        out_specs=pl.BlockSpec((RR, D_MODEL), lambda i, *_: (i, 0)),
        scratch_shapes=[pltpu.VMEM((2, SORT_ROWS, D_MODEL), F32), pltpu.SemaphoreType.DMA((2,))],
    )
    return pl.pallas_call(
        _combine_body,
        grid_spec=grid_spec,
        out_shape=jax.ShapeDtypeStruct((T, D_MODEL), F32),
        compiler_params=pltpu.CompilerParams(
            dimension_semantics=("arbitrary",), vmem_limit_bytes=VMEM_LIMIT),
        name="combine",
    )(*sched, y_rows, x1, meta, lst_rows, ln2_g, ln2_b)


def _np_consts():
    TL = TILE_ROWS
    r = np.arange(TL)
    same = (r[:, None] // CHUNK) == (r[None, :] // CHUNK)
    tril = (same & (r[None, :] <= r[:, None])).astype(np.float32)
    trils = (r[None, :] < r[:, None]).astype(np.float32)
    e128 = np.zeros((LANES, SSD_WIDTH), np.float32)
    for h in range(SSD_HEADS):
        e128[h, h * SSD_HEAD_DIM:(h + 1) * SSD_HEAD_DIM] = 1.0
    return tril, trils, e128


def kernel(x, w_in, hg_lower_bound, hg_norm_w, conv_w, conv_b, dt_bias, a_log, d_skip, ssd_norm_w, w_out,
           ln1_g, ln1_b, router_w, router_b, w_gate, b_gate, w_up, b_up, w_down, b_down, ln2_g, ln2_b):
    batch, seq, d = x.shape
    assert d == D_MODEL and seq % STEP_ROWS == 0 and w_in.shape[0] == DEPTH
    assert ROUTE_TILES == 2 and (batch * seq) % (ROUTE_TILES * TILE_ROWS) == 0
    T = batch * seq
    n_tiles = T // TILE_ROWS
    max_rows = T * TOP_K + n_tiles * N_EXPERTS * (RUN_ALIGN - 1)
    n_blocks = -(-max_rows // EXPERT_BLOCK) + N_EXPERTS
    cap = n_blocks * EXPERT_BLOCK
    assert SORT_ROWS >= TILE_ROWS * TOP_K + N_EXPERTS * (RUN_ALIGN - 1)

    tril, trils, e128 = _np_consts()
    w = w_in[0]
    pad_l = LANES - SSD_HEADS
    rw = jnp.pad(router_w[0], ((0, 0), (0, LANES - N_EXPERTS)))
    rwh = rw.astype(BF16)
    rwm = (rw - rwh.astype(F32)).astype(BF16)
    consts = [
        jnp.pad(w.astype(BF16), ((0, 0), (0, pad_l))),
        hg_lower_bound,
        hg_norm_w[0].reshape(1, HG_HEAD_DIM),
        conv_w[0],
        conv_b[0].reshape(1, SSD_CONV_DIM),
        jnp.pad(dt_bias[0], (0, pad_l)).reshape(1, LANES),
        jnp.pad(a_log[0], (0, pad_l)).reshape(1, LANES),
        jnp.repeat(d_skip[0], SSD_HEAD_DIM).reshape(1, SSD_WIDTH),
        ssd_norm_w[0].reshape(1, SSD_WIDTH),
        w_out[0].astype(BF16),
        ln1_g[0].reshape(1, D_MODEL),
        ln1_b[0].reshape(1, D_MODEL),
        jnp.concatenate([rwh, rwm], axis=1),
        rwh,
        jnp.pad(router_b[0], (0, LANES - N_EXPERTS)).reshape(1, LANES),
        jnp.asarray(tril, BF16), jnp.asarray(trils, BF16), jnp.asarray(e128, BF16),
    ]
    x1, meta, cnt = _mixer_call(x.reshape(T, D_MODEL), consts, batch, seq)

    counts = cnt.reshape(n_tiles, SUBLANES, LANES)[:, 0, :N_EXPERTS].astype(jnp.int32)
    c8 = (counts + RUN_ALIGN - 1) // RUN_ALIGN * RUN_ALIGN
    used = jnp.sum(c8, axis=0)
    region = (used + EXPERT_BLOCK - 1) // EXPERT_BLOCK * EXPERT_BLOCK
    region_end = jnp.cumsum(region)
    region_start = region_end - region
    gstart = region_start[None, :] + jnp.cumsum(c8, axis=0) - c8
    lstart = jnp.cumsum(c8, axis=1) - c8
    n_valid = region_end[-1] // EXPERT_BLOCK
    has = region > 0
    eidx = jnp.arange(N_EXPERTS, dtype=jnp.int32)
    suffix_min = lax.cummin(jnp.where(has, eidx, N_EXPERTS), reverse=True)
    nxt_e = jnp.concatenate([suffix_min[1:], jnp.full((1,), N_EXPERTS, jnp.int32)])
    nxt_e = jnp.where(nxt_e == N_EXPERTS, eidx, nxt_e)
    par_e = (jnp.cumsum(has.astype(jnp.int32)) - 1) % 2
    left = region // EXPERT_BLOCK
    n_of = []
    for nb in CHUNK_BLOCKS:
        n_of.append(left // nb)
        left = left % nb
    it_cnt = sum(n_of)
    it_end = jnp.cumsum(it_cnt)
    n_iter = it_end[-1]
    max_iter = n_blocks // max(CHUNK_BLOCKS) + (len(CHUNK_BLOCKS) - 1) * N_EXPERTS
    it = jnp.minimum(jnp.arange(max_iter, dtype=jnp.int32), n_iter - 1)
    it_e = jnp.minimum(jnp.sum(it_end[None, :] <= it[:, None], axis=1), N_EXPERTS - 1).astype(jnp.int32)
    it_onehot = it_e[:, None] == eidx[None, :]
    lookup = lambda tab: jnp.sum(jnp.where(it_onehot, tab[None, :].astype(jnp.int32), 0), axis=1)
    k = it - lookup(it_end - it_cnt)
    it_sz = jnp.zeros_like(it)
    it_blk = lookup(region_start // EXPERT_BLOCK)
    for idx, nb in enumerate(CHUNK_BLOCKS):
        n_here = lookup(n_of[idx])
        inside = jnp.logical_and(k >= 0, k < n_here)
        it_sz = jnp.where(inside, idx, it_sz)
        it_blk = it_blk + nb * jnp.clip(k, 0, n_here)
        k = k - n_here
    expert_sched = (it_e, it_blk.astype(jnp.int32), it_sz.astype(jnp.int32), lookup(nxt_e).astype(jnp.int32),
                    lookup(par_e).astype(jnp.int32),
                    n_iter.astype(jnp.int32).reshape(1), n_valid.astype(jnp.int32).reshape(1))
    as_units = lambda a: (a // RUN_ALIGN).astype(jnp.int32).reshape(-1)
    units, g8, l8 = as_units(c8), as_units(gstart), as_units(lstart)
    tot = (jnp.sum(c8, axis=1) // RUN_ALIGN).astype(jnp.int32)
    tail8 = as_units(region_start + used)
    tailn = as_units(region - used)
    lu = lstart // RUN_ALIGN
    lst3 = jnp.stack([lu // LST_SPLIT, lu % LST_SPLIT, lu], axis=1).astype(F32)
    lst_rows = jnp.pad(lst3, ((0, 0), (0, SUBLANES - 3), (0, LANES - N_EXPERTS))).reshape(
        n_tiles * SUBLANES, LANES)

    nv = n_valid.astype(jnp.int32).reshape(1)
    xs = _dispatch_call((units, g8, l8, tot, tail8, tailn, nv), x1, meta, lst_rows, cap)
    y_rows = _expert_call(expert_sched, xs, w_gate[0], b_gate[0], w_up[0], b_up[0], w_down[0], b_down[0])
    out = _combine_call((units, g8, l8, tot), y_rows, x1, meta, lst_rows,
                        ln2_g[0].reshape(1, D_MODEL), ln2_b[0].reshape(1, D_MODEL))
    return out.reshape(batch, seq, D_MODEL)
```

```python
import functools

import jax
import jax.numpy as jnp
import numpy as np
from jax import lax
from jax.experimental import pallas as pl
from jax.experimental.pallas import tpu as pltpu

F32 = jnp.float32
BF16 = jnp.bfloat16

D_MODEL = 1024
CHUNK = 64
HG_WIDTH = 512
HG_HEAD_DIM = 128
HG_HEADS = 4
SSD_WIDTH = 512
SSD_HEAD_DIM = 64
SSD_HEADS = 8
SSD_GROUPS = 2
SSD_STATE = 128
SSD_CONV = 4
SSD_CONV_DIM = SSD_WIDTH + 2 * SSD_GROUPS * SSD_STATE
N_EXPERTS = 32
TOP_K = 4
EXPERT_BLOCK = 256
CHUNK_BLOCKS = (2, 1)
SWIGLU_LIMIT = 7.0
SWIGLU_ALPHA = 1.702
DEPTH = 1
DEEPNORM_ALPHA = (2 * DEPTH) ** 0.25
LN_EPS = 1e-5
RMS_EPS = 1e-5

LANES = 128
SUBLANES = 8
SUB_CHUNK = 16
EXP_CAP = 60.0
TILE_ROWS = 256
MIX_TILES = 2
ROUTE_TILES = 2
STEP_ROWS = MIX_TILES * TILE_ROWS
RUN_ALIGN = SUBLANES
SORT_ROWS = 1280
LST_SPLIT = 32
XS_WIDTH = D_MODEL + LANES
VMEM_LIMIT = 56 * 1024 * 1024

OFF_Q, OFF_F, OFF_I, OFF_G = 0, 512, 1024, 1536
OFF_Z, OFF_XBC = 2048, 2560
OFF_XS, OFF_B, OFF_C = 2560, 3072, 3328
OFF_DT = 3584
PROJ_COLS = OFF_DT + LANES
PROJ_SLICE = 256
CONV_SLICE = 256
OUT_SLICE = 256


def _bdot(a, b):
    return jnp.dot(a.astype(BF16), b.astype(BF16), preferred_element_type=F32)


def _bdot_nt(a, b):
    return lax.dot_general(a.astype(BF16), b.astype(BF16), (((1,), (1,)), ((), ())),
                           preferred_element_type=F32)


def _bdot_tn(a, b):
    return lax.dot_general(a.astype(BF16), b.astype(BF16), (((0,), (0,)), ((), ())),
                           preferred_element_type=F32)


def _split3(a):
    hi = a.astype(BF16)
    r1 = a - hi.astype(F32)
    mid = r1.astype(BF16)
    lo = (r1 - mid.astype(F32)).astype(BF16)
    return hi, mid, lo


def _dot01_left(m01, a):
    hi, mid, lo = _split3(a)
    d = functools.partial(jnp.dot, m01, preferred_element_type=F32)
    return d(hi) + d(mid) + d(lo)


def _dot01_right(a, m01):
    hi, mid, lo = _split3(a)
    return (jnp.dot(hi, m01, preferred_element_type=F32) + jnp.dot(mid, m01, preferred_element_type=F32)
            + jnp.dot(lo, m01, preferred_element_type=F32))


def _sigmoid(x):
    return 1.0 / (1.0 + jnp.exp(-x))


def _silu(x):
    return x * _sigmoid(x)


def _softplus(x):
    return jnp.maximum(x, 0.0) + jnp.log(1.0 + jnp.exp(-jnp.abs(x)))


def _mixer_body(x_ref, xnext_ref, win_ref, lbp_ref, hgnw_ref, convw_ref, convb_ref, dtb_ref, alog_ref, dskip_ref,
                ssdnw_ref, wout_ref, ln1g_ref, ln1b_ref, rw1_ref, rwh_ref, rb_ref,
                tril_ref, trils_ref, e128_ref,
                x1_ref, meta_ref, cnt_ref,
                proj_s, xb_s, xn_s, b_s, xpad_s, xdt_s, cse_s, st_s, pt_s, cat_s, ossd_s, mix_s):
    TL = TILE_ROWS
    t = pl.program_id(1)

    @pl.when(t == 0)
    def _():
        xpad_s[0:SUBLANES, :] = jnp.zeros((SUBLANES, SSD_CONV_DIM), F32)
        st_s[...] = jnp.zeros_like(st_s)
        pt_s[...] = jnp.zeros_like(pt_s)

    row64 = lax.broadcasted_iota(jnp.int32, (CHUNK, CHUNK), 0)
    col64 = lax.broadcasted_iota(jnp.int32, (CHUNK, CHUNK), 1)
    causal = row64 >= col64
    gw = SSD_WIDTH // SSD_GROUPS
    hpg = SSD_HEADS // SSD_GROUPS
    lane_head = lax.broadcasted_iota(jnp.int32, (CHUNK, gw), 1) // SSD_HEAD_DIM

    pending = []

    def filler():
        if pending:
            pending.pop(0)()

    def project_slices(lhs_ref, lhs_base, base):
        def one(c0, c1):
            def run():
                proj_s[base:base + TL, c0:c1] = jnp.dot(lhs_ref[lhs_base:lhs_base + TL, :], win_ref[:, c0:c1],
                                                        preferred_element_type=F32)
            return run
        return [one(c0, min(c0 + PROJ_SLICE, PROJ_COLS)) for c0 in range(0, PROJ_COLS, PROJ_SLICE)]

    def hg_front(base):
        a0 = lbp_ref[0:1, :]
        a1 = lbp_ref[1:2, :]
        am = jnp.maximum(a0, a1)
        e0 = jnp.exp(a0 - am)
        e1 = jnp.exp(a1 - am)
        lb = e0 / (e0 + e1)
        f = lb + (1.0 - lb) * _sigmoid(proj_s[base:base + TL, OFF_F:OFF_F + HG_WIDTH])
        b_s[base:base + TL, :] = _dot01_left(tril_ref[...], jnp.log(f))
        proj_s[base:base + TL, OFF_F:OFF_F + HG_WIDTH] = f
        proj_s[base:base + TL, OFF_Q:OFF_Q + HG_WIDTH] = _silu(proj_s[base:base + TL, OFF_Q:OFF_Q + HG_WIDTH])
        units = {}
        for c in range(TL // CHUNK):
            r0 = base + c * CHUNK
            for h in range(HG_HEADS):
                h0 = h * HG_HEAD_DIM
                bc = b_s[r0:r0 + CHUNK, h0:h0 + HG_HEAD_DIM]
                qc = proj_s[r0:r0 + CHUNK, OFF_Q + h0:OFF_Q + h0 + HG_HEAD_DIM]
                kc = 1.0 - proj_s[r0:r0 + CHUNK, OFF_F + h0:OFF_F + h0 + HG_HEAD_DIM]
                vcb = proj_s[r0:r0 + CHUNK, OFF_I + h0:OFF_I + h0 + HG_HEAD_DIM].astype(BF16)
                parts = []
                for i in range(CHUNK // SUB_CHUNK):
                    s0 = i * SUB_CHUNK
                    if i == 0:
                        qi = qc[0:SUB_CHUNK] * jnp.exp(bc[0:SUB_CHUNK])
                        ki = kc * jnp.exp(jnp.minimum(-bc, EXP_CAP))
                    else:
                        ref_i = bc[s0 - 1:s0, :]
                        qi = qc[s0:s0 + SUB_CHUNK] * jnp.exp(bc[s0:s0 + SUB_CHUNK] - ref_i)
                        ki = kc * jnp.exp(jnp.minimum(ref_i - bc, EXP_CAP))
                    parts.append(_bdot_nt(qi, ki))
                b_end = bc[CHUNK - 1:CHUNK, :]
                kdec = kc * jnp.exp(b_end - bc)
                units[c, h] = dict(
                    parts=parts, vcb=vcb, qdec=(qc * jnp.exp(bc)).astype(BF16),
                    local=_bdot_tn(vcb, kdec),
                    decay=jnp.exp(b_end))
                if h % 2 == 1:
                    filler()
        return units

    def hg_back(base, units):
        hgnw = hgnw_ref[...]
        for h in range(HG_HEADS):
            st = st_s[h]
            for c in range(TL // CHUNK):
                u = units[c, h]
                u['state'] = st
                st = st * u['decay'] + u['local']
            st_s[h] = st
        for c in range(TL // CHUNK):
            r0 = base + c * CHUNK
            for h in range(HG_HEADS):
                h0 = h * HG_HEAD_DIM
                u = units[c, h]
                sc = jnp.where(causal, jnp.concatenate(u['parts'], axis=0), 0.0)
                o = _bdot(sc, u['vcb']) + _bdot_nt(u['qdec'], u['state'])
                ms = jnp.mean(o * o, axis=-1, keepdims=True)
                on = o * lax.rsqrt(ms + RMS_EPS) * hgnw
                gc = proj_s[r0:r0 + CHUNK, OFF_G + h0:OFF_G + h0 + HG_HEAD_DIM]
                cat_s[r0:r0 + CHUNK, h0:h0 + HG_HEAD_DIM] = (on * _silu(gc)).astype(BF16)
                filler()

    def ssd_front(base):
        e128 = e128_ref[...]
        dtc = _softplus(proj_s[base:base + TL, OFF_DT:OFF_DT + LANES] + dtb_ref[...])
        a_row = -jnp.exp(alog_ref[...])
        cs_c = _dot01_left(tril_ref[...], dtc * a_row)
        cs_r = jnp.transpose(cs_c)[0:SSD_HEADS, :]
        cse_s[base:base + TL, :] = _dot01_right(cs_c, e128)
        dt_exp = _dot01_right(dtc, e128)

        xpad_s[SUBLANES + base:SUBLANES + base + TL, :] = proj_s[base:base + TL, OFF_XBC:OFF_XBC + SSD_CONV_DIM]
        for c0 in range(0, SSD_CONV_DIM, CONV_SLICE):
            acc = jnp.broadcast_to(convb_ref[:, c0:c0 + CONV_SLICE], (TL, CONV_SLICE))
            for j in range(SSD_CONV):
                off = base + SUBLANES - (SSD_CONV - 1) + j
                acc = acc + convw_ref[j:j + 1, c0:c0 + CONV_SLICE] * xpad_s[off:off + TL, c0:c0 + CONV_SLICE]
            proj_s[base:base + TL, OFF_XBC + c0:OFF_XBC + c0 + CONV_SLICE] = _silu(acc)
            filler()
        xdt_s[base:base + TL, :] = proj_s[base:base + TL, OFF_XS:OFF_XS + SSD_WIDTH] * dt_exp

        units = {}
        for c in range(TL // CHUNK):
            r0 = base + c * CHUNK
            for g in range(SSD_GROUPS):
                g0 = g * gw
                bgb = proj_s[r0:r0 + CHUNK, OFF_B + g * SSD_STATE:OFF_B + (g + 1) * SSD_STATE].astype(BF16)
                cgb = proj_s[r0:r0 + CHUNK, OFF_C + g * SSD_STATE:OFF_C + (g + 1) * SSD_STATE].astype(BF16)
                cse_g = cse_s[r0:r0 + CHUNK, g0:g0 + gw]
                cs_end = cse_s[r0 + CHUNK - 1:r0 + CHUNK, g0:g0 + gw]
                xdt_g = xdt_s[r0:r0 + CHUNK, g0:g0 + gw]
                units[c, g] = dict(
                    cgb=cgb, gm=_bdot_nt(cgb, bgb),
                    local=_bdot_tn(bgb, xdt_g * jnp.exp(cs_end - cse_g)),
                    decay=jnp.exp(cs_end))
                filler()
        return units, cs_r

    def ssd_back(base, units, cs_r):
        for g in range(SSD_GROUPS):
            pt = pt_s[g]
            for c in range(TL // CHUNK):
                u = units[c, g]
                u['state'] = pt
                pt = pt * u['decay'] + u['local']
            pt_s[g] = pt
        for c in range(TL // CHUNK):
            r0 = base + c * CHUNK
            for g in range(SSD_GROUPS):
                g0 = g * gw
                u = units[c, g]
                cse_g = cse_s[r0:r0 + CHUNK, g0:g0 + gw]
                xdt_g = xdt_s[r0:r0 + CHUNK, g0:g0 + gw]
                ydiag = jnp.zeros((CHUNK, gw), F32)
                for hl in range(hpg):
                    hh = g * hpg + hl
                    seg = (cse_g[:, hl * SSD_HEAD_DIM:(hl + 1) * SSD_HEAD_DIM]
                           - cs_r[hh:hh + 1, c * CHUNK:(c + 1) * CHUNK])
                    lm = jnp.where(causal, jnp.exp(jnp.minimum(seg, 0.0)), 0.0)
                    xm = jnp.where(lane_head == hl, xdt_g, 0.0)
                    ydiag = ydiag + _bdot(u['gm'] * lm, xm)
                yoff = _bdot(u['cgb'], u['state']) * jnp.exp(cse_g)
                xs_g = proj_s[r0:r0 + CHUNK, OFF_XS + g0:OFF_XS + g0 + gw]
                ossd_s[r0:r0 + CHUNK, g0:g0 + gw] = ydiag + yoff + xs_g * dskip_ref[:, g0:g0 + gw]
                filler()

    def ssd_gate_norm(tile):
        base = tile * TL
        y = ossd_s[base:base + TL, :] * _silu(proj_s[base:base + TL, OFF_Z:OFF_Z + SSD_WIDTH])
        for g in range(SSD_GROUPS):
            yg = y[:, g * gw:(g + 1) * gw]
            ms = jnp.mean(yg * yg, axis=-1, keepdims=True)
            yn = yg * lax.rsqrt(ms + RMS_EPS) * ssdnw_ref[:, g * gw:(g + 1) * gw]
            cat_s[base:base + TL, HG_WIDTH + g * gw:HG_WIDTH + (g + 1) * gw] = yn.astype(BF16)

    def out_slice(tile, c0):
        base = tile * TL
        mix_s[base:base + TL, c0:c0 + OUT_SLICE] = jnp.dot(
            cat_s[base:base + TL, :], wout_ref[:, c0:c0 + OUT_SLICE], preferred_element_type=F32)

    def norm_route(tile):
        base = tile * TL
        hres = DEEPNORM_ALPHA * x_ref[base:base + TL, :] + mix_s[base:base + TL, :]
        mu = jnp.mean(hres, axis=-1, keepdims=True)
        hc = hres - mu
        var = jnp.mean(hc * hc, axis=-1, keepdims=True)
        x1 = hc * lax.rsqrt(var + LN_EPS) * ln1g_ref[...] + ln1b_ref[...]
        x1_ref[base:base + TL, :] = x1
        filler()

        xh = x1.astype(BF16)
        xm_ = (x1 - xh.astype(F32)).astype(BF16)
        t1 = jnp.dot(xh, rw1_ref[...], preferred_element_type=F32)
        logits = (t1[:, 0:LANES] + t1[:, LANES:2 * LANES]
                  + jnp.dot(xm_, rwh_ref[...], preferred_element_type=F32) + rb_ref[...])
        filler()
        lane = lax.broadcasted_iota(jnp.int32, (TL, LANES), 1)
        lane_f = lane.astype(F32)
        neg = jnp.float32(-jnp.inf)
        work = jnp.where(lane < N_EXPERTS, logits, neg)
        onehots, vals, idxs = [], [], []
        for j in range(TOP_K):
            m = jnp.max(work, axis=-1, keepdims=True)
            filler()
            idx = jnp.min(jnp.where(work == m, lane_f, float(LANES)), axis=-1, keepdims=True)
            oh = lane_f == idx
            onehots.append(oh)
            vals.append(m)
            idxs.append(idx)
            work = jnp.where(oh, neg, work)
            filler()
        es = [jnp.exp(v - vals[0]) for v in vals]
        den = es[0] + es[1] + es[2] + es[3]
        gates = [e / den for e in es]
        sel = jnp.zeros((TL, LANES), F32)
        for oh in onehots:
            sel = jnp.where(oh, 1.0, sel)
        rankmat = jnp.dot(trils_ref[...], sel.astype(BF16), preferred_element_type=F32)
        filler()
        cnt_ref[tile * SUBLANES:(tile + 1) * SUBLANES, :] = jnp.broadcast_to(
            jnp.sum(sel, axis=0, keepdims=True), (SUBLANES, LANES))
        meta = jnp.zeros((TL, LANES), F32)
        for j in range(TOP_K):
            rank_j = jnp.sum(jnp.where(onehots[j], rankmat, 0.0), axis=-1, keepdims=True)
            meta = jnp.where(lane == j, idxs[j], meta)
            meta = jnp.where(lane == TOP_K + j, rank_j, meta)
            meta = jnp.where(lane == 2 * TOP_K + j, gates[j], meta)
        meta_ref[base:base + TL, :] = meta

    def post_thunks(tile):
        return ([functools.partial(ssd_gate_norm, tile)]
                + [functools.partial(out_slice, tile, c0) for c0 in range(0, D_MODEL, OUT_SLICE)]
                + [functools.partial(norm_route, tile)])

    xb_s[...] = x_ref[...].astype(BF16)
    xn_s[...] = xnext_ref[...].astype(BF16)

    @pl.when(jnp.logical_and(pl.program_id(0) == 0, t == 0))
    def _():
        for run in project_slices(xb_s, 0, 0):
            run()

    for tile in range(MIX_TILES):
        base = tile * TL
        if tile + 1 < MIX_TILES:
            pending.extend(project_slices(xb_s, base + TL, base + TL))
        hg_units = hg_front(base)
        ssd_units, cs_r = ssd_front(base)
        while pending:
            filler()
        hg_back(base, hg_units)
        ssd_back(base, ssd_units, cs_r)
        if tile + 1 < MIX_TILES:
            pending.extend(post_thunks(tile))
    tail = post_thunks(MIX_TILES - 1)
    slices = project_slices(xn_s, 0, 0)
    tail[0]()
    for run in tail[1:-1]:
        run()
        slices.pop(0)()
    pending.extend(slices)
    tail[-1]()
    while pending:
        filler()
    xpad_s[0:SUBLANES, :] = xpad_s[STEP_ROWS:STEP_ROWS + SUBLANES, :]


def _full(shape):
    nd = len(shape)
    return pl.BlockSpec(shape, lambda *_: (0,) * nd)


def _mixer_call(x2d, consts, batch, seq):
    SR = STEP_ROWS
    nt = seq // SR
    T = batch * seq
    last_tile = T // TILE_ROWS - 1
    in_specs = [pl.BlockSpec((SR, D_MODEL), lambda b, t: (b * nt + t, 0)),
                pl.BlockSpec((TILE_ROWS, D_MODEL),
                             lambda b, t: (jnp.minimum((b * nt + t + 1) * MIX_TILES, last_tile), 0))]
    in_specs += [_full(c.shape) for c in consts]
    out_shape = (jax.ShapeDtypeStruct((T, D_MODEL), F32),
                 jax.ShapeDtypeStruct((T, LANES), F32),
                 jax.ShapeDtypeStruct((T // TILE_ROWS * SUBLANES, LANES), F32))
    out_specs = (pl.BlockSpec((SR, D_MODEL), lambda b, t: (b * nt + t, 0)),
                 pl.BlockSpec((SR, LANES), lambda b, t: (b * nt + t, 0)),
                 pl.BlockSpec((MIX_TILES * SUBLANES, LANES), lambda b, t: (b * nt + t, 0)))
    scratch = [
        pltpu.VMEM((SR, PROJ_COLS), F32),
        pltpu.VMEM((SR, D_MODEL), BF16),
        pltpu.VMEM((TILE_ROWS, D_MODEL), BF16),
        pltpu.VMEM((SR, HG_WIDTH), F32),
        pltpu.VMEM((SR + 2 * SUBLANES, SSD_CONV_DIM), F32),
        pltpu.VMEM((SR, SSD_WIDTH), F32),
        pltpu.VMEM((SR, SSD_WIDTH), F32),
        pltpu.VMEM((HG_HEADS, HG_HEAD_DIM, HG_HEAD_DIM), F32),
        pltpu.VMEM((SSD_GROUPS, SSD_STATE, SSD_WIDTH // SSD_GROUPS), F32),
        pltpu.VMEM((SR, D_MODEL), BF16),
        pltpu.VMEM((SR, SSD_WIDTH), F32),
        pltpu.VMEM((SR, D_MODEL), F32),
    ]
    return pl.pallas_call(
        _mixer_body,
        grid=(batch, nt),
        in_specs=in_specs,
        out_specs=out_specs,
        out_shape=out_shape,
        scratch_shapes=scratch,
        compiler_params=pltpu.CompilerParams(
            dimension_semantics=("arbitrary", "arbitrary"), vmem_limit_bytes=VMEM_LIMIT),
        name="mixer",
    )(x2d, x2d, *consts)


def _for_each_run(units_ref, tile, fn):
    def per_expert(e, carry):
        k = tile * N_EXPERTS + e
        n = units_ref[k]

        @pl.when(n > 0)
        def _():
            fn(k, n)
        return carry
    lax.fori_loop(0, N_EXPERTS, per_expert, 0)


def _rows(unit, n_units=1):
    start = unit * RUN_ALIGN
    if RUN_ALIGN > 1:
        start = pl.multiple_of(start, RUN_ALIGN)
    return pl.ds(start, n_units * RUN_ALIGN)


def _dispatch_body(units_ref, g8_ref, l8_ref, tot_ref, tail8_ref, tailn_ref, nv_ref,
                   x1_ref, meta_ref, lst_ref, xs_hbm, sorted_s, zero_s, sems, zsem, bsem):
    step = pl.program_id(0)
    n = pl.num_programs(0)
    TD, S = TILE_ROWS, SORT_ROWS
    n_blocks = xs_hbm.shape[0] // EXPERT_BLOCK

    def unused_block_copy(m):
        rows = pl.ds(pl.multiple_of(m * EXPERT_BLOCK, EXPERT_BLOCK), EXPERT_BLOCK)
        return pltpu.make_async_copy(zero_s, xs_hbm.at[rows, :], bsem)

    def run_copy(s, l_unit, g_unit, n_units):
        return pltpu.make_async_copy(sorted_s.at[s, _rows(l_unit, n_units), :],
                                     xs_hbm.at[_rows(g_unit, n_units), :], sems.at[s])

    def wait_units(s, count):
        run_copy(s, 0, 0, count).wait()

    @pl.when(step == 0)
    def _():
        zero_s[...] = jnp.zeros_like(zero_s)

        def tail_copy(e):
            n_units = tailn_ref[e]
            return pltpu.make_async_copy(zero_s.at[pl.ds(0, n_units * RUN_ALIGN), :],
                                         xs_hbm.at[_rows(tail8_ref[e], n_units), :], zsem)

        def start_unused(m, carry):
            unused_block_copy(m).start()
            return carry
        lax.fori_loop(nv_ref[0], n_blocks, start_unused, 0)

        def start_e(e, carry):
            @pl.when(tailn_ref[e] > 0)
            def _():
                tail_copy(e).start()
            return carry
        lax.fori_loop(0, N_EXPERTS, start_e, 0)

        def wait_e(e, carry):
            @pl.when(tailn_ref[e] > 0)
            def _():
                tail_copy(e).wait()
            return carry
        lax.fori_loop(0, N_EXPERTS, wait_e, 0)

    lane = lax.broadcasted_iota(jnp.int32, (TD, LANES), 1)
    lane_f = lane.astype(F32)
    ones8 = jnp.ones((SUBLANES, LANES), BF16)
    r_iota = lax.broadcasted_iota(jnp.int32, (S, TD), 0).astype(F32)
    nt_dims = (((1,), (1,)), ((), ()))
    for slot in range(ROUTE_TILES):
        _dispatch_tile(step * ROUTE_TILES + slot, slot, step, units_ref, g8_ref, l8_ref, tot_ref,
                       x1_ref, meta_ref, lst_ref, sorted_s, run_copy, wait_units,
                       lane, lane_f, ones8, r_iota, nt_dims)

    @pl.when(step == n - 1)
    def _():
        for slot in range(ROUTE_TILES):
            wait_units(slot, tot_ref[step * ROUTE_TILES + slot])

        def wait_unused(m, carry):
            unused_block_copy(m).wait()
            return carry
        lax.fori_loop(nv_ref[0], n_blocks, wait_unused, 0)


def _dispatch_tile(tau, slot, step, units_ref, g8_ref, l8_ref, tot_ref, x1_ref, meta_ref, lst_ref, sorted_s,
                   run_copy, wait_units, lane, lane_f, ones8, r_iota, nt_dims):
    TD = TILE_ROWS
    rows = slice(slot * TD, (slot + 1) * TD)

    @pl.when(step >= 1)
    def _():
        wait_units(slot, tot_ref[jnp.maximum(tau - ROUTE_TILES, 0)])

    meta = meta_ref[rows, :]
    u8 = lst_ref[slot * SUBLANES:(slot + 1) * SUBLANES, :].astype(BF16)
    gh = meta.astype(BF16).astype(F32)
    g1 = meta - gh
    gm = g1.astype(BF16).astype(F32)
    gl = g1 - gm
    conds, gparts = [], []
    for j in range(TOP_K):
        oh = lane_f == meta[:, j:j + 1]
        ohb = jnp.where(oh, 1.0, 0.0).astype(BF16)
        rkb = jnp.where(oh, meta[:, TOP_K + j:TOP_K + j + 1], 0.0).astype(BF16)
        m1 = lax.dot_general(u8, ohb, nt_dims, preferred_element_type=F32)
        m2 = lax.dot_general(ones8, rkb, nt_dims, preferred_element_type=F32)
        lpos = RUN_ALIGN * (LST_SPLIT * m1[0:1, :] + m1[1:2, :]) + m2[0:1, :]
        conds.append(r_iota == lpos)
        gc = 2 * TOP_K + j
        gparts.append(jnp.where(lane == 0, gh[:, gc:gc + 1],
                                jnp.where(lane == 1, gm[:, gc:gc + 1],
                                          jnp.where(lane == 2, gl[:, gc:gc + 1], 0.0))).astype(BF16))
    pcat = jnp.concatenate([jnp.where(c, 1.0, 0.0).astype(BF16) for c in conds], axis=1)
    sorted_g = jnp.dot(pcat, jnp.concatenate(gparts, axis=0), preferred_element_type=F32)
    perm = jnp.where(conds[0], 1.0, jnp.where(conds[1], 1.0, jnp.where(conds[2], 1.0,
                     jnp.where(conds[3], 1.0, 0.0)))).astype(BF16)
    sorted_x = jnp.dot(perm, x1_ref[rows, :].astype(BF16), preferred_element_type=F32)
    sorted_s[slot, :, 0:D_MODEL] = sorted_x
    sorted_s[slot, :, D_MODEL:XS_WIDTH] = sorted_g

    _for_each_run(units_ref, tau, lambda k, cnt: run_copy(slot, l8_ref[k], g8_ref[k], cnt).start())


def _dispatch_call(sched, x1, meta, lst_rows, cap):
    T = x1.shape[0]
    RR = ROUTE_TILES * TILE_ROWS
    grid_spec = pltpu.PrefetchScalarGridSpec(
        num_scalar_prefetch=7,
        grid=(T // RR,),
        in_specs=[
            pl.BlockSpec((RR, D_MODEL), lambda i, *_: (i, 0)),
            pl.BlockSpec((RR, LANES), lambda i, *_: (i, 0)),
            pl.BlockSpec((ROUTE_TILES * SUBLANES, LANES), lambda i, *_: (i, 0)),
        ],
        out_specs=pl.BlockSpec(memory_space=pl.ANY),
        scratch_shapes=[
            pltpu.VMEM((2, SORT_ROWS, XS_WIDTH), F32),
            pltpu.VMEM((EXPERT_BLOCK, XS_WIDTH), F32),
            pltpu.SemaphoreType.DMA((2,)),
            pltpu.SemaphoreType.DMA(()),
            pltpu.SemaphoreType.DMA(()),
        ],
    )
    return pl.pallas_call(
        _dispatch_body,
        grid_spec=grid_spec,
        out_shape=jax.ShapeDtypeStruct((cap, XS_WIDTH), F32),
        compiler_params=pltpu.CompilerParams(
            dimension_semantics=("arbitrary",), vmem_limit_bytes=VMEM_LIMIT),
        name="dispatch",
    )(*sched, x1, meta, lst_rows)


def _expert_body(ite_ref, itb_ref, sz_ref, nxt_ref, par_ref, nit_ref, nv_ref,
                 xs_hbm, wg_hbm, bg_ref, wu_hbm, bu_ref, wd_hbm, bd_ref,
                 y_hbm, xbuf, ybuf, zbuf, wbuf, xsem, ysem, zsem, wsem, wg_s, wu_s, wd_s):
    n_iter = nit_ref[0]
    n_blocks = xs_hbm.shape[0] // EXPERT_BLOCK
    sizes = tuple(n * EXPERT_BLOCK for n in CHUNK_BLOCKS)

    def rows_of(i, rows):
        return pl.ds(pl.multiple_of(itb_ref[i] * EXPERT_BLOCK, EXPERT_BLOCK), rows)

    def x_copy(i, s, rows):
        return pltpu.make_async_copy(xs_hbm.at[rows_of(i, rows), :], xbuf.at[s, pl.ds(0, rows), :], xsem.at[s])

    def y_copy(i, s, rows):
        return pltpu.make_async_copy(ybuf.at[s, pl.ds(0, rows), :], y_hbm.at[rows_of(i, rows), :], ysem.at[s])

    def by_size(i, fn):
        for k, rows in enumerate(sizes):
            @pl.when(sz_ref[i] == k)
            def _():
                fn(rows)

    def zero_copy(m):
        rows = pl.ds(pl.multiple_of(m * EXPERT_BLOCK, EXPERT_BLOCK), EXPERT_BLOCK)
        return pltpu.make_async_copy(zbuf, y_hbm.at[rows, :], zsem)

    def weight_copies(e, s):
        return [pltpu.make_async_copy(w.at[e], wbuf.at[s, k], wsem.at[s])
                for k, w in enumerate((wg_hbm, wu_hbm, wd_hbm))]

    for c in weight_copies(ite_ref[0], par_ref[0]):
        c.start()
    by_size(0, lambda rows: x_copy(0, 0, rows).start())

    zbuf[...] = jnp.zeros_like(zbuf)

    def start_zero(m, carry):
        zero_copy(m).start()
        return carry
    lax.fori_loop(nv_ref[0], n_blocks, start_zero, 0)

    def ffn(s, e, rows):
        xb = xbuf[s, 0:rows, 0:D_MODEL].astype(BF16)
        gate = (xbuf[s, 0:rows, D_MODEL:D_MODEL + 1] + xbuf[s, 0:rows, D_MODEL + 1:D_MODEL + 2]
                + xbuf[s, 0:rows, D_MODEL + 2:D_MODEL + 3])
        hg = jnp.minimum(jnp.dot(xb, wg_s[...], preferred_element_type=F32) + bg_ref[e], SWIGLU_LIMIT)
        hu = jnp.clip(jnp.dot(xb, wu_s[...], preferred_element_type=F32) + bu_ref[e],
                      -SWIGLU_LIMIT, SWIGLU_LIMIT)
        hact = (hu + 1.0) * (hg * _sigmoid(SWIGLU_ALPHA * hg))
        y = jnp.dot(hact.astype(BF16), wd_s[...], preferred_element_type=F32) + bd_ref[e]
        ybuf[s, 0:rows, :] = y * gate

    def body(i, carry):
        s = lax.rem(i, 2)
        e = ite_ref[i]
        by_size(i, lambda rows: x_copy(i, s, rows).wait())

        @pl.when(i + 1 < n_iter)
        def _():
            by_size(i + 1, lambda rows: x_copy(i + 1, 1 - s, rows).start())

        @pl.when(jnp.logical_or(i == 0, e != ite_ref[jnp.maximum(i - 1, 0)]))
        def _():
            ws = par_ref[i]
            for c in weight_copies(e, ws):
                c.wait()

            @pl.when(nxt_ref[i] != e)
            def _():
                for c in weight_copies(nxt_ref[i], 1 - ws):
                    c.start()
            wg_s[...] = wbuf[ws, 0].astype(BF16)
            wu_s[...] = wbuf[ws, 1].astype(BF16)
            wd_s[...] = wbuf[ws, 2].astype(BF16)

        @pl.when(i >= 2)
        def _():
            by_size(i - 2, lambda rows: y_copy(i - 2, s, rows).wait())

        def chunk(rows):
            ffn(s, e, rows)
            y_copy(i, s, rows).start()
        by_size(i, chunk)
        return carry
    lax.fori_loop(0, n_iter, body, 0)

    @pl.when(n_iter >= 2)
    def _():
        by_size(n_iter - 2, lambda rows: y_copy(n_iter - 2, lax.rem(n_iter, 2), rows).wait())
    by_size(n_iter - 1, lambda rows: y_copy(n_iter - 1, lax.rem(n_iter - 1, 2), rows).wait())

    def wait_zero(m, carry):
        zero_copy(m).wait()
        return carry
    lax.fori_loop(nv_ref[0], n_blocks, wait_zero, 0)


def _expert_call(sched, xs, w_gate, b_gate, w_up, b_up, w_down, b_down):
    n_blocks = xs.shape[0] // EXPERT_BLOCK
    anyspec = pl.BlockSpec(memory_space=pl.ANY)
    bspec = pl.BlockSpec((N_EXPERTS, 1, D_MODEL), lambda i, *_: (0, 0, 0))
    grid_spec = pltpu.PrefetchScalarGridSpec(
        num_scalar_prefetch=len(sched),
        grid=(1,),
        in_specs=[anyspec, anyspec, bspec, anyspec, bspec, anyspec, bspec],
        out_specs=anyspec,
        scratch_shapes=[
            pltpu.VMEM((2, max(CHUNK_BLOCKS) * EXPERT_BLOCK, XS_WIDTH), F32),
            pltpu.VMEM((2, max(CHUNK_BLOCKS) * EXPERT_BLOCK, D_MODEL), F32),
            pltpu.VMEM((EXPERT_BLOCK, D_MODEL), F32),
            pltpu.VMEM((2, 3, D_MODEL, D_MODEL), F32),
            pltpu.SemaphoreType.DMA((2,)),
            pltpu.SemaphoreType.DMA((2,)),
            pltpu.SemaphoreType.DMA(()),
            pltpu.SemaphoreType.DMA((2,)),
            pltpu.VMEM((D_MODEL, D_MODEL), BF16),
            pltpu.VMEM((D_MODEL, D_MODEL), BF16),
            pltpu.VMEM((D_MODEL, D_MODEL), BF16),
        ],
    )
    return pl.pallas_call(
        _expert_body,
        grid_spec=grid_spec,
        out_shape=jax.ShapeDtypeStruct((n_blocks * EXPERT_BLOCK, D_MODEL), F32),
        compiler_params=pltpu.CompilerParams(
            dimension_semantics=("arbitrary",), vmem_limit_bytes=VMEM_LIMIT),
        name="experts",
    )(*sched, xs, w_gate, b_gate.reshape(N_EXPERTS, 1, D_MODEL),
      w_up, b_up.reshape(N_EXPERTS, 1, D_MODEL), w_down, b_down.reshape(N_EXPERTS, 1, D_MODEL))


def _combine_body(units_ref, g8_ref, l8_ref, tot_ref,
                  y_hbm, x1_ref, meta_ref, lst_ref, g_ref, b_ref, out_ref, ys_s, sems):
    step = pl.program_id(0)
    n_tiles = pl.num_programs(0) * ROUTE_TILES
    TD, S = TILE_ROWS, SORT_ROWS

    def run_copy(s, g_unit, l_unit, n_units):
        return pltpu.make_async_copy(y_hbm.at[_rows(g_unit, n_units), :],
                                     ys_s.at[s, _rows(l_unit, n_units), :], sems.at[s])

    def fetch(tile, s):
        _for_each_run(units_ref, tile, lambda k, cnt: run_copy(s, g8_ref[k], l8_ref[k], cnt).start())

    @pl.when(step == 0)
    def _():
        ys_s[...] = jnp.zeros_like(ys_s)
        fetch(0, 0)

    lane_f = lax.broadcasted_iota(jnp.int32, (TD, LANES), 1).astype(F32)
    s_iota = lax.broadcasted_iota(jnp.int32, (TD, S), 1).astype(F32)
    for h in range(ROUTE_TILES):
        tau = step * ROUTE_TILES + h
        slot = h % 2
        rows = slice(h * TD, (h + 1) * TD)

        @pl.when(tau + 1 < n_tiles)
        def _():
            fetch(tau + 1, 1 - slot)

        run_copy(slot, 0, 0, tot_ref[tau]).wait()

        meta = meta_ref[rows, :]
        u_row = lst_ref[h * SUBLANES + 2:h * SUBLANES + 3, :]
        conds = []
        for j in range(TOP_K):
            oh = lane_f == meta[:, j:j + 1]
            start8 = jnp.sum(jnp.where(oh, u_row, 0.0), axis=-1, keepdims=True)
            lpos = RUN_ALIGN * start8 + meta[:, TOP_K + j:TOP_K + j + 1]
            conds.append(s_iota == lpos)
        perm = jnp.where(conds[0], 1.0, jnp.where(conds[1], 1.0, jnp.where(conds[2], 1.0,
                         jnp.where(conds[3], 1.0, 0.0)))).astype(BF16)
        ys = ys_s[slot]
        yh = ys.astype(BF16)
        yl = (ys - yh.astype(F32)).astype(BF16)
        ffn = (jnp.dot(perm, yh, preferred_element_type=F32) + jnp.dot(perm, yl, preferred_element_type=F32))
        acc = DEEPNORM_ALPHA * x1_ref[rows, :] + ffn
        mu = jnp.mean(acc, axis=-1, keepdims=True)
        hc = acc - mu
        var = jnp.mean(hc * hc, axis=-1, keepdims=True)
        out_ref[rows, :] = hc * lax.rsqrt(var + LN_EPS) * g_ref[...] + b_ref[...]


def _combine_call(sched, y_rows, x1, meta, lst_rows, ln2_g, ln2_b):
    T = x1.shape[0]
    RR = ROUTE_TILES * TILE_ROWS
    grid_spec = pltpu.PrefetchScalarGridSpec(
        num_scalar_prefetch=4,
        grid=(T // RR,),
        in_specs=[
            pl.BlockSpec(memory_space=pl.ANY),
            pl.BlockSpec((RR, D_MODEL), lambda i, *_: (i, 0)),
            pl.BlockSpec((RR, LANES), lambda i, *_: (i, 0)),
            pl.BlockSpec((ROUTE_TILES * SUBLANES, LANES), lambda i, *_: (i, 0)),
            pl.BlockSpec((1, D_MODEL), lambda i, *_: (0, 0)),
            pl.BlockSpec((1, D_MODEL), lambda i, *_: (0, 0)),
        ],
        out_specs=pl.BlockSpec((RR, D_MODEL), lambda i, *_: (i, 0)),
        scratch_shapes=[pltpu.VMEM((2, SORT_ROWS, D_MODEL), F32), pltpu.SemaphoreType.DMA((2,))],
    )
    return pl.pallas_call(
        _combine_body,
        grid_spec=grid_spec,
        out_shape=jax.ShapeDtypeStruct((T, D_MODEL), F32),
        compiler_params=pltpu.CompilerParams(
            dimension_semantics=("arbitrary",), vmem_limit_bytes=VMEM_LIMIT),
        name="combine",
    )(*sched, y_rows, x1, meta, lst_rows, ln2_g, ln2_b)


def _np_consts():
    TL = TILE_ROWS
    r = np.arange(TL)
    same = (r[:, None] // CHUNK) == (r[None, :] // CHUNK)
    tril = (same & (r[None, :] <= r[:, None])).astype(np.float32)
    trils = (r[None, :] < r[:, None]).astype(np.float32)
    e128 = np.zeros((LANES, SSD_WIDTH), np.float32)
    for h in range(SSD_HEADS):
        e128[h, h * SSD_HEAD_DIM:(h + 1) * SSD_HEAD_DIM] = 1.0
    return tril, trils, e128


def kernel(x, w_in, hg_lower_bound, hg_norm_w, conv_w, conv_b, dt_bias, a_log, d_skip, ssd_norm_w, w_out,
           ln1_g, ln1_b, router_w, router_b, w_gate, b_gate, w_up, b_up, w_down, b_down, ln2_g, ln2_b):
    batch, seq, d = x.shape
    assert d == D_MODEL and seq % STEP_ROWS == 0 and w_in.shape[0] == DEPTH
    assert ROUTE_TILES == 2 and (batch * seq) % (ROUTE_TILES * TILE_ROWS) == 0
    T = batch * seq
    n_tiles = T // TILE_ROWS
    max_rows = T * TOP_K + n_tiles * N_EXPERTS * (RUN_ALIGN - 1)
    n_blocks = -(-max_rows // EXPERT_BLOCK) + N_EXPERTS
    cap = n_blocks * EXPERT_BLOCK
    assert SORT_ROWS >= TILE_ROWS * TOP_K + N_EXPERTS * (RUN_ALIGN - 1)

    tril, trils, e128 = _np_consts()
    w = w_in[0]
    pad_l = LANES - SSD_HEADS
    rw = jnp.pad(router_w[0], ((0, 0), (0, LANES - N_EXPERTS)))
    rwh = rw.astype(BF16)
    rwm = (rw - rwh.astype(F32)).astype(BF16)
    consts = [
        jnp.pad(w.astype(BF16), ((0, 0), (0, pad_l))),
        hg_lower_bound,
        hg_norm_w[0].reshape(1, HG_HEAD_DIM),
        conv_w[0],
        conv_b[0].reshape(1, SSD_CONV_DIM),
        jnp.pad(dt_bias[0], (0, pad_l)).reshape(1, LANES),
        jnp.pad(a_log[0], (0, pad_l)).reshape(1, LANES),
        jnp.repeat(d_skip[0], SSD_HEAD_DIM).reshape(1, SSD_WIDTH),
        ssd_norm_w[0].reshape(1, SSD_WIDTH),
        w_out[0].astype(BF16),
        ln1_g[0].reshape(1, D_MODEL),
        ln1_b[0].reshape(1, D_MODEL),
        jnp.concatenate([rwh, rwm], axis=1),
        rwh,
        jnp.pad(router_b[0], (0, LANES - N_EXPERTS)).reshape(1, LANES),
        jnp.asarray(tril, BF16), jnp.asarray(trils, BF16), jnp.asarray(e128, BF16),
    ]
    x1, meta, cnt = _mixer_call(x.reshape(T, D_MODEL), consts, batch, seq)

    counts = cnt.reshape(n_tiles, SUBLANES, LANES)[:, 0, :N_EXPERTS].astype(jnp.int32)
    c8 = (counts + RUN_ALIGN - 1) // RUN_ALIGN * RUN_ALIGN
    used = jnp.sum(c8, axis=0)
    region = (used + EXPERT_BLOCK - 1) // EXPERT_BLOCK * EXPERT_BLOCK
    region_end = jnp.cumsum(region)
    region_start = region_end - region
    gstart = region_start[None, :] + jnp.cumsum(c8, axis=0) - c8
    lstart = jnp.cumsum(c8, axis=1) - c8
    n_valid = region_end[-1] // EXPERT_BLOCK
    has = region > 0
    eidx = jnp.arange(N_EXPERTS, dtype=jnp.int32)
    suffix_min = lax.cummin(jnp.where(has, eidx, N_EXPERTS), reverse=True)
    nxt_e = jnp.concatenate([suffix_min[1:], jnp.full((1,), N_EXPERTS, jnp.int32)])
    nxt_e = jnp.where(nxt_e == N_EXPERTS, eidx, nxt_e)
    par_e = (jnp.cumsum(has.astype(jnp.int32)) - 1) % 2
    left = region // EXPERT_BLOCK
    n_of = []
    for nb in CHUNK_BLOCKS:
        n_of.append(left // nb)
        left = left % nb
    it_cnt = sum(n_of)
    it_end = jnp.cumsum(it_cnt)
    n_iter = it_end[-1]
    max_iter = n_blocks // max(CHUNK_BLOCKS) + (len(CHUNK_BLOCKS) - 1) * N_EXPERTS
    it = jnp.minimum(jnp.arange(max_iter, dtype=jnp.int32), n_iter - 1)
    it_e = jnp.minimum(jnp.sum(it_end[None, :] <= it[:, None], axis=1), N_EXPERTS - 1).astype(jnp.int32)
    it_onehot = it_e[:, None] == eidx[None, :]
    lookup = lambda tab: jnp.sum(jnp.where(it_onehot, tab[None, :].astype(jnp.int32), 0), axis=1)
    k = it - lookup(it_end - it_cnt)
    it_sz = jnp.zeros_like(it)
    it_blk = lookup(region_start // EXPERT_BLOCK)
    for idx, nb in enumerate(CHUNK_BLOCKS):
        n_here = lookup(n_of[idx])
        inside = jnp.logical_and(k >= 0, k < n_here)
        it_sz = jnp.where(inside, idx, it_sz)
        it_blk = it_blk + nb * jnp.clip(k, 0, n_here)
        k = k - n_here
    expert_sched = (it_e, it_blk.astype(jnp.int32), it_sz.astype(jnp.int32), lookup(nxt_e).astype(jnp.int32),
                    lookup(par_e).astype(jnp.int32),
                    n_iter.astype(jnp.int32).reshape(1), n_valid.astype(jnp.int32).reshape(1))
    as_units = lambda a: (a // RUN_ALIGN).astype(jnp.int32).reshape(-1)
    units, g8, l8 = as_units(c8), as_units(gstart), as_units(lstart)
    tot = (jnp.sum(c8, axis=1) // RUN_ALIGN).astype(jnp.int32)
    tail8 = as_units(region_start + used)
    tailn = as_units(region - used)
    lu = lstart // RUN_ALIGN
    lst3 = jnp.stack([lu // LST_SPLIT, lu % LST_SPLIT, lu], axis=1).astype(F32)
    lst_rows = jnp.pad(lst3, ((0, 0), (0, SUBLANES - 3), (0, LANES - N_EXPERTS))).reshape(
        n_tiles * SUBLANES, LANES)

    nv = n_valid.astype(jnp.int32).reshape(1)
    xs = _dispatch_call((units, g8, l8, tot, tail8, tailn, nv), x1, meta, lst_rows, cap)
    y_rows = _expert_call(expert_sched, xs, w_gate[0], b_gate[0], w_up[0], b_up[0], w_down[0], b_down[0])
    out = _combine_call((units, g8, l8, tot), y_rows, x1, meta, lst_rows,
                        ln2_g[0].reshape(1, D_MODEL), ln2_b[0].reshape(1, D_MODEL))
    return out.reshape(batch, seq, D_MODEL)
```

```python
import functools

import jax
import jax.numpy as jnp
import numpy as np
from jax import lax
from jax.experimental import pallas as pl
from jax.experimental.pallas import tpu as pltpu

F32 = jnp.float32
BF16 = jnp.bfloat16

D_MODEL = 1024
CHUNK = 64
HG_WIDTH = 512
HG_HEAD_DIM = 128
HG_HEADS = 4
SSD_WIDTH = 512
SSD_HEAD_DIM = 64
SSD_HEADS = 8
SSD_GROUPS = 2
SSD_STATE = 128
SSD_CONV = 4
SSD_CONV_DIM = SSD_WIDTH + 2 * SSD_GROUPS * SSD_STATE
N_EXPERTS = 32
TOP_K = 4
EXPERT_BLOCK = 256
CHUNK_BLOCKS = (2, 1)
SWIGLU_LIMIT = 7.0
SWIGLU_ALPHA = 1.702
DEPTH = 1
DEEPNORM_ALPHA = (2 * DEPTH) ** 0.25
LN_EPS = 1e-5
RMS_EPS = 1e-5

LANES = 128
SUBLANES = 8
SUB_CHUNK = 16
EXP_CAP = 60.0
TILE_ROWS = 256
MIX_TILES = 2
ROUTE_TILES = 2
STEP_ROWS = MIX_TILES * TILE_ROWS
RUN_ALIGN = SUBLANES
SORT_ROWS = 1280
LST_SPLIT = 32
GATE_PARTS = 3
XS_WIDTH = D_MODEL + LANES
VMEM_LIMIT = 56 * 1024 * 1024

OFF_Q, OFF_F, OFF_I, OFF_G = 0, 512, 1024, 1536
OFF_Z, OFF_XBC = 2048, 2560
OFF_XS, OFF_B, OFF_C = 2560, 3072, 3328
OFF_DT = 3584
PROJ_COLS = OFF_DT + LANES
PROJ_SLICE = 256
CONV_SLICE = 256
OUT_SLICE = 256


def _bdot(a, b):
    return jnp.dot(a.astype(BF16), b.astype(BF16), preferred_element_type=F32)


def _bdot_nt(a, b):
    return lax.dot_general(a.astype(BF16), b.astype(BF16), (((1,), (1,)), ((), ())),
                           preferred_element_type=F32)


def _bdot_tn(a, b):
    return lax.dot_general(a.astype(BF16), b.astype(BF16), (((0,), (0,)), ((), ())),
                           preferred_element_type=F32)


def _split3(a):
    hi = a.astype(BF16)
    r1 = a - hi.astype(F32)
    mid = r1.astype(BF16)
    lo = (r1 - mid.astype(F32)).astype(BF16)
    return hi, mid, lo


def _dot01_left(m01, a):
    hi, mid, lo = _split3(a)
    d = functools.partial(jnp.dot, m01, preferred_element_type=F32)
    return d(hi) + d(mid) + d(lo)


def _dot01_right(a, m01):
    hi, mid, lo = _split3(a)
    return (jnp.dot(hi, m01, preferred_element_type=F32) + jnp.dot(mid, m01, preferred_element_type=F32)
            + jnp.dot(lo, m01, preferred_element_type=F32))


def _sigmoid(x):
    return 1.0 / (1.0 + jnp.exp(-x))


def _silu(x):
    return x * _sigmoid(x)


def _softplus(x):
    return jnp.maximum(x, 0.0) + jnp.log(1.0 + jnp.exp(-jnp.abs(x)))


def _mixer_body(x_ref, xnext_ref, win_ref, lbp_ref, hgnw_ref, convw_ref, convb_ref, dtb_ref, alog_ref, dskip_ref,
                ssdnw_ref, wout_ref, ln1g_ref, ln1b_ref, rw1_ref, rwh_ref, rb_ref,
                tril_ref, trils_ref, e128_ref,
                x1_ref, meta_ref, cnt_ref,
                proj_s, xb_s, xn_s, b_s, xpad_s, xdt_s, cse_s, st_s, pt_s, cat_s, ossd_s, mix_s):
    TL = TILE_ROWS
    t = pl.program_id(1)

    @pl.when(t == 0)
    def _():
        xpad_s[0:SUBLANES, :] = jnp.zeros((SUBLANES, SSD_CONV_DIM), F32)
        st_s[...] = jnp.zeros_like(st_s)
        pt_s[...] = jnp.zeros_like(pt_s)

    row64 = lax.broadcasted_iota(jnp.int32, (CHUNK, CHUNK), 0)
    col64 = lax.broadcasted_iota(jnp.int32, (CHUNK, CHUNK), 1)
    causal = row64 >= col64
    gw = SSD_WIDTH // SSD_GROUPS
    hpg = SSD_HEADS // SSD_GROUPS
    lane_head = lax.broadcasted_iota(jnp.int32, (CHUNK, gw), 1) // SSD_HEAD_DIM

    pending = []

    def filler():
        if pending:
            pending.pop(0)()

    def project_slices(lhs_ref, lhs_base, base):
        def one(c0, c1):
            def run():
                proj_s[base:base + TL, c0:c1] = jnp.dot(lhs_ref[lhs_base:lhs_base + TL, :], win_ref[:, c0:c1],
                                                        preferred_element_type=F32)
            return run
        return [one(c0, min(c0 + PROJ_SLICE, PROJ_COLS)) for c0 in range(0, PROJ_COLS, PROJ_SLICE)]

    def hg_front(base):
        a0 = lbp_ref[0:1, :]
        a1 = lbp_ref[1:2, :]
        am = jnp.maximum(a0, a1)
        e0 = jnp.exp(a0 - am)
        e1 = jnp.exp(a1 - am)
        lb = e0 / (e0 + e1)
        f = lb + (1.0 - lb) * _sigmoid(proj_s[base:base + TL, OFF_F:OFF_F + HG_WIDTH])
        b_s[base:base + TL, :] = _dot01_left(tril_ref[...], jnp.log(f))
        proj_s[base:base + TL, OFF_F:OFF_F + HG_WIDTH] = f
        proj_s[base:base + TL, OFF_Q:OFF_Q + HG_WIDTH] = _silu(proj_s[base:base + TL, OFF_Q:OFF_Q + HG_WIDTH])
        units = {}
        for c in range(TL // CHUNK):
            r0 = base + c * CHUNK
            for h in range(HG_HEADS):
                h0 = h * HG_HEAD_DIM
                bc = b_s[r0:r0 + CHUNK, h0:h0 + HG_HEAD_DIM]
                qc = proj_s[r0:r0 + CHUNK, OFF_Q + h0:OFF_Q + h0 + HG_HEAD_DIM]
                kc = 1.0 - proj_s[r0:r0 + CHUNK, OFF_F + h0:OFF_F + h0 + HG_HEAD_DIM]
                vcb = proj_s[r0:r0 + CHUNK, OFF_I + h0:OFF_I + h0 + HG_HEAD_DIM].astype(BF16)
                parts = []
                for i in range(CHUNK // SUB_CHUNK):
                    s0 = i * SUB_CHUNK
                    if i == 0:
                        qi = qc[0:SUB_CHUNK] * jnp.exp(bc[0:SUB_CHUNK])
                        ki = kc * jnp.exp(jnp.minimum(-bc, EXP_CAP))
                    else:
                        ref_i = bc[s0 - 1:s0, :]
                        qi = qc[s0:s0 + SUB_CHUNK] * jnp.exp(bc[s0:s0 + SUB_CHUNK] - ref_i)
                        ki = kc * jnp.exp(jnp.minimum(ref_i - bc, EXP_CAP))
                    parts.append(_bdot_nt(qi, ki))
                b_end = bc[CHUNK - 1:CHUNK, :]
                kdec = kc * jnp.exp(b_end - bc)
                units[c, h] = dict(
                    parts=parts, vcb=vcb, qdec=(qc * jnp.exp(bc)).astype(BF16),
                    local=_bdot_tn(vcb, kdec),
                    decay=jnp.exp(b_end))
                if h % 2 == 1:
                    filler()
        return units

    def hg_back(base, units):
        hgnw = hgnw_ref[...]
        for h in range(HG_HEADS):
            st = st_s[h]
            for c in range(TL // CHUNK):
                u = units[c, h]
                u['state'] = st
                st = st * u['decay'] + u['local']
            st_s[h] = st
        for c in range(TL // CHUNK):
            r0 = base + c * CHUNK
            for h in range(HG_HEADS):
                h0 = h * HG_HEAD_DIM
                u = units[c, h]
                sc = jnp.where(causal, jnp.concatenate(u['parts'], axis=0), 0.0)
                o = _bdot(sc, u['vcb']) + _bdot_nt(u['qdec'], u['state'])
                ms = jnp.mean(o * o, axis=-1, keepdims=True)
                on = o * lax.rsqrt(ms + RMS_EPS) * hgnw
                gc = proj_s[r0:r0 + CHUNK, OFF_G + h0:OFF_G + h0 + HG_HEAD_DIM]
                cat_s[r0:r0 + CHUNK, h0:h0 + HG_HEAD_DIM] = (on * _silu(gc)).astype(BF16)
                filler()

    def ssd_front(base):
        e128 = e128_ref[...]
        dtc = _softplus(proj_s[base:base + TL, OFF_DT:OFF_DT + LANES] + dtb_ref[...])
        a_row = -jnp.exp(alog_ref[...])
        cs_c = _dot01_left(tril_ref[...], dtc * a_row)
        cs_r = jnp.transpose(cs_c)[0:SSD_HEADS, :]
        cse_s[base:base + TL, :] = _dot01_right(cs_c, e128)
        dt_exp = _dot01_right(dtc, e128)

        xpad_s[SUBLANES + base:SUBLANES + base + TL, :] = proj_s[base:base + TL, OFF_XBC:OFF_XBC + SSD_CONV_DIM]
        for c0 in range(0, SSD_CONV_DIM, CONV_SLICE):
            acc = jnp.broadcast_to(convb_ref[:, c0:c0 + CONV_SLICE], (TL, CONV_SLICE))
            for j in range(SSD_CONV):
                off = base + SUBLANES - (SSD_CONV - 1) + j
                acc = acc + convw_ref[j:j + 1, c0:c0 + CONV_SLICE] * xpad_s[off:off + TL, c0:c0 + CONV_SLICE]
            proj_s[base:base + TL, OFF_XBC + c0:OFF_XBC + c0 + CONV_SLICE] = _silu(acc)
            filler()
        xdt_s[base:base + TL, :] = proj_s[base:base + TL, OFF_XS:OFF_XS + SSD_WIDTH] * dt_exp

        units = {}
        for c in range(TL // CHUNK):
            r0 = base + c * CHUNK
            for g in range(SSD_GROUPS):
                g0 = g * gw
                bgb = proj_s[r0:r0 + CHUNK, OFF_B + g * SSD_STATE:OFF_B + (g + 1) * SSD_STATE].astype(BF16)
                cgb = proj_s[r0:r0 + CHUNK, OFF_C + g * SSD_STATE:OFF_C + (g + 1) * SSD_STATE].astype(BF16)
                cse_g = cse_s[r0:r0 + CHUNK, g0:g0 + gw]
                cs_end = cse_s[r0 + CHUNK - 1:r0 + CHUNK, g0:g0 + gw]
                xdt_g = xdt_s[r0:r0 + CHUNK, g0:g0 + gw]
                units[c, g] = dict(
                    cgb=cgb, gm=_bdot_nt(cgb, bgb),
                    local=_bdot_tn(bgb, xdt_g * jnp.exp(cs_end - cse_g)),
                    decay=jnp.exp(cs_end))
                filler()
        return units, cs_r

    def ssd_back(base, units, cs_r):
        for g in range(SSD_GROUPS):
            pt = pt_s[g]
            for c in range(TL // CHUNK):
                u = units[c, g]
                u['state'] = pt
                pt = pt * u['decay'] + u['local']
            pt_s[g] = pt
        for c in range(TL // CHUNK):
            r0 = base + c * CHUNK
            for g in range(SSD_GROUPS):
                g0 = g * gw
                u = units[c, g]
                cse_g = cse_s[r0:r0 + CHUNK, g0:g0 + gw]
                xdt_g = xdt_s[r0:r0 + CHUNK, g0:g0 + gw]
                ydiag = jnp.zeros((CHUNK, gw), F32)
                for hl in range(hpg):
                    hh = g * hpg + hl
                    seg = (cse_g[:, hl * SSD_HEAD_DIM:(hl + 1) * SSD_HEAD_DIM]
                           - cs_r[hh:hh + 1, c * CHUNK:(c + 1) * CHUNK])
                    lm = jnp.where(causal, jnp.exp(jnp.minimum(seg, 0.0)), 0.0)
                    xm = jnp.where(lane_head == hl, xdt_g, 0.0)
                    ydiag = ydiag + _bdot(u['gm'] * lm, xm)
                yoff = _bdot(u['cgb'], u['state']) * jnp.exp(cse_g)
                xs_g = proj_s[r0:r0 + CHUNK, OFF_XS + g0:OFF_XS + g0 + gw]
                ossd_s[r0:r0 + CHUNK, g0:g0 + gw] = ydiag + yoff + xs_g * dskip_ref[:, g0:g0 + gw]
                filler()

    def ssd_gate_norm(tile):
        base = tile * TL
        y = ossd_s[base:base + TL, :] * _silu(proj_s[base:base + TL, OFF_Z:OFF_Z + SSD_WIDTH])
        for g in range(SSD_GROUPS):
            yg = y[:, g * gw:(g + 1) * gw]
            ms = jnp.mean(yg * yg, axis=-1, keepdims=True)
            yn = yg * lax.rsqrt(ms + RMS_EPS) * ssdnw_ref[:, g * gw:(g + 1) * gw]
            cat_s[base:base + TL, HG_WIDTH + g * gw:HG_WIDTH + (g + 1) * gw] = yn.astype(BF16)

    def out_slice(tile, c0):
        base = tile * TL
        mix_s[base:base + TL, c0:c0 + OUT_SLICE] = jnp.dot(
            cat_s[base:base + TL, :], wout_ref[:, c0:c0 + OUT_SLICE], preferred_element_type=F32)

    def norm_route(tile):
        base = tile * TL
        hres = DEEPNORM_ALPHA * x_ref[base:base + TL, :] + mix_s[base:base + TL, :]
        mu = jnp.mean(hres, axis=-1, keepdims=True)
        hc = hres - mu
        var = jnp.mean(hc * hc, axis=-1, keepdims=True)
        x1 = hc * lax.rsqrt(var + LN_EPS) * ln1g_ref[...] + ln1b_ref[...]
        x1_ref[base:base + TL, :] = x1
        filler()

        xh = x1.astype(BF16)
        xm_ = (x1 - xh.astype(F32)).astype(BF16)
        t1 = jnp.dot(xh, rw1_ref[...], preferred_element_type=F32)
        logits = (t1[:, 0:LANES] + t1[:, LANES:2 * LANES]
                  + jnp.dot(xm_, rwh_ref[...], preferred_element_type=F32) + rb_ref[...])
        filler()
        lane = lax.broadcasted_iota(jnp.int32, (TL, LANES), 1)
        lane_f = lane.astype(F32)
        neg = jnp.float32(-jnp.inf)
        work = jnp.where(lane < N_EXPERTS, logits, neg)
        onehots, vals, idxs = [], [], []
        for j in range(TOP_K):
            m = jnp.max(work, axis=-1, keepdims=True)
            filler()
            idx = jnp.min(jnp.where(work == m, lane_f, float(LANES)), axis=-1, keepdims=True)
            oh = lane_f == idx
            onehots.append(oh)
            vals.append(m)
            idxs.append(idx)
            work = jnp.where(oh, neg, work)
            filler()
        es = [jnp.exp(v - vals[0]) for v in vals]
        den = es[0] + es[1] + es[2] + es[3]
        gates = [e / den for e in es]
        sel = jnp.zeros((TL, LANES), F32)
        for oh in onehots:
            sel = jnp.where(oh, 1.0, sel)
        rankmat = jnp.dot(trils_ref[...], sel.astype(BF16), preferred_element_type=F32)
        filler()
        cnt_ref[tile * SUBLANES:(tile + 1) * SUBLANES, :] = jnp.broadcast_to(
            jnp.sum(sel, axis=0, keepdims=True), (SUBLANES, LANES))
        meta = jnp.zeros((TL, LANES), F32)
        for j in range(TOP_K):
            rank_j = jnp.sum(jnp.where(onehots[j], rankmat, 0.0), axis=-1, keepdims=True)
            meta = jnp.where(lane == j, idxs[j], meta)
            meta = jnp.where(lane == TOP_K + j, rank_j, meta)
            meta = jnp.where(lane == 2 * TOP_K + j, gates[j], meta)
        meta_ref[base:base + TL, :] = meta

    def post_thunks(tile):
        return ([functools.partial(ssd_gate_norm, tile)]
                + [functools.partial(out_slice, tile, c0) for c0 in range(0, D_MODEL, OUT_SLICE)]
                + [functools.partial(norm_route, tile)])

    xb_s[...] = x_ref[...].astype(BF16)
    xn_s[...] = xnext_ref[...].astype(BF16)

    @pl.when(jnp.logical_and(pl.program_id(0) == 0, t == 0))
    def _():
        for run in project_slices(xb_s, 0, 0):
            run()

    for tile in range(MIX_TILES):
        base = tile * TL
        if tile + 1 < MIX_TILES:
            pending.extend(project_slices(xb_s, base + TL, base + TL))
        hg_units = hg_front(base)
        ssd_units, cs_r = ssd_front(base)
        while pending:
            filler()
        hg_back(base, hg_units)
        ssd_back(base, ssd_units, cs_r)
        if tile + 1 < MIX_TILES:
            pending.extend(post_thunks(tile))
    tail = post_thunks(MIX_TILES - 1)
    slices = project_slices(xn_s, 0, 0)
    tail[0]()
    for run in tail[1:-1]:
        run()
        slices.pop(0)()
    pending.extend(slices)
    tail[-1]()
    while pending:
        filler()
    xpad_s[0:SUBLANES, :] = xpad_s[STEP_ROWS:STEP_ROWS + SUBLANES, :]


def _full(shape):
    nd = len(shape)
    return pl.BlockSpec(shape, lambda *_: (0,) * nd)


def _mixer_call(x2d, consts, batch, seq):
    SR = STEP_ROWS
    nt = seq // SR
    T = batch * seq
    last_tile = T // TILE_ROWS - 1
    in_specs = [pl.BlockSpec((SR, D_MODEL), lambda b, t: (b * nt + t, 0)),
                pl.BlockSpec((TILE_ROWS, D_MODEL),
                             lambda b, t: (jnp.minimum((b * nt + t + 1) * MIX_TILES, last_tile), 0))]
    in_specs += [_full(c.shape) for c in consts]
    out_shape = (jax.ShapeDtypeStruct((T, D_MODEL), F32),
                 jax.ShapeDtypeStruct((T, LANES), F32),
                 jax.ShapeDtypeStruct((T // TILE_ROWS * SUBLANES, LANES), F32))
    out_specs = (pl.BlockSpec((SR, D_MODEL), lambda b, t: (b * nt + t, 0)),
                 pl.BlockSpec((SR, LANES), lambda b, t: (b * nt + t, 0)),
                 pl.BlockSpec((MIX_TILES * SUBLANES, LANES), lambda b, t: (b * nt + t, 0)))
    scratch = [
        pltpu.VMEM((SR, PROJ_COLS), F32),
        pltpu.VMEM((SR, D_MODEL), BF16),
        pltpu.VMEM((TILE_ROWS, D_MODEL), BF16),
        pltpu.VMEM((SR, HG_WIDTH), F32),
        pltpu.VMEM((SR + 2 * SUBLANES, SSD_CONV_DIM), F32),
        pltpu.VMEM((SR, SSD_WIDTH), F32),
        pltpu.VMEM((SR, SSD_WIDTH), F32),
        pltpu.VMEM((HG_HEADS, HG_HEAD_DIM, HG_HEAD_DIM), F32),
        pltpu.VMEM((SSD_GROUPS, SSD_STATE, SSD_WIDTH // SSD_GROUPS), F32),
        pltpu.VMEM((SR, D_MODEL), BF16),
        pltpu.VMEM((SR, SSD_WIDTH), F32),
        pltpu.VMEM((SR, D_MODEL), F32),
    ]
    return pl.pallas_call(
        _mixer_body,
        grid=(batch, nt),
        in_specs=in_specs,
        out_specs=out_specs,
        out_shape=out_shape,
        scratch_shapes=scratch,
        compiler_params=pltpu.CompilerParams(
            dimension_semantics=("arbitrary", "arbitrary"), vmem_limit_bytes=VMEM_LIMIT),
        name="mixer",
    )(x2d, x2d, *consts)


def _for_each_run(units_ref, tile, fn):
    def per_expert(e, carry):
        k = tile * N_EXPERTS + e
        n = units_ref[k]

        @pl.when(n > 0)
        def _():
            fn(k, n)
        return carry
    lax.fori_loop(0, N_EXPERTS, per_expert, 0)


def _rows(unit, n_units=1):
    start = unit * RUN_ALIGN
    if RUN_ALIGN > 1:
        start = pl.multiple_of(start, RUN_ALIGN)
    return pl.ds(start, n_units * RUN_ALIGN)


def _dispatch_body(units_ref, g8_ref, l8_ref, tot_ref, tail8_ref, tailn_ref, nv_ref,
                   x1_ref, meta_ref, lst_ref, xs_hbm, sorted_s, zero_s, sems, zsem, bsem):
    step = pl.program_id(0)
    n = pl.num_programs(0)
    TD, S = TILE_ROWS, SORT_ROWS
    n_blocks = xs_hbm.shape[0] // EXPERT_BLOCK

    def unused_block_copy(m):
        rows = pl.ds(pl.multiple_of(m * EXPERT_BLOCK, EXPERT_BLOCK), EXPERT_BLOCK)
        return pltpu.make_async_copy(zero_s, xs_hbm.at[rows, :], bsem)

    def run_copy(s, l_unit, g_unit, n_units):
        return pltpu.make_async_copy(sorted_s.at[s, _rows(l_unit, n_units), :],
                                     xs_hbm.at[_rows(g_unit, n_units), :], sems.at[s])

    def wait_units(s, count):
        run_copy(s, 0, 0, count).wait()

    @pl.when(step == 0)
    def _():
        zero_s[...] = jnp.zeros_like(zero_s)

        def tail_copy(e):
            n_units = tailn_ref[e]
            return pltpu.make_async_copy(zero_s.at[pl.ds(0, n_units * RUN_ALIGN), :],
                                         xs_hbm.at[_rows(tail8_ref[e], n_units), :], zsem)

        def start_unused(m, carry):
            unused_block_copy(m).start()
            return carry
        lax.fori_loop(nv_ref[0], n_blocks, start_unused, 0)

        def start_e(e, carry):
            @pl.when(tailn_ref[e] > 0)
            def _():
                tail_copy(e).start()
            return carry
        lax.fori_loop(0, N_EXPERTS, start_e, 0)

        def wait_e(e, carry):
            @pl.when(tailn_ref[e] > 0)
            def _():
                tail_copy(e).wait()
            return carry
        lax.fori_loop(0, N_EXPERTS, wait_e, 0)

    lane = lax.broadcasted_iota(jnp.int32, (TD, LANES), 1)
    lane_f = lane.astype(F32)
    ones8 = jnp.ones((SUBLANES, LANES), BF16)
    r_iota = lax.broadcasted_iota(jnp.int32, (S, TD), 0).astype(F32)
    nt_dims = (((1,), (1,)), ((), ()))
    for slot in range(ROUTE_TILES):
        _dispatch_tile(step * ROUTE_TILES + slot, slot, step, units_ref, g8_ref, l8_ref, tot_ref,
                       x1_ref, meta_ref, lst_ref, sorted_s, run_copy, wait_units,
                       lane, lane_f, ones8, r_iota, nt_dims)

    @pl.when(step == n - 1)
    def _():
        for slot in range(ROUTE_TILES):
            wait_units(slot, tot_ref[step * ROUTE_TILES + slot])

        def wait_unused(m, carry):
            unused_block_copy(m).wait()
            return carry
        lax.fori_loop(nv_ref[0], n_blocks, wait_unused, 0)


def _dispatch_tile(tau, slot, step, units_ref, g8_ref, l8_ref, tot_ref, x1_ref, meta_ref, lst_ref, sorted_s,
                   run_copy, wait_units, lane, lane_f, ones8, r_iota, nt_dims):
    TD = TILE_ROWS
    rows = slice(slot * TD, (slot + 1) * TD)

    @pl.when(step >= 1)
    def _():
        wait_units(slot, tot_ref[jnp.maximum(tau - ROUTE_TILES, 0)])

    meta = meta_ref[rows, :]
    u8 = lst_ref[slot * SUBLANES:(slot + 1) * SUBLANES, :].astype(BF16)
    u_row = lst_ref[slot * SUBLANES + 2:slot * SUBLANES + 3, :]
    gh = meta.astype(BF16).astype(F32)
    g1 = meta - gh
    gm = g1.astype(BF16).astype(F32)
    gl = g1 - gm
    lane_j = jnp.floor((lane_f + 0.5) * (1.0 / GATE_PARTS))
    lane_p = lane_f - GATE_PARTS * lane_j
    aux_g = jnp.zeros((TD, LANES), F32)
    aux_h = jnp.zeros((TD, LANES), F32)
    aux_l = jnp.zeros((TD, LANES), F32)
    conds = []
    for j in range(TOP_K):
        oh = lane_f == meta[:, j:j + 1]
        rank_col = meta[:, TOP_K + j:TOP_K + j + 1]
        ohb = jnp.where(oh, 1.0, 0.0).astype(BF16)
        rkb = jnp.where(oh, rank_col, 0.0).astype(BF16)
        m1 = lax.dot_general(u8, ohb, nt_dims, preferred_element_type=F32)
        m2 = lax.dot_general(ones8, rkb, nt_dims, preferred_element_type=F32)
        lpos = RUN_ALIGN * (LST_SPLIT * m1[0:1, :] + m1[1:2, :]) + m2[0:1, :]
        conds.append(r_iota == lpos)
        lpos_col = RUN_ALIGN * jnp.sum(jnp.where(oh, u_row, 0.0), axis=-1, keepdims=True) + rank_col
        hi_col = jnp.floor(lpos_col * (1.0 / LST_SPLIT))
        gc = 2 * TOP_K + j
        g_j = jnp.where(lane_p == 0, gh[:, gc:gc + 1], jnp.where(lane_p == 1, gm[:, gc:gc + 1], gl[:, gc:gc + 1]))
        aux_g = jnp.where(lane_j == j, g_j, aux_g)
        aux_h = jnp.where(lane_j == j, hi_col, aux_h)
        aux_l = jnp.where(lane_j == j, lpos_col - LST_SPLIT * hi_col, aux_l)
    perm = jnp.where(conds[0], 1.0, jnp.where(conds[1], 1.0, jnp.where(conds[2], 1.0,
                     jnp.where(conds[3], 1.0, 0.0)))).astype(BF16)
    sorted_x = jnp.dot(perm, x1_ref[rows, :].astype(BF16), preferred_element_type=F32)
    aux = jnp.concatenate([aux_g, aux_h, aux_l], axis=1).astype(BF16)
    sorted_aux = jnp.dot(perm, aux, preferred_element_type=F32)
    own_row = lax.broadcasted_iota(jnp.int32, (sorted_aux.shape[0], LANES), 0).astype(F32)
    match = LST_SPLIT * sorted_aux[:, LANES:2 * LANES] + sorted_aux[:, 2 * LANES:3 * LANES] == own_row
    sorted_g = jnp.where(match, sorted_aux[:, 0:LANES], 0.0)
    sorted_s[slot, :, 0:D_MODEL] = sorted_x
    sorted_s[slot, :, D_MODEL:XS_WIDTH] = sorted_g

    _for_each_run(units_ref, tau, lambda k, cnt: run_copy(slot, l8_ref[k], g8_ref[k], cnt).start())


def _dispatch_call(sched, x1, meta, lst_rows, cap):
    T = x1.shape[0]
    RR = ROUTE_TILES * TILE_ROWS
    grid_spec = pltpu.PrefetchScalarGridSpec(
        num_scalar_prefetch=7,
        grid=(T // RR,),
        in_specs=[
            pl.BlockSpec((RR, D_MODEL), lambda i, *_: (i, 0)),
            pl.BlockSpec((RR, LANES), lambda i, *_: (i, 0)),
            pl.BlockSpec((ROUTE_TILES * SUBLANES, LANES), lambda i, *_: (i, 0)),
        ],
        out_specs=pl.BlockSpec(memory_space=pl.ANY),
        scratch_shapes=[
            pltpu.VMEM((2, SORT_ROWS, XS_WIDTH), F32),
            pltpu.VMEM((EXPERT_BLOCK, XS_WIDTH), F32),
            pltpu.SemaphoreType.DMA((2,)),
            pltpu.SemaphoreType.DMA(()),
            pltpu.SemaphoreType.DMA(()),
        ],
    )
    return pl.pallas_call(
        _dispatch_body,
        grid_spec=grid_spec,
        out_shape=jax.ShapeDtypeStruct((cap, XS_WIDTH), F32),
        compiler_params=pltpu.CompilerParams(
            dimension_semantics=("arbitrary",), vmem_limit_bytes=VMEM_LIMIT),
        name="dispatch",
    )(*sched, x1, meta, lst_rows)


def _expert_body(ite_ref, itb_ref, sz_ref, nxt_ref, par_ref, nit_ref, nv_ref,
                 xs_hbm, wg_hbm, bg_ref, wu_hbm, bu_ref, wd_hbm, bd_ref,
                 y_hbm, xbuf, ybuf, zbuf, wbuf, xsem, ysem, zsem, wsem, wg_s, wu_s, wd_s):
    n_iter = nit_ref[0]
    n_blocks = xs_hbm.shape[0] // EXPERT_BLOCK
    sizes = tuple(n * EXPERT_BLOCK for n in CHUNK_BLOCKS)

    def rows_of(i, rows):
        return pl.ds(pl.multiple_of(itb_ref[i] * EXPERT_BLOCK, EXPERT_BLOCK), rows)

    def x_copy(i, s, rows):
        return pltpu.make_async_copy(xs_hbm.at[rows_of(i, rows), :], xbuf.at[s, pl.ds(0, rows), :], xsem.at[s])

    def y_copy(i, s, rows):
        return pltpu.make_async_copy(ybuf.at[s, pl.ds(0, rows), :], y_hbm.at[rows_of(i, rows), :], ysem.at[s])

    def by_size(i, fn):
        for k, rows in enumerate(sizes):
            @pl.when(sz_ref[i] == k)
            def _():
                fn(rows)

    def zero_copy(m):
        rows = pl.ds(pl.multiple_of(m * EXPERT_BLOCK, EXPERT_BLOCK), EXPERT_BLOCK)
        return pltpu.make_async_copy(zbuf, y_hbm.at[rows, :], zsem)

    def weight_copies(e, s):
        return [pltpu.make_async_copy(w.at[e], wbuf.at[s, k], wsem.at[s])
                for k, w in enumerate((wg_hbm, wu_hbm, wd_hbm))]

    for c in weight_copies(ite_ref[0], par_ref[0]):
        c.start()
    by_size(0, lambda rows: x_copy(0, 0, rows).start())

    zbuf[...] = jnp.zeros_like(zbuf)

    def start_zero(m, carry):
        zero_copy(m).start()
        return carry
    lax.fori_loop(nv_ref[0], n_blocks, start_zero, 0)

    def ffn(s, e, rows):
        xb = xbuf[s, 0:rows, 0:D_MODEL].astype(BF16)
        gate = jnp.sum(xbuf[s, 0:rows, D_MODEL:XS_WIDTH], axis=-1, keepdims=True)
        hg = jnp.minimum(jnp.dot(xb, wg_s[...], preferred_element_type=F32) + bg_ref[e], SWIGLU_LIMIT)
        hu = jnp.clip(jnp.dot(xb, wu_s[...], preferred_element_type=F32) + bu_ref[e],
                      -SWIGLU_LIMIT, SWIGLU_LIMIT)
        hact = (hu + 1.0) * (hg * _sigmoid(SWIGLU_ALPHA * hg))
        y = jnp.dot(hact.astype(BF16), wd_s[...], preferred_element_type=F32) + bd_ref[e]
        ybuf[s, 0:rows, :] = y * gate

    def body(i, carry):
        s = lax.rem(i, 2)
        e = ite_ref[i]
        by_size(i, lambda rows: x_copy(i, s, rows).wait())

        @pl.when(i + 1 < n_iter)
        def _():
            by_size(i + 1, lambda rows: x_copy(i + 1, 1 - s, rows).start())

        @pl.when(jnp.logical_or(i == 0, e != ite_ref[jnp.maximum(i - 1, 0)]))
        def _():
            ws = par_ref[i]
            for c in weight_copies(e, ws):
                c.wait()

            @pl.when(nxt_ref[i] != e)
            def _():
                for c in weight_copies(nxt_ref[i], 1 - ws):
                    c.start()
            wg_s[...] = wbuf[ws, 0].astype(BF16)
            wu_s[...] = wbuf[ws, 1].astype(BF16)
            wd_s[...] = wbuf[ws, 2].astype(BF16)

        @pl.when(i >= 2)
        def _():
            by_size(i - 2, lambda rows: y_copy(i - 2, s, rows).wait())

        def chunk(rows):
            ffn(s, e, rows)
            y_copy(i, s, rows).start()
        by_size(i, chunk)
        return carry
    lax.fori_loop(0, n_iter, body, 0)

    @pl.when(n_iter >= 2)
    def _():
        by_size(n_iter - 2, lambda rows: y_copy(n_iter - 2, lax.rem(n_iter, 2), rows).wait())
    by_size(n_iter - 1, lambda rows: y_copy(n_iter - 1, lax.rem(n_iter - 1, 2), rows).wait())

    def wait_zero(m, carry):
        zero_copy(m).wait()
        return carry
    lax.fori_loop(nv_ref[0], n_blocks, wait_zero, 0)


def _expert_call(sched, xs, w_gate, b_gate, w_up, b_up, w_down, b_down):
    n_blocks = xs.shape[0] // EXPERT_BLOCK
    anyspec = pl.BlockSpec(memory_space=pl.ANY)
    bspec = pl.BlockSpec((N_EXPERTS, 1, D_MODEL), lambda i, *_: (0, 0, 0))
    grid_spec = pltpu.PrefetchScalarGridSpec(
        num_scalar_prefetch=len(sched),
        grid=(1,),
        in_specs=[anyspec, anyspec, bspec, anyspec, bspec, anyspec, bspec],
        out_specs=anyspec,
        scratch_shapes=[
            pltpu.VMEM((2, max(CHUNK_BLOCKS) * EXPERT_BLOCK, XS_WIDTH), F32),
            pltpu.VMEM((2, max(CHUNK_BLOCKS) * EXPERT_BLOCK, D_MODEL), F32),
            pltpu.VMEM((EXPERT_BLOCK, D_MODEL), F32),
            pltpu.VMEM((2, 3, D_MODEL, D_MODEL), F32),
            pltpu.SemaphoreType.DMA((2,)),
            pltpu.SemaphoreType.DMA((2,)),
            pltpu.SemaphoreType.DMA(()),
            pltpu.SemaphoreType.DMA((2,)),
            pltpu.VMEM((D_MODEL, D_MODEL), BF16),
            pltpu.VMEM((D_MODEL, D_MODEL), BF16),
            pltpu.VMEM((D_MODEL, D_MODEL), BF16),
        ],
    )
    return pl.pallas_call(
        _expert_body,
        grid_spec=grid_spec,
        out_shape=jax.ShapeDtypeStruct((n_blocks * EXPERT_BLOCK, D_MODEL), F32),
        compiler_params=pltpu.CompilerParams(
            dimension_semantics=("arbitrary",), vmem_limit_bytes=VMEM_LIMIT),
        name="experts",
    )(*sched, xs, w_gate, b_gate.reshape(N_EXPERTS, 1, D_MODEL),
      w_up, b_up.reshape(N_EXPERTS, 1, D_MODEL), w_down, b_down.reshape(N_EXPERTS, 1, D_MODEL))


def _combine_body(units_ref, g8_ref, l8_ref, tot_ref,
                  y_hbm, x1_ref, meta_ref, lst_ref, g_ref, b_ref, out_ref, ys_s, sems):
    step = pl.program_id(0)
    n_tiles = pl.num_programs(0) * ROUTE_TILES
    TD, S = TILE_ROWS, SORT_ROWS

    def run_copy(s, g_unit, l_unit, n_units):
        return pltpu.make_async_copy(y_hbm.at[_rows(g_unit, n_units), :],
                                     ys_s.at[s, _rows(l_unit, n_units), :], sems.at[s])

    def fetch(tile, s):
        _for_each_run(units_ref, tile, lambda k, cnt: run_copy(s, g8_ref[k], l8_ref[k], cnt).start())

    @pl.when(step == 0)
    def _():
        ys_s[...] = jnp.zeros_like(ys_s)
        fetch(0, 0)

    lane_f = lax.broadcasted_iota(jnp.int32, (TD, LANES), 1).astype(F32)
    s_iota = lax.broadcasted_iota(jnp.int32, (TD, S), 1).astype(F32)
    for h in range(ROUTE_TILES):
        tau = step * ROUTE_TILES + h
        slot = h % 2
        rows = slice(h * TD, (h + 1) * TD)

        @pl.when(tau + 1 < n_tiles)
        def _():
            fetch(tau + 1, 1 - slot)

        run_copy(slot, 0, 0, tot_ref[tau]).wait()

        meta = meta_ref[rows, :]
        u_row = lst_ref[h * SUBLANES + 2:h * SUBLANES + 3, :]
        conds = []
        for j in range(TOP_K):
            oh = lane_f == meta[:, j:j + 1]
            start8 = jnp.sum(jnp.where(oh, u_row, 0.0), axis=-1, keepdims=True)
            lpos = RUN_ALIGN * start8 + meta[:, TOP_K + j:TOP_K + j + 1]
            conds.append(s_iota == lpos)
        perm = jnp.where(conds[0], 1.0, jnp.where(conds[1], 1.0, jnp.where(conds[2], 1.0,
                         jnp.where(conds[3], 1.0, 0.0)))).astype(BF16)
        ys = ys_s[slot]
        yh = ys.astype(BF16)
        yl = (ys - yh.astype(F32)).astype(BF16)
        ffn = (jnp.dot(perm, yh, preferred_element_type=F32) + jnp.dot(perm, yl, preferred_element_type=F32))
        acc = DEEPNORM_ALPHA * x1_ref[rows, :] + ffn
        mu = jnp.mean(acc, axis=-1, keepdims=True)
        hc = acc - mu
        var = jnp.mean(hc * hc, axis=-1, keepdims=True)
        out_ref[rows, :] = hc * lax.rsqrt(var + LN_EPS) * g_ref[...] + b_ref[...]


def _combine_call(sched, y_rows, x1, meta, lst_rows, ln2_g, ln2_b):
    T = x1.shape[0]
    RR = ROUTE_TILES * TILE_ROWS
    grid_spec = pltpu.PrefetchScalarGridSpec(
        num_scalar_prefetch=4,
        grid=(T // RR,),
        in_specs=[
            pl.BlockSpec(memory_space=pl.ANY),
            pl.BlockSpec((RR, D_MODEL), lambda i, *_: (i, 0)),
            pl.BlockSpec((RR, LANES), lambda i, *_: (i, 0)),
            pl.BlockSpec((ROUTE_TILES * SUBLANES, LANES), lambda i, *_: (i, 0)),
            pl.BlockSpec((1, D_MODEL), lambda i, *_: (0, 0)),
            pl.BlockSpec((1, D_MODEL), lambda i, *_: (0, 0)),
        ],
        out_specs=pl.BlockSpec((RR, D_MODEL), lambda i, *_: (i, 0)),
        scratch_shapes=[pltpu.VMEM((2, SORT_ROWS, D_MODEL), F32), pltpu.SemaphoreType.DMA((2,))],
    )
    return pl.pallas_call(
        _combine_body,
        grid_spec=grid_spec,
        out_shape=jax.ShapeDtypeStruct((T, D_MODEL), F32),
        compiler_params=pltpu.CompilerParams(
            dimension_semantics=("arbitrary",), vmem_limit_bytes=VMEM_LIMIT),
        name="combine",
    )(*sched, y_rows, x1, meta, lst_rows, ln2_g, ln2_b)


def _np_consts():
    TL = TILE_ROWS
    r = np.arange(TL)
    same = (r[:, None] // CHUNK) == (r[None, :] // CHUNK)
    tril = (same & (r[None, :] <= r[:, None])).astype(np.float32)
    trils = (r[None, :] < r[:, None]).astype(np.float32)
    e128 = np.zeros((LANES, SSD_WIDTH), np.float32)
    for h in range(SSD_HEADS):
        e128[h, h * SSD_HEAD_DIM:(h + 1) * SSD_HEAD_DIM] = 1.0
    return tril, trils, e128


def kernel(x, w_in, hg_lower_bound, hg_norm_w, conv_w, conv_b, dt_bias, a_log, d_skip, ssd_norm_w, w_out,
           ln1_g, ln1_b, router_w, router_b, w_gate, b_gate, w_up, b_up, w_down, b_down, ln2_g, ln2_b):
    batch, seq, d = x.shape
    assert d == D_MODEL and seq % STEP_ROWS == 0 and w_in.shape[0] == DEPTH
    assert ROUTE_TILES == 2 and (batch * seq) % (ROUTE_TILES * TILE_ROWS) == 0
    T = batch * seq
    n_tiles = T // TILE_ROWS
    max_rows = T * TOP_K + n_tiles * N_EXPERTS * (RUN_ALIGN - 1)
    n_blocks = -(-max_rows // EXPERT_BLOCK) + N_EXPERTS
    cap = n_blocks * EXPERT_BLOCK
    assert SORT_ROWS >= TILE_ROWS * TOP_K + N_EXPERTS * (RUN_ALIGN - 1)

    tril, trils, e128 = _np_consts()
    w = w_in[0]
    pad_l = LANES - SSD_HEADS
    rw = jnp.pad(router_w[0], ((0, 0), (0, LANES - N_EXPERTS)))
    rwh = rw.astype(BF16)
    rwm = (rw - rwh.astype(F32)).astype(BF16)
    consts = [
        jnp.pad(w.astype(BF16), ((0, 0), (0, pad_l))),
        hg_lower_bound,
        hg_norm_w[0].reshape(1, HG_HEAD_DIM),
        conv_w[0],
        conv_b[0].reshape(1, SSD_CONV_DIM),
        jnp.pad(dt_bias[0], (0, pad_l)).reshape(1, LANES),
        jnp.pad(a_log[0], (0, pad_l)).reshape(1, LANES),
        jnp.repeat(d_skip[0], SSD_HEAD_DIM).reshape(1, SSD_WIDTH),
        ssd_norm_w[0].reshape(1, SSD_WIDTH),
        w_out[0].astype(BF16),
        ln1_g[0].reshape(1, D_MODEL),
        ln1_b[0].reshape(1, D_MODEL),
        jnp.concatenate([rwh, rwm], axis=1),
        rwh,
        jnp.pad(router_b[0], (0, LANES - N_EXPERTS)).reshape(1, LANES),
        jnp.asarray(tril, BF16), jnp.asarray(trils, BF16), jnp.asarray(e128, BF16),
    ]
    x1, meta, cnt = _mixer_call(x.reshape(T, D_MODEL), consts, batch, seq)

    counts = cnt.reshape(n_tiles, SUBLANES, LANES)[:, 0, :N_EXPERTS].astype(jnp.int32)
    c8 = (counts + RUN_ALIGN - 1) // RUN_ALIGN * RUN_ALIGN
    used = jnp.sum(c8, axis=0)
    region = (used + EXPERT_BLOCK - 1) // EXPERT_BLOCK * EXPERT_BLOCK
    region_end = jnp.cumsum(region)
    region_start = region_end - region
    gstart = region_start[None, :] + jnp.cumsum(c8, axis=0) - c8
    lstart = jnp.cumsum(c8, axis=1) - c8
    n_valid = region_end[-1] // EXPERT_BLOCK
    has = region > 0
    eidx = jnp.arange(N_EXPERTS, dtype=jnp.int32)
    suffix_min = lax.cummin(jnp.where(has, eidx, N_EXPERTS), reverse=True)
    nxt_e = jnp.concatenate([suffix_min[1:], jnp.full((1,), N_EXPERTS, jnp.int32)])
    nxt_e = jnp.where(nxt_e == N_EXPERTS, eidx, nxt_e)
    par_e = (jnp.cumsum(has.astype(jnp.int32)) - 1) % 2
    left = region // EXPERT_BLOCK
    n_of = []
    for nb in CHUNK_BLOCKS:
        n_of.append(left // nb)
        left = left % nb
    it_cnt = sum(n_of)
    it_end = jnp.cumsum(it_cnt)
    n_iter = it_end[-1]
    max_iter = n_blocks // max(CHUNK_BLOCKS) + (len(CHUNK_BLOCKS) - 1) * N_EXPERTS
    it = jnp.minimum(jnp.arange(max_iter, dtype=jnp.int32), n_iter - 1)
    it_e = jnp.minimum(jnp.sum(it_end[None, :] <= it[:, None], axis=1), N_EXPERTS - 1).astype(jnp.int32)
    it_onehot = it_e[:, None] == eidx[None, :]
    lookup = lambda tab: jnp.sum(jnp.where(it_onehot, tab[None, :].astype(jnp.int32), 0), axis=1)
    k = it - lookup(it_end - it_cnt)
    it_sz = jnp.zeros_like(it)
    it_blk = lookup(region_start // EXPERT_BLOCK)
    for idx, nb in enumerate(CHUNK_BLOCKS):
        n_here = lookup(n_of[idx])
        inside = jnp.logical_and(k >= 0, k < n_here)
        it_sz = jnp.where(inside, idx, it_sz)
        it_blk = it_blk + nb * jnp.clip(k, 0, n_here)
        k = k - n_here
    expert_sched = (it_e, it_blk.astype(jnp.int32), it_sz.astype(jnp.int32), lookup(nxt_e).astype(jnp.int32),
                    lookup(par_e).astype(jnp.int32),
                    n_iter.astype(jnp.int32).reshape(1), n_valid.astype(jnp.int32).reshape(1))
    as_units = lambda a: (a // RUN_ALIGN).astype(jnp.int32).reshape(-1)
    units, g8, l8 = as_units(c8), as_units(gstart), as_units(lstart)
    tot = (jnp.sum(c8, axis=1) // RUN_ALIGN).astype(jnp.int32)
    tail8 = as_units(region_start + used)
    tailn = as_units(region - used)
    lu = lstart // RUN_ALIGN
    lst3 = jnp.stack([lu // LST_SPLIT, lu % LST_SPLIT, lu], axis=1).astype(F32)
    lst_rows = jnp.pad(lst3, ((0, 0), (0, SUBLANES - 3), (0, LANES - N_EXPERTS))).reshape(
        n_tiles * SUBLANES, LANES)

    nv = n_valid.astype(jnp.int32).reshape(1)
    xs = _dispatch_call((units, g8, l8, tot, tail8, tailn, nv), x1, meta, lst_rows, cap)
    y_rows = _expert_call(expert_sched, xs, w_gate[0], b_gate[0], w_up[0], b_up[0], w_down[0], b_down[0])
    out = _combine_call((units, g8, l8, tot), y_rows, x1, meta, lst_rows,
                        ln2_g[0].reshape(1, D_MODEL), ln2_b[0].reshape(1, D_MODEL))
    return out.reshape(batch, seq, D_MODEL)
```

```python
import functools

import jax
import jax.numpy as jnp
import numpy as np
from jax import lax
from jax.experimental import pallas as pl
from jax.experimental.pallas import tpu as pltpu

F32 = jnp.float32
BF16 = jnp.bfloat16

D_MODEL = 1024
CHUNK = 64
HG_WIDTH = 512
HG_HEAD_DIM = 128
HG_HEADS = 4
SSD_WIDTH = 512
SSD_HEAD_DIM = 64
SSD_HEADS = 8
SSD_GROUPS = 2
SSD_STATE = 128
SSD_CONV = 4
SSD_CONV_DIM = SSD_WIDTH + 2 * SSD_GROUPS * SSD_STATE
N_EXPERTS = 32
TOP_K = 4
EXPERT_BLOCK = 128
CHUNK_BLOCKS = (4, 2, 1)
SWIGLU_LIMIT = 7.0
SWIGLU_ALPHA = 1.702
DEPTH = 1
DEEPNORM_ALPHA = (2 * DEPTH) ** 0.25
LN_EPS = 1e-5
RMS_EPS = 1e-5

LANES = 128
SUBLANES = 8
SUB_CHUNK = 16
EXP_CAP = 60.0
TILE_ROWS = 256
MIX_TILES = 2
ROUTE_TILES = 2
STEP_ROWS = MIX_TILES * TILE_ROWS
RUN_ALIGN = SUBLANES
SORT_ROWS = 1280
LST_SPLIT = 32
GATE_PARTS = 3
XS_WIDTH = D_MODEL + LANES
VMEM_LIMIT = 56 * 1024 * 1024

OFF_Q, OFF_F, OFF_I, OFF_G = 0, 512, 1024, 1536
OFF_Z, OFF_XBC = 2048, 2560
OFF_XS, OFF_B, OFF_C = 2560, 3072, 3328
OFF_DT = 3584
PROJ_COLS = OFF_DT + LANES
PROJ_SLICE = 256
CONV_SLICE = 256
OUT_SLICE = 256


def _bdot(a, b):
    return jnp.dot(a.astype(BF16), b.astype(BF16), preferred_element_type=F32)


def _bdot_nt(a, b):
    return lax.dot_general(a.astype(BF16), b.astype(BF16), (((1,), (1,)), ((), ())),
                           preferred_element_type=F32)


def _bdot_tn(a, b):
    return lax.dot_general(a.astype(BF16), b.astype(BF16), (((0,), (0,)), ((), ())),
                           preferred_element_type=F32)


def _split3(a):
    hi = a.astype(BF16)
    r1 = a - hi.astype(F32)
    mid = r1.astype(BF16)
    lo = (r1 - mid.astype(F32)).astype(BF16)
    return hi, mid, lo


def _dot01_left(m01, a):
    hi, mid, lo = _split3(a)
    d = functools.partial(jnp.dot, m01, preferred_element_type=F32)
    return d(hi) + d(mid) + d(lo)


def _dot01_right(a, m01):
    hi, mid, lo = _split3(a)
    return (jnp.dot(hi, m01, preferred_element_type=F32) + jnp.dot(mid, m01, preferred_element_type=F32)
            + jnp.dot(lo, m01, preferred_element_type=F32))


def _sigmoid(x):
    return 1.0 / (1.0 + jnp.exp(-x))


def _silu(x):
    return x * _sigmoid(x)


def _softplus(x):
    return jnp.maximum(x, 0.0) + jnp.log(1.0 + jnp.exp(-jnp.abs(x)))


def _mixer_body(x_ref, xnext_ref, win_ref, lbp_ref, hgnw_ref, convw_ref, convb_ref, dtb_ref, alog_ref, dskip_ref,
                ssdnw_ref, wout_ref, ln1g_ref, ln1b_ref, rw1_ref, rwh_ref, rb_ref,
                tril_ref, trils_ref, e128_ref,
                x1_ref, meta_ref, cnt_ref,
                proj_s, xb_s, xn_s, b_s, xpad_s, xdt_s, cse_s, st_s, pt_s, cat_s, ossd_s, mix_s):
    TL = TILE_ROWS
    t = pl.program_id(1)

    @pl.when(t == 0)
    def _():
        xpad_s[0:SUBLANES, :] = jnp.zeros((SUBLANES, SSD_CONV_DIM), F32)
        st_s[...] = jnp.zeros_like(st_s)
        pt_s[...] = jnp.zeros_like(pt_s)

    row64 = lax.broadcasted_iota(jnp.int32, (CHUNK, CHUNK), 0)
    col64 = lax.broadcasted_iota(jnp.int32, (CHUNK, CHUNK), 1)
    causal = row64 >= col64
    gw = SSD_WIDTH // SSD_GROUPS
    hpg = SSD_HEADS // SSD_GROUPS
    lane_head = lax.broadcasted_iota(jnp.int32, (CHUNK, gw), 1) // SSD_HEAD_DIM

    pending = []

    def filler():
        if pending:
            pending.pop(0)()

    def project_slices(lhs_ref, lhs_base, base):
        def one(c0, c1):
            def run():
                proj_s[base:base + TL, c0:c1] = jnp.dot(lhs_ref[lhs_base:lhs_base + TL, :], win_ref[:, c0:c1],
                                                        preferred_element_type=F32)
            return run
        return [one(c0, min(c0 + PROJ_SLICE, PROJ_COLS)) for c0 in range(0, PROJ_COLS, PROJ_SLICE)]

    def hg_front(base):
        a0 = lbp_ref[0:1, :]
        a1 = lbp_ref[1:2, :]
        am = jnp.maximum(a0, a1)
        e0 = jnp.exp(a0 - am)
        e1 = jnp.exp(a1 - am)
        lb = e0 / (e0 + e1)
        f = lb + (1.0 - lb) * _sigmoid(proj_s[base:base + TL, OFF_F:OFF_F + HG_WIDTH])
        b_s[base:base + TL, :] = _dot01_left(tril_ref[...], jnp.log(f))
        proj_s[base:base + TL, OFF_F:OFF_F + HG_WIDTH] = f
        proj_s[base:base + TL, OFF_Q:OFF_Q + HG_WIDTH] = _silu(proj_s[base:base + TL, OFF_Q:OFF_Q + HG_WIDTH])
        units = {}
        for c in range(TL // CHUNK):
            r0 = base + c * CHUNK
            for h in range(HG_HEADS):
                h0 = h * HG_HEAD_DIM
                bc = b_s[r0:r0 + CHUNK, h0:h0 + HG_HEAD_DIM]
                qc = proj_s[r0:r0 + CHUNK, OFF_Q + h0:OFF_Q + h0 + HG_HEAD_DIM]
                kc = 1.0 - proj_s[r0:r0 + CHUNK, OFF_F + h0:OFF_F + h0 + HG_HEAD_DIM]
                vcb = proj_s[r0:r0 + CHUNK, OFF_I + h0:OFF_I + h0 + HG_HEAD_DIM].astype(BF16)
                parts = []
                for i in range(CHUNK // SUB_CHUNK):
                    s0 = i * SUB_CHUNK
                    if i == 0:
                        qi = qc[0:SUB_CHUNK] * jnp.exp(bc[0:SUB_CHUNK])
                        ki = kc * jnp.exp(jnp.minimum(-bc, EXP_CAP))
                    else:
                        ref_i = bc[s0 - 1:s0, :]
                        qi = qc[s0:s0 + SUB_CHUNK] * jnp.exp(bc[s0:s0 + SUB_CHUNK] - ref_i)
                        ki = kc * jnp.exp(jnp.minimum(ref_i - bc, EXP_CAP))
                    parts.append(_bdot_nt(qi, ki))
                b_end = bc[CHUNK - 1:CHUNK, :]
                kdec = kc * jnp.exp(b_end - bc)
                units[c, h] = dict(
                    parts=parts, vcb=vcb, qdec=(qc * jnp.exp(bc)).astype(BF16),
                    local=_bdot_tn(vcb, kdec),
                    decay=jnp.exp(b_end))
                if h % 2 == 1:
                    filler()
        return units

    def hg_back(base, units):
        hgnw = hgnw_ref[...]
        for h in range(HG_HEADS):
            st = st_s[h]
            for c in range(TL // CHUNK):
                u = units[c, h]
                u['state'] = st
                st = st * u['decay'] + u['local']
            st_s[h] = st
        for c in range(TL // CHUNK):
            r0 = base + c * CHUNK
            for h in range(HG_HEADS):
                h0 = h * HG_HEAD_DIM
                u = units[c, h]
                sc = jnp.where(causal, jnp.concatenate(u['parts'], axis=0), 0.0)
                o = _bdot(sc, u['vcb']) + _bdot_nt(u['qdec'], u['state'])
                ms = jnp.mean(o * o, axis=-1, keepdims=True)
                on = o * lax.rsqrt(ms + RMS_EPS) * hgnw
                gc = proj_s[r0:r0 + CHUNK, OFF_G + h0:OFF_G + h0 + HG_HEAD_DIM]
                cat_s[r0:r0 + CHUNK, h0:h0 + HG_HEAD_DIM] = (on * _silu(gc)).astype(BF16)
                filler()

    def ssd_front(base):
        e128 = e128_ref[...]
        dtc = _softplus(proj_s[base:base + TL, OFF_DT:OFF_DT + LANES] + dtb_ref[...])
        a_row = -jnp.exp(alog_ref[...])
        cs_c = _dot01_left(tril_ref[...], dtc * a_row)
        cs_r = jnp.transpose(cs_c)[0:SSD_HEADS, :]
        cse_s[base:base + TL, :] = _dot01_right(cs_c, e128)
        dt_exp = _dot01_right(dtc, e128)

        xpad_s[SUBLANES + base:SUBLANES + base + TL, :] = proj_s[base:base + TL, OFF_XBC:OFF_XBC + SSD_CONV_DIM]
        for c0 in range(0, SSD_CONV_DIM, CONV_SLICE):
            acc = jnp.broadcast_to(convb_ref[:, c0:c0 + CONV_SLICE], (TL, CONV_SLICE))
            for j in range(SSD_CONV):
                off = base + SUBLANES - (SSD_CONV - 1) + j
                acc = acc + convw_ref[j:j + 1, c0:c0 + CONV_SLICE] * xpad_s[off:off + TL, c0:c0 + CONV_SLICE]
            proj_s[base:base + TL, OFF_XBC + c0:OFF_XBC + c0 + CONV_SLICE] = _silu(acc)
            filler()
        xdt_s[base:base + TL, :] = proj_s[base:base + TL, OFF_XS:OFF_XS + SSD_WIDTH] * dt_exp

        units = {}
        for c in range(TL // CHUNK):
            r0 = base + c * CHUNK
            for g in range(SSD_GROUPS):
                g0 = g * gw
                bgb = proj_s[r0:r0 + CHUNK, OFF_B + g * SSD_STATE:OFF_B + (g + 1) * SSD_STATE].astype(BF16)
                cgb = proj_s[r0:r0 + CHUNK, OFF_C + g * SSD_STATE:OFF_C + (g + 1) * SSD_STATE].astype(BF16)
                cse_g = cse_s[r0:r0 + CHUNK, g0:g0 + gw]
                cs_end = cse_s[r0 + CHUNK - 1:r0 + CHUNK, g0:g0 + gw]
                xdt_g = xdt_s[r0:r0 + CHUNK, g0:g0 + gw]
                units[c, g] = dict(
                    cgb=cgb, gm=_bdot_nt(cgb, bgb),
                    local=_bdot_tn(bgb, xdt_g * jnp.exp(cs_end - cse_g)),
                    decay=jnp.exp(cs_end))
                filler()
        return units, cs_r

    def ssd_back(base, units, cs_r):
        for g in range(SSD_GROUPS):
            pt = pt_s[g]
            for c in range(TL // CHUNK):
                u = units[c, g]
                u['state'] = pt
                pt = pt * u['decay'] + u['local']
            pt_s[g] = pt
        for c in range(TL // CHUNK):
            r0 = base + c * CHUNK
            for g in range(SSD_GROUPS):
                g0 = g * gw
                u = units[c, g]
                cse_g = cse_s[r0:r0 + CHUNK, g0:g0 + gw]
                xdt_g = xdt_s[r0:r0 + CHUNK, g0:g0 + gw]
                ydiag = jnp.zeros((CHUNK, gw), F32)
                for hl in range(hpg):
                    hh = g * hpg + hl
                    seg = (cse_g[:, hl * SSD_HEAD_DIM:(hl + 1) * SSD_HEAD_DIM]
                           - cs_r[hh:hh + 1, c * CHUNK:(c + 1) * CHUNK])
                    lm = jnp.where(causal, jnp.exp(jnp.minimum(seg, 0.0)), 0.0)
                    xm = jnp.where(lane_head == hl, xdt_g, 0.0)
                    ydiag = ydiag + _bdot(u['gm'] * lm, xm)
                yoff = _bdot(u['cgb'], u['state']) * jnp.exp(cse_g)
                xs_g = proj_s[r0:r0 + CHUNK, OFF_XS + g0:OFF_XS + g0 + gw]
                ossd_s[r0:r0 + CHUNK, g0:g0 + gw] = ydiag + yoff + xs_g * dskip_ref[:, g0:g0 + gw]
                filler()

    def ssd_gate_norm(tile):
        base = tile * TL
        y = ossd_s[base:base + TL, :] * _silu(proj_s[base:base + TL, OFF_Z:OFF_Z + SSD_WIDTH])
        for g in range(SSD_GROUPS):
            yg = y[:, g * gw:(g + 1) * gw]
            ms = jnp.mean(yg * yg, axis=-1, keepdims=True)
            yn = yg * lax.rsqrt(ms + RMS_EPS) * ssdnw_ref[:, g * gw:(g + 1) * gw]
            cat_s[base:base + TL, HG_WIDTH + g * gw:HG_WIDTH + (g + 1) * gw] = yn.astype(BF16)

    def out_slice(tile, c0):
        base = tile * TL
        mix_s[base:base + TL, c0:c0 + OUT_SLICE] = jnp.dot(
            cat_s[base:base + TL, :], wout_ref[:, c0:c0 + OUT_SLICE], preferred_element_type=F32)

    def norm_route(tile):
        base = tile * TL
        hres = DEEPNORM_ALPHA * x_ref[base:base + TL, :] + mix_s[base:base + TL, :]
        mu = jnp.mean(hres, axis=-1, keepdims=True)
        hc = hres - mu
        var = jnp.mean(hc * hc, axis=-1, keepdims=True)
        x1 = hc * lax.rsqrt(var + LN_EPS) * ln1g_ref[...] + ln1b_ref[...]
        x1_ref[base:base + TL, :] = x1
        filler()

        xh = x1.astype(BF16)
        xm_ = (x1 - xh.astype(F32)).astype(BF16)
        t1 = jnp.dot(xh, rw1_ref[...], preferred_element_type=F32)
        logits = (t1[:, 0:LANES] + t1[:, LANES:2 * LANES]
                  + jnp.dot(xm_, rwh_ref[...], preferred_element_type=F32) + rb_ref[...])
        filler()
        lane = lax.broadcasted_iota(jnp.int32, (TL, LANES), 1)
        lane_f = lane.astype(F32)
        neg = jnp.float32(-jnp.inf)
        work = jnp.where(lane < N_EXPERTS, logits, neg)
        onehots, vals, idxs = [], [], []
        for j in range(TOP_K):
            m = jnp.max(work, axis=-1, keepdims=True)
            filler()
            idx = jnp.min(jnp.where(work == m, lane_f, float(LANES)), axis=-1, keepdims=True)
            oh = lane_f == idx
            onehots.append(oh)
            vals.append(m)
            idxs.append(idx)
            work = jnp.where(oh, neg, work)
            filler()
        es = [jnp.exp(v - vals[0]) for v in vals]
        den = es[0] + es[1] + es[2] + es[3]
        gates = [e / den for e in es]
        sel = jnp.zeros((TL, LANES), F32)
        for oh in onehots:
            sel = jnp.where(oh, 1.0, sel)
        rankmat = jnp.dot(trils_ref[...], sel.astype(BF16), preferred_element_type=F32)
        filler()
        cnt_ref[tile * SUBLANES:(tile + 1) * SUBLANES, :] = jnp.broadcast_to(
            jnp.sum(sel, axis=0, keepdims=True), (SUBLANES, LANES))
        meta = jnp.zeros((TL, LANES), F32)
        for j in range(TOP_K):
            rank_j = jnp.sum(jnp.where(onehots[j], rankmat, 0.0), axis=-1, keepdims=True)
            meta = jnp.where(lane == j, idxs[j], meta)
            meta = jnp.where(lane == TOP_K + j, rank_j, meta)
            meta = jnp.where(lane == 2 * TOP_K + j, gates[j], meta)
        meta_ref[base:base + TL, :] = meta

    def post_thunks(tile):
        return ([functools.partial(ssd_gate_norm, tile)]
                + [functools.partial(out_slice, tile, c0) for c0 in range(0, D_MODEL, OUT_SLICE)]
                + [functools.partial(norm_route, tile)])

    xb_s[...] = x_ref[...].astype(BF16)
    xn_s[...] = xnext_ref[...].astype(BF16)

    @pl.when(jnp.logical_and(pl.program_id(0) == 0, t == 0))
    def _():
        for run in project_slices(xb_s, 0, 0):
            run()

    for tile in range(MIX_TILES):
        base = tile * TL
        if tile + 1 < MIX_TILES:
            pending.extend(project_slices(xb_s, base + TL, base + TL))
        hg_units = hg_front(base)
        ssd_units, cs_r = ssd_front(base)
        while pending:
            filler()
        hg_back(base, hg_units)
        ssd_back(base, ssd_units, cs_r)
        if tile + 1 < MIX_TILES:
            pending.extend(post_thunks(tile))
    tail = post_thunks(MIX_TILES - 1)
    slices = project_slices(xn_s, 0, 0)
    tail[0]()
    for run in tail[1:-1]:
        run()
        slices.pop(0)()
    pending.extend(slices)
    tail[-1]()
    while pending:
        filler()
    xpad_s[0:SUBLANES, :] = xpad_s[STEP_ROWS:STEP_ROWS + SUBLANES, :]


def _full(shape):
    nd = len(shape)
    return pl.BlockSpec(shape, lambda *_: (0,) * nd)


def _mixer_call(x2d, consts, batch, seq):
    SR = STEP_ROWS
    nt = seq // SR
    T = batch * seq
    last_tile = T // TILE_ROWS - 1
    in_specs = [pl.BlockSpec((SR, D_MODEL), lambda b, t: (b * nt + t, 0)),
                pl.BlockSpec((TILE_ROWS, D_MODEL),
                             lambda b, t: (jnp.minimum((b * nt + t + 1) * MIX_TILES, last_tile), 0))]
    in_specs += [_full(c.shape) for c in consts]
    out_shape = (jax.ShapeDtypeStruct((T, D_MODEL), F32),
                 jax.ShapeDtypeStruct((T, LANES), F32),
                 jax.ShapeDtypeStruct((T // TILE_ROWS * SUBLANES, LANES), F32))
    out_specs = (pl.BlockSpec((SR, D_MODEL), lambda b, t: (b * nt + t, 0)),
                 pl.BlockSpec((SR, LANES), lambda b, t: (b * nt + t, 0)),
                 pl.BlockSpec((MIX_TILES * SUBLANES, LANES), lambda b, t: (b * nt + t, 0)))
    scratch = [
        pltpu.VMEM((SR, PROJ_COLS), F32),
        pltpu.VMEM((SR, D_MODEL), BF16),
        pltpu.VMEM((TILE_ROWS, D_MODEL), BF16),
        pltpu.VMEM((SR, HG_WIDTH), F32),
        pltpu.VMEM((SR + 2 * SUBLANES, SSD_CONV_DIM), F32),
        pltpu.VMEM((SR, SSD_WIDTH), F32),
        pltpu.VMEM((SR, SSD_WIDTH), F32),
        pltpu.VMEM((HG_HEADS, HG_HEAD_DIM, HG_HEAD_DIM), F32),
        pltpu.VMEM((SSD_GROUPS, SSD_STATE, SSD_WIDTH // SSD_GROUPS), F32),
        pltpu.VMEM((SR, D_MODEL), BF16),
        pltpu.VMEM((SR, SSD_WIDTH), F32),
        pltpu.VMEM((SR, D_MODEL), F32),
    ]
    return pl.pallas_call(
        _mixer_body,
        grid=(batch, nt),
        in_specs=in_specs,
        out_specs=out_specs,
        out_shape=out_shape,
        scratch_shapes=scratch,
        compiler_params=pltpu.CompilerParams(
            dimension_semantics=("arbitrary", "arbitrary"), vmem_limit_bytes=VMEM_LIMIT),
        name="mixer",
    )(x2d, x2d, *consts)


def _for_each_run(units_ref, tile, fn):
    def per_expert(e, carry):
        k = tile * N_EXPERTS + e
        n = units_ref[k]

        @pl.when(n > 0)
        def _():
            fn(k, n)
        return carry
    lax.fori_loop(0, N_EXPERTS, per_expert, 0)


def _rows(unit, n_units=1):
    start = unit * RUN_ALIGN
    if RUN_ALIGN > 1:
        start = pl.multiple_of(start, RUN_ALIGN)
    return pl.ds(start, n_units * RUN_ALIGN)


def _dispatch_body(units_ref, g8_ref, l8_ref, tot_ref, tail8_ref, tailn_ref, nv_ref,
                   x1_ref, meta_ref, lst_ref, xs_hbm, sorted_s, zero_s, sems, zsem, bsem):
    step = pl.program_id(0)
    n = pl.num_programs(0)
    TD, S = TILE_ROWS, SORT_ROWS
    n_blocks = xs_hbm.shape[0] // EXPERT_BLOCK

    def unused_block_copy(m):
        rows = pl.ds(pl.multiple_of(m * EXPERT_BLOCK, EXPERT_BLOCK), EXPERT_BLOCK)
        return pltpu.make_async_copy(zero_s, xs_hbm.at[rows, :], bsem)

    def run_copy(s, l_unit, g_unit, n_units):
        return pltpu.make_async_copy(sorted_s.at[s, _rows(l_unit, n_units), :],
                                     xs_hbm.at[_rows(g_unit, n_units), :], sems.at[s])

    def wait_units(s, count):
        run_copy(s, 0, 0, count).wait()

    @pl.when(step == 0)
    def _():
        zero_s[...] = jnp.zeros_like(zero_s)

        def tail_copy(e):
            n_units = tailn_ref[e]
            return pltpu.make_async_copy(zero_s.at[pl.ds(0, n_units * RUN_ALIGN), :],
                                         xs_hbm.at[_rows(tail8_ref[e], n_units), :], zsem)

        def start_unused(m, carry):
            unused_block_copy(m).start()
            return carry
        lax.fori_loop(nv_ref[0], n_blocks, start_unused, 0)

        def start_e(e, carry):
            @pl.when(tailn_ref[e] > 0)
            def _():
                tail_copy(e).start()
            return carry
        lax.fori_loop(0, N_EXPERTS, start_e, 0)

        def wait_e(e, carry):
            @pl.when(tailn_ref[e] > 0)
            def _():
                tail_copy(e).wait()
            return carry
        lax.fori_loop(0, N_EXPERTS, wait_e, 0)

    lane_f = lax.broadcasted_iota(jnp.int32, (TD, LANES), 1).astype(F32)
    ones8 = jnp.ones((SUBLANES, LANES), BF16)
    r_iota = lax.broadcasted_iota(jnp.int32, (S, TD), 0).astype(F32)
    nt_dims = (((1,), (1,)), ((), ()))
    for slot in range(ROUTE_TILES):
        _dispatch_tile(step * ROUTE_TILES + slot, slot, step, units_ref, g8_ref, l8_ref, tot_ref,
                       x1_ref, meta_ref, lst_ref, sorted_s, run_copy, wait_units,
                       lane_f, ones8, r_iota, nt_dims)

    @pl.when(step == n - 1)
    def _():
        for slot in range(ROUTE_TILES):
            wait_units(slot, tot_ref[step * ROUTE_TILES + slot])

        def wait_unused(m, carry):
            unused_block_copy(m).wait()
            return carry
        lax.fori_loop(nv_ref[0], n_blocks, wait_unused, 0)


def _dispatch_tile(tau, slot, step, units_ref, g8_ref, l8_ref, tot_ref, x1_ref, meta_ref, lst_ref, sorted_s,
                   run_copy, wait_units, lane_f, ones8, r_iota, nt_dims):
    TD = TILE_ROWS
    rows = slice(slot * TD, (slot + 1) * TD)

    @pl.when(step >= 1)
    def _():
        wait_units(slot, tot_ref[jnp.maximum(tau - ROUTE_TILES, 0)])

    meta = meta_ref[rows, :]
    u8 = lst_ref[slot * SUBLANES:(slot + 1) * SUBLANES, :].astype(BF16)
    u_row = lst_ref[slot * SUBLANES + 2:slot * SUBLANES + 3, :]
    gh = meta.astype(BF16).astype(F32)
    g1 = meta - gh
    gm = g1.astype(BF16).astype(F32)
    gl = g1 - gm
    lane_j = jnp.floor((lane_f + 0.5) * (1.0 / GATE_PARTS))
    lane_p = lane_f - GATE_PARTS * lane_j
    aux_g = jnp.zeros((TD, LANES), F32)
    aux_h = jnp.zeros((TD, LANES), F32)
    aux_l = jnp.zeros((TD, LANES), F32)
    conds = []
    for j in range(TOP_K):
        oh = lane_f == meta[:, j:j + 1]
        rank_col = meta[:, TOP_K + j:TOP_K + j + 1]
        ohb = jnp.where(oh, 1.0, 0.0).astype(BF16)
        rkb = jnp.where(oh, rank_col, 0.0).astype(BF16)
        m1 = lax.dot_general(u8, ohb, nt_dims, preferred_element_type=F32)
        m2 = lax.dot_general(ones8, rkb, nt_dims, preferred_element_type=F32)
        lpos = RUN_ALIGN * (LST_SPLIT * m1[0:1, :] + m1[1:2, :]) + m2[0:1, :]
        conds.append(r_iota == lpos)
        lpos_col = RUN_ALIGN * jnp.sum(jnp.where(oh, u_row, 0.0), axis=-1, keepdims=True) + rank_col
        hi_col = jnp.floor(lpos_col * (1.0 / LST_SPLIT))
        gc = 2 * TOP_K + j
        g_j = jnp.where(lane_p == 0, gh[:, gc:gc + 1], jnp.where(lane_p == 1, gm[:, gc:gc + 1], gl[:, gc:gc + 1]))
        aux_g = jnp.where(lane_j == j, g_j, aux_g)
        aux_h = jnp.where(lane_j == j, hi_col, aux_h)
        aux_l = jnp.where(lane_j == j, lpos_col - LST_SPLIT * hi_col, aux_l)
    perm = jnp.where(conds[0], 1.0, jnp.where(conds[1], 1.0, jnp.where(conds[2], 1.0,
                     jnp.where(conds[3], 1.0, 0.0)))).astype(BF16)
    sorted_x = jnp.dot(perm, x1_ref[rows, :].astype(BF16), preferred_element_type=F32)
    aux = jnp.concatenate([aux_g, aux_h, aux_l], axis=1).astype(BF16)
    sorted_aux = jnp.dot(perm, aux, preferred_element_type=F32)
    own_row = lax.broadcasted_iota(jnp.int32, (sorted_aux.shape[0], LANES), 0).astype(F32)
    match = LST_SPLIT * sorted_aux[:, LANES:2 * LANES] + sorted_aux[:, 2 * LANES:3 * LANES] == own_row
    sorted_g = jnp.where(match, sorted_aux[:, 0:LANES], 0.0)
    sorted_s[slot, :, 0:D_MODEL] = sorted_x
    sorted_s[slot, :, D_MODEL:XS_WIDTH] = sorted_g

    _for_each_run(units_ref, tau, lambda k, cnt: run_copy(slot, l8_ref[k], g8_ref[k], cnt).start())


def _dispatch_call(sched, x1, meta, lst_rows, cap):
    T = x1.shape[0]
    RR = ROUTE_TILES * TILE_ROWS
    grid_spec = pltpu.PrefetchScalarGridSpec(
        num_scalar_prefetch=7,
        grid=(T // RR,),
        in_specs=[
            pl.BlockSpec((RR, D_MODEL), lambda i, *_: (i, 0)),
            pl.BlockSpec((RR, LANES), lambda i, *_: (i, 0)),
            pl.BlockSpec((ROUTE_TILES * SUBLANES, LANES), lambda i, *_: (i, 0)),
        ],
        out_specs=pl.BlockSpec(memory_space=pl.ANY),
        scratch_shapes=[
            pltpu.VMEM((2, SORT_ROWS, XS_WIDTH), F32),
            pltpu.VMEM((EXPERT_BLOCK, XS_WIDTH), F32),
            pltpu.SemaphoreType.DMA((2,)),
            pltpu.SemaphoreType.DMA(()),
            pltpu.SemaphoreType.DMA(()),
        ],
    )
    return pl.pallas_call(
        _dispatch_body,
        grid_spec=grid_spec,
        out_shape=jax.ShapeDtypeStruct((cap, XS_WIDTH), F32),
        compiler_params=pltpu.CompilerParams(
            dimension_semantics=("arbitrary",), vmem_limit_bytes=VMEM_LIMIT),
        name="dispatch",
    )(*sched, x1, meta, lst_rows)


def _expert_body(ite_ref, itb_ref, sz_ref, nxt_ref, par_ref, nit_ref, nv_ref,
                 xs_hbm, wg_hbm, bg_ref, wu_hbm, bu_ref, wd_hbm, bd_ref,
                 y_hbm, xbuf, ybuf, zbuf, wbuf, xsem, ysem, zsem, wsem, wg_s, wu_s, wd_s):
    n_iter = nit_ref[0]
    n_blocks = xs_hbm.shape[0] // EXPERT_BLOCK
    sizes = tuple(n * EXPERT_BLOCK for n in CHUNK_BLOCKS)

    def rows_of(i, rows):
        return pl.ds(pl.multiple_of(itb_ref[i] * EXPERT_BLOCK, EXPERT_BLOCK), rows)

    def x_copy(i, s, rows):
        return pltpu.make_async_copy(xs_hbm.at[rows_of(i, rows), :], xbuf.at[s, pl.ds(0, rows), :], xsem.at[s])

    def y_copy(i, s, rows):
        return pltpu.make_async_copy(ybuf.at[s, pl.ds(0, rows), :], y_hbm.at[rows_of(i, rows), :], ysem.at[s])

    def by_size(i, fn):
        for k, rows in enumerate(sizes):
            @pl.when(sz_ref[i] == k)
            def _():
                fn(rows)

    def zero_copy(m):
        rows = pl.ds(pl.multiple_of(m * EXPERT_BLOCK, EXPERT_BLOCK), EXPERT_BLOCK)
        return pltpu.make_async_copy(zbuf, y_hbm.at[rows, :], zsem)

    def weight_copies(e, s):
        return [pltpu.make_async_copy(w.at[e], wbuf.at[s, k], wsem.at[s])
                for k, w in enumerate((wg_hbm, wu_hbm, wd_hbm))]

    for c in weight_copies(ite_ref[0], par_ref[0]):
        c.start()
    by_size(0, lambda rows: x_copy(0, 0, rows).start())

    zbuf[...] = jnp.zeros_like(zbuf)

    def start_zero(m, carry):
        zero_copy(m).start()
        return carry
    lax.fori_loop(nv_ref[0], n_blocks, start_zero, 0)

    def ffn(s, e, rows):
        xb = xbuf[s, 0:rows, 0:D_MODEL].astype(BF16)
        gate = jnp.sum(xbuf[s, 0:rows, D_MODEL:XS_WIDTH], axis=-1, keepdims=True)
        hg = jnp.minimum(jnp.dot(xb, wg_s[...], preferred_element_type=F32) + bg_ref[e], SWIGLU_LIMIT)
        hu = jnp.clip(jnp.dot(xb, wu_s[...], preferred_element_type=F32) + bu_ref[e],
                      -SWIGLU_LIMIT, SWIGLU_LIMIT)
        hact = (hu + 1.0) * (hg * _sigmoid(SWIGLU_ALPHA * hg))
        y = jnp.dot(hact.astype(BF16), wd_s[...], preferred_element_type=F32) + bd_ref[e]
        ybuf[s, 0:rows, :] = y * gate

    def body(i, carry):
        s = lax.rem(i, 2)
        e = ite_ref[i]
        by_size(i, lambda rows: x_copy(i, s, rows).wait())

        @pl.when(i + 1 < n_iter)
        def _():
            by_size(i + 1, lambda rows: x_copy(i + 1, 1 - s, rows).start())

        @pl.when(jnp.logical_or(i == 0, e != ite_ref[jnp.maximum(i - 1, 0)]))
        def _():
            ws = par_ref[i]
            for c in weight_copies(e, ws):
                c.wait()

            @pl.when(nxt_ref[i] != e)
            def _():
                for c in weight_copies(nxt_ref[i], 1 - ws):
                    c.start()
            wg_s[...] = wbuf[ws, 0].astype(BF16)
            wu_s[...] = wbuf[ws, 1].astype(BF16)
            wd_s[...] = wbuf[ws, 2].astype(BF16)

        @pl.when(i >= 2)
        def _():
            by_size(i - 2, lambda rows: y_copy(i - 2, s, rows).wait())

        def chunk(rows):
            ffn(s, e, rows)
            y_copy(i, s, rows).start()
        by_size(i, chunk)
        return carry
    lax.fori_loop(0, n_iter, body, 0)

    @pl.when(n_iter >= 2)
    def _():
        by_size(n_iter - 2, lambda rows: y_copy(n_iter - 2, lax.rem(n_iter, 2), rows).wait())
    by_size(n_iter - 1, lambda rows: y_copy(n_iter - 1, lax.rem(n_iter - 1, 2), rows).wait())

    def wait_zero(m, carry):
        zero_copy(m).wait()
        return carry
    lax.fori_loop(nv_ref[0], n_blocks, wait_zero, 0)


def _expert_call(sched, xs, w_gate, b_gate, w_up, b_up, w_down, b_down):
    n_blocks = xs.shape[0] // EXPERT_BLOCK
    anyspec = pl.BlockSpec(memory_space=pl.ANY)
    bspec = pl.BlockSpec((N_EXPERTS, 1, D_MODEL), lambda i, *_: (0, 0, 0))
    grid_spec = pltpu.PrefetchScalarGridSpec(
        num_scalar_prefetch=len(sched),
        grid=(1,),
        in_specs=[anyspec, anyspec, bspec, anyspec, bspec, anyspec, bspec],
        out_specs=anyspec,
        scratch_shapes=[
            pltpu.VMEM((2, max(CHUNK_BLOCKS) * EXPERT_BLOCK, XS_WIDTH), F32),
            pltpu.VMEM((2, max(CHUNK_BLOCKS) * EXPERT_BLOCK, D_MODEL), F32),
            pltpu.VMEM((EXPERT_BLOCK, D_MODEL), F32),
            pltpu.VMEM((2, 3, D_MODEL, D_MODEL), F32),
            pltpu.SemaphoreType.DMA((2,)),
            pltpu.SemaphoreType.DMA((2,)),
            pltpu.SemaphoreType.DMA(()),
            pltpu.SemaphoreType.DMA((2,)),
            pltpu.VMEM((D_MODEL, D_MODEL), BF16),
            pltpu.VMEM((D_MODEL, D_MODEL), BF16),
            pltpu.VMEM((D_MODEL, D_MODEL), BF16),
        ],
    )
    return pl.pallas_call(
        _expert_body,
        grid_spec=grid_spec,
        out_shape=jax.ShapeDtypeStruct((n_blocks * EXPERT_BLOCK, D_MODEL), F32),
        compiler_params=pltpu.CompilerParams(
            dimension_semantics=("arbitrary",), vmem_limit_bytes=VMEM_LIMIT),
        name="experts",
    )(*sched, xs, w_gate, b_gate.reshape(N_EXPERTS, 1, D_MODEL),
      w_up, b_up.reshape(N_EXPERTS, 1, D_MODEL), w_down, b_down.reshape(N_EXPERTS, 1, D_MODEL))


def _combine_body(units_ref, g8_ref, l8_ref, tot_ref,
                  y_hbm, x1_ref, meta_ref, lst_ref, g_ref, b_ref, out_ref, ys_s, sems):
    step = pl.program_id(0)
    n_tiles = pl.num_programs(0) * ROUTE_TILES
    TD, S = TILE_ROWS, SORT_ROWS

    def run_copy(s, g_unit, l_unit, n_units):
        return pltpu.make_async_copy(y_hbm.at[_rows(g_unit, n_units), :],
                                     ys_s.at[s, _rows(l_unit, n_units), :], sems.at[s])

    def fetch(tile, s):
        _for_each_run(units_ref, tile, lambda k, cnt: run_copy(s, g8_ref[k], l8_ref[k], cnt).start())

    @pl.when(step == 0)
    def _():
        ys_s[...] = jnp.zeros_like(ys_s)
        fetch(0, 0)

    lane_f = lax.broadcasted_iota(jnp.int32, (TD, LANES), 1).astype(F32)
    s_iota = lax.broadcasted_iota(jnp.int32, (TD, S), 1).astype(F32)
    for h in range(ROUTE_TILES):
        tau = step * ROUTE_TILES + h
        slot = h % 2
        rows = slice(h * TD, (h + 1) * TD)

        @pl.when(tau + 1 < n_tiles)
        def _():
            fetch(tau + 1, 1 - slot)

        run_copy(slot, 0, 0, tot_ref[tau]).wait()

        meta = meta_ref[rows, :]
        u_row = lst_ref[h * SUBLANES + 2:h * SUBLANES + 3, :]
        conds = []
        for j in range(TOP_K):
            oh = lane_f == meta[:, j:j + 1]
            start8 = jnp.sum(jnp.where(oh, u_row, 0.0), axis=-1, keepdims=True)
            lpos = RUN_ALIGN * start8 + meta[:, TOP_K + j:TOP_K + j + 1]
            conds.append(s_iota == lpos)
        perm = jnp.where(conds[0], 1.0, jnp.where(conds[1], 1.0, jnp.where(conds[2], 1.0,
                         jnp.where(conds[3], 1.0, 0.0)))).astype(BF16)
        ys = ys_s[slot]
        yh = ys.astype(BF16)
        yl = (ys - yh.astype(F32)).astype(BF16)
        ffn = (jnp.dot(perm, yh, preferred_element_type=F32) + jnp.dot(perm, yl, preferred_element_type=F32))
        acc = DEEPNORM_ALPHA * x1_ref[rows, :] + ffn
        mu = jnp.mean(acc, axis=-1, keepdims=True)
        hc = acc - mu
        var = jnp.mean(hc * hc, axis=-1, keepdims=True)
        out_ref[rows, :] = hc * lax.rsqrt(var + LN_EPS) * g_ref[...] + b_ref[...]


def _combine_call(sched, y_rows, x1, meta, lst_rows, ln2_g, ln2_b):
    T = x1.shape[0]
    RR = ROUTE_TILES * TILE_ROWS
    grid_spec = pltpu.PrefetchScalarGridSpec(
        num_scalar_prefetch=4,
        grid=(T // RR,),
        in_specs=[
            pl.BlockSpec(memory_space=pl.ANY),
            pl.BlockSpec((RR, D_MODEL), lambda i, *_: (i, 0)),
            pl.BlockSpec((RR, LANES), lambda i, *_: (i, 0)),
            pl.BlockSpec((ROUTE_TILES * SUBLANES, LANES), lambda i, *_: (i, 0)),
            pl.BlockSpec((1, D_MODEL), lambda i, *_: (0, 0)),
            pl.BlockSpec((1, D_MODEL), lambda i, *_: (0, 0)),
        ],
        out_specs=pl.BlockSpec((RR, D_MODEL), lambda i, *_: (i, 0)),
        scratch_shapes=[pltpu.VMEM((2, SORT_ROWS, D_MODEL), F32), pltpu.SemaphoreType.DMA((2,))],
    )
    return pl.pallas_call(
        _combine_body,
        grid_spec=grid_spec,
        out_shape=jax.ShapeDtypeStruct((T, D_MODEL), F32),
        compiler_params=pltpu.CompilerParams(
            dimension_semantics=("arbitrary",), vmem_limit_bytes=VMEM_LIMIT),
        name="combine",
    )(*sched, y_rows, x1, meta, lst_rows, ln2_g, ln2_b)


def _np_consts():
    TL = TILE_ROWS
    r = np.arange(TL)
    same = (r[:, None] // CHUNK) == (r[None, :] // CHUNK)
    tril = (same & (r[None, :] <= r[:, None])).astype(np.float32)
    trils = (r[None, :] < r[:, None]).astype(np.float32)
    e128 = np.zeros((LANES, SSD_WIDTH), np.float32)
    for h in range(SSD_HEADS):
        e128[h, h * SSD_HEAD_DIM:(h + 1) * SSD_HEAD_DIM] = 1.0
    return tril, trils, e128


def kernel(x, w_in, hg_lower_bound, hg_norm_w, conv_w, conv_b, dt_bias, a_log, d_skip, ssd_norm_w, w_out,
           ln1_g, ln1_b, router_w, router_b, w_gate, b_gate, w_up, b_up, w_down, b_down, ln2_g, ln2_b):
    batch, seq, d = x.shape
    assert d == D_MODEL and seq % STEP_ROWS == 0 and w_in.shape[0] == DEPTH
    assert ROUTE_TILES == 2 and (batch * seq) % (ROUTE_TILES * TILE_ROWS) == 0
    T = batch * seq
    n_tiles = T // TILE_ROWS
    max_rows = T * TOP_K + n_tiles * N_EXPERTS * (RUN_ALIGN - 1)
    n_blocks = -(-max_rows // EXPERT_BLOCK) + N_EXPERTS
    cap = n_blocks * EXPERT_BLOCK
    assert SORT_ROWS >= TILE_ROWS * TOP_K + N_EXPERTS * (RUN_ALIGN - 1)

    tril, trils, e128 = _np_consts()
    w = w_in[0]
    pad_l = LANES - SSD_HEADS
    rw = jnp.pad(router_w[0], ((0, 0), (0, LANES - N_EXPERTS)))
    rwh = rw.astype(BF16)
    rwm = (rw - rwh.astype(F32)).astype(BF16)
    consts = [
        jnp.pad(w.astype(BF16), ((0, 0), (0, pad_l))),
        hg_lower_bound,
        hg_norm_w[0].reshape(1, HG_HEAD_DIM),
        conv_w[0],
        conv_b[0].reshape(1, SSD_CONV_DIM),
        jnp.pad(dt_bias[0], (0, pad_l)).reshape(1, LANES),
        jnp.pad(a_log[0], (0, pad_l)).reshape(1, LANES),
        jnp.repeat(d_skip[0], SSD_HEAD_DIM).reshape(1, SSD_WIDTH),
        ssd_norm_w[0].reshape(1, SSD_WIDTH),
        w_out[0].astype(BF16),
        ln1_g[0].reshape(1, D_MODEL),
        ln1_b[0].reshape(1, D_MODEL),
        jnp.concatenate([rwh, rwm], axis=1),
        rwh,
        jnp.pad(router_b[0], (0, LANES - N_EXPERTS)).reshape(1, LANES),
        jnp.asarray(tril, BF16), jnp.asarray(trils, BF16), jnp.asarray(e128, BF16),
    ]
    x1, meta, cnt = _mixer_call(x.reshape(T, D_MODEL), consts, batch, seq)

    counts = cnt.reshape(n_tiles, SUBLANES, LANES)[:, 0, :N_EXPERTS].astype(jnp.int32)
    c8 = (counts + RUN_ALIGN - 1) // RUN_ALIGN * RUN_ALIGN
    used = jnp.sum(c8, axis=0)
    region = (used + EXPERT_BLOCK - 1) // EXPERT_BLOCK * EXPERT_BLOCK
    region_end = jnp.cumsum(region)
    region_start = region_end - region
    gstart = region_start[None, :] + jnp.cumsum(c8, axis=0) - c8
    lstart = jnp.cumsum(c8, axis=1) - c8
    n_valid = region_end[-1] // EXPERT_BLOCK
    has = region > 0
    eidx = jnp.arange(N_EXPERTS, dtype=jnp.int32)
    suffix_min = lax.cummin(jnp.where(has, eidx, N_EXPERTS), reverse=True)
    nxt_e = jnp.concatenate([suffix_min[1:], jnp.full((1,), N_EXPERTS, jnp.int32)])
    nxt_e = jnp.where(nxt_e == N_EXPERTS, eidx, nxt_e)
    par_e = (jnp.cumsum(has.astype(jnp.int32)) - 1) % 2
    left = region // EXPERT_BLOCK
    n_of = []
    for nb in CHUNK_BLOCKS:
        n_of.append(left // nb)
        left = left % nb
    it_cnt = sum(n_of)
    it_end = jnp.cumsum(it_cnt)
    n_iter = it_end[-1]
    max_iter = n_blocks // max(CHUNK_BLOCKS) + (len(CHUNK_BLOCKS) - 1) * N_EXPERTS
    it = jnp.minimum(jnp.arange(max_iter, dtype=jnp.int32), n_iter - 1)
    it_e = jnp.minimum(jnp.sum(it_end[None, :] <= it[:, None], axis=1), N_EXPERTS - 1).astype(jnp.int32)
    it_onehot = it_e[:, None] == eidx[None, :]
    lookup = lambda tab: jnp.sum(jnp.where(it_onehot, tab[None, :].astype(jnp.int32), 0), axis=1)
    k = it - lookup(it_end - it_cnt)
    it_sz = jnp.zeros_like(it)
    it_blk = lookup(region_start // EXPERT_BLOCK)
    for idx, nb in enumerate(CHUNK_BLOCKS):
        n_here = lookup(n_of[idx])
        inside = jnp.logical_and(k >= 0, k < n_here)
        it_sz = jnp.where(inside, idx, it_sz)
        it_blk = it_blk + nb * jnp.clip(k, 0, n_here)
        k = k - n_here
    expert_sched = (it_e, it_blk.astype(jnp.int32), it_sz.astype(jnp.int32), lookup(nxt_e).astype(jnp.int32),
                    lookup(par_e).astype(jnp.int32),
                    n_iter.astype(jnp.int32).reshape(1), n_valid.astype(jnp.int32).reshape(1))
    as_units = lambda a: (a // RUN_ALIGN).astype(jnp.int32).reshape(-1)
    units, g8, l8 = as_units(c8), as_units(gstart), as_units(lstart)
    tot = (jnp.sum(c8, axis=1) // RUN_ALIGN).astype(jnp.int32)
    tail8 = as_units(region_start + used)
    tailn = as_units(region - used)
    lu = lstart // RUN_ALIGN
    lst3 = jnp.stack([lu // LST_SPLIT, lu % LST_SPLIT, lu], axis=1).astype(F32)
    lst_rows = jnp.pad(lst3, ((0, 0), (0, SUBLANES - 3), (0, LANES - N_EXPERTS))).reshape(
        n_tiles * SUBLANES, LANES)

    nv = n_valid.astype(jnp.int32).reshape(1)
    xs = _dispatch_call((units, g8, l8, tot, tail8, tailn, nv), x1, meta, lst_rows, cap)
    y_rows = _expert_call(expert_sched, xs, w_gate[0], b_gate[0], w_up[0], b_up[0], w_down[0], b_down[0])
    out = _combine_call((units, g8, l8, tot), y_rows, x1, meta, lst_rows,
                        ln2_g[0].reshape(1, D_MODEL), ln2_b[0].reshape(1, D_MODEL))
    return out.reshape(batch, seq, D_MODEL)
```

```python
import functools

import jax
import jax.numpy as jnp
import numpy as np
from jax import lax
from jax.experimental import pallas as pl
from jax.experimental.pallas import tpu as pltpu

F32 = jnp.float32
BF16 = jnp.bfloat16

D_MODEL = 1024
CHUNK = 64
HG_WIDTH = 512
HG_HEAD_DIM = 128
HG_HEADS = 4
SSD_WIDTH = 512
SSD_HEAD_DIM = 64
SSD_HEADS = 8
SSD_GROUPS = 2
SSD_STATE = 128
SSD_CONV = 4
SSD_CONV_DIM = SSD_WIDTH + 2 * SSD_GROUPS * SSD_STATE
N_EXPERTS = 32
TOP_K = 4
EXPERT_BLOCK = 128
CHUNK_BLOCKS = (6, 4, 2, 1)
SWIGLU_LIMIT = 7.0
SWIGLU_ALPHA = 1.702
DEPTH = 1
DEEPNORM_ALPHA = (2 * DEPTH) ** 0.25
LN_EPS = 1e-5
RMS_EPS = 1e-5

LANES = 128
SUBLANES = 8
SUB_CHUNK = 16
EXP_CAP = 60.0
TILE_ROWS = 256
MIX_TILES = 2
ROUTE_TILES = 2
STEP_ROWS = MIX_TILES * TILE_ROWS
RUN_ALIGN = SUBLANES
SORT_ROWS = 1280
LST_SPLIT = 32
GATE_PARTS = 3
XS_WIDTH = D_MODEL + LANES
VMEM_LIMIT = 56 * 1024 * 1024

OFF_Q, OFF_F, OFF_I, OFF_G = 0, 512, 1024, 1536
OFF_Z, OFF_XBC = 2048, 2560
OFF_XS, OFF_B, OFF_C = 2560, 3072, 3328
OFF_DT = 3584
PROJ_COLS = OFF_DT + LANES
PROJ_SLICE = 256
CONV_SLICE = 256
OUT_SLICE = 256


def _bdot(a, b):
    return jnp.dot(a.astype(BF16), b.astype(BF16), preferred_element_type=F32)


def _bdot_nt(a, b):
    return lax.dot_general(a.astype(BF16), b.astype(BF16), (((1,), (1,)), ((), ())),
                           preferred_element_type=F32)


def _bdot_tn(a, b):
    return lax.dot_general(a.astype(BF16), b.astype(BF16), (((0,), (0,)), ((), ())),
                           preferred_element_type=F32)


def _split3(a):
    hi = a.astype(BF16)
    r1 = a - hi.astype(F32)
    mid = r1.astype(BF16)
    lo = (r1 - mid.astype(F32)).astype(BF16)
    return hi, mid, lo


def _dot01_left(m01, a):
    hi, mid, lo = _split3(a)
    d = functools.partial(jnp.dot, m01, preferred_element_type=F32)
    return d(hi) + d(mid) + d(lo)


def _dot01_right(a, m01):
    hi, mid, lo = _split3(a)
    return (jnp.dot(hi, m01, preferred_element_type=F32) + jnp.dot(mid, m01, preferred_element_type=F32)
            + jnp.dot(lo, m01, preferred_element_type=F32))


def _sigmoid(x):
    return 1.0 / (1.0 + jnp.exp(-x))


def _silu(x):
    return x * _sigmoid(x)


def _softplus(x):
    return jnp.maximum(x, 0.0) + jnp.log(1.0 + jnp.exp(-jnp.abs(x)))


def _mixer_body(x_ref, xnext_ref, win_ref, lbp_ref, hgnw_ref, convw_ref, convb_ref, dtb_ref, alog_ref, dskip_ref,
                ssdnw_ref, wout_ref, ln1g_ref, ln1b_ref, rw1_ref, rwh_ref, rb_ref,
                tril_ref, trils_ref, e128_ref,
                x1_ref, meta_ref, cnt_ref,
                proj_s, xb_s, xn_s, b_s, xpad_s, xdt_s, cse_s, st_s, pt_s, cat_s, ossd_s, mix_s):
    TL = TILE_ROWS
    t = pl.program_id(1)

    @pl.when(t == 0)
    def _():
        xpad_s[0:SUBLANES, :] = jnp.zeros((SUBLANES, SSD_CONV_DIM), F32)
        st_s[...] = jnp.zeros_like(st_s)
        pt_s[...] = jnp.zeros_like(pt_s)

    row64 = lax.broadcasted_iota(jnp.int32, (CHUNK, CHUNK), 0)
    col64 = lax.broadcasted_iota(jnp.int32, (CHUNK, CHUNK), 1)
    causal = row64 >= col64
    gw = SSD_WIDTH // SSD_GROUPS
    hpg = SSD_HEADS // SSD_GROUPS
    lane_head = lax.broadcasted_iota(jnp.int32, (CHUNK, gw), 1) // SSD_HEAD_DIM

    pending = []

    def filler():
        if pending:
            pending.pop(0)()

    def project_slices(lhs_ref, lhs_base, base):
        def one(c0, c1):
            def run():
                proj_s[base:base + TL, c0:c1] = jnp.dot(lhs_ref[lhs_base:lhs_base + TL, :], win_ref[:, c0:c1],
                                                        preferred_element_type=F32)
            return run
        return [one(c0, min(c0 + PROJ_SLICE, PROJ_COLS)) for c0 in range(0, PROJ_COLS, PROJ_SLICE)]

    def hg_front(base):
        a0 = lbp_ref[0:1, :]
        a1 = lbp_ref[1:2, :]
        am = jnp.maximum(a0, a1)
        e0 = jnp.exp(a0 - am)
        e1 = jnp.exp(a1 - am)
        lb = e0 / (e0 + e1)
        f = lb + (1.0 - lb) * _sigmoid(proj_s[base:base + TL, OFF_F:OFF_F + HG_WIDTH])
        b_s[base:base + TL, :] = _dot01_left(tril_ref[...], jnp.log(f))
        proj_s[base:base + TL, OFF_F:OFF_F + HG_WIDTH] = f
        proj_s[base:base + TL, OFF_Q:OFF_Q + HG_WIDTH] = _silu(proj_s[base:base + TL, OFF_Q:OFF_Q + HG_WIDTH])
        units = {}
        for c in range(TL // CHUNK):
            r0 = base + c * CHUNK
            for h in range(HG_HEADS):
                h0 = h * HG_HEAD_DIM
                bc = b_s[r0:r0 + CHUNK, h0:h0 + HG_HEAD_DIM]
                qc = proj_s[r0:r0 + CHUNK, OFF_Q + h0:OFF_Q + h0 + HG_HEAD_DIM]
                kc = 1.0 - proj_s[r0:r0 + CHUNK, OFF_F + h0:OFF_F + h0 + HG_HEAD_DIM]
                vcb = proj_s[r0:r0 + CHUNK, OFF_I + h0:OFF_I + h0 + HG_HEAD_DIM].astype(BF16)
                parts = []
                for i in range(CHUNK // SUB_CHUNK):
                    s0 = i * SUB_CHUNK
                    if i == 0:
                        qi = qc[0:SUB_CHUNK] * jnp.exp(bc[0:SUB_CHUNK])
                        ki = kc * jnp.exp(jnp.minimum(-bc, EXP_CAP))
                    else:
                        ref_i = bc[s0 - 1:s0, :]
                        qi = qc[s0:s0 + SUB_CHUNK] * jnp.exp(bc[s0:s0 + SUB_CHUNK] - ref_i)
                        ki = kc * jnp.exp(jnp.minimum(ref_i - bc, EXP_CAP))
                    parts.append(_bdot_nt(qi, ki))
                b_end = bc[CHUNK - 1:CHUNK, :]
                kdec = kc * jnp.exp(b_end - bc)
                units[c, h] = dict(
                    parts=parts, vcb=vcb, qdec=(qc * jnp.exp(bc)).astype(BF16),
                    local=_bdot_tn(vcb, kdec),
                    decay=jnp.exp(b_end))
                if h % 2 == 1:
                    filler()
        return units

    def hg_back(base, units):
        hgnw = hgnw_ref[...]
        for h in range(HG_HEADS):
            st = st_s[h]
            for c in range(TL // CHUNK):
                u = units[c, h]
                u['state'] = st
                st = st * u['decay'] + u['local']
            st_s[h] = st
        for c in range(TL // CHUNK):
            r0 = base + c * CHUNK
            for h in range(HG_HEADS):
                h0 = h * HG_HEAD_DIM
                u = units[c, h]
                sc = jnp.where(causal, jnp.concatenate(u['parts'], axis=0), 0.0)
                o = _bdot(sc, u['vcb']) + _bdot_nt(u['qdec'], u['state'])
                ms = jnp.mean(o * o, axis=-1, keepdims=True)
                on = o * lax.rsqrt(ms + RMS_EPS) * hgnw
                gc = proj_s[r0:r0 + CHUNK, OFF_G + h0:OFF_G + h0 + HG_HEAD_DIM]
                cat_s[r0:r0 + CHUNK, h0:h0 + HG_HEAD_DIM] = (on * _silu(gc)).astype(BF16)
                filler()

    def ssd_front(base):
        e128 = e128_ref[...]
        dtc = _softplus(proj_s[base:base + TL, OFF_DT:OFF_DT + LANES] + dtb_ref[...])
        a_row = -jnp.exp(alog_ref[...])
        cs_c = _dot01_left(tril_ref[...], dtc * a_row)
        cs_r = jnp.transpose(cs_c)[0:SSD_HEADS, :]
        cse_s[base:base + TL, :] = _dot01_right(cs_c, e128)
        dt_exp = _dot01_right(dtc, e128)

        xpad_s[SUBLANES + base:SUBLANES + base + TL, :] = proj_s[base:base + TL, OFF_XBC:OFF_XBC + SSD_CONV_DIM]
        for c0 in range(0, SSD_CONV_DIM, CONV_SLICE):
            acc = jnp.broadcast_to(convb_ref[:, c0:c0 + CONV_SLICE], (TL, CONV_SLICE))
            for j in range(SSD_CONV):
                off = base + SUBLANES - (SSD_CONV - 1) + j
                acc = acc + convw_ref[j:j + 1, c0:c0 + CONV_SLICE] * xpad_s[off:off + TL, c0:c0 + CONV_SLICE]
            proj_s[base:base + TL, OFF_XBC + c0:OFF_XBC + c0 + CONV_SLICE] = _silu(acc)
            filler()
        xdt_s[base:base + TL, :] = proj_s[base:base + TL, OFF_XS:OFF_XS + SSD_WIDTH] * dt_exp

        units = {}
        for c in range(TL // CHUNK):
            r0 = base + c * CHUNK
            for g in range(SSD_GROUPS):
                g0 = g * gw
                bgb = proj_s[r0:r0 + CHUNK, OFF_B + g * SSD_STATE:OFF_B + (g + 1) * SSD_STATE].astype(BF16)
                cgb = proj_s[r0:r0 + CHUNK, OFF_C + g * SSD_STATE:OFF_C + (g + 1) * SSD_STATE].astype(BF16)
                cse_g = cse_s[r0:r0 + CHUNK, g0:g0 + gw]
                cs_end = cse_s[r0 + CHUNK - 1:r0 + CHUNK, g0:g0 + gw]
                xdt_g = xdt_s[r0:r0 + CHUNK, g0:g0 + gw]
                units[c, g] = dict(
                    cgb=cgb, gm=_bdot_nt(cgb, bgb),
                    local=_bdot_tn(bgb, xdt_g * jnp.exp(cs_end - cse_g)),
                    decay=jnp.exp(cs_end))
                filler()
        return units, cs_r

    def ssd_back(base, units, cs_r):
        for g in range(SSD_GROUPS):
            pt = pt_s[g]
            for c in range(TL // CHUNK):
                u = units[c, g]
                u['state'] = pt
                pt = pt * u['decay'] + u['local']
            pt_s[g] = pt
        for c in range(TL // CHUNK):
            r0 = base + c * CHUNK
            for g in range(SSD_GROUPS):
                g0 = g * gw
                u = units[c, g]
                cse_g = cse_s[r0:r0 + CHUNK, g0:g0 + gw]
                xdt_g = xdt_s[r0:r0 + CHUNK, g0:g0 + gw]
                ydiag = jnp.zeros((CHUNK, gw), F32)
                for hl in range(hpg):
                    hh = g * hpg + hl
                    seg = (cse_g[:, hl * SSD_HEAD_DIM:(hl + 1) * SSD_HEAD_DIM]
                           - cs_r[hh:hh + 1, c * CHUNK:(c + 1) * CHUNK])
                    lm = jnp.where(causal, jnp.exp(jnp.minimum(seg, 0.0)), 0.0)
                    xm = jnp.where(lane_head == hl, xdt_g, 0.0)
                    ydiag = ydiag + _bdot(u['gm'] * lm, xm)
                yoff = _bdot(u['cgb'], u['state']) * jnp.exp(cse_g)
                xs_g = proj_s[r0:r0 + CHUNK, OFF_XS + g0:OFF_XS + g0 + gw]
                ossd_s[r0:r0 + CHUNK, g0:g0 + gw] = ydiag + yoff + xs_g * dskip_ref[:, g0:g0 + gw]
                filler()

    def ssd_gate_norm(tile):
        base = tile * TL
        y = ossd_s[base:base + TL, :] * _silu(proj_s[base:base + TL, OFF_Z:OFF_Z + SSD_WIDTH])
        for g in range(SSD_GROUPS):
            yg = y[:, g * gw:(g + 1) * gw]
            ms = jnp.mean(yg * yg, axis=-1, keepdims=True)
            yn = yg * lax.rsqrt(ms + RMS_EPS) * ssdnw_ref[:, g * gw:(g + 1) * gw]
            cat_s[base:base + TL, HG_WIDTH + g * gw:HG_WIDTH + (g + 1) * gw] = yn.astype(BF16)

    def out_slice(tile, c0):
        base = tile * TL
        mix_s[base:base + TL, c0:c0 + OUT_SLICE] = jnp.dot(
            cat_s[base:base + TL, :], wout_ref[:, c0:c0 + OUT_SLICE], preferred_element_type=F32)

    def norm_route(tile):
        base = tile * TL
        hres = DEEPNORM_ALPHA * x_ref[base:base + TL, :] + mix_s[base:base + TL, :]
        mu = jnp.mean(hres, axis=-1, keepdims=True)
        hc = hres - mu
        var = jnp.mean(hc * hc, axis=-1, keepdims=True)
        x1 = hc * lax.rsqrt(var + LN_EPS) * ln1g_ref[...] + ln1b_ref[...]
        x1_ref[base:base + TL, :] = x1
        filler()

        xh = x1.astype(BF16)
        xm_ = (x1 - xh.astype(F32)).astype(BF16)
        t1 = jnp.dot(xh, rw1_ref[...], preferred_element_type=F32)
        logits = (t1[:, 0:LANES] + t1[:, LANES:2 * LANES]
                  + jnp.dot(xm_, rwh_ref[...], preferred_element_type=F32) + rb_ref[...])
        filler()
        lane = lax.broadcasted_iota(jnp.int32, (TL, LANES), 1)
        lane_f = lane.astype(F32)
        neg = jnp.float32(-jnp.inf)
        work = jnp.where(lane < N_EXPERTS, logits, neg)
        onehots, vals, idxs = [], [], []
        for j in range(TOP_K):
            m = jnp.max(work, axis=-1, keepdims=True)
            filler()
            idx = jnp.min(jnp.where(work == m, lane_f, float(LANES)), axis=-1, keepdims=True)
            oh = lane_f == idx
            onehots.append(oh)
            vals.append(m)
            idxs.append(idx)
            work = jnp.where(oh, neg, work)
            filler()
        es = [jnp.exp(v - vals[0]) for v in vals]
        den = es[0] + es[1] + es[2] + es[3]
        gates = [e / den for e in es]
        sel = jnp.zeros((TL, LANES), F32)
        for oh in onehots:
            sel = jnp.where(oh, 1.0, sel)
        rankmat = jnp.dot(trils_ref[...], sel.astype(BF16), preferred_element_type=F32)
        filler()
        cnt_ref[tile * SUBLANES:(tile + 1) * SUBLANES, :] = jnp.broadcast_to(
            jnp.sum(sel, axis=0, keepdims=True), (SUBLANES, LANES))
        meta = jnp.zeros((TL, LANES), F32)
        for j in range(TOP_K):
            rank_j = jnp.sum(jnp.where(onehots[j], rankmat, 0.0), axis=-1, keepdims=True)
            meta = jnp.where(lane == j, idxs[j], meta)
            meta = jnp.where(lane == TOP_K + j, rank_j, meta)
            meta = jnp.where(lane == 2 * TOP_K + j, gates[j], meta)
        meta_ref[base:base + TL, :] = meta

    def post_thunks(tile):
        return ([functools.partial(ssd_gate_norm, tile)]
                + [functools.partial(out_slice, tile, c0) for c0 in range(0, D_MODEL, OUT_SLICE)]
                + [functools.partial(norm_route, tile)])

    xb_s[...] = x_ref[...].astype(BF16)
    xn_s[...] = xnext_ref[...].astype(BF16)

    @pl.when(jnp.logical_and(pl.program_id(0) == 0, t == 0))
    def _():
        for run in project_slices(xb_s, 0, 0):
            run()

    for tile in range(MIX_TILES):
        base = tile * TL
        if tile + 1 < MIX_TILES:
            pending.extend(project_slices(xb_s, base + TL, base + TL))
        hg_units = hg_front(base)
        ssd_units, cs_r = ssd_front(base)
        while pending:
            filler()
        hg_back(base, hg_units)
        ssd_back(base, ssd_units, cs_r)
        if tile + 1 < MIX_TILES:
            pending.extend(post_thunks(tile))
    tail = post_thunks(MIX_TILES - 1)
    slices = project_slices(xn_s, 0, 0)
    tail[0]()
    for run in tail[1:-1]:
        run()
        slices.pop(0)()
    pending.extend(slices)
    tail[-1]()
    while pending:
        filler()
    xpad_s[0:SUBLANES, :] = xpad_s[STEP_ROWS:STEP_ROWS + SUBLANES, :]


def _full(shape):
    nd = len(shape)
    return pl.BlockSpec(shape, lambda *_: (0,) * nd)


def _mixer_call(x2d, consts, batch, seq):
    SR = STEP_ROWS
    nt = seq // SR
    T = batch * seq
    last_tile = T // TILE_ROWS - 1
    in_specs = [pl.BlockSpec((SR, D_MODEL), lambda b, t: (b * nt + t, 0)),
                pl.BlockSpec((TILE_ROWS, D_MODEL),
                             lambda b, t: (jnp.minimum((b * nt + t + 1) * MIX_TILES, last_tile), 0))]
    in_specs += [_full(c.shape) for c in consts]
    out_shape = (jax.ShapeDtypeStruct((T, D_MODEL), F32),
                 jax.ShapeDtypeStruct((T, LANES), F32),
                 jax.ShapeDtypeStruct((T // TILE_ROWS * SUBLANES, LANES), F32))
    out_specs = (pl.BlockSpec((SR, D_MODEL), lambda b, t: (b * nt + t, 0)),
                 pl.BlockSpec((SR, LANES), lambda b, t: (b * nt + t, 0)),
                 pl.BlockSpec((MIX_TILES * SUBLANES, LANES), lambda b, t: (b * nt + t, 0)))
    scratch = [
        pltpu.VMEM((SR, PROJ_COLS), F32),
        pltpu.VMEM((SR, D_MODEL), BF16),
        pltpu.VMEM((TILE_ROWS, D_MODEL), BF16),
        pltpu.VMEM((SR, HG_WIDTH), F32),
        pltpu.VMEM((SR + 2 * SUBLANES, SSD_CONV_DIM), F32),
        pltpu.VMEM((SR, SSD_WIDTH), F32),
        pltpu.VMEM((SR, SSD_WIDTH), F32),
        pltpu.VMEM((HG_HEADS, HG_HEAD_DIM, HG_HEAD_DIM), F32),
        pltpu.VMEM((SSD_GROUPS, SSD_STATE, SSD_WIDTH // SSD_GROUPS), F32),
        pltpu.VMEM((SR, D_MODEL), BF16),
        pltpu.VMEM((SR, SSD_WIDTH), F32),
        pltpu.VMEM((SR, D_MODEL), F32),
    ]
    return pl.pallas_call(
        _mixer_body,
        grid=(batch, nt),
        in_specs=in_specs,
        out_specs=out_specs,
        out_shape=out_shape,
        scratch_shapes=scratch,
        compiler_params=pltpu.CompilerParams(
            dimension_semantics=("arbitrary", "arbitrary"), vmem_limit_bytes=VMEM_LIMIT),
        name="mixer",
    )(x2d, x2d, *consts)


def _for_each_run(units_ref, tile, fn):
    def per_expert(e, carry):
        k = tile * N_EXPERTS + e
        n = units_ref[k]

        @pl.when(n > 0)
        def _():
            fn(k, n)
        return carry
    lax.fori_loop(0, N_EXPERTS, per_expert, 0)


def _rows(unit, n_units=1):
    start = unit * RUN_ALIGN
    if RUN_ALIGN > 1:
        start = pl.multiple_of(start, RUN_ALIGN)
    return pl.ds(start, n_units * RUN_ALIGN)


def _dispatch_body(units_ref, g8_ref, l8_ref, tot_ref, tail8_ref, tailn_ref, nv_ref,
                   x1_ref, meta_ref, lst_ref, xs_hbm, sorted_s, zero_s, sems, zsem, bsem):
    step = pl.program_id(0)
    n = pl.num_programs(0)
    TD, S = TILE_ROWS, SORT_ROWS
    n_blocks = xs_hbm.shape[0] // EXPERT_BLOCK

    def unused_block_copy(m):
        rows = pl.ds(pl.multiple_of(m * EXPERT_BLOCK, EXPERT_BLOCK), EXPERT_BLOCK)
        return pltpu.make_async_copy(zero_s, xs_hbm.at[rows, :], bsem)

    def run_copy(s, l_unit, g_unit, n_units):
        return pltpu.make_async_copy(sorted_s.at[s, _rows(l_unit, n_units), :],
                                     xs_hbm.at[_rows(g_unit, n_units), :], sems.at[s])

    def wait_units(s, count):
        run_copy(s, 0, 0, count).wait()

    @pl.when(step == 0)
    def _():
        zero_s[...] = jnp.zeros_like(zero_s)

        def tail_copy(e):
            n_units = tailn_ref[e]
            return pltpu.make_async_copy(zero_s.at[pl.ds(0, n_units * RUN_ALIGN), :],
                                         xs_hbm.at[_rows(tail8_ref[e], n_units), :], zsem)

        def start_unused(m, carry):
            unused_block_copy(m).start()
            return carry
        lax.fori_loop(nv_ref[0], n_blocks, start_unused, 0)

        def start_e(e, carry):
            @pl.when(tailn_ref[e] > 0)
            def _():
                tail_copy(e).start()
            return carry
        lax.fori_loop(0, N_EXPERTS, start_e, 0)

        def wait_e(e, carry):
            @pl.when(tailn_ref[e] > 0)
            def _():
                tail_copy(e).wait()
            return carry
        lax.fori_loop(0, N_EXPERTS, wait_e, 0)

    lane_f = lax.broadcasted_iota(jnp.int32, (TD, LANES), 1).astype(F32)
    ones8 = jnp.ones((SUBLANES, LANES), BF16)
    r_iota = lax.broadcasted_iota(jnp.int32, (S, TD), 0).astype(F32)
    nt_dims = (((1,), (1,)), ((), ()))
    for slot in range(ROUTE_TILES):
        _dispatch_tile(step * ROUTE_TILES + slot, slot, step, units_ref, g8_ref, l8_ref, tot_ref,
                       x1_ref, meta_ref, lst_ref, sorted_s, run_copy, wait_units,
                       lane_f, ones8, r_iota, nt_dims)

    @pl.when(step == n - 1)
    def _():
        for slot in range(ROUTE_TILES):
            wait_units(slot, tot_ref[step * ROUTE_TILES + slot])

        def wait_unused(m, carry):
            unused_block_copy(m).wait()
            return carry
        lax.fori_loop(nv_ref[0], n_blocks, wait_unused, 0)


def _dispatch_tile(tau, slot, step, units_ref, g8_ref, l8_ref, tot_ref, x1_ref, meta_ref, lst_ref, sorted_s,
                   run_copy, wait_units, lane_f, ones8, r_iota, nt_dims):
    TD = TILE_ROWS
    rows = slice(slot * TD, (slot + 1) * TD)

    @pl.when(step >= 1)
    def _():
        wait_units(slot, tot_ref[jnp.maximum(tau - ROUTE_TILES, 0)])

    meta = meta_ref[rows, :]
    u8 = lst_ref[slot * SUBLANES:(slot + 1) * SUBLANES, :].astype(BF16)
    u_row = lst_ref[slot * SUBLANES + 2:slot * SUBLANES + 3, :]
    gh = meta.astype(BF16).astype(F32)
    g1 = meta - gh
    gm = g1.astype(BF16).astype(F32)
    gl = g1 - gm
    lane_j = jnp.floor((lane_f + 0.5) * (1.0 / GATE_PARTS))
    lane_p = lane_f - GATE_PARTS * lane_j
    aux_g = jnp.zeros((TD, LANES), F32)
    aux_h = jnp.zeros((TD, LANES), F32)
    aux_l = jnp.zeros((TD, LANES), F32)
    conds = []
    for j in range(TOP_K):
        oh = lane_f == meta[:, j:j + 1]
        rank_col = meta[:, TOP_K + j:TOP_K + j + 1]
        ohb = jnp.where(oh, 1.0, 0.0).astype(BF16)
        rkb = jnp.where(oh, rank_col, 0.0).astype(BF16)
        m1 = lax.dot_general(u8, ohb, nt_dims, preferred_element_type=F32)
        m2 = lax.dot_general(ones8, rkb, nt_dims, preferred_element_type=F32)
        lpos = RUN_ALIGN * (LST_SPLIT * m1[0:1, :] + m1[1:2, :]) + m2[0:1, :]
        conds.append(r_iota == lpos)
        lpos_col = RUN_ALIGN * jnp.sum(jnp.where(oh, u_row, 0.0), axis=-1, keepdims=True) + rank_col
        hi_col = jnp.floor(lpos_col * (1.0 / LST_SPLIT))
        gc = 2 * TOP_K + j
        g_j = jnp.where(lane_p == 0, gh[:, gc:gc + 1], jnp.where(lane_p == 1, gm[:, gc:gc + 1], gl[:, gc:gc + 1]))
        aux_g = jnp.where(lane_j == j, g_j, aux_g)
        aux_h = jnp.where(lane_j == j, hi_col, aux_h)
        aux_l = jnp.where(lane_j == j, lpos_col - LST_SPLIT * hi_col, aux_l)
    perm = jnp.where(conds[0], 1.0, jnp.where(conds[1], 1.0, jnp.where(conds[2], 1.0,
                     jnp.where(conds[3], 1.0, 0.0)))).astype(BF16)
    sorted_x = jnp.dot(perm, x1_ref[rows, :].astype(BF16), preferred_element_type=F32)
    aux = jnp.concatenate([aux_g, aux_h, aux_l], axis=1).astype(BF16)
    sorted_aux = jnp.dot(perm, aux, preferred_element_type=F32)
    own_row = lax.broadcasted_iota(jnp.int32, (sorted_aux.shape[0], LANES), 0).astype(F32)
    match = LST_SPLIT * sorted_aux[:, LANES:2 * LANES] + sorted_aux[:, 2 * LANES:3 * LANES] == own_row
    sorted_g = jnp.where(match, sorted_aux[:, 0:LANES], 0.0)
    sorted_s[slot, :, 0:D_MODEL] = sorted_x
    sorted_s[slot, :, D_MODEL:XS_WIDTH] = sorted_g

    _for_each_run(units_ref, tau, lambda k, cnt: run_copy(slot, l8_ref[k], g8_ref[k], cnt).start())


def _dispatch_call(sched, x1, meta, lst_rows, cap):
    T = x1.shape[0]
    RR = ROUTE_TILES * TILE_ROWS
    grid_spec = pltpu.PrefetchScalarGridSpec(
        num_scalar_prefetch=7,
        grid=(T // RR,),
        in_specs=[
            pl.BlockSpec((RR, D_MODEL), lambda i, *_: (i, 0)),
            pl.BlockSpec((RR, LANES), lambda i, *_: (i, 0)),
            pl.BlockSpec((ROUTE_TILES * SUBLANES, LANES), lambda i, *_: (i, 0)),
        ],
        out_specs=pl.BlockSpec(memory_space=pl.ANY),
        scratch_shapes=[
            pltpu.VMEM((2, SORT_ROWS, XS_WIDTH), F32),
            pltpu.VMEM((EXPERT_BLOCK, XS_WIDTH), F32),
            pltpu.SemaphoreType.DMA((2,)),
            pltpu.SemaphoreType.DMA(()),
            pltpu.SemaphoreType.DMA(()),
        ],
    )
    return pl.pallas_call(
        _dispatch_body,
        grid_spec=grid_spec,
        out_shape=jax.ShapeDtypeStruct((cap, XS_WIDTH), F32),
        compiler_params=pltpu.CompilerParams(
            dimension_semantics=("arbitrary",), vmem_limit_bytes=VMEM_LIMIT),
        name="dispatch",
    )(*sched, x1, meta, lst_rows)


def _expert_body(ite_ref, itb_ref, sz_ref, nxt_ref, par_ref, nit_ref, nv_ref,
                 xs_hbm, wg_hbm, bg_ref, wu_hbm, bu_ref, wd_hbm, bd_ref,
                 y_hbm, xbuf, ybuf, zbuf, wbuf, xsem, ysem, zsem, wsem, wg_s, wu_s, wd_s):
    n_iter = nit_ref[0]
    n_blocks = xs_hbm.shape[0] // EXPERT_BLOCK
    sizes = tuple(n * EXPERT_BLOCK for n in CHUNK_BLOCKS)

    def rows_of(i, rows):
        return pl.ds(pl.multiple_of(itb_ref[i] * EXPERT_BLOCK, EXPERT_BLOCK), rows)

    def x_copy(i, s, rows):
        return pltpu.make_async_copy(xs_hbm.at[rows_of(i, rows), :], xbuf.at[s, pl.ds(0, rows), :], xsem.at[s])

    def y_copy(i, s, rows):
        return pltpu.make_async_copy(ybuf.at[s, pl.ds(0, rows), :], y_hbm.at[rows_of(i, rows), :], ysem.at[s])

    def by_size(i, fn):
        for k, rows in enumerate(sizes):
            @pl.when(sz_ref[i] == k)
            def _():
                fn(rows)

    def zero_copy(m):
        rows = pl.ds(pl.multiple_of(m * EXPERT_BLOCK, EXPERT_BLOCK), EXPERT_BLOCK)
        return pltpu.make_async_copy(zbuf, y_hbm.at[rows, :], zsem)

    def weight_copies(e, s):
        return [pltpu.make_async_copy(w.at[e], wbuf.at[s, k], wsem.at[s])
                for k, w in enumerate((wg_hbm, wu_hbm, wd_hbm))]

    for c in weight_copies(ite_ref[0], par_ref[0]):
        c.start()
    by_size(0, lambda rows: x_copy(0, 0, rows).start())

    zbuf[...] = jnp.zeros_like(zbuf)

    def start_zero(m, carry):
        zero_copy(m).start()
        return carry
    lax.fori_loop(nv_ref[0], n_blocks, start_zero, 0)

    def ffn(s, e, rows):
        xb = xbuf[s, 0:rows, 0:D_MODEL].astype(BF16)
        gate = jnp.sum(xbuf[s, 0:rows, D_MODEL:XS_WIDTH], axis=-1, keepdims=True)
        hg = jnp.minimum(jnp.dot(xb, wg_s[...], preferred_element_type=F32) + bg_ref[e], SWIGLU_LIMIT)
        hu = jnp.clip(jnp.dot(xb, wu_s[...], preferred_element_type=F32) + bu_ref[e],
                      -SWIGLU_LIMIT, SWIGLU_LIMIT)
        hact = (hu + 1.0) * (hg * _sigmoid(SWIGLU_ALPHA * hg))
        y = jnp.dot(hact.astype(BF16), wd_s[...], preferred_element_type=F32) + bd_ref[e]
        ybuf[s, 0:rows, :] = y * gate

    def body(i, carry):
        s = lax.rem(i, 2)
        e = ite_ref[i]
        by_size(i, lambda rows: x_copy(i, s, rows).wait())

        @pl.when(i + 1 < n_iter)
        def _():
            by_size(i + 1, lambda rows: x_copy(i + 1, 1 - s, rows).start())

        @pl.when(jnp.logical_or(i == 0, e != ite_ref[jnp.maximum(i - 1, 0)]))
        def _():
            ws = par_ref[i]
            for c in weight_copies(e, ws):
                c.wait()

            @pl.when(nxt_ref[i] != e)
            def _():
                for c in weight_copies(nxt_ref[i], 1 - ws):
                    c.start()
            wg_s[...] = wbuf[ws, 0].astype(BF16)
            wu_s[...] = wbuf[ws, 1].astype(BF16)
            wd_s[...] = wbuf[ws, 2].astype(BF16)

        @pl.when(i >= 2)
        def _():
            by_size(i - 2, lambda rows: y_copy(i - 2, s, rows).wait())

        def chunk(rows):
            ffn(s, e, rows)
            y_copy(i, s, rows).start()
        by_size(i, chunk)
        return carry
    lax.fori_loop(0, n_iter, body, 0)

    @pl.when(n_iter >= 2)
    def _():
        by_size(n_iter - 2, lambda rows: y_copy(n_iter - 2, lax.rem(n_iter, 2), rows).wait())
    by_size(n_iter - 1, lambda rows: y_copy(n_iter - 1, lax.rem(n_iter - 1, 2), rows).wait())

    def wait_zero(m, carry):
        zero_copy(m).wait()
        return carry
    lax.fori_loop(nv_ref[0], n_blocks, wait_zero, 0)


def _expert_call(sched, xs, w_gate, b_gate, w_up, b_up, w_down, b_down):
    n_blocks = xs.shape[0] // EXPERT_BLOCK
    anyspec = pl.BlockSpec(memory_space=pl.ANY)
    bspec = pl.BlockSpec((N_EXPERTS, 1, D_MODEL), lambda i, *_: (0, 0, 0))
    grid_spec = pltpu.PrefetchScalarGridSpec(
        num_scalar_prefetch=len(sched),
        grid=(1,),
        in_specs=[anyspec, anyspec, bspec, anyspec, bspec, anyspec, bspec],
        out_specs=anyspec,
        scratch_shapes=[
            pltpu.VMEM((2, max(CHUNK_BLOCKS) * EXPERT_BLOCK, XS_WIDTH), F32),
            pltpu.VMEM((2, max(CHUNK_BLOCKS) * EXPERT_BLOCK, D_MODEL), F32),
            pltpu.VMEM((EXPERT_BLOCK, D_MODEL), F32),
            pltpu.VMEM((2, 3, D_MODEL, D_MODEL), F32),
            pltpu.SemaphoreType.DMA((2,)),
            pltpu.SemaphoreType.DMA((2,)),
            pltpu.SemaphoreType.DMA(()),
            pltpu.SemaphoreType.DMA((2,)),
            pltpu.VMEM((D_MODEL, D_MODEL), BF16),
            pltpu.VMEM((D_MODEL, D_MODEL), BF16),
            pltpu.VMEM((D_MODEL, D_MODEL), BF16),
        ],
    )
    return pl.pallas_call(
        _expert_body,
        grid_spec=grid_spec,
        out_shape=jax.ShapeDtypeStruct((n_blocks * EXPERT_BLOCK, D_MODEL), F32),
        compiler_params=pltpu.CompilerParams(
            dimension_semantics=("arbitrary",), vmem_limit_bytes=VMEM_LIMIT),
        name="experts",
    )(*sched, xs, w_gate, b_gate.reshape(N_EXPERTS, 1, D_MODEL),
      w_up, b_up.reshape(N_EXPERTS, 1, D_MODEL), w_down, b_down.reshape(N_EXPERTS, 1, D_MODEL))


def _combine_body(units_ref, g8_ref, l8_ref, tot_ref,
                  y_hbm, x1_ref, meta_ref, lst_ref, g_ref, b_ref, out_ref, ys_s, sems):
    step = pl.program_id(0)
    n_tiles = pl.num_programs(0) * ROUTE_TILES
    TD, S = TILE_ROWS, SORT_ROWS

    def run_copy(s, g_unit, l_unit, n_units):
        return pltpu.make_async_copy(y_hbm.at[_rows(g_unit, n_units), :],
                                     ys_s.at[s, _rows(l_unit, n_units), :], sems.at[s])

    def fetch(tile, s):
        _for_each_run(units_ref, tile, lambda k, cnt: run_copy(s, g8_ref[k], l8_ref[k], cnt).start())

    @pl.when(step == 0)
    def _():
        ys_s[...] = jnp.zeros_like(ys_s)
        fetch(0, 0)

    lane_f = lax.broadcasted_iota(jnp.int32, (TD, LANES), 1).astype(F32)
    s_iota = lax.broadcasted_iota(jnp.int32, (TD, S), 1).astype(F32)
    for h in range(ROUTE_TILES):
        tau = step * ROUTE_TILES + h
        slot = h % 2
        rows = slice(h * TD, (h + 1) * TD)

        @pl.when(tau + 1 < n_tiles)
        def _():
            fetch(tau + 1, 1 - slot)

        run_copy(slot, 0, 0, tot_ref[tau]).wait()

        meta = meta_ref[rows, :]
        u_row = lst_ref[h * SUBLANES + 2:h * SUBLANES + 3, :]
        conds = []
        for j in range(TOP_K):
            oh = lane_f == meta[:, j:j + 1]
            start8 = jnp.sum(jnp.where(oh, u_row, 0.0), axis=-1, keepdims=True)
            lpos = RUN_ALIGN * start8 + meta[:, TOP_K + j:TOP_K + j + 1]
            conds.append(s_iota == lpos)
        perm = jnp.where(conds[0], 1.0, jnp.where(conds[1], 1.0, jnp.where(conds[2], 1.0,
                         jnp.where(conds[3], 1.0, 0.0)))).astype(BF16)
        ys = ys_s[slot]
        yh = ys.astype(BF16)
        yl = (ys - yh.astype(F32)).astype(BF16)
        ffn = (jnp.dot(perm, yh, preferred_element_type=F32) + jnp.dot(perm, yl, preferred_element_type=F32))
        acc = DEEPNORM_ALPHA * x1_ref[rows, :] + ffn
        mu = jnp.mean(acc, axis=-1, keepdims=True)
        hc = acc - mu
        var = jnp.mean(hc * hc, axis=-1, keepdims=True)
        out_ref[rows, :] = hc * lax.rsqrt(var + LN_EPS) * g_ref[...] + b_ref[...]


def _combine_call(sched, y_rows, x1, meta, lst_rows, ln2_g, ln2_b):
    T = x1.shape[0]
    RR = ROUTE_TILES * TILE_ROWS
    grid_spec = pltpu.PrefetchScalarGridSpec(
        num_scalar_prefetch=4,
        grid=(T // RR,),
        in_specs=[
            pl.BlockSpec(memory_space=pl.ANY),
            pl.BlockSpec((RR, D_MODEL), lambda i, *_: (i, 0)),
            pl.BlockSpec((RR, LANES), lambda i, *_: (i, 0)),
            pl.BlockSpec((ROUTE_TILES * SUBLANES, LANES), lambda i, *_: (i, 0)),
            pl.BlockSpec((1, D_MODEL), lambda i, *_: (0, 0)),
            pl.BlockSpec((1, D_MODEL), lambda i, *_: (0, 0)),
        ],
        out_specs=pl.BlockSpec((RR, D_MODEL), lambda i, *_: (i, 0)),
        scratch_shapes=[pltpu.VMEM((2, SORT_ROWS, D_MODEL), F32), pltpu.SemaphoreType.DMA((2,))],
    )
    return pl.pallas_call(
        _combine_body,
        grid_spec=grid_spec,
        out_shape=jax.ShapeDtypeStruct((T, D_MODEL), F32),
        compiler_params=pltpu.CompilerParams(
            dimension_semantics=("arbitrary",), vmem_limit_bytes=VMEM_LIMIT),
        name="combine",
    )(*sched, y_rows, x1, meta, lst_rows, ln2_g, ln2_b)


def _np_consts():
    TL = TILE_ROWS
    r = np.arange(TL)
    same = (r[:, None] // CHUNK) == (r[None, :] // CHUNK)
    tril = (same & (r[None, :] <= r[:, None])).astype(np.float32)
    trils = (r[None, :] < r[:, None]).astype(np.float32)
    e128 = np.zeros((LANES, SSD_WIDTH), np.float32)
    for h in range(SSD_HEADS):
        e128[h, h * SSD_HEAD_DIM:(h + 1) * SSD_HEAD_DIM] = 1.0
    return tril, trils, e128


def kernel(x, w_in, hg_lower_bound, hg_norm_w, conv_w, conv_b, dt_bias, a_log, d_skip, ssd_norm_w, w_out,
           ln1_g, ln1_b, router_w, router_b, w_gate, b_gate, w_up, b_up, w_down, b_down, ln2_g, ln2_b):
    batch, seq, d = x.shape
    assert d == D_MODEL and seq % STEP_ROWS == 0 and w_in.shape[0] == DEPTH
    assert ROUTE_TILES == 2 and (batch * seq) % (ROUTE_TILES * TILE_ROWS) == 0
    T = batch * seq
    n_tiles = T // TILE_ROWS
    max_rows = T * TOP_K + n_tiles * N_EXPERTS * (RUN_ALIGN - 1)
    n_blocks = -(-max_rows // EXPERT_BLOCK) + N_EXPERTS
    cap = n_blocks * EXPERT_BLOCK
    assert SORT_ROWS >= TILE_ROWS * TOP_K + N_EXPERTS * (RUN_ALIGN - 1)

    tril, trils, e128 = _np_consts()
    w = w_in[0]
    pad_l = LANES - SSD_HEADS
    rw = jnp.pad(router_w[0], ((0, 0), (0, LANES - N_EXPERTS)))
    rwh = rw.astype(BF16)
    rwm = (rw - rwh.astype(F32)).astype(BF16)
    consts = [
        jnp.pad(w.astype(BF16), ((0, 0), (0, pad_l))),
        hg_lower_bound,
        hg_norm_w[0].reshape(1, HG_HEAD_DIM),
        conv_w[0],
        conv_b[0].reshape(1, SSD_CONV_DIM),
        jnp.pad(dt_bias[0], (0, pad_l)).reshape(1, LANES),
        jnp.pad(a_log[0], (0, pad_l)).reshape(1, LANES),
        jnp.repeat(d_skip[0], SSD_HEAD_DIM).reshape(1, SSD_WIDTH),
        ssd_norm_w[0].reshape(1, SSD_WIDTH),
        w_out[0].astype(BF16),
        ln1_g[0].reshape(1, D_MODEL),
        ln1_b[0].reshape(1, D_MODEL),
        jnp.concatenate([rwh, rwm], axis=1),
        rwh,
        jnp.pad(router_b[0], (0, LANES - N_EXPERTS)).reshape(1, LANES),
        jnp.asarray(tril, BF16), jnp.asarray(trils, BF16), jnp.asarray(e128, BF16),
    ]
    x1, meta, cnt = _mixer_call(x.reshape(T, D_MODEL), consts, batch, seq)

    counts = cnt.reshape(n_tiles, SUBLANES, LANES)[:, 0, :N_EXPERTS].astype(jnp.int32)
    c8 = (counts + RUN_ALIGN - 1) // RUN_ALIGN * RUN_ALIGN
    used = jnp.sum(c8, axis=0)
    region = (used + EXPERT_BLOCK - 1) // EXPERT_BLOCK * EXPERT_BLOCK
    region_end = jnp.cumsum(region)
    region_start = region_end - region
    gstart = region_start[None, :] + jnp.cumsum(c8, axis=0) - c8
    lstart = jnp.cumsum(c8, axis=1) - c8
    n_valid = region_end[-1] // EXPERT_BLOCK
    has = region > 0
    eidx = jnp.arange(N_EXPERTS, dtype=jnp.int32)
    suffix_min = lax.cummin(jnp.where(has, eidx, N_EXPERTS), reverse=True)
    nxt_e = jnp.concatenate([suffix_min[1:], jnp.full((1,), N_EXPERTS, jnp.int32)])
    nxt_e = jnp.where(nxt_e == N_EXPERTS, eidx, nxt_e)
    par_e = (jnp.cumsum(has.astype(jnp.int32)) - 1) % 2
    left = region // EXPERT_BLOCK
    n_of = []
    for nb in CHUNK_BLOCKS:
        n_of.append(left // nb)
        left = left % nb
    it_cnt = sum(n_of)
    it_end = jnp.cumsum(it_cnt)
    n_iter = it_end[-1]
    max_iter = n_blocks // max(CHUNK_BLOCKS) + (len(CHUNK_BLOCKS) - 1) * N_EXPERTS
    it = jnp.minimum(jnp.arange(max_iter, dtype=jnp.int32), n_iter - 1)
    it_e = jnp.minimum(jnp.sum(it_end[None, :] <= it[:, None], axis=1), N_EXPERTS - 1).astype(jnp.int32)
    it_onehot = it_e[:, None] == eidx[None, :]
    lookup = lambda tab: jnp.sum(jnp.where(it_onehot, tab[None, :].astype(jnp.int32), 0), axis=1)
    k = it - lookup(it_end - it_cnt)
    it_sz = jnp.zeros_like(it)
    it_blk = lookup(region_start // EXPERT_BLOCK)
    for idx, nb in enumerate(CHUNK_BLOCKS):
        n_here = lookup(n_of[idx])
        inside = jnp.logical_and(k >= 0, k < n_here)
        it_sz = jnp.where(inside, idx, it_sz)
        it_blk = it_blk + nb * jnp.clip(k, 0, n_here)
        k = k - n_here
    expert_sched = (it_e, it_blk.astype(jnp.int32), it_sz.astype(jnp.int32), lookup(nxt_e).astype(jnp.int32),
                    lookup(par_e).astype(jnp.int32),
                    n_iter.astype(jnp.int32).reshape(1), n_valid.astype(jnp.int32).reshape(1))
    as_units = lambda a: (a // RUN_ALIGN).astype(jnp.int32).reshape(-1)
    units, g8, l8 = as_units(c8), as_units(gstart), as_units(lstart)
    tot = (jnp.sum(c8, axis=1) // RUN_ALIGN).astype(jnp.int32)
    tail8 = as_units(region_start + used)
    tailn = as_units(region - used)
    lu = lstart // RUN_ALIGN
    lst3 = jnp.stack([lu // LST_SPLIT, lu % LST_SPLIT, lu], axis=1).astype(F32)
    lst_rows = jnp.pad(lst3, ((0, 0), (0, SUBLANES - 3), (0, LANES - N_EXPERTS))).reshape(
        n_tiles * SUBLANES, LANES)

    nv = n_valid.astype(jnp.int32).reshape(1)
    xs = _dispatch_call((units, g8, l8, tot, tail8, tailn, nv), x1, meta, lst_rows, cap)
    y_rows = _expert_call(expert_sched, xs, w_gate[0], b_gate[0], w_up[0], b_up[0], w_down[0], b_down[0])
    out = _combine_call((units, g8, l8, tot), y_rows, x1, meta, lst_rows,
                        ln2_g[0].reshape(1, D_MODEL), ln2_b[0].reshape(1, D_MODEL))
    return out.reshape(batch, seq, D_MODEL)
```

```python
import functools

import jax
import jax.numpy as jnp
import numpy as np
from jax import lax
from jax.experimental import pallas as pl
from jax.experimental.pallas import tpu as pltpu

F32 = jnp.float32
BF16 = jnp.bfloat16

D_MODEL = 1024
CHUNK = 64
HG_WIDTH = 512
HG_HEAD_DIM = 128
HG_HEADS = 4
SSD_WIDTH = 512
SSD_HEAD_DIM = 64
SSD_HEADS = 8
SSD_GROUPS = 2
SSD_STATE = 128
SSD_CONV = 4
SSD_CONV_DIM = SSD_WIDTH + 2 * SSD_GROUPS * SSD_STATE
N_EXPERTS = 32
TOP_K = 4
EXPERT_BLOCK = 128
CHUNK_BLOCKS = (4, 2, 1)
SWIGLU_LIMIT = 7.0
SWIGLU_ALPHA = 1.702
DEPTH = 1
DEEPNORM_ALPHA = (2 * DEPTH) ** 0.25
LN_EPS = 1e-5
RMS_EPS = 1e-5

LANES = 128
SUBLANES = 8
SUB_CHUNK = 16
EXP_CAP = 60.0
TILE_ROWS = 256
MIX_TILES = 2
ROUTE_TILES = 2
STEP_ROWS = MIX_TILES * TILE_ROWS
RUN_ALIGN = SUBLANES
SORT_ROWS = 1280
LST_SPLIT = 32
GATE_PARTS = 3
XS_WIDTH = D_MODEL + LANES
VMEM_LIMIT = 56 * 1024 * 1024

OFF_Q, OFF_F, OFF_I, OFF_G = 0, 512, 1024, 1536
OFF_Z, OFF_XBC = 2048, 2560
OFF_XS, OFF_B, OFF_C = 2560, 3072, 3328
OFF_DT = 3584
PROJ_COLS = OFF_DT + LANES
PROJ_SLICE = 256
CONV_SLICE = 256
OUT_SLICE = 256


def _bdot(a, b):
    return jnp.dot(a.astype(BF16), b.astype(BF16), preferred_element_type=F32)


def _bdot_nt(a, b):
    return lax.dot_general(a.astype(BF16), b.astype(BF16), (((1,), (1,)), ((), ())),
                           preferred_element_type=F32)


def _bdot_tn(a, b):
    return lax.dot_general(a.astype(BF16), b.astype(BF16), (((0,), (0,)), ((), ())),
                           preferred_element_type=F32)


def _split3(a):
    hi = a.astype(BF16)
    r1 = a - hi.astype(F32)
    mid = r1.astype(BF16)
    lo = (r1 - mid.astype(F32)).astype(BF16)
    return hi, mid, lo


def _dot01_left(m01, a):
    hi, mid, lo = _split3(a)
    d = functools.partial(jnp.dot, m01, preferred_element_type=F32)
    return d(hi) + d(mid) + d(lo)


def _dot01_right(a, m01):
    hi, mid, lo = _split3(a)
    return (jnp.dot(hi, m01, preferred_element_type=F32) + jnp.dot(mid, m01, preferred_element_type=F32)
            + jnp.dot(lo, m01, preferred_element_type=F32))


def _sigmoid(x):
    return 1.0 / (1.0 + jnp.exp(-x))


def _silu(x):
    return x * _sigmoid(x)


def _softplus(x):
    return jnp.maximum(x, 0.0) + jnp.log(1.0 + jnp.exp(-jnp.abs(x)))


def _mixer_body(x_ref, xnext_ref, win_ref, lbp_ref, hgnw_ref, convw_ref, convb_ref, dtb_ref, alog_ref, dskip_ref,
                ssdnw_ref, wout_ref, ln1g_ref, ln1b_ref, rw1_ref, rwh_ref, rb_ref,
                tril_ref, trils_ref, e128_ref,
                x1_ref, meta_ref, cnt_ref,
                proj_s, xb_s, xn_s, b_s, xpad_s, xdt_s, cse_s, st_s, pt_s, cat_s, ossd_s, mix_s):
    TL = TILE_ROWS
    t = pl.program_id(1)

    @pl.when(t == 0)
    def _():
        xpad_s[0:SUBLANES, :] = jnp.zeros((SUBLANES, SSD_CONV_DIM), F32)
        st_s[...] = jnp.zeros_like(st_s)
        pt_s[...] = jnp.zeros_like(pt_s)

    row64 = lax.broadcasted_iota(jnp.int32, (CHUNK, CHUNK), 0)
    col64 = lax.broadcasted_iota(jnp.int32, (CHUNK, CHUNK), 1)
    causal = row64 >= col64
    gw = SSD_WIDTH // SSD_GROUPS
    hpg = SSD_HEADS // SSD_GROUPS
    lane_head = lax.broadcasted_iota(jnp.int32, (CHUNK, gw), 1) // SSD_HEAD_DIM

    pending = []

    def filler():
        if pending:
            pending.pop(0)()

    def project_slices(lhs_ref, lhs_base, base):
        def one(c0, c1):
            def run():
                proj_s[base:base + TL, c0:c1] = jnp.dot(lhs_ref[lhs_base:lhs_base + TL, :], win_ref[:, c0:c1],
                                                        preferred_element_type=F32)
            return run
        return [one(c0, min(c0 + PROJ_SLICE, PROJ_COLS)) for c0 in range(0, PROJ_COLS, PROJ_SLICE)]

    def hg_front(base):
        a0 = lbp_ref[0:1, :]
        a1 = lbp_ref[1:2, :]
        am = jnp.maximum(a0, a1)
        e0 = jnp.exp(a0 - am)
        e1 = jnp.exp(a1 - am)
        lb = e0 / (e0 + e1)
        f = lb + (1.0 - lb) * _sigmoid(proj_s[base:base + TL, OFF_F:OFF_F + HG_WIDTH])
        b_s[base:base + TL, :] = _dot01_left(tril_ref[...], jnp.log(f))
        proj_s[base:base + TL, OFF_F:OFF_F + HG_WIDTH] = f
        proj_s[base:base + TL, OFF_Q:OFF_Q + HG_WIDTH] = _silu(proj_s[base:base + TL, OFF_Q:OFF_Q + HG_WIDTH])
        units = {}
        for c in range(TL // CHUNK):
            r0 = base + c * CHUNK
            for h in range(HG_HEADS):
                h0 = h * HG_HEAD_DIM
                bc = b_s[r0:r0 + CHUNK, h0:h0 + HG_HEAD_DIM]
                qc = proj_s[r0:r0 + CHUNK, OFF_Q + h0:OFF_Q + h0 + HG_HEAD_DIM]
                kc = 1.0 - proj_s[r0:r0 + CHUNK, OFF_F + h0:OFF_F + h0 + HG_HEAD_DIM]
                vcb = proj_s[r0:r0 + CHUNK, OFF_I + h0:OFF_I + h0 + HG_HEAD_DIM].astype(BF16)
                parts = []
                for i in range(CHUNK // SUB_CHUNK):
                    s0 = i * SUB_CHUNK
                    if i == 0:
                        qi = qc[0:SUB_CHUNK] * jnp.exp(bc[0:SUB_CHUNK])
                        ki = kc * jnp.exp(jnp.minimum(-bc, EXP_CAP))
                    else:
                        ref_i = bc[s0 - 1:s0, :]
                        qi = qc[s0:s0 + SUB_CHUNK] * jnp.exp(bc[s0:s0 + SUB_CHUNK] - ref_i)
                        ki = kc * jnp.exp(jnp.minimum(ref_i - bc, EXP_CAP))
                    parts.append(_bdot_nt(qi, ki))
                b_end = bc[CHUNK - 1:CHUNK, :]
                kdec = kc * jnp.exp(b_end - bc)
                units[c, h] = dict(
                    parts=parts, vcb=vcb, qdec=(qc * jnp.exp(bc)).astype(BF16),
                    local=_bdot_tn(vcb, kdec),
                    decay=jnp.exp(b_end))
                if h % 2 == 1:
                    filler()
        return units

    def hg_back(base, units):
        hgnw = hgnw_ref[...]
        for h in range(HG_HEADS):
            st = st_s[h]
            for c in range(TL // CHUNK):
                u = units[c, h]
                u['state'] = st
                st = st * u['decay'] + u['local']
            st_s[h] = st
        for c in range(TL // CHUNK):
            r0 = base + c * CHUNK
            for h in range(HG_HEADS):
                h0 = h * HG_HEAD_DIM
                u = units[c, h]
                sc = jnp.where(causal, jnp.concatenate(u['parts'], axis=0), 0.0)
                o = _bdot(sc, u['vcb']) + _bdot_nt(u['qdec'], u['state'])
                ms = jnp.mean(o * o, axis=-1, keepdims=True)
                on = o * lax.rsqrt(ms + RMS_EPS) * hgnw
                gc = proj_s[r0:r0 + CHUNK, OFF_G + h0:OFF_G + h0 + HG_HEAD_DIM]
                cat_s[r0:r0 + CHUNK, h0:h0 + HG_HEAD_DIM] = (on * _silu(gc)).astype(BF16)
                filler()

    def ssd_front(base):
        e128 = e128_ref[...]
        dtc = _softplus(proj_s[base:base + TL, OFF_DT:OFF_DT + LANES] + dtb_ref[...])
        a_row = -jnp.exp(alog_ref[...])
        cs_c = _dot01_left(tril_ref[...], dtc * a_row)
        cs_r = jnp.transpose(cs_c)[0:SSD_HEADS, :]
        cse_s[base:base + TL, :] = _dot01_right(cs_c, e128)
        dt_exp = _dot01_right(dtc, e128)

        xpad_s[SUBLANES + base:SUBLANES + base + TL, :] = proj_s[base:base + TL, OFF_XBC:OFF_XBC + SSD_CONV_DIM]
        for c0 in range(0, SSD_CONV_DIM, CONV_SLICE):
            acc = jnp.broadcast_to(convb_ref[:, c0:c0 + CONV_SLICE], (TL, CONV_SLICE))
            for j in range(SSD_CONV):
                off = base + SUBLANES - (SSD_CONV - 1) + j
                acc = acc + convw_ref[j:j + 1, c0:c0 + CONV_SLICE] * xpad_s[off:off + TL, c0:c0 + CONV_SLICE]
            proj_s[base:base + TL, OFF_XBC + c0:OFF_XBC + c0 + CONV_SLICE] = _silu(acc)
            filler()
        xdt_s[base:base + TL, :] = proj_s[base:base + TL, OFF_XS:OFF_XS + SSD_WIDTH] * dt_exp

        units = {}
        for c in range(TL // CHUNK):
            r0 = base + c * CHUNK
            for g in range(SSD_GROUPS):
                g0 = g * gw
                bgb = proj_s[r0:r0 + CHUNK, OFF_B + g * SSD_STATE:OFF_B + (g + 1) * SSD_STATE].astype(BF16)
                cgb = proj_s[r0:r0 + CHUNK, OFF_C + g * SSD_STATE:OFF_C + (g + 1) * SSD_STATE].astype(BF16)
                cse_g = cse_s[r0:r0 + CHUNK, g0:g0 + gw]
                cs_end = cse_s[r0 + CHUNK - 1:r0 + CHUNK, g0:g0 + gw]
                xdt_g = xdt_s[r0:r0 + CHUNK, g0:g0 + gw]
                units[c, g] = dict(
                    cgb=cgb, gm=_bdot_nt(cgb, bgb),
                    local=_bdot_tn(bgb, xdt_g * jnp.exp(cs_end - cse_g)),
                    decay=jnp.exp(cs_end))
                filler()
        return units, cs_r

    def ssd_back(base, units, cs_r):
        for g in range(SSD_GROUPS):
            pt = pt_s[g]
            for c in range(TL // CHUNK):
                u = units[c, g]
                u['state'] = pt
                pt = pt * u['decay'] + u['local']
            pt_s[g] = pt
        for c in range(TL // CHUNK):
            r0 = base + c * CHUNK
            for g in range(SSD_GROUPS):
                g0 = g * gw
                u = units[c, g]
                cse_g = cse_s[r0:r0 + CHUNK, g0:g0 + gw]
                xdt_g = xdt_s[r0:r0 + CHUNK, g0:g0 + gw]
                ydiag = jnp.zeros((CHUNK, gw), F32)
                for hl in range(hpg):
                    hh = g * hpg + hl
                    seg = (cse_g[:, hl * SSD_HEAD_DIM:(hl + 1) * SSD_HEAD_DIM]
                           - cs_r[hh:hh + 1, c * CHUNK:(c + 1) * CHUNK])
                    lm = jnp.where(causal, jnp.exp(jnp.minimum(seg, 0.0)), 0.0)
                    xm = jnp.where(lane_head == hl, xdt_g, 0.0)
                    ydiag = ydiag + _bdot(u['gm'] * lm, xm)
                yoff = _bdot(u['cgb'], u['state']) * jnp.exp(cse_g)
                xs_g = proj_s[r0:r0 + CHUNK, OFF_XS + g0:OFF_XS + g0 + gw]
                ossd_s[r0:r0 + CHUNK, g0:g0 + gw] = ydiag + yoff + xs_g * dskip_ref[:, g0:g0 + gw]
                filler()

    def ssd_gate_norm(tile):
        base = tile * TL
        y = ossd_s[base:base + TL, :] * _silu(proj_s[base:base + TL, OFF_Z:OFF_Z + SSD_WIDTH])
        for g in range(SSD_GROUPS):
            yg = y[:, g * gw:(g + 1) * gw]
            ms = jnp.mean(yg * yg, axis=-1, keepdims=True)
            yn = yg * lax.rsqrt(ms + RMS_EPS) * ssdnw_ref[:, g * gw:(g + 1) * gw]
            cat_s[base:base + TL, HG_WIDTH + g * gw:HG_WIDTH + (g + 1) * gw] = yn.astype(BF16)

    def out_slice(tile, c0):
        base = tile * TL
        mix_s[base:base + TL, c0:c0 + OUT_SLICE] = jnp.dot(
            cat_s[base:base + TL, :], wout_ref[:, c0:c0 + OUT_SLICE], preferred_element_type=F32)

    def norm_route(tile):
        base = tile * TL
        hres = DEEPNORM_ALPHA * x_ref[base:base + TL, :] + mix_s[base:base + TL, :]
        mu = jnp.mean(hres, axis=-1, keepdims=True)
        hc = hres - mu
        var = jnp.mean(hc * hc, axis=-1, keepdims=True)
        x1 = hc * lax.rsqrt(var + LN_EPS) * ln1g_ref[...] + ln1b_ref[...]
        x1_ref[base:base + TL, :] = x1
        filler()

        xh = x1.astype(BF16)
        xm_ = (x1 - xh.astype(F32)).astype(BF16)
        t1 = jnp.dot(xh, rw1_ref[...], preferred_element_type=F32)
        logits = (t1[:, 0:LANES] + t1[:, LANES:2 * LANES]
                  + jnp.dot(xm_, rwh_ref[...], preferred_element_type=F32) + rb_ref[...])
        filler()
        lane = lax.broadcasted_iota(jnp.int32, (TL, LANES), 1)
        lane_f = lane.astype(F32)
        neg = jnp.float32(-jnp.inf)
        work = jnp.where(lane < N_EXPERTS, logits, neg)
        onehots, vals, idxs = [], [], []
        for j in range(TOP_K):
            m = jnp.max(work, axis=-1, keepdims=True)
            filler()
            idx = jnp.min(jnp.where(work == m, lane_f, float(LANES)), axis=-1, keepdims=True)
            oh = lane_f == idx
            onehots.append(oh)
            vals.append(m)
            idxs.append(idx)
            work = jnp.where(oh, neg, work)
            filler()
        es = [jnp.exp(v - vals[0]) for v in vals]
        den = es[0] + es[1] + es[2] + es[3]
        gates = [e / den for e in es]
        sel = jnp.zeros((TL, LANES), F32)
        for oh in onehots:
            sel = jnp.where(oh, 1.0, sel)
        rankmat = jnp.dot(trils_ref[...], sel.astype(BF16), preferred_element_type=F32)
        filler()
        cnt_ref[tile * SUBLANES:(tile + 1) * SUBLANES, :] = jnp.broadcast_to(
            jnp.sum(sel, axis=0, keepdims=True), (SUBLANES, LANES))
        meta = jnp.zeros((TL, LANES), F32)
        for j in range(TOP_K):
            rank_j = jnp.sum(jnp.where(onehots[j], rankmat, 0.0), axis=-1, keepdims=True)
            meta = jnp.where(lane == j, idxs[j], meta)
            meta = jnp.where(lane == TOP_K + j, rank_j, meta)
            meta = jnp.where(lane == 2 * TOP_K + j, gates[j], meta)
        meta_ref[base:base + TL, :] = meta

    def post_thunks(tile):
        return ([functools.partial(ssd_gate_norm, tile)]
                + [functools.partial(out_slice, tile, c0) for c0 in range(0, D_MODEL, OUT_SLICE)]
                + [functools.partial(norm_route, tile)])

    xb_s[...] = x_ref[...].astype(BF16)
    xn_s[...] = xnext_ref[...].astype(BF16)

    @pl.when(jnp.logical_and(pl.program_id(0) == 0, t == 0))
    def _():
        for run in project_slices(xb_s, 0, 0):
            run()

    for tile in range(MIX_TILES):
        base = tile * TL
        if tile + 1 < MIX_TILES:
            pending.extend(project_slices(xb_s, base + TL, base + TL))
        hg_units = hg_front(base)
        ssd_units, cs_r = ssd_front(base)
        while pending:
            filler()
        hg_back(base, hg_units)
        ssd_back(base, ssd_units, cs_r)
        if tile + 1 < MIX_TILES:
            pending.extend(post_thunks(tile))
    tail = post_thunks(MIX_TILES - 1)
    slices = project_slices(xn_s, 0, 0)
    tail[0]()
    for run in tail[1:-1]:
        run()
        slices.pop(0)()
    pending.extend(slices)
    tail[-1]()
    while pending:
        filler()
    xpad_s[0:SUBLANES, :] = xpad_s[STEP_ROWS:STEP_ROWS + SUBLANES, :]


def _full(shape):
    nd = len(shape)
    return pl.BlockSpec(shape, lambda *_: (0,) * nd)


def _mixer_call(x2d, consts, batch, seq):
    SR = STEP_ROWS
    nt = seq // SR
    T = batch * seq
    last_tile = T // TILE_ROWS - 1
    in_specs = [pl.BlockSpec((SR, D_MODEL), lambda b, t: (b * nt + t, 0)),
                pl.BlockSpec((TILE_ROWS, D_MODEL),
                             lambda b, t: (jnp.minimum((b * nt + t + 1) * MIX_TILES, last_tile), 0))]
    in_specs += [_full(c.shape) for c in consts]
    out_shape = (jax.ShapeDtypeStruct((T, D_MODEL), F32),
                 jax.ShapeDtypeStruct((T, LANES), F32),
                 jax.ShapeDtypeStruct((T // TILE_ROWS * SUBLANES, LANES), F32))
    out_specs = (pl.BlockSpec((SR, D_MODEL), lambda b, t: (b * nt + t, 0)),
                 pl.BlockSpec((SR, LANES), lambda b, t: (b * nt + t, 0)),
                 pl.BlockSpec((MIX_TILES * SUBLANES, LANES), lambda b, t: (b * nt + t, 0)))
    scratch = [
        pltpu.VMEM((SR, PROJ_COLS), F32),
        pltpu.VMEM((SR, D_MODEL), BF16),
        pltpu.VMEM((TILE_ROWS, D_MODEL), BF16),
        pltpu.VMEM((SR, HG_WIDTH), F32),
        pltpu.VMEM((SR + 2 * SUBLANES, SSD_CONV_DIM), F32),
        pltpu.VMEM((SR, SSD_WIDTH), F32),
        pltpu.VMEM((SR, SSD_WIDTH), F32),
        pltpu.VMEM((HG_HEADS, HG_HEAD_DIM, HG_HEAD_DIM), F32),
        pltpu.VMEM((SSD_GROUPS, SSD_STATE, SSD_WIDTH // SSD_GROUPS), F32),
        pltpu.VMEM((SR, D_MODEL), BF16),
        pltpu.VMEM((SR, SSD_WIDTH), F32),
        pltpu.VMEM((SR, D_MODEL), F32),
    ]
    return pl.pallas_call(
        _mixer_body,
        grid=(batch, nt),
        in_specs=in_specs,
        out_specs=out_specs,
        out_shape=out_shape,
        scratch_shapes=scratch,
        compiler_params=pltpu.CompilerParams(
            dimension_semantics=("arbitrary", "arbitrary"), vmem_limit_bytes=VMEM_LIMIT),
        name="mixer",
    )(x2d, x2d, *consts)


def _for_each_run(units_ref, tile, fn):
    def per_expert(e, carry):
        k = tile * N_EXPERTS + e
        n = units_ref[k]

        @pl.when(n > 0)
        def _():
            fn(k, n)
        return carry
    lax.fori_loop(0, N_EXPERTS, per_expert, 0)


def _rows(unit, n_units=1):
    start = unit * RUN_ALIGN
    if RUN_ALIGN > 1:
        start = pl.multiple_of(start, RUN_ALIGN)
    return pl.ds(start, n_units * RUN_ALIGN)


def _dispatch_body(units_ref, g8_ref, l8_ref, tot_ref, tail8_ref, tailn_ref, nv_ref,
                   x1_ref, meta_ref, lst_ref, xs_hbm, sorted_s, zero_s, sems, zsem, bsem):
    step = pl.program_id(0)
    n = pl.num_programs(0)
    TD, S = TILE_ROWS, SORT_ROWS
    n_blocks = xs_hbm.shape[0] // EXPERT_BLOCK

    def unused_block_copy(m):
        rows = pl.ds(pl.multiple_of(m * EXPERT_BLOCK, EXPERT_BLOCK), EXPERT_BLOCK)
        return pltpu.make_async_copy(zero_s, xs_hbm.at[rows, :], bsem)

    def run_copy(s, l_unit, g_unit, n_units):
        return pltpu.make_async_copy(sorted_s.at[s, _rows(l_unit, n_units), :],
                                     xs_hbm.at[_rows(g_unit, n_units), :], sems.at[s])

    def wait_units(s, count):
        run_copy(s, 0, 0, count).wait()

    @pl.when(step == 0)
    def _():
        zero_s[...] = jnp.zeros_like(zero_s)

        def tail_copy(e):
            n_units = tailn_ref[e]
            return pltpu.make_async_copy(zero_s.at[pl.ds(0, n_units * RUN_ALIGN), :],
                                         xs_hbm.at[_rows(tail8_ref[e], n_units), :], zsem)

        def start_unused(m, carry):
            unused_block_copy(m).start()
            return carry
        lax.fori_loop(nv_ref[0], n_blocks, start_unused, 0)

        def start_e(e, carry):
            @pl.when(tailn_ref[e] > 0)
            def _():
                tail_copy(e).start()
            return carry
        lax.fori_loop(0, N_EXPERTS, start_e, 0)

        def wait_e(e, carry):
            @pl.when(tailn_ref[e] > 0)
            def _():
                tail_copy(e).wait()
            return carry
        lax.fori_loop(0, N_EXPERTS, wait_e, 0)

    lane_f = lax.broadcasted_iota(jnp.int32, (TD, LANES), 1).astype(F32)
    ones8 = jnp.ones((SUBLANES, LANES), BF16)
    r_iota = lax.broadcasted_iota(jnp.int32, (S, TD), 0).astype(F32)
    nt_dims = (((1,), (1,)), ((), ()))
    for slot in range(ROUTE_TILES):
        _dispatch_tile(step * ROUTE_TILES + slot, slot, step, units_ref, g8_ref, l8_ref, tot_ref,
                       x1_ref, meta_ref, lst_ref, sorted_s, run_copy, wait_units,
                       lane_f, ones8, r_iota, nt_dims)

    @pl.when(step == n - 1)
    def _():
        for slot in range(ROUTE_TILES):
            wait_units(slot, tot_ref[step * ROUTE_TILES + slot])

        def wait_unused(m, carry):
            unused_block_copy(m).wait()
            return carry
        lax.fori_loop(nv_ref[0], n_blocks, wait_unused, 0)


def _dispatch_tile(tau, slot, step, units_ref, g8_ref, l8_ref, tot_ref, x1_ref, meta_ref, lst_ref, sorted_s,
                   run_copy, wait_units, lane_f, ones8, r_iota, nt_dims):
    TD = TILE_ROWS
    rows = slice(slot * TD, (slot + 1) * TD)

    @pl.when(step >= 1)
    def _():
        wait_units(slot, tot_ref[jnp.maximum(tau - ROUTE_TILES, 0)])

    meta = meta_ref[rows, :]
    u8 = lst_ref[slot * SUBLANES:(slot + 1) * SUBLANES, :].astype(BF16)
    u_row = lst_ref[slot * SUBLANES + 2:slot * SUBLANES + 3, :]
    gh = meta.astype(BF16).astype(F32)
    g1 = meta - gh
    gm = g1.astype(BF16).astype(F32)
    gl = g1 - gm
    lane_j = jnp.floor((lane_f + 0.5) * (1.0 / GATE_PARTS))
    lane_p = lane_f - GATE_PARTS * lane_j
    aux_g = jnp.zeros((TD, LANES), F32)
    aux_h = jnp.zeros((TD, LANES), F32)
    aux_l = jnp.zeros((TD, LANES), F32)
    conds = []
    for j in range(TOP_K):
        oh = lane_f == meta[:, j:j + 1]
        rank_col = meta[:, TOP_K + j:TOP_K + j + 1]
        ohb = jnp.where(oh, 1.0, 0.0).astype(BF16)
        rkb = jnp.where(oh, rank_col, 0.0).astype(BF16)
        m1 = lax.dot_general(u8, ohb, nt_dims, preferred_element_type=F32)
        m2 = lax.dot_general(ones8, rkb, nt_dims, preferred_element_type=F32)
        lpos = RUN_ALIGN * (LST_SPLIT * m1[0:1, :] + m1[1:2, :]) + m2[0:1, :]
        conds.append(r_iota == lpos)
        lpos_col = RUN_ALIGN * jnp.sum(jnp.where(oh, u_row, 0.0), axis=-1, keepdims=True) + rank_col
        hi_col = jnp.floor(lpos_col * (1.0 / LST_SPLIT))
        gc = 2 * TOP_K + j
        g_j = jnp.where(lane_p == 0, gh[:, gc:gc + 1], jnp.where(lane_p == 1, gm[:, gc:gc + 1], gl[:, gc:gc + 1]))
        aux_g = jnp.where(lane_j == j, g_j, aux_g)
        aux_h = jnp.where(lane_j == j, hi_col, aux_h)
        aux_l = jnp.where(lane_j == j, lpos_col - LST_SPLIT * hi_col, aux_l)
    perm = jnp.where(conds[0], 1.0, jnp.where(conds[1], 1.0, jnp.where(conds[2], 1.0,
                     jnp.where(conds[3], 1.0, 0.0)))).astype(BF16)
    sorted_x = jnp.dot(perm, x1_ref[rows, :].astype(BF16), preferred_element_type=F32)
    aux = jnp.concatenate([aux_g, aux_h, aux_l], axis=1).astype(BF16)
    sorted_aux = jnp.dot(perm, aux, preferred_element_type=F32)
    own_row = lax.broadcasted_iota(jnp.int32, (sorted_aux.shape[0], LANES), 0).astype(F32)
    match = LST_SPLIT * sorted_aux[:, LANES:2 * LANES] + sorted_aux[:, 2 * LANES:3 * LANES] == own_row
    sorted_g = jnp.where(match, sorted_aux[:, 0:LANES], 0.0)
    sorted_s[slot, :, 0:D_MODEL] = sorted_x
    sorted_s[slot, :, D_MODEL:XS_WIDTH] = sorted_g

    _for_each_run(units_ref, tau, lambda k, cnt: run_copy(slot, l8_ref[k], g8_ref[k], cnt).start())


def _dispatch_call(sched, x1, meta, lst_rows, cap):
    T = x1.shape[0]
    RR = ROUTE_TILES * TILE_ROWS
    grid_spec = pltpu.PrefetchScalarGridSpec(
        num_scalar_prefetch=7,
        grid=(T // RR,),
        in_specs=[
            pl.BlockSpec((RR, D_MODEL), lambda i, *_: (i, 0)),
            pl.BlockSpec((RR, LANES), lambda i, *_: (i, 0)),
            pl.BlockSpec((ROUTE_TILES * SUBLANES, LANES), lambda i, *_: (i, 0)),
        ],
        out_specs=pl.BlockSpec(memory_space=pl.ANY),
        scratch_shapes=[
            pltpu.VMEM((2, SORT_ROWS, XS_WIDTH), F32),
            pltpu.VMEM((EXPERT_BLOCK, XS_WIDTH), F32),
            pltpu.SemaphoreType.DMA((2,)),
            pltpu.SemaphoreType.DMA(()),
            pltpu.SemaphoreType.DMA(()),
        ],
    )
    return pl.pallas_call(
        _dispatch_body,
        grid_spec=grid_spec,
        out_shape=jax.ShapeDtypeStruct((cap, XS_WIDTH), F32),
        compiler_params=pltpu.CompilerParams(
            dimension_semantics=("arbitrary",), vmem_limit_bytes=VMEM_LIMIT),
        name="dispatch",
    )(*sched, x1, meta, lst_rows)


def _expert_body(ite_ref, itb_ref, sz_ref, nxt_ref, par_ref, nit_ref, nv_ref,
                 xs_hbm, wg_hbm, bg_ref, wu_hbm, bu_ref, wd_hbm, bd_ref,
                 y_hbm, xbuf, ybuf, zbuf, wbuf, xsem, ysem, zsem, wsem, wg_s, wu_s, wd_s):
    n_iter = nit_ref[0]
    n_blocks = xs_hbm.shape[0] // EXPERT_BLOCK
    sizes = tuple(n * EXPERT_BLOCK for n in CHUNK_BLOCKS)

    def rows_of(i, rows):
        return pl.ds(pl.multiple_of(itb_ref[i] * EXPERT_BLOCK, EXPERT_BLOCK), rows)

    def x_copy(i, s, rows):
        return pltpu.make_async_copy(xs_hbm.at[rows_of(i, rows), :], xbuf.at[s, pl.ds(0, rows), :], xsem.at[s])

    def y_copy(i, s, rows):
        return pltpu.make_async_copy(ybuf.at[s, pl.ds(0, rows), :], y_hbm.at[rows_of(i, rows), :], ysem.at[s])

    def by_size(i, fn):
        for k, rows in enumerate(sizes):
            @pl.when(sz_ref[i] == k)
            def _():
                fn(rows)

    def zero_copy(m):
        rows = pl.ds(pl.multiple_of(m * EXPERT_BLOCK, EXPERT_BLOCK), EXPERT_BLOCK)
        return pltpu.make_async_copy(zbuf, y_hbm.at[rows, :], zsem)

    def weight_copies(e, s):
        return [pltpu.make_async_copy(w.at[e], wbuf.at[s, k], wsem.at[s])
                for k, w in enumerate((wg_hbm, wu_hbm, wd_hbm))]

    for c in weight_copies(ite_ref[0], par_ref[0]):
        c.start()
    by_size(0, lambda rows: x_copy(0, 0, rows).start())

    zbuf[...] = jnp.zeros_like(zbuf)

    def start_zero(m, carry):
        zero_copy(m).start()
        return carry
    lax.fori_loop(nv_ref[0], n_blocks, start_zero, 0)

    def ffn(s, e, rows):
        xb = xbuf[s, 0:rows, 0:D_MODEL].astype(BF16)
        gate = jnp.sum(xbuf[s, 0:rows, D_MODEL:XS_WIDTH], axis=-1, keepdims=True)
        hg = jnp.minimum(jnp.dot(xb, wg_s[...], preferred_element_type=F32) + bg_ref[e], SWIGLU_LIMIT)
        hu = jnp.clip(jnp.dot(xb, wu_s[...], preferred_element_type=F32) + bu_ref[e],
                      -SWIGLU_LIMIT, SWIGLU_LIMIT)
        hact = (hu + 1.0) * (hg * _sigmoid(SWIGLU_ALPHA * hg))
        y = jnp.dot(hact.astype(BF16), wd_s[...], preferred_element_type=F32) + bd_ref[e]
        ybuf[s, 0:rows, :] = y * gate

    def body(i, carry):
        s = lax.rem(i, 2)
        e = ite_ref[i]
        by_size(i, lambda rows: x_copy(i, s, rows).wait())

        @pl.when(i + 1 < n_iter)
        def _():
            by_size(i + 1, lambda rows: x_copy(i + 1, 1 - s, rows).start())

        @pl.when(jnp.logical_or(i == 0, e != ite_ref[jnp.maximum(i - 1, 0)]))
        def _():
            ws = par_ref[i]
            for c in weight_copies(e, ws):
                c.wait()

            @pl.when(nxt_ref[i] != e)
            def _():
                for c in weight_copies(nxt_ref[i], 1 - ws):
                    c.start()
            wg_s[...] = wbuf[ws, 0].astype(BF16)
            wu_s[...] = wbuf[ws, 1].astype(BF16)
            wd_s[...] = wbuf[ws, 2].astype(BF16)

        @pl.when(i >= 2)
        def _():
            by_size(i - 2, lambda rows: y_copy(i - 2, s, rows).wait())

        def chunk(rows):
            ffn(s, e, rows)
            y_copy(i, s, rows).start()
        by_size(i, chunk)
        return carry
    lax.fori_loop(0, n_iter, body, 0)

    @pl.when(n_iter >= 2)
    def _():
        by_size(n_iter - 2, lambda rows: y_copy(n_iter - 2, lax.rem(n_iter, 2), rows).wait())
    by_size(n_iter - 1, lambda rows: y_copy(n_iter - 1, lax.rem(n_iter - 1, 2), rows).wait())

    def wait_zero(m, carry):
        zero_copy(m).wait()
        return carry
    lax.fori_loop(nv_ref[0], n_blocks, wait_zero, 0)


def _expert_call(sched, xs, w_gate, b_gate, w_up, b_up, w_down, b_down):
    n_blocks = xs.shape[0] // EXPERT_BLOCK
    anyspec = pl.BlockSpec(memory_space=pl.ANY)
    bspec = pl.BlockSpec((N_EXPERTS, 1, D_MODEL), lambda i, *_: (0, 0, 0))
    grid_spec = pltpu.PrefetchScalarGridSpec(
        num_scalar_prefetch=len(sched),
        grid=(1,),
        in_specs=[anyspec, anyspec, bspec, anyspec, bspec, anyspec, bspec],
        out_specs=anyspec,
        scratch_shapes=[
            pltpu.VMEM((2, max(CHUNK_BLOCKS) * EXPERT_BLOCK, XS_WIDTH), F32),
            pltpu.VMEM((2, max(CHUNK_BLOCKS) * EXPERT_BLOCK, D_MODEL), F32),
            pltpu.VMEM((EXPERT_BLOCK, D_MODEL), F32),
            pltpu.VMEM((2, 3, D_MODEL, D_MODEL), F32),
            pltpu.SemaphoreType.DMA((2,)),
            pltpu.SemaphoreType.DMA((2,)),
            pltpu.SemaphoreType.DMA(()),
            pltpu.SemaphoreType.DMA((2,)),
            pltpu.VMEM((D_MODEL, D_MODEL), BF16),
            pltpu.VMEM((D_MODEL, D_MODEL), BF16),
            pltpu.VMEM((D_MODEL, D_MODEL), BF16),
        ],
    )
    return pl.pallas_call(
        _expert_body,
        grid_spec=grid_spec,
        out_shape=jax.ShapeDtypeStruct((n_blocks * EXPERT_BLOCK, D_MODEL), F32),
        compiler_params=pltpu.CompilerParams(
            dimension_semantics=("arbitrary",), vmem_limit_bytes=VMEM_LIMIT),
        name="experts",
    )(*sched, xs, w_gate, b_gate.reshape(N_EXPERTS, 1, D_MODEL),
      w_up, b_up.reshape(N_EXPERTS, 1, D_MODEL), w_down, b_down.reshape(N_EXPERTS, 1, D_MODEL))


def _combine_body(units_ref, g8_ref, l8_ref, tot_ref,
                  y_hbm, x1_ref, meta_ref, lst_ref, g_ref, b_ref, out_ref, ys_s, sems):
    step = pl.program_id(0)
    n_tiles = pl.num_programs(0) * ROUTE_TILES
    TD, S = TILE_ROWS, SORT_ROWS

    def run_copy(s, g_unit, l_unit, n_units):
        return pltpu.make_async_copy(y_hbm.at[_rows(g_unit, n_units), :],
                                     ys_s.at[s, _rows(l_unit, n_units), :], sems.at[s])

    def fetch(tile, s):
        _for_each_run(units_ref, tile, lambda k, cnt: run_copy(s, g8_ref[k], l8_ref[k], cnt).start())

    @pl.when(step == 0)
    def _():
        ys_s[...] = jnp.zeros_like(ys_s)
        fetch(0, 0)

    lane_f = lax.broadcasted_iota(jnp.int32, (TD, LANES), 1).astype(F32)
    s_iota = lax.broadcasted_iota(jnp.int32, (TD, S), 1).astype(F32)
    for h in range(ROUTE_TILES):
        tau = step * ROUTE_TILES + h
        slot = h % 2
        rows = slice(h * TD, (h + 1) * TD)

        @pl.when(tau + 1 < n_tiles)
        def _():
            fetch(tau + 1, 1 - slot)

        run_copy(slot, 0, 0, tot_ref[tau]).wait()

        meta = meta_ref[rows, :]
        u_row = lst_ref[h * SUBLANES + 2:h * SUBLANES + 3, :]
        conds = []
        for j in range(TOP_K):
            oh = lane_f == meta[:, j:j + 1]
            start8 = jnp.sum(jnp.where(oh, u_row, 0.0), axis=-1, keepdims=True)
            lpos = RUN_ALIGN * start8 + meta[:, TOP_K + j:TOP_K + j + 1]
            conds.append(s_iota == lpos)
        perm = jnp.where(conds[0], 1.0, jnp.where(conds[1], 1.0, jnp.where(conds[2], 1.0,
                         jnp.where(conds[3], 1.0, 0.0)))).astype(BF16)
        ffn = jnp.dot(perm, ys_s[slot].astype(BF16), preferred_element_type=F32)
        acc = DEEPNORM_ALPHA * x1_ref[rows, :] + ffn
        mu = jnp.mean(acc, axis=-1, keepdims=True)
        hc = acc - mu
        var = jnp.mean(hc * hc, axis=-1, keepdims=True)
        out_ref[rows, :] = hc * lax.rsqrt(var + LN_EPS) * g_ref[...] + b_ref[...]


def _combine_call(sched, y_rows, x1, meta, lst_rows, ln2_g, ln2_b):
    T = x1.shape[0]
    RR = ROUTE_TILES * TILE_ROWS
    grid_spec = pltpu.PrefetchScalarGridSpec(
        num_scalar_prefetch=4,
        grid=(T // RR,),
        in_specs=[
            pl.BlockSpec(memory_space=pl.ANY),
            pl.BlockSpec((RR, D_MODEL), lambda i, *_: (i, 0)),
            pl.BlockSpec((RR, LANES), lambda i, *_: (i, 0)),
            pl.BlockSpec((ROUTE_TILES * SUBLANES, LANES), lambda i, *_: (i, 0)),
            pl.BlockSpec((1, D_MODEL), lambda i, *_: (0, 0)),
            pl.BlockSpec((1, D_MODEL), lambda i, *_: (0, 0)),
        ],
        out_specs=pl.BlockSpec((RR, D_MODEL), lambda i, *_: (i, 0)),
        scratch_shapes=[pltpu.VMEM((2, SORT_ROWS, D_MODEL), F32), pltpu.SemaphoreType.DMA((2,))],
    )
    return pl.pallas_call(
        _combine_body,
        grid_spec=grid_spec,
        out_shape=jax.ShapeDtypeStruct((T, D_MODEL), F32),
        compiler_params=pltpu.CompilerParams(
            dimension_semantics=("arbitrary",), vmem_limit_bytes=VMEM_LIMIT),
        name="combine",
    )(*sched, y_rows, x1, meta, lst_rows, ln2_g, ln2_b)


def _np_consts():
    TL = TILE_ROWS
    r = np.arange(TL)
    same = (r[:, None] // CHUNK) == (r[None, :] // CHUNK)
    tril = (same & (r[None, :] <= r[:, None])).astype(np.float32)
    trils = (r[None, :] < r[:, None]).astype(np.float32)
    e128 = np.zeros((LANES, SSD_WIDTH), np.float32)
    for h in range(SSD_HEADS):
        e128[h, h * SSD_HEAD_DIM:(h + 1) * SSD_HEAD_DIM] = 1.0
    return tril, trils, e128


def kernel(x, w_in, hg_lower_bound, hg_norm_w, conv_w, conv_b, dt_bias, a_log, d_skip, ssd_norm_w, w_out,
           ln1_g, ln1_b, router_w, router_b, w_gate, b_gate, w_up, b_up, w_down, b_down, ln2_g, ln2_b):
    batch, seq, d = x.shape
    assert d == D_MODEL and seq % STEP_ROWS == 0 and w_in.shape[0] == DEPTH
    assert ROUTE_TILES == 2 and (batch * seq) % (ROUTE_TILES * TILE_ROWS) == 0
    T = batch * seq
    n_tiles = T // TILE_ROWS
    max_rows = T * TOP_K + n_tiles * N_EXPERTS * (RUN_ALIGN - 1)
    n_blocks = -(-max_rows // EXPERT_BLOCK) + N_EXPERTS
    cap = n_blocks * EXPERT_BLOCK
    assert SORT_ROWS >= TILE_ROWS * TOP_K + N_EXPERTS * (RUN_ALIGN - 1)

    tril, trils, e128 = _np_consts()
    w = w_in[0]
    pad_l = LANES - SSD_HEADS
    rw = jnp.pad(router_w[0], ((0, 0), (0, LANES - N_EXPERTS)))
    rwh = rw.astype(BF16)
    rwm = (rw - rwh.astype(F32)).astype(BF16)
    consts = [
        jnp.pad(w.astype(BF16), ((0, 0), (0, pad_l))),
        hg_lower_bound,
        hg_norm_w[0].reshape(1, HG_HEAD_DIM),
        conv_w[0],
        conv_b[0].reshape(1, SSD_CONV_DIM),
        jnp.pad(dt_bias[0], (0, pad_l)).reshape(1, LANES),
        jnp.pad(a_log[0], (0, pad_l)).reshape(1, LANES),
        jnp.repeat(d_skip[0], SSD_HEAD_DIM).reshape(1, SSD_WIDTH),
        ssd_norm_w[0].reshape(1, SSD_WIDTH),
        w_out[0].astype(BF16),
        ln1_g[0].reshape(1, D_MODEL),
        ln1_b[0].reshape(1, D_MODEL),
        jnp.concatenate([rwh, rwm], axis=1),
        rwh,
        jnp.pad(router_b[0], (0, LANES - N_EXPERTS)).reshape(1, LANES),
        jnp.asarray(tril, BF16), jnp.asarray(trils, BF16), jnp.asarray(e128, BF16),
    ]
    x1, meta, cnt = _mixer_call(x.reshape(T, D_MODEL), consts, batch, seq)

    counts = cnt.reshape(n_tiles, SUBLANES, LANES)[:, 0, :N_EXPERTS].astype(jnp.int32)
    c8 = (counts + RUN_ALIGN - 1) // RUN_ALIGN * RUN_ALIGN
    used = jnp.sum(c8, axis=0)
    region = (used + EXPERT_BLOCK - 1) // EXPERT_BLOCK * EXPERT_BLOCK
    region_end = jnp.cumsum(region)
    region_start = region_end - region
    gstart = region_start[None, :] + jnp.cumsum(c8, axis=0) - c8
    lstart = jnp.cumsum(c8, axis=1) - c8
    n_valid = region_end[-1] // EXPERT_BLOCK
    has = region > 0
    eidx = jnp.arange(N_EXPERTS, dtype=jnp.int32)
    suffix_min = lax.cummin(jnp.where(has, eidx, N_EXPERTS), reverse=True)
    nxt_e = jnp.concatenate([suffix_min[1:], jnp.full((1,), N_EXPERTS, jnp.int32)])
    nxt_e = jnp.where(nxt_e == N_EXPERTS, eidx, nxt_e)
    par_e = (jnp.cumsum(has.astype(jnp.int32)) - 1) % 2
    left = region // EXPERT_BLOCK
    n_of = []
    for nb in CHUNK_BLOCKS:
        n_of.append(left // nb)
        left = left % nb
    it_cnt = sum(n_of)
    it_end = jnp.cumsum(it_cnt)
    n_iter = it_end[-1]
    max_iter = n_blocks // max(CHUNK_BLOCKS) + (len(CHUNK_BLOCKS) - 1) * N_EXPERTS
    it = jnp.minimum(jnp.arange(max_iter, dtype=jnp.int32), n_iter - 1)
    it_e = jnp.minimum(jnp.sum(it_end[None, :] <= it[:, None], axis=1), N_EXPERTS - 1).astype(jnp.int32)
    it_onehot = it_e[:, None] == eidx[None, :]
    lookup = lambda tab: jnp.sum(jnp.where(it_onehot, tab[None, :].astype(jnp.int32), 0), axis=1)
    k = it - lookup(it_end - it_cnt)
    it_sz = jnp.zeros_like(it)
    it_blk = lookup(region_start // EXPERT_BLOCK)
    for idx, nb in enumerate(CHUNK_BLOCKS):
        n_here = lookup(n_of[idx])
        inside = jnp.logical_and(k >= 0, k < n_here)
        it_sz = jnp.where(inside, idx, it_sz)
        it_blk = it_blk + nb * jnp.clip(k, 0, n_here)
        k = k - n_here
    expert_sched = (it_e, it_blk.astype(jnp.int32), it_sz.astype(jnp.int32), lookup(nxt_e).astype(jnp.int32),
                    lookup(par_e).astype(jnp.int32),
                    n_iter.astype(jnp.int32).reshape(1), n_valid.astype(jnp.int32).reshape(1))
    as_units = lambda a: (a // RUN_ALIGN).astype(jnp.int32).reshape(-1)
    units, g8, l8 = as_units(c8), as_units(gstart), as_units(lstart)
    tot = (jnp.sum(c8, axis=1) // RUN_ALIGN).astype(jnp.int32)
    tail8 = as_units(region_start + used)
    tailn = as_units(region - used)
    lu = lstart // RUN_ALIGN
    lst3 = jnp.stack([lu // LST_SPLIT, lu % LST_SPLIT, lu], axis=1).astype(F32)
    lst_rows = jnp.pad(lst3, ((0, 0), (0, SUBLANES - 3), (0, LANES - N_EXPERTS))).reshape(
        n_tiles * SUBLANES, LANES)

    nv = n_valid.astype(jnp.int32).reshape(1)
    xs = _dispatch_call((units, g8, l8, tot, tail8, tailn, nv), x1, meta, lst_rows, cap)
    y_rows = _expert_call(expert_sched, xs, w_gate[0], b_gate[0], w_up[0], b_up[0], w_down[0], b_down[0])
    out = _combine_call((units, g8, l8, tot), y_rows, x1, meta, lst_rows,
                        ln2_g[0].reshape(1, D_MODEL), ln2_b[0].reshape(1, D_MODEL))
    return out.reshape(batch, seq, D_MODEL)
```

```python
import functools

import jax
import jax.numpy as jnp
import numpy as np
from jax import lax
from jax.experimental import pallas as pl
from jax.experimental.pallas import tpu as pltpu

F32 = jnp.float32
BF16 = jnp.bfloat16

D_MODEL = 1024
CHUNK = 64
HG_WIDTH = 512
HG_HEAD_DIM = 128
HG_HEADS = 4
SSD_WIDTH = 512
SSD_HEAD_DIM = 64
SSD_HEADS = 8
SSD_GROUPS = 2
SSD_STATE = 128
SSD_CONV = 4
SSD_CONV_DIM = SSD_WIDTH + 2 * SSD_GROUPS * SSD_STATE
N_EXPERTS = 32
TOP_K = 4
EXPERT_BLOCK = 128
CHUNK_BLOCKS = (4, 2, 1)
SWIGLU_LIMIT = 7.0
SWIGLU_ALPHA = 1.702
DEPTH = 1
DEEPNORM_ALPHA = (2 * DEPTH) ** 0.25
LN_EPS = 1e-5
RMS_EPS = 1e-5

LANES = 128
SUBLANES = 8
SUB_CHUNK = 16
EXP_CAP = 60.0
TILE_ROWS = 256
MIX_TILES = 2
ROUTE_TILES = 2
STEP_ROWS = MIX_TILES * TILE_ROWS
RUN_ALIGN = SUBLANES
SORT_ROWS = 1280
LST_SPLIT = 32
GATE_PARTS = 3
XS_HALF = D_MODEL // 2
XS_WIDTH = XS_HALF + LANES
VMEM_LIMIT = 56 * 1024 * 1024

OFF_Q, OFF_F, OFF_I, OFF_G = 0, 512, 1024, 1536
OFF_Z, OFF_XBC = 2048, 2560
OFF_XS, OFF_B, OFF_C = 2560, 3072, 3328
OFF_DT = 3584
PROJ_COLS = OFF_DT + LANES
PROJ_SLICE = 256
CONV_SLICE = 256
OUT_SLICE = 256


def _bdot(a, b):
    return jnp.dot(a.astype(BF16), b.astype(BF16), preferred_element_type=F32)


def _bdot_nt(a, b):
    return lax.dot_general(a.astype(BF16), b.astype(BF16), (((1,), (1,)), ((), ())),
                           preferred_element_type=F32)


def _bdot_tn(a, b):
    return lax.dot_general(a.astype(BF16), b.astype(BF16), (((0,), (0,)), ((), ())),
                           preferred_element_type=F32)


def _split3(a):
    hi = a.astype(BF16)
    r1 = a - hi.astype(F32)
    mid = r1.astype(BF16)
    lo = (r1 - mid.astype(F32)).astype(BF16)
    return hi, mid, lo


def _dot01_left(m01, a):
    hi, mid, lo = _split3(a)
    d = functools.partial(jnp.dot, m01, preferred_element_type=F32)
    return d(hi) + d(mid) + d(lo)


def _dot01_right(a, m01):
    hi, mid, lo = _split3(a)
    return (jnp.dot(hi, m01, preferred_element_type=F32) + jnp.dot(mid, m01, preferred_element_type=F32)
            + jnp.dot(lo, m01, preferred_element_type=F32))


def _sigmoid(x):
    return 1.0 / (1.0 + jnp.exp(-x))


def _silu(x):
    return x * _sigmoid(x)


def _softplus(x):
    return jnp.maximum(x, 0.0) + jnp.log(1.0 + jnp.exp(-jnp.abs(x)))


def _mixer_body(x_ref, xnext_ref, win_ref, lbp_ref, hgnw_ref, convw_ref, convb_ref, dtb_ref, alog_ref, dskip_ref,
                ssdnw_ref, wout_ref, ln1g_ref, ln1b_ref, rw1_ref, rwh_ref, rb_ref,
                tril_ref, trils_ref, e128_ref,
                x1_ref, meta_ref, cnt_ref,
                proj_s, xb_s, xn_s, b_s, xpad_s, xdt_s, cse_s, st_s, pt_s, cat_s, ossd_s, mix_s):
    TL = TILE_ROWS
    t = pl.program_id(1)

    @pl.when(t == 0)
    def _():
        xpad_s[0:SUBLANES, :] = jnp.zeros((SUBLANES, SSD_CONV_DIM), F32)
        st_s[...] = jnp.zeros_like(st_s)
        pt_s[...] = jnp.zeros_like(pt_s)

    row64 = lax.broadcasted_iota(jnp.int32, (CHUNK, CHUNK), 0)
    col64 = lax.broadcasted_iota(jnp.int32, (CHUNK, CHUNK), 1)
    causal = row64 >= col64
    gw = SSD_WIDTH // SSD_GROUPS
    hpg = SSD_HEADS // SSD_GROUPS
    lane_head = lax.broadcasted_iota(jnp.int32, (CHUNK, gw), 1) // SSD_HEAD_DIM

    pending = []

    def filler():
        if pending:
            pending.pop(0)()

    def project_slices(lhs_ref, lhs_base, base):
        def one(c0, c1):
            def run():
                proj_s[base:base + TL, c0:c1] = jnp.dot(lhs_ref[lhs_base:lhs_base + TL, :], win_ref[:, c0:c1],
                                                        preferred_element_type=F32)
            return run
        return [one(c0, min(c0 + PROJ_SLICE, PROJ_COLS)) for c0 in range(0, PROJ_COLS, PROJ_SLICE)]

    def hg_front(base):
        a0 = lbp_ref[0:1, :]
        a1 = lbp_ref[1:2, :]
        am = jnp.maximum(a0, a1)
        e0 = jnp.exp(a0 - am)
        e1 = jnp.exp(a1 - am)
        lb = e0 / (e0 + e1)
        f = lb + (1.0 - lb) * _sigmoid(proj_s[base:base + TL, OFF_F:OFF_F + HG_WIDTH])
        b_s[base:base + TL, :] = _dot01_left(tril_ref[...], jnp.log(f))
        proj_s[base:base + TL, OFF_F:OFF_F + HG_WIDTH] = f
        proj_s[base:base + TL, OFF_Q:OFF_Q + HG_WIDTH] = _silu(proj_s[base:base + TL, OFF_Q:OFF_Q + HG_WIDTH])
        units = {}
        for c in range(TL // CHUNK):
            r0 = base + c * CHUNK
            for h in range(HG_HEADS):
                h0 = h * HG_HEAD_DIM
                bc = b_s[r0:r0 + CHUNK, h0:h0 + HG_HEAD_DIM]
                qc = proj_s[r0:r0 + CHUNK, OFF_Q + h0:OFF_Q + h0 + HG_HEAD_DIM]
                kc = 1.0 - proj_s[r0:r0 + CHUNK, OFF_F + h0:OFF_F + h0 + HG_HEAD_DIM]
                vcb = proj_s[r0:r0 + CHUNK, OFF_I + h0:OFF_I + h0 + HG_HEAD_DIM].astype(BF16)
                parts = []
                for i in range(CHUNK // SUB_CHUNK):
                    s0 = i * SUB_CHUNK
                    if i == 0:
                        qi = qc[0:SUB_CHUNK] * jnp.exp(bc[0:SUB_CHUNK])
                        ki = kc * jnp.exp(jnp.minimum(-bc, EXP_CAP))
                    else:
                        ref_i = bc[s0 - 1:s0, :]
                        qi = qc[s0:s0 + SUB_CHUNK] * jnp.exp(bc[s0:s0 + SUB_CHUNK] - ref_i)
                        ki = kc * jnp.exp(jnp.minimum(ref_i - bc, EXP_CAP))
                    parts.append(_bdot_nt(qi, ki))
                b_end = bc[CHUNK - 1:CHUNK, :]
                kdec = kc * jnp.exp(b_end - bc)
                units[c, h] = dict(
                    parts=parts, vcb=vcb, qdec=(qc * jnp.exp(bc)).astype(BF16),
                    local=_bdot_tn(vcb, kdec),
                    decay=jnp.exp(b_end))
                if h % 2 == 1:
                    filler()
        return units

    def hg_back(base, units):
        hgnw = hgnw_ref[...]
        for h in range(HG_HEADS):
            st = st_s[h]
            for c in range(TL // CHUNK):
                u = units[c, h]
                u['state'] = st
                st = st * u['decay'] + u['local']
            st_s[h] = st
        for c in range(TL // CHUNK):
            r0 = base + c * CHUNK
            for h in range(HG_HEADS):
                h0 = h * HG_HEAD_DIM
                u = units[c, h]
                sc = jnp.where(causal, jnp.concatenate(u['parts'], axis=0), 0.0)
                o = _bdot(sc, u['vcb']) + _bdot_nt(u['qdec'], u['state'])
                ms = jnp.mean(o * o, axis=-1, keepdims=True)
                on = o * lax.rsqrt(ms + RMS_EPS) * hgnw
                gc = proj_s[r0:r0 + CHUNK, OFF_G + h0:OFF_G + h0 + HG_HEAD_DIM]
                cat_s[r0:r0 + CHUNK, h0:h0 + HG_HEAD_DIM] = (on * _silu(gc)).astype(BF16)
                filler()

    def ssd_front(base):
        e128 = e128_ref[...]
        dtc = _softplus(proj_s[base:base + TL, OFF_DT:OFF_DT + LANES] + dtb_ref[...])
        a_row = -jnp.exp(alog_ref[...])
        cs_c = _dot01_left(tril_ref[...], dtc * a_row)
        cs_r = jnp.transpose(cs_c)[0:SSD_HEADS, :]
        cse_s[base:base + TL, :] = _dot01_right(cs_c, e128)
        dt_exp = _dot01_right(dtc, e128)

        xpad_s[SUBLANES + base:SUBLANES + base + TL, :] = proj_s[base:base + TL, OFF_XBC:OFF_XBC + SSD_CONV_DIM]
        for c0 in range(0, SSD_CONV_DIM, CONV_SLICE):
            acc = jnp.broadcast_to(convb_ref[:, c0:c0 + CONV_SLICE], (TL, CONV_SLICE))
            for j in range(SSD_CONV):
                off = base + SUBLANES - (SSD_CONV - 1) + j
                acc = acc + convw_ref[j:j + 1, c0:c0 + CONV_SLICE] * xpad_s[off:off + TL, c0:c0 + CONV_SLICE]
            proj_s[base:base + TL, OFF_XBC + c0:OFF_XBC + c0 + CONV_SLICE] = _silu(acc)
            filler()
        xdt_s[base:base + TL, :] = proj_s[base:base + TL, OFF_XS:OFF_XS + SSD_WIDTH] * dt_exp

        units = {}
        for c in range(TL // CHUNK):
            r0 = base + c * CHUNK
            for g in range(SSD_GROUPS):
                g0 = g * gw
                bgb = proj_s[r0:r0 + CHUNK, OFF_B + g * SSD_STATE:OFF_B + (g + 1) * SSD_STATE].astype(BF16)
                cgb = proj_s[r0:r0 + CHUNK, OFF_C + g * SSD_STATE:OFF_C + (g + 1) * SSD_STATE].astype(BF16)
                cse_g = cse_s[r0:r0 + CHUNK, g0:g0 + gw]
                cs_end = cse_s[r0 + CHUNK - 1:r0 + CHUNK, g0:g0 + gw]
                xdt_g = xdt_s[r0:r0 + CHUNK, g0:g0 + gw]
                units[c, g] = dict(
                    cgb=cgb, gm=_bdot_nt(cgb, bgb),
                    local=_bdot_tn(bgb, xdt_g * jnp.exp(cs_end - cse_g)),
                    decay=jnp.exp(cs_end))
                filler()
        return units, cs_r

    def ssd_back(base, units, cs_r):
        for g in range(SSD_GROUPS):
            pt = pt_s[g]
            for c in range(TL // CHUNK):
                u = units[c, g]
                u['state'] = pt
                pt = pt * u['decay'] + u['local']
            pt_s[g] = pt
        for c in range(TL // CHUNK):
            r0 = base + c * CHUNK
            for g in range(SSD_GROUPS):
                g0 = g * gw
                u = units[c, g]
                cse_g = cse_s[r0:r0 + CHUNK, g0:g0 + gw]
                xdt_g = xdt_s[r0:r0 + CHUNK, g0:g0 + gw]
                ydiag = jnp.zeros((CHUNK, gw), F32)
                for hl in range(hpg):
                    hh = g * hpg + hl
                    seg = (cse_g[:, hl * SSD_HEAD_DIM:(hl + 1) * SSD_HEAD_DIM]
                           - cs_r[hh:hh + 1, c * CHUNK:(c + 1) * CHUNK])
                    lm = jnp.where(causal, jnp.exp(jnp.minimum(seg, 0.0)), 0.0)
                    xm = jnp.where(lane_head == hl, xdt_g, 0.0)
                    ydiag = ydiag + _bdot(u['gm'] * lm, xm)
                yoff = _bdot(u['cgb'], u['state']) * jnp.exp(cse_g)
                xs_g = proj_s[r0:r0 + CHUNK, OFF_XS + g0:OFF_XS + g0 + gw]
                ossd_s[r0:r0 + CHUNK, g0:g0 + gw] = ydiag + yoff + xs_g * dskip_ref[:, g0:g0 + gw]
                filler()

    def ssd_gate_norm(tile):
        base = tile * TL
        y = ossd_s[base:base + TL, :] * _silu(proj_s[base:base + TL, OFF_Z:OFF_Z + SSD_WIDTH])
        for g in range(SSD_GROUPS):
            yg = y[:, g * gw:(g + 1) * gw]
            ms = jnp.mean(yg * yg, axis=-1, keepdims=True)
            yn = yg * lax.rsqrt(ms + RMS_EPS) * ssdnw_ref[:, g * gw:(g + 1) * gw]
            cat_s[base:base + TL, HG_WIDTH + g * gw:HG_WIDTH + (g + 1) * gw] = yn.astype(BF16)

    def out_slice(tile, c0):
        base = tile * TL
        mix_s[base:base + TL, c0:c0 + OUT_SLICE] = jnp.dot(
            cat_s[base:base + TL, :], wout_ref[:, c0:c0 + OUT_SLICE], preferred_element_type=F32)

    def norm_route(tile):
        base = tile * TL
        hres = DEEPNORM_ALPHA * x_ref[base:base + TL, :] + mix_s[base:base + TL, :]
        mu = jnp.mean(hres, axis=-1, keepdims=True)
        hc = hres - mu
        var = jnp.mean(hc * hc, axis=-1, keepdims=True)
        x1 = hc * lax.rsqrt(var + LN_EPS) * ln1g_ref[...] + ln1b_ref[...]
        x1_ref[base:base + TL, :] = x1
        filler()

        xh = x1.astype(BF16)
        xm_ = (x1 - xh.astype(F32)).astype(BF16)
        t1 = jnp.dot(xh, rw1_ref[...], preferred_element_type=F32)
        logits = (t1[:, 0:LANES] + t1[:, LANES:2 * LANES]
                  + jnp.dot(xm_, rwh_ref[...], preferred_element_type=F32) + rb_ref[...])
        filler()
        lane = lax.broadcasted_iota(jnp.int32, (TL, LANES), 1)
        lane_f = lane.astype(F32)
        neg = jnp.float32(-jnp.inf)
        work = jnp.where(lane < N_EXPERTS, logits, neg)
        onehots, vals, idxs = [], [], []
        for j in range(TOP_K):
            m = jnp.max(work, axis=-1, keepdims=True)
            filler()
            idx = jnp.min(jnp.where(work == m, lane_f, float(LANES)), axis=-1, keepdims=True)
            oh = lane_f == idx
            onehots.append(oh)
            vals.append(m)
            idxs.append(idx)
            work = jnp.where(oh, neg, work)
            filler()
        es = [jnp.exp(v - vals[0]) for v in vals]
        den = es[0] + es[1] + es[2] + es[3]
        gates = [e / den for e in es]
        sel = jnp.zeros((TL, LANES), F32)
        for oh in onehots:
            sel = jnp.where(oh, 1.0, sel)
        rankmat = jnp.dot(trils_ref[...], sel.astype(BF16), preferred_element_type=F32)
        filler()
        cnt_ref[tile * SUBLANES:(tile + 1) * SUBLANES, :] = jnp.broadcast_to(
            jnp.sum(sel, axis=0, keepdims=True), (SUBLANES, LANES))
        meta = jnp.zeros((TL, LANES), F32)
        for j in range(TOP_K):
            rank_j = jnp.sum(jnp.where(onehots[j], rankmat, 0.0), axis=-1, keepdims=True)
            meta = jnp.where(lane == j, idxs[j], meta)
            meta = jnp.where(lane == TOP_K + j, rank_j, meta)
            meta = jnp.where(lane == 2 * TOP_K + j, gates[j], meta)
        meta_ref[base:base + TL, :] = meta

    def post_thunks(tile):
        return ([functools.partial(ssd_gate_norm, tile)]
                + [functools.partial(out_slice, tile, c0) for c0 in range(0, D_MODEL, OUT_SLICE)]
                + [functools.partial(norm_route, tile)])

    xb_s[...] = x_ref[...].astype(BF16)
    xn_s[...] = xnext_ref[...].astype(BF16)

    @pl.when(jnp.logical_and(pl.program_id(0) == 0, t == 0))
    def _():
        for run in project_slices(xb_s, 0, 0):
            run()

    for tile in range(MIX_TILES):
        base = tile * TL
        if tile + 1 < MIX_TILES:
            pending.extend(project_slices(xb_s, base + TL, base + TL))
        hg_units = hg_front(base)
        ssd_units, cs_r = ssd_front(base)
        while pending:
            filler()
        hg_back(base, hg_units)
        ssd_back(base, ssd_units, cs_r)
        if tile + 1 < MIX_TILES:
            pending.extend(post_thunks(tile))
    tail = post_thunks(MIX_TILES - 1)
    slices = project_slices(xn_s, 0, 0)
    tail[0]()
    for run in tail[1:-1]:
        run()
        slices.pop(0)()
    pending.extend(slices)
    tail[-1]()
    while pending:
        filler()
    xpad_s[0:SUBLANES, :] = xpad_s[STEP_ROWS:STEP_ROWS + SUBLANES, :]


def _full(shape):
    nd = len(shape)
    return pl.BlockSpec(shape, lambda *_: (0,) * nd)


def _mixer_call(x2d, consts, batch, seq):
    SR = STEP_ROWS
    nt = seq // SR
    T = batch * seq
    last_tile = T // TILE_ROWS - 1
    in_specs = [pl.BlockSpec((SR, D_MODEL), lambda b, t: (b * nt + t, 0)),
                pl.BlockSpec((TILE_ROWS, D_MODEL),
                             lambda b, t: (jnp.minimum((b * nt + t + 1) * MIX_TILES, last_tile), 0))]
    in_specs += [_full(c.shape) for c in consts]
    out_shape = (jax.ShapeDtypeStruct((T, D_MODEL), F32),
                 jax.ShapeDtypeStruct((T, LANES), F32),
                 jax.ShapeDtypeStruct((T // TILE_ROWS * SUBLANES, LANES), F32))
    out_specs = (pl.BlockSpec((SR, D_MODEL), lambda b, t: (b * nt + t, 0)),
                 pl.BlockSpec((SR, LANES), lambda b, t: (b * nt + t, 0)),
                 pl.BlockSpec((MIX_TILES * SUBLANES, LANES), lambda b, t: (b * nt + t, 0)))
    scratch = [
        pltpu.VMEM((SR, PROJ_COLS), F32),
        pltpu.VMEM((SR, D_MODEL), BF16),
        pltpu.VMEM((TILE_ROWS, D_MODEL), BF16),
        pltpu.VMEM((SR, HG_WIDTH), F32),
        pltpu.VMEM((SR + 2 * SUBLANES, SSD_CONV_DIM), F32),
        pltpu.VMEM((SR, SSD_WIDTH), F32),
        pltpu.VMEM((SR, SSD_WIDTH), F32),
        pltpu.VMEM((HG_HEADS, HG_HEAD_DIM, HG_HEAD_DIM), F32),
        pltpu.VMEM((SSD_GROUPS, SSD_STATE, SSD_WIDTH // SSD_GROUPS), F32),
        pltpu.VMEM((SR, D_MODEL), BF16),
        pltpu.VMEM((SR, SSD_WIDTH), F32),
        pltpu.VMEM((SR, D_MODEL), F32),
    ]
    return pl.pallas_call(
        _mixer_body,
        grid=(batch, nt),
        in_specs=in_specs,
        out_specs=out_specs,
        out_shape=out_shape,
        scratch_shapes=scratch,
        compiler_params=pltpu.CompilerParams(
            dimension_semantics=("arbitrary", "arbitrary"), vmem_limit_bytes=VMEM_LIMIT),
        name="mixer",
    )(x2d, x2d, *consts)


def _for_each_run(units_ref, tile, fn):
    def per_expert(e, carry):
        k = tile * N_EXPERTS + e
        n = units_ref[k]

        @pl.when(n > 0)
        def _():
            fn(k, n)
        return carry
    lax.fori_loop(0, N_EXPERTS, per_expert, 0)


def _rows(unit, n_units=1):
    start = unit * RUN_ALIGN
    if RUN_ALIGN > 1:
        start = pl.multiple_of(start, RUN_ALIGN)
    return pl.ds(start, n_units * RUN_ALIGN)


def _dispatch_body(units_ref, g8_ref, l8_ref, tot_ref, tail8_ref, tailn_ref, nv_ref,
                   x1_ref, meta_ref, lst_ref, xs_hbm, sorted_s, zero_s, sems, zsem, bsem):
    step = pl.program_id(0)
    n = pl.num_programs(0)
    TD, S = TILE_ROWS, SORT_ROWS
    n_blocks = xs_hbm.shape[0] // EXPERT_BLOCK

    def unused_block_copy(m):
        rows = pl.ds(pl.multiple_of(m * EXPERT_BLOCK, EXPERT_BLOCK), EXPERT_BLOCK)
        return pltpu.make_async_copy(zero_s, xs_hbm.at[rows, :], bsem)

    def run_copy(s, l_unit, g_unit, n_units):
        return pltpu.make_async_copy(sorted_s.at[s, _rows(l_unit, n_units), :],
                                     xs_hbm.at[_rows(g_unit, n_units), :], sems.at[s])

    def wait_units(s, count):
        run_copy(s, 0, 0, count).wait()

    @pl.when(step == 0)
    def _():
        zero_s[...] = jnp.zeros_like(zero_s)

        def tail_copy(e):
            n_units = tailn_ref[e]
            return pltpu.make_async_copy(zero_s.at[pl.ds(0, n_units * RUN_ALIGN), :],
                                         xs_hbm.at[_rows(tail8_ref[e], n_units), :], zsem)

        def start_unused(m, carry):
            unused_block_copy(m).start()
            return carry
        lax.fori_loop(nv_ref[0], n_blocks, start_unused, 0)

        def start_e(e, carry):
            @pl.when(tailn_ref[e] > 0)
            def _():
                tail_copy(e).start()
            return carry
        lax.fori_loop(0, N_EXPERTS, start_e, 0)

        def wait_e(e, carry):
            @pl.when(tailn_ref[e] > 0)
            def _():
                tail_copy(e).wait()
            return carry
        lax.fori_loop(0, N_EXPERTS, wait_e, 0)

    lane_f = lax.broadcasted_iota(jnp.int32, (TD, LANES), 1).astype(F32)
    ones8 = jnp.ones((SUBLANES, LANES), BF16)
    r_iota = lax.broadcasted_iota(jnp.int32, (S, TD), 0).astype(F32)
    nt_dims = (((1,), (1,)), ((), ()))
    for slot in range(ROUTE_TILES):
        _dispatch_tile(step * ROUTE_TILES + slot, slot, step, units_ref, g8_ref, l8_ref, tot_ref,
                       x1_ref, meta_ref, lst_ref, sorted_s, run_copy, wait_units,
                       lane_f, ones8, r_iota, nt_dims)

    @pl.when(step == n - 1)
    def _():
        for slot in range(ROUTE_TILES):
            wait_units(slot, tot_ref[step * ROUTE_TILES + slot])

        def wait_unused(m, carry):
            unused_block_copy(m).wait()
            return carry
        lax.fori_loop(nv_ref[0], n_blocks, wait_unused, 0)


def _dispatch_tile(tau, slot, step, units_ref, g8_ref, l8_ref, tot_ref, x1_ref, meta_ref, lst_ref, sorted_s,
                   run_copy, wait_units, lane_f, ones8, r_iota, nt_dims):
    TD = TILE_ROWS
    rows = slice(slot * TD, (slot + 1) * TD)

    @pl.when(step >= 1)
    def _():
        wait_units(slot, tot_ref[jnp.maximum(tau - ROUTE_TILES, 0)])

    meta = meta_ref[rows, :]
    u8 = lst_ref[slot * SUBLANES:(slot + 1) * SUBLANES, :].astype(BF16)
    u_row = lst_ref[slot * SUBLANES + 2:slot * SUBLANES + 3, :]
    gh = meta.astype(BF16).astype(F32)
    g1 = meta - gh
    gm = g1.astype(BF16).astype(F32)
    gl = g1 - gm
    lane_j = jnp.floor((lane_f + 0.5) * (1.0 / GATE_PARTS))
    lane_p = lane_f - GATE_PARTS * lane_j
    aux_g = jnp.zeros((TD, LANES), F32)
    aux_h = jnp.zeros((TD, LANES), F32)
    aux_l = jnp.zeros((TD, LANES), F32)
    conds = []
    for j in range(TOP_K):
        oh = lane_f == meta[:, j:j + 1]
        rank_col = meta[:, TOP_K + j:TOP_K + j + 1]
        ohb = jnp.where(oh, 1.0, 0.0).astype(BF16)
        rkb = jnp.where(oh, rank_col, 0.0).astype(BF16)
        m1 = lax.dot_general(u8, ohb, nt_dims, preferred_element_type=F32)
        m2 = lax.dot_general(ones8, rkb, nt_dims, preferred_element_type=F32)
        lpos = RUN_ALIGN * (LST_SPLIT * m1[0:1, :] + m1[1:2, :]) + m2[0:1, :]
        conds.append(r_iota == lpos)
        lpos_col = RUN_ALIGN * jnp.sum(jnp.where(oh, u_row, 0.0), axis=-1, keepdims=True) + rank_col
        hi_col = jnp.floor(lpos_col * (1.0 / LST_SPLIT))
        gc = 2 * TOP_K + j
        g_j = jnp.where(lane_p == 0, gh[:, gc:gc + 1], jnp.where(lane_p == 1, gm[:, gc:gc + 1], gl[:, gc:gc + 1]))
        aux_g = jnp.where(lane_j == j, g_j, aux_g)
        aux_h = jnp.where(lane_j == j, hi_col, aux_h)
        aux_l = jnp.where(lane_j == j, lpos_col - LST_SPLIT * hi_col, aux_l)
    perm = jnp.where(conds[0], 1.0, jnp.where(conds[1], 1.0, jnp.where(conds[2], 1.0,
                     jnp.where(conds[3], 1.0, 0.0)))).astype(BF16)
    sorted_x = jnp.dot(perm, x1_ref[rows, :].astype(BF16), preferred_element_type=F32)
    aux = jnp.concatenate([aux_g, aux_h, aux_l], axis=1).astype(BF16)
    sorted_aux = jnp.dot(perm, aux, preferred_element_type=F32)
    own_row = lax.broadcasted_iota(jnp.int32, (sorted_aux.shape[0], LANES), 0).astype(F32)
    match = LST_SPLIT * sorted_aux[:, LANES:2 * LANES] + sorted_aux[:, 2 * LANES:3 * LANES] == own_row
    sorted_g = jnp.where(match, sorted_aux[:, 0:LANES], 0.0)
    packed = pltpu.pack_elementwise([sorted_x[:, 0:XS_HALF], sorted_x[:, XS_HALF:D_MODEL]], packed_dtype=BF16)
    sorted_s[slot, :, 0:XS_HALF] = pltpu.bitcast(packed, F32)
    sorted_s[slot, :, XS_HALF:XS_WIDTH] = sorted_g

    _for_each_run(units_ref, tau, lambda k, cnt: run_copy(slot, l8_ref[k], g8_ref[k], cnt).start())


def _dispatch_call(sched, x1, meta, lst_rows, cap):
    T = x1.shape[0]
    RR = ROUTE_TILES * TILE_ROWS
    grid_spec = pltpu.PrefetchScalarGridSpec(
        num_scalar_prefetch=7,
        grid=(T // RR,),
        in_specs=[
            pl.BlockSpec((RR, D_MODEL), lambda i, *_: (i, 0)),
            pl.BlockSpec((RR, LANES), lambda i, *_: (i, 0)),
            pl.BlockSpec((ROUTE_TILES * SUBLANES, LANES), lambda i, *_: (i, 0)),
        ],
        out_specs=pl.BlockSpec(memory_space=pl.ANY),
        scratch_shapes=[
            pltpu.VMEM((2, SORT_ROWS, XS_WIDTH), F32),
            pltpu.VMEM((EXPERT_BLOCK, XS_WIDTH), F32),
            pltpu.SemaphoreType.DMA((2,)),
            pltpu.SemaphoreType.DMA(()),
            pltpu.SemaphoreType.DMA(()),
        ],
    )
    return pl.pallas_call(
        _dispatch_body,
        grid_spec=grid_spec,
        out_shape=jax.ShapeDtypeStruct((cap, XS_WIDTH), F32),
        compiler_params=pltpu.CompilerParams(
            dimension_semantics=("arbitrary",), vmem_limit_bytes=VMEM_LIMIT),
        name="dispatch",
    )(*sched, x1, meta, lst_rows)


def _expert_body(ite_ref, itb_ref, sz_ref, nxt_ref, par_ref, nit_ref, nv_ref,
                 xs_hbm, wg_hbm, bg_ref, wu_hbm, bu_ref, wd_hbm, bd_ref,
                 y_hbm, xbuf, ybuf, zbuf, wbuf, xsem, ysem, zsem, wsem, wg_s, wu_s, wd_s):
    n_iter = nit_ref[0]
    n_blocks = xs_hbm.shape[0] // EXPERT_BLOCK
    sizes = tuple(n * EXPERT_BLOCK for n in CHUNK_BLOCKS)

    def rows_of(i, rows):
        return pl.ds(pl.multiple_of(itb_ref[i] * EXPERT_BLOCK, EXPERT_BLOCK), rows)

    def x_copy(i, s, rows):
        return pltpu.make_async_copy(xs_hbm.at[rows_of(i, rows), :], xbuf.at[s, pl.ds(0, rows), :], xsem.at[s])

    def y_copy(i, s, rows):
        return pltpu.make_async_copy(ybuf.at[s, pl.ds(0, rows), :], y_hbm.at[rows_of(i, rows), :], ysem.at[s])

    def by_size(i, fn):
        for k, rows in enumerate(sizes):
            @pl.when(sz_ref[i] == k)
            def _():
                fn(rows)

    def zero_copy(m):
        rows = pl.ds(pl.multiple_of(m * EXPERT_BLOCK, EXPERT_BLOCK), EXPERT_BLOCK)
        return pltpu.make_async_copy(zbuf, y_hbm.at[rows, :], zsem)

    def weight_copies(e, s):
        return [pltpu.make_async_copy(w.at[e], wbuf.at[s, k], wsem.at[s])
                for k, w in enumerate((wg_hbm, wu_hbm, wd_hbm))]

    for c in weight_copies(ite_ref[0], par_ref[0]):
        c.start()
    by_size(0, lambda rows: x_copy(0, 0, rows).start())

    zbuf[...] = jnp.zeros_like(zbuf)

    def start_zero(m, carry):
        zero_copy(m).start()
        return carry
    lax.fori_loop(nv_ref[0], n_blocks, start_zero, 0)

    def ffn(s, e, rows):
        words = pltpu.bitcast(xbuf[s, 0:rows, 0:XS_HALF], jnp.uint32)
        halves = [pltpu.unpack_elementwise(words, index=k, packed_dtype=BF16, unpacked_dtype=F32)
                  for k in range(2)]
        xb = jnp.concatenate([h.astype(BF16) for h in halves], axis=1)
        gate = jnp.sum(xbuf[s, 0:rows, XS_HALF:XS_WIDTH], axis=-1, keepdims=True)
        hg = jnp.minimum(jnp.dot(xb, wg_s[...], preferred_element_type=F32) + bg_ref[e], SWIGLU_LIMIT)
        hu = jnp.clip(jnp.dot(xb, wu_s[...], preferred_element_type=F32) + bu_ref[e],
                      -SWIGLU_LIMIT, SWIGLU_LIMIT)
        hact = (hu + 1.0) * (hg * _sigmoid(SWIGLU_ALPHA * hg))
        y = jnp.dot(hact.astype(BF16), wd_s[...], preferred_element_type=F32) + bd_ref[e]
        ybuf[s, 0:rows, :] = y * gate

    def body(i, carry):
        s = lax.rem(i, 2)
        e = ite_ref[i]
        by_size(i, lambda rows: x_copy(i, s, rows).wait())

        @pl.when(i + 1 < n_iter)
        def _():
            by_size(i + 1, lambda rows: x_copy(i + 1, 1 - s, rows).start())

        @pl.when(jnp.logical_or(i == 0, e != ite_ref[jnp.maximum(i - 1, 0)]))
        def _():
            ws = par_ref[i]
            for c in weight_copies(e, ws):
                c.wait()

            @pl.when(nxt_ref[i] != e)
            def _():
                for c in weight_copies(nxt_ref[i], 1 - ws):
                    c.start()
            wg_s[...] = wbuf[ws, 0].astype(BF16)
            wu_s[...] = wbuf[ws, 1].astype(BF16)
            wd_s[...] = wbuf[ws, 2].astype(BF16)

        @pl.when(i >= 2)
        def _():
            by_size(i - 2, lambda rows: y_copy(i - 2, s, rows).wait())

        def chunk(rows):
            ffn(s, e, rows)
            y_copy(i, s, rows).start()
        by_size(i, chunk)
        return carry
    lax.fori_loop(0, n_iter, body, 0)

    @pl.when(n_iter >= 2)
    def _():
        by_size(n_iter - 2, lambda rows: y_copy(n_iter - 2, lax.rem(n_iter, 2), rows).wait())
    by_size(n_iter - 1, lambda rows: y_copy(n_iter - 1, lax.rem(n_iter - 1, 2), rows).wait())

    def wait_zero(m, carry):
        zero_copy(m).wait()
        return carry
    lax.fori_loop(nv_ref[0], n_blocks, wait_zero, 0)


def _expert_call(sched, xs, w_gate, b_gate, w_up, b_up, w_down, b_down):
    n_blocks = xs.shape[0] // EXPERT_BLOCK
    anyspec = pl.BlockSpec(memory_space=pl.ANY)
    bspec = pl.BlockSpec((N_EXPERTS, 1, D_MODEL), lambda i, *_: (0, 0, 0))
    grid_spec = pltpu.PrefetchScalarGridSpec(
        num_scalar_prefetch=len(sched),
        grid=(1,),
        in_specs=[anyspec, anyspec, bspec, anyspec, bspec, anyspec, bspec],
        out_specs=anyspec,
        scratch_shapes=[
            pltpu.VMEM((2, max(CHUNK_BLOCKS) * EXPERT_BLOCK, XS_WIDTH), F32),
            pltpu.VMEM((2, max(CHUNK_BLOCKS) * EXPERT_BLOCK, D_MODEL), F32),
            pltpu.VMEM((EXPERT_BLOCK, D_MODEL), F32),
            pltpu.VMEM((2, 3, D_MODEL, D_MODEL), F32),
            pltpu.SemaphoreType.DMA((2,)),
            pltpu.SemaphoreType.DMA((2,)),
            pltpu.SemaphoreType.DMA(()),
            pltpu.SemaphoreType.DMA((2,)),
            pltpu.VMEM((D_MODEL, D_MODEL), BF16),
            pltpu.VMEM((D_MODEL, D_MODEL), BF16),
            pltpu.VMEM((D_MODEL, D_MODEL), BF16),
        ],
    )
    return pl.pallas_call(
        _expert_body,
        grid_spec=grid_spec,
        out_shape=jax.ShapeDtypeStruct((n_blocks * EXPERT_BLOCK, D_MODEL), F32),
        compiler_params=pltpu.CompilerParams(
            dimension_semantics=("arbitrary",), vmem_limit_bytes=VMEM_LIMIT),
        name="experts",
    )(*sched, xs, w_gate, b_gate.reshape(N_EXPERTS, 1, D_MODEL),
      w_up, b_up.reshape(N_EXPERTS, 1, D_MODEL), w_down, b_down.reshape(N_EXPERTS, 1, D_MODEL))


def _combine_body(units_ref, g8_ref, l8_ref, tot_ref,
                  y_hbm, x1_ref, meta_ref, lst_ref, g_ref, b_ref, out_ref, ys_s, sems):
    step = pl.program_id(0)
    n_tiles = pl.num_programs(0) * ROUTE_TILES
    TD, S = TILE_ROWS, SORT_ROWS

    def run_copy(s, g_unit, l_unit, n_units):
        return pltpu.make_async_copy(y_hbm.at[_rows(g_unit, n_units), :],
                                     ys_s.at[s, _rows(l_unit, n_units), :], sems.at[s])

    def fetch(tile, s):
        _for_each_run(units_ref, tile, lambda k, cnt: run_copy(s, g8_ref[k], l8_ref[k], cnt).start())

    @pl.when(step == 0)
    def _():
        ys_s[...] = jnp.zeros_like(ys_s)
        fetch(0, 0)

    lane_f = lax.broadcasted_iota(jnp.int32, (TD, LANES), 1).astype(F32)
    s_iota = lax.broadcasted_iota(jnp.int32, (TD, S), 1).astype(F32)
    for h in range(ROUTE_TILES):
        tau = step * ROUTE_TILES + h
        slot = h % 2
        rows = slice(h * TD, (h + 1) * TD)

        @pl.when(tau + 1 < n_tiles)
        def _():
            fetch(tau + 1, 1 - slot)

        run_copy(slot, 0, 0, tot_ref[tau]).wait()

        meta = meta_ref[rows, :]
        u_row = lst_ref[h * SUBLANES + 2:h * SUBLANES + 3, :]
        conds = []
        for j in range(TOP_K):
            oh = lane_f == meta[:, j:j + 1]
            start8 = jnp.sum(jnp.where(oh, u_row, 0.0), axis=-1, keepdims=True)
            lpos = RUN_ALIGN * start8 + meta[:, TOP_K + j:TOP_K + j + 1]
            conds.append(s_iota == lpos)
        perm = jnp.where(conds[0], 1.0, jnp.where(conds[1], 1.0, jnp.where(conds[2], 1.0,
                         jnp.where(conds[3], 1.0, 0.0)))).astype(BF16)
        ffn = jnp.dot(perm, ys_s[slot].astype(BF16), preferred_element_type=F32)
        acc = DEEPNORM_ALPHA * x1_ref[rows, :] + ffn
        mu = jnp.mean(acc, axis=-1, keepdims=True)
        hc = acc - mu
        var = jnp.mean(hc * hc, axis=-1, keepdims=True)
        out_ref[rows, :] = hc * lax.rsqrt(var + LN_EPS) * g_ref[...] + b_ref[...]


def _combine_call(sched, y_rows, x1, meta, lst_rows, ln2_g, ln2_b):
    T = x1.shape[0]
    RR = ROUTE_TILES * TILE_ROWS
    grid_spec = pltpu.PrefetchScalarGridSpec(
        num_scalar_prefetch=4,
        grid=(T // RR,),
        in_specs=[
            pl.BlockSpec(memory_space=pl.ANY),
            pl.BlockSpec((RR, D_MODEL), lambda i, *_: (i, 0)),
            pl.BlockSpec((RR, LANES), lambda i, *_: (i, 0)),
            pl.BlockSpec((ROUTE_TILES * SUBLANES, LANES), lambda i, *_: (i, 0)),
            pl.BlockSpec((1, D_MODEL), lambda i, *_: (0, 0)),
            pl.BlockSpec((1, D_MODEL), lambda i, *_: (0, 0)),
        ],
        out_specs=pl.BlockSpec((RR, D_MODEL), lambda i, *_: (i, 0)),
        scratch_shapes=[pltpu.VMEM((2, SORT_ROWS, D_MODEL), F32), pltpu.SemaphoreType.DMA((2,))],
    )
    return pl.pallas_call(
        _combine_body,
        grid_spec=grid_spec,
        out_shape=jax.ShapeDtypeStruct((T, D_MODEL), F32),
        compiler_params=pltpu.CompilerParams(
            dimension_semantics=("arbitrary",), vmem_limit_bytes=VMEM_LIMIT),
        name="combine",
    )(*sched, y_rows, x1, meta, lst_rows, ln2_g, ln2_b)


def _np_consts():
    TL = TILE_ROWS
    r = np.arange(TL)
    same = (r[:, None] // CHUNK) == (r[None, :] // CHUNK)
    tril = (same & (r[None, :] <= r[:, None])).astype(np.float32)
    trils = (r[None, :] < r[:, None]).astype(np.float32)
    e128 = np.zeros((LANES, SSD_WIDTH), np.float32)
    for h in range(SSD_HEADS):
        e128[h, h * SSD_HEAD_DIM:(h + 1) * SSD_HEAD_DIM] = 1.0
    return tril, trils, e128


def kernel(x, w_in, hg_lower_bound, hg_norm_w, conv_w, conv_b, dt_bias, a_log, d_skip, ssd_norm_w, w_out,
           ln1_g, ln1_b, router_w, router_b, w_gate, b_gate, w_up, b_up, w_down, b_down, ln2_g, ln2_b):
    batch, seq, d = x.shape
    assert d == D_MODEL and seq % STEP_ROWS == 0 and w_in.shape[0] == DEPTH
    assert ROUTE_TILES == 2 and (batch * seq) % (ROUTE_TILES * TILE_ROWS) == 0
    T = batch * seq
    n_tiles = T // TILE_ROWS
    max_rows = T * TOP_K + n_tiles * N_EXPERTS * (RUN_ALIGN - 1)
    n_blocks = -(-max_rows // EXPERT_BLOCK) + N_EXPERTS
    cap = n_blocks * EXPERT_BLOCK
    assert SORT_ROWS >= TILE_ROWS * TOP_K + N_EXPERTS * (RUN_ALIGN - 1)

    tril, trils, e128 = _np_consts()
    w = w_in[0]
    pad_l = LANES - SSD_HEADS
    rw = jnp.pad(router_w[0], ((0, 0), (0, LANES - N_EXPERTS)))
    rwh = rw.astype(BF16)
    rwm = (rw - rwh.astype(F32)).astype(BF16)
    consts = [
        jnp.pad(w.astype(BF16), ((0, 0), (0, pad_l))),
        hg_lower_bound,
        hg_norm_w[0].reshape(1, HG_HEAD_DIM),
        conv_w[0],
        conv_b[0].reshape(1, SSD_CONV_DIM),
        jnp.pad(dt_bias[0], (0, pad_l)).reshape(1, LANES),
        jnp.pad(a_log[0], (0, pad_l)).reshape(1, LANES),
        jnp.repeat(d_skip[0], SSD_HEAD_DIM).reshape(1, SSD_WIDTH),
        ssd_norm_w[0].reshape(1, SSD_WIDTH),
        w_out[0].astype(BF16),
        ln1_g[0].reshape(1, D_MODEL),
        ln1_b[0].reshape(1, D_MODEL),
        jnp.concatenate([rwh, rwm], axis=1),
        rwh,
        jnp.pad(router_b[0], (0, LANES - N_EXPERTS)).reshape(1, LANES),
        jnp.asarray(tril, BF16), jnp.asarray(trils, BF16), jnp.asarray(e128, BF16),
    ]
    x1, meta, cnt = _mixer_call(x.reshape(T, D_MODEL), consts, batch, seq)

    counts = cnt.reshape(n_tiles, SUBLANES, LANES)[:, 0, :N_EXPERTS].astype(jnp.int32)
    c8 = (counts + RUN_ALIGN - 1) // RUN_ALIGN * RUN_ALIGN
    used = jnp.sum(c8, axis=0)
    region = (used + EXPERT_BLOCK - 1) // EXPERT_BLOCK * EXPERT_BLOCK
    region_end = jnp.cumsum(region)
    region_start = region_end - region
    gstart = region_start[None, :] + jnp.cumsum(c8, axis=0) - c8
    lstart = jnp.cumsum(c8, axis=1) - c8
    n_valid = region_end[-1] // EXPERT_BLOCK
    has = region > 0
    eidx = jnp.arange(N_EXPERTS, dtype=jnp.int32)
    suffix_min = lax.cummin(jnp.where(has, eidx, N_EXPERTS), reverse=True)
    nxt_e = jnp.concatenate([suffix_min[1:], jnp.full((1,), N_EXPERTS, jnp.int32)])
    nxt_e = jnp.where(nxt_e == N_EXPERTS, eidx, nxt_e)
    par_e = (jnp.cumsum(has.astype(jnp.int32)) - 1) % 2
    left = region // EXPERT_BLOCK
    n_of = []
    for nb in CHUNK_BLOCKS:
        n_of.append(left // nb)
        left = left % nb
    it_cnt = sum(n_of)
    it_end = jnp.cumsum(it_cnt)
    n_iter = it_end[-1]
    max_iter = n_blocks // max(CHUNK_BLOCKS) + (len(CHUNK_BLOCKS) - 1) * N_EXPERTS
    it = jnp.minimum(jnp.arange(max_iter, dtype=jnp.int32), n_iter - 1)
    it_e = jnp.minimum(jnp.sum(it_end[None, :] <= it[:, None], axis=1), N_EXPERTS - 1).astype(jnp.int32)
    it_onehot = it_e[:, None] == eidx[None, :]
    lookup = lambda tab: jnp.sum(jnp.where(it_onehot, tab[None, :].astype(jnp.int32), 0), axis=1)
    k = it - lookup(it_end - it_cnt)
    it_sz = jnp.zeros_like(it)
    it_blk = lookup(region_start // EXPERT_BLOCK)
    for idx, nb in enumerate(CHUNK_BLOCKS):
        n_here = lookup(n_of[idx])
        inside = jnp.logical_and(k >= 0, k < n_here)
        it_sz = jnp.where(inside, idx, it_sz)
        it_blk = it_blk + nb * jnp.clip(k, 0, n_here)
        k = k - n_here
    expert_sched = (it_e, it_blk.astype(jnp.int32), it_sz.astype(jnp.int32), lookup(nxt_e).astype(jnp.int32),
                    lookup(par_e).astype(jnp.int32),
                    n_iter.astype(jnp.int32).reshape(1), n_valid.astype(jnp.int32).reshape(1))
    as_units = lambda a: (a // RUN_ALIGN).astype(jnp.int32).reshape(-1)
    units, g8, l8 = as_units(c8), as_units(gstart), as_units(lstart)
    tot = (jnp.sum(c8, axis=1) // RUN_ALIGN).astype(jnp.int32)
    tail8 = as_units(region_start + used)
    tailn = as_units(region - used)
    lu = lstart // RUN_ALIGN
    lst3 = jnp.stack([lu // LST_SPLIT, lu % LST_SPLIT, lu], axis=1).astype(F32)
    lst_rows = jnp.pad(lst3, ((0, 0), (0, SUBLANES - 3), (0, LANES - N_EXPERTS))).reshape(
        n_tiles * SUBLANES, LANES)

    nv = n_valid.astype(jnp.int32).reshape(1)
    xs = _dispatch_call((units, g8, l8, tot, tail8, tailn, nv), x1, meta, lst_rows, cap)
    y_rows = _expert_call(expert_sched, xs, w_gate[0], b_gate[0], w_up[0], b_up[0], w_down[0], b_down[0])
    out = _combine_call((units, g8, l8, tot), y_rows, x1, meta, lst_rows,
                        ln2_g[0].reshape(1, D_MODEL), ln2_b[0].reshape(1, D_MODEL))
    return out.reshape(batch, seq, D_MODEL)
```

```python
import functools

import jax
import jax.numpy as jnp
import numpy as np
from jax import lax
from jax.experimental import pallas as pl
from jax.experimental.pallas import tpu as pltpu

F32 = jnp.float32
BF16 = jnp.bfloat16

D_MODEL = 1024
CHUNK = 64
HG_WIDTH = 512
HG_HEAD_DIM = 128
HG_HEADS = 4
SSD_WIDTH = 512
SSD_HEAD_DIM = 64
SSD_HEADS = 8
SSD_GROUPS = 2
SSD_STATE = 128
SSD_CONV = 4
SSD_CONV_DIM = SSD_WIDTH + 2 * SSD_GROUPS * SSD_STATE
N_EXPERTS = 32
TOP_K = 4
EXPERT_BLOCK = 128
CHUNK_BLOCKS = (4, 2, 1)
SWIGLU_LIMIT = 7.0
SWIGLU_ALPHA = 1.702
DEPTH = 1
DEEPNORM_ALPHA = (2 * DEPTH) ** 0.25
LN_EPS = 1e-5
RMS_EPS = 1e-5

LANES = 128
SUBLANES = 8
SUB_CHUNK = 16
EXP_CAP = 60.0
TILE_ROWS = 256
MIX_TILES = 2
ROUTE_TILES = 2
STEP_ROWS = MIX_TILES * TILE_ROWS
RUN_ALIGN = SUBLANES
SORT_ROWS = 1280
LST_SPLIT = 32
DMA_PRIORITIES = 2
GATE_PARTS = 3
XS_HALF = D_MODEL // 2
XS_WIDTH = XS_HALF + LANES
VMEM_LIMIT = 56 * 1024 * 1024

OFF_Q, OFF_F, OFF_I, OFF_G = 0, 512, 1024, 1536
OFF_Z, OFF_XBC = 2048, 2560
OFF_XS, OFF_B, OFF_C = 2560, 3072, 3328
OFF_DT = 3584
PROJ_COLS = OFF_DT + LANES
PROJ_SLICE = 256
CONV_SLICE = 256
OUT_SLICE = 256


def _bdot(a, b):
    return jnp.dot(a.astype(BF16), b.astype(BF16), preferred_element_type=F32)


def _bdot_nt(a, b):
    return lax.dot_general(a.astype(BF16), b.astype(BF16), (((1,), (1,)), ((), ())),
                           preferred_element_type=F32)


def _bdot_tn(a, b):
    return lax.dot_general(a.astype(BF16), b.astype(BF16), (((0,), (0,)), ((), ())),
                           preferred_element_type=F32)


def _split3(a):
    hi = a.astype(BF16)
    r1 = a - hi.astype(F32)
    mid = r1.astype(BF16)
    lo = (r1 - mid.astype(F32)).astype(BF16)
    return hi, mid, lo


def _dot01_left(m01, a):
    hi, mid, lo = _split3(a)
    d = functools.partial(jnp.dot, m01, preferred_element_type=F32)
    return d(hi) + d(mid) + d(lo)


def _dot01_right(a, m01):
    hi, mid, lo = _split3(a)
    return (jnp.dot(hi, m01, preferred_element_type=F32) + jnp.dot(mid, m01, preferred_element_type=F32)
            + jnp.dot(lo, m01, preferred_element_type=F32))


def _sigmoid(x):
    return 1.0 / (1.0 + jnp.exp(-x))


def _silu(x):
    return x * _sigmoid(x)


def _softplus(x):
    return jnp.maximum(x, 0.0) + jnp.log(1.0 + jnp.exp(-jnp.abs(x)))


def _mixer_body(x_ref, xnext_ref, win_ref, lbp_ref, hgnw_ref, convw_ref, convb_ref, dtb_ref, alog_ref, dskip_ref,
                ssdnw_ref, wout_ref, ln1g_ref, ln1b_ref, rw1_ref, rwh_ref, rb_ref,
                tril_ref, trils_ref, e128_ref,
                x1_ref, meta_ref, cnt_ref,
                proj_s, xb_s, xn_s, b_s, xpad_s, xdt_s, cse_s, st_s, pt_s, cat_s, ossd_s, mix_s):
    TL = TILE_ROWS
    t = pl.program_id(1)

    @pl.when(t == 0)
    def _():
        xpad_s[0:SUBLANES, :] = jnp.zeros((SUBLANES, SSD_CONV_DIM), F32)
        st_s[...] = jnp.zeros_like(st_s)
        pt_s[...] = jnp.zeros_like(pt_s)

    row64 = lax.broadcasted_iota(jnp.int32, (CHUNK, CHUNK), 0)
    col64 = lax.broadcasted_iota(jnp.int32, (CHUNK, CHUNK), 1)
    causal = row64 >= col64
    gw = SSD_WIDTH // SSD_GROUPS
    hpg = SSD_HEADS // SSD_GROUPS
    lane_head = lax.broadcasted_iota(jnp.int32, (CHUNK, gw), 1) // SSD_HEAD_DIM

    pending = []

    def filler():
        if pending:
            pending.pop(0)()

    def project_slices(lhs_ref, lhs_base, base):
        def one(c0, c1):
            def run():
                proj_s[base:base + TL, c0:c1] = jnp.dot(lhs_ref[lhs_base:lhs_base + TL, :], win_ref[:, c0:c1],
                                                        preferred_element_type=F32)
            return run
        return [one(c0, min(c0 + PROJ_SLICE, PROJ_COLS)) for c0 in range(0, PROJ_COLS, PROJ_SLICE)]

    def hg_front(base):
        a0 = lbp_ref[0:1, :]
        a1 = lbp_ref[1:2, :]
        am = jnp.maximum(a0, a1)
        e0 = jnp.exp(a0 - am)
        e1 = jnp.exp(a1 - am)
        lb = e0 / (e0 + e1)
        f = lb + (1.0 - lb) * _sigmoid(proj_s[base:base + TL, OFF_F:OFF_F + HG_WIDTH])
        b_s[base:base + TL, :] = _dot01_left(tril_ref[...], jnp.log(f))
        proj_s[base:base + TL, OFF_F:OFF_F + HG_WIDTH] = f
        proj_s[base:base + TL, OFF_Q:OFF_Q + HG_WIDTH] = _silu(proj_s[base:base + TL, OFF_Q:OFF_Q + HG_WIDTH])
        units = {}
        for c in range(TL // CHUNK):
            r0 = base + c * CHUNK
            for h in range(HG_HEADS):
                h0 = h * HG_HEAD_DIM
                bc = b_s[r0:r0 + CHUNK, h0:h0 + HG_HEAD_DIM]
                qc = proj_s[r0:r0 + CHUNK, OFF_Q + h0:OFF_Q + h0 + HG_HEAD_DIM]
                kc = 1.0 - proj_s[r0:r0 + CHUNK, OFF_F + h0:OFF_F + h0 + HG_HEAD_DIM]
                vcb = proj_s[r0:r0 + CHUNK, OFF_I + h0:OFF_I + h0 + HG_HEAD_DIM].astype(BF16)
                parts = []
                for i in range(CHUNK // SUB_CHUNK):
                    s0 = i * SUB_CHUNK
                    if i == 0:
                        qi = qc[0:SUB_CHUNK] * jnp.exp(bc[0:SUB_CHUNK])
                        ki = kc * jnp.exp(jnp.minimum(-bc, EXP_CAP))
                    else:
                        ref_i = bc[s0 - 1:s0, :]
                        qi = qc[s0:s0 + SUB_CHUNK] * jnp.exp(bc[s0:s0 + SUB_CHUNK] - ref_i)
                        ki = kc * jnp.exp(jnp.minimum(ref_i - bc, EXP_CAP))
                    parts.append(_bdot_nt(qi, ki))
                b_end = bc[CHUNK - 1:CHUNK, :]
                kdec = kc * jnp.exp(b_end - bc)
                units[c, h] = dict(
                    parts=parts, vcb=vcb, qdec=(qc * jnp.exp(bc)).astype(BF16),
                    local=_bdot_tn(vcb, kdec),
                    decay=jnp.exp(b_end))
                if h % 2 == 1:
                    filler()
        return units

    def hg_back(base, units):
        hgnw = hgnw_ref[...]
        for h in range(HG_HEADS):
            st = st_s[h]
            for c in range(TL // CHUNK):
                u = units[c, h]
                u['state'] = st
                st = st * u['decay'] + u['local']
            st_s[h] = st
        for c in range(TL // CHUNK):
            r0 = base + c * CHUNK
            for h in range(HG_HEADS):
                h0 = h * HG_HEAD_DIM
                u = units[c, h]
                sc = jnp.where(causal, jnp.concatenate(u['parts'], axis=0), 0.0)
                o = _bdot(sc, u['vcb']) + _bdot_nt(u['qdec'], u['state'])
                ms = jnp.mean(o * o, axis=-1, keepdims=True)
                on = o * lax.rsqrt(ms + RMS_EPS) * hgnw
                gc = proj_s[r0:r0 + CHUNK, OFF_G + h0:OFF_G + h0 + HG_HEAD_DIM]
                cat_s[r0:r0 + CHUNK, h0:h0 + HG_HEAD_DIM] = (on * _silu(gc)).astype(BF16)
                filler()

    def ssd_front(base):
        e128 = e128_ref[...]
        dtc = _softplus(proj_s[base:base + TL, OFF_DT:OFF_DT + LANES] + dtb_ref[...])
        a_row = -jnp.exp(alog_ref[...])
        cs_c = _dot01_left(tril_ref[...], dtc * a_row)
        cs_r = jnp.transpose(cs_c)[0:SSD_HEADS, :]
        cse_s[base:base + TL, :] = _dot01_right(cs_c, e128)
        dt_exp = _dot01_right(dtc, e128)

        xpad_s[SUBLANES + base:SUBLANES + base + TL, :] = proj_s[base:base + TL, OFF_XBC:OFF_XBC + SSD_CONV_DIM]
        for c0 in range(0, SSD_CONV_DIM, CONV_SLICE):
            acc = jnp.broadcast_to(convb_ref[:, c0:c0 + CONV_SLICE], (TL, CONV_SLICE))
            for j in range(SSD_CONV):
                off = base + SUBLANES - (SSD_CONV - 1) + j
                acc = acc + convw_ref[j:j + 1, c0:c0 + CONV_SLICE] * xpad_s[off:off + TL, c0:c0 + CONV_SLICE]
            proj_s[base:base + TL, OFF_XBC + c0:OFF_XBC + c0 + CONV_SLICE] = _silu(acc)
            filler()
        xdt_s[base:base + TL, :] = proj_s[base:base + TL, OFF_XS:OFF_XS + SSD_WIDTH] * dt_exp

        units = {}
        for c in range(TL // CHUNK):
            r0 = base + c * CHUNK
            for g in range(SSD_GROUPS):
                g0 = g * gw
                bgb = proj_s[r0:r0 + CHUNK, OFF_B + g * SSD_STATE:OFF_B + (g + 1) * SSD_STATE].astype(BF16)
                cgb = proj_s[r0:r0 + CHUNK, OFF_C + g * SSD_STATE:OFF_C + (g + 1) * SSD_STATE].astype(BF16)
                cse_g = cse_s[r0:r0 + CHUNK, g0:g0 + gw]
                cs_end = cse_s[r0 + CHUNK - 1:r0 + CHUNK, g0:g0 + gw]
                xdt_g = xdt_s[r0:r0 + CHUNK, g0:g0 + gw]
                units[c, g] = dict(
                    cgb=cgb, gm=_bdot_nt(cgb, bgb),
                    local=_bdot_tn(bgb, xdt_g * jnp.exp(cs_end - cse_g)),
                    decay=jnp.exp(cs_end))
                filler()
        return units, cs_r

    def ssd_back(base, units, cs_r):
        for g in range(SSD_GROUPS):
            pt = pt_s[g]
            for c in range(TL // CHUNK):
                u = units[c, g]
                u['state'] = pt
                pt = pt * u['decay'] + u['local']
            pt_s[g] = pt
        for c in range(TL // CHUNK):
            r0 = base + c * CHUNK
            for g in range(SSD_GROUPS):
                g0 = g * gw
                u = units[c, g]
                cse_g = cse_s[r0:r0 + CHUNK, g0:g0 + gw]
                xdt_g = xdt_s[r0:r0 + CHUNK, g0:g0 + gw]
                ydiag = jnp.zeros((CHUNK, gw), F32)
                for hl in range(hpg):
                    hh = g * hpg + hl
                    seg = (cse_g[:, hl * SSD_HEAD_DIM:(hl + 1) * SSD_HEAD_DIM]
                           - cs_r[hh:hh + 1, c * CHUNK:(c + 1) * CHUNK])
                    lm = jnp.where(causal, jnp.exp(jnp.minimum(seg, 0.0)), 0.0)
                    xm = jnp.where(lane_head == hl, xdt_g, 0.0)
                    ydiag = ydiag + _bdot(u['gm'] * lm, xm)
                yoff = _bdot(u['cgb'], u['state']) * jnp.exp(cse_g)
                xs_g = proj_s[r0:r0 + CHUNK, OFF_XS + g0:OFF_XS + g0 + gw]
                ossd_s[r0:r0 + CHUNK, g0:g0 + gw] = ydiag + yoff + xs_g * dskip_ref[:, g0:g0 + gw]
                filler()

    def ssd_gate_norm(tile):
        base = tile * TL
        y = ossd_s[base:base + TL, :] * _silu(proj_s[base:base + TL, OFF_Z:OFF_Z + SSD_WIDTH])
        for g in range(SSD_GROUPS):
            yg = y[:, g * gw:(g + 1) * gw]
            ms = jnp.mean(yg * yg, axis=-1, keepdims=True)
            yn = yg * lax.rsqrt(ms + RMS_EPS) * ssdnw_ref[:, g * gw:(g + 1) * gw]
            cat_s[base:base + TL, HG_WIDTH + g * gw:HG_WIDTH + (g + 1) * gw] = yn.astype(BF16)

    def out_slice(tile, c0):
        base = tile * TL
        mix_s[base:base + TL, c0:c0 + OUT_SLICE] = jnp.dot(
            cat_s[base:base + TL, :], wout_ref[:, c0:c0 + OUT_SLICE], preferred_element_type=F32)

    def norm_route(tile):
        base = tile * TL
        hres = DEEPNORM_ALPHA * x_ref[base:base + TL, :] + mix_s[base:base + TL, :]
        mu = jnp.mean(hres, axis=-1, keepdims=True)
        hc = hres - mu
        var = jnp.mean(hc * hc, axis=-1, keepdims=True)
        x1 = hc * lax.rsqrt(var + LN_EPS) * ln1g_ref[...] + ln1b_ref[...]
        x1_ref[base:base + TL, :] = x1
        filler()

        xh = x1.astype(BF16)
        xm_ = (x1 - xh.astype(F32)).astype(BF16)
        t1 = jnp.dot(xh, rw1_ref[...], preferred_element_type=F32)
        logits = (t1[:, 0:LANES] + t1[:, LANES:2 * LANES]
                  + jnp.dot(xm_, rwh_ref[...], preferred_element_type=F32) + rb_ref[...])
        filler()
        lane = lax.broadcasted_iota(jnp.int32, (TL, LANES), 1)
        lane_f = lane.astype(F32)
        neg = jnp.float32(-jnp.inf)
        work = jnp.where(lane < N_EXPERTS, logits, neg)
        onehots, vals, idxs = [], [], []
        for j in range(TOP_K):
            m = jnp.max(work, axis=-1, keepdims=True)
            filler()
            idx = jnp.min(jnp.where(work == m, lane_f, float(LANES)), axis=-1, keepdims=True)
            oh = lane_f == idx
            onehots.append(oh)
            vals.append(m)
            idxs.append(idx)
            work = jnp.where(oh, neg, work)
            filler()
        es = [jnp.exp(v - vals[0]) for v in vals]
        den = es[0] + es[1] + es[2] + es[3]
        gates = [e / den for e in es]
        sel = jnp.zeros((TL, LANES), F32)
        for oh in onehots:
            sel = jnp.where(oh, 1.0, sel)
        rankmat = jnp.dot(trils_ref[...], sel.astype(BF16), preferred_element_type=F32)
        filler()
        cnt_ref[tile * SUBLANES:(tile + 1) * SUBLANES, :] = jnp.broadcast_to(
            jnp.sum(sel, axis=0, keepdims=True), (SUBLANES, LANES))
        meta = jnp.zeros((TL, LANES), F32)
        for j in range(TOP_K):
            rank_j = jnp.sum(jnp.where(onehots[j], rankmat, 0.0), axis=-1, keepdims=True)
            meta = jnp.where(lane == j, idxs[j], meta)
            meta = jnp.where(lane == TOP_K + j, rank_j, meta)
            meta = jnp.where(lane == 2 * TOP_K + j, gates[j], meta)
        meta_ref[base:base + TL, :] = meta

    def post_thunks(tile):
        return ([functools.partial(ssd_gate_norm, tile)]
                + [functools.partial(out_slice, tile, c0) for c0 in range(0, D_MODEL, OUT_SLICE)]
                + [functools.partial(norm_route, tile)])

    xb_s[...] = x_ref[...].astype(BF16)
    xn_s[...] = xnext_ref[...].astype(BF16)

    @pl.when(jnp.logical_and(pl.program_id(0) == 0, t == 0))
    def _():
        for run in project_slices(xb_s, 0, 0):
            run()

    for tile in range(MIX_TILES):
        base = tile * TL
        if tile + 1 < MIX_TILES:
            pending.extend(project_slices(xb_s, base + TL, base + TL))
        hg_units = hg_front(base)
        ssd_units, cs_r = ssd_front(base)
        while pending:
            filler()
        hg_back(base, hg_units)
        ssd_back(base, ssd_units, cs_r)
        if tile + 1 < MIX_TILES:
            pending.extend(post_thunks(tile))
    tail = post_thunks(MIX_TILES - 1)
    slices = project_slices(xn_s, 0, 0)
    tail[0]()
    for run in tail[1:-1]:
        run()
        slices.pop(0)()
    pending.extend(slices)
    tail[-1]()
    while pending:
        filler()
    xpad_s[0:SUBLANES, :] = xpad_s[STEP_ROWS:STEP_ROWS + SUBLANES, :]


def _full(shape):
    nd = len(shape)
    return pl.BlockSpec(shape, lambda *_: (0,) * nd)


def _mixer_call(x2d, consts, batch, seq):
    SR = STEP_ROWS
    nt = seq // SR
    T = batch * seq
    last_tile = T // TILE_ROWS - 1
    in_specs = [pl.BlockSpec((SR, D_MODEL), lambda b, t: (b * nt + t, 0)),
                pl.BlockSpec((TILE_ROWS, D_MODEL),
                             lambda b, t: (jnp.minimum((b * nt + t + 1) * MIX_TILES, last_tile), 0))]
    in_specs += [_full(c.shape) for c in consts]
    out_shape = (jax.ShapeDtypeStruct((T, D_MODEL), F32),
                 jax.ShapeDtypeStruct((T, LANES), F32),
                 jax.ShapeDtypeStruct((T // TILE_ROWS * SUBLANES, LANES), F32))
    out_specs = (pl.BlockSpec((SR, D_MODEL), lambda b, t: (b * nt + t, 0)),
                 pl.BlockSpec((SR, LANES), lambda b, t: (b * nt + t, 0)),
                 pl.BlockSpec((MIX_TILES * SUBLANES, LANES), lambda b, t: (b * nt + t, 0)))
    scratch = [
        pltpu.VMEM((SR, PROJ_COLS), F32),
        pltpu.VMEM((SR, D_MODEL), BF16),
        pltpu.VMEM((TILE_ROWS, D_MODEL), BF16),
        pltpu.VMEM((SR, HG_WIDTH), F32),
        pltpu.VMEM((SR + 2 * SUBLANES, SSD_CONV_DIM), F32),
        pltpu.VMEM((SR, SSD_WIDTH), F32),
        pltpu.VMEM((SR, SSD_WIDTH), F32),
        pltpu.VMEM((HG_HEADS, HG_HEAD_DIM, HG_HEAD_DIM), F32),
        pltpu.VMEM((SSD_GROUPS, SSD_STATE, SSD_WIDTH // SSD_GROUPS), F32),
        pltpu.VMEM((SR, D_MODEL), BF16),
        pltpu.VMEM((SR, SSD_WIDTH), F32),
        pltpu.VMEM((SR, D_MODEL), F32),
    ]
    return pl.pallas_call(
        _mixer_body,
        grid=(batch, nt),
        in_specs=in_specs,
        out_specs=out_specs,
        out_shape=out_shape,
        scratch_shapes=scratch,
        compiler_params=pltpu.CompilerParams(
            dimension_semantics=("arbitrary", "arbitrary"), vmem_limit_bytes=VMEM_LIMIT),
        name="mixer",
    )(x2d, x2d, *consts)


def _for_each_run(units_ref, tile, fn):
    def per_pair(i, carry):
        for priority in range(DMA_PRIORITIES):
            k = tile * N_EXPERTS + DMA_PRIORITIES * i + priority
            n = units_ref[k]

            @pl.when(n > 0)
            def _():
                fn(k, n, priority)
        return carry
    lax.fori_loop(0, N_EXPERTS // DMA_PRIORITIES, per_pair, 0)


def _rows(unit, n_units=1):
    start = unit * RUN_ALIGN
    if RUN_ALIGN > 1:
        start = pl.multiple_of(start, RUN_ALIGN)
    return pl.ds(start, n_units * RUN_ALIGN)


def _dispatch_body(units_ref, g8_ref, l8_ref, tot_ref, tail8_ref, tailn_ref, nv_ref,
                   x1_ref, meta_ref, lst_ref, xs_hbm, sorted_s, zero_s, sems, zsem, bsem):
    step = pl.program_id(0)
    n = pl.num_programs(0)
    TD, S = TILE_ROWS, SORT_ROWS
    n_blocks = xs_hbm.shape[0] // EXPERT_BLOCK

    def unused_block_copy(m):
        rows = pl.ds(pl.multiple_of(m * EXPERT_BLOCK, EXPERT_BLOCK), EXPERT_BLOCK)
        return pltpu.make_async_copy(zero_s, xs_hbm.at[rows, :], bsem)

    def run_copy(s, l_unit, g_unit, n_units):
        return pltpu.make_async_copy(sorted_s.at[s, _rows(l_unit, n_units), :],
                                     xs_hbm.at[_rows(g_unit, n_units), :], sems.at[s])

    def wait_units(s, count):
        run_copy(s, 0, 0, count).wait()

    @pl.when(step == 0)
    def _():
        zero_s[...] = jnp.zeros_like(zero_s)

        def tail_copy(e):
            n_units = tailn_ref[e]
            return pltpu.make_async_copy(zero_s.at[pl.ds(0, n_units * RUN_ALIGN), :],
                                         xs_hbm.at[_rows(tail8_ref[e], n_units), :], zsem)

        def start_unused(m, carry):
            unused_block_copy(m).start()
            return carry
        lax.fori_loop(nv_ref[0], n_blocks, start_unused, 0)

        def start_e(e, carry):
            @pl.when(tailn_ref[e] > 0)
            def _():
                tail_copy(e).start()
            return carry
        lax.fori_loop(0, N_EXPERTS, start_e, 0)

        def wait_e(e, carry):
            @pl.when(tailn_ref[e] > 0)
            def _():
                tail_copy(e).wait()
            return carry
        lax.fori_loop(0, N_EXPERTS, wait_e, 0)

    lane_f = lax.broadcasted_iota(jnp.int32, (TD, LANES), 1).astype(F32)
    ones8 = jnp.ones((SUBLANES, LANES), BF16)
    r_iota = lax.broadcasted_iota(jnp.int32, (S, TD), 0).astype(F32)
    nt_dims = (((1,), (1,)), ((), ()))
    for slot in range(ROUTE_TILES):
        _dispatch_tile(step * ROUTE_TILES + slot, slot, step, units_ref, g8_ref, l8_ref, tot_ref,
                       x1_ref, meta_ref, lst_ref, sorted_s, run_copy, wait_units,
                       lane_f, ones8, r_iota, nt_dims)

    @pl.when(step == n - 1)
    def _():
        for slot in range(ROUTE_TILES):
            wait_units(slot, tot_ref[step * ROUTE_TILES + slot])

        def wait_unused(m, carry):
            unused_block_copy(m).wait()
            return carry
        lax.fori_loop(nv_ref[0], n_blocks, wait_unused, 0)


def _dispatch_tile(tau, slot, step, units_ref, g8_ref, l8_ref, tot_ref, x1_ref, meta_ref, lst_ref, sorted_s,
                   run_copy, wait_units, lane_f, ones8, r_iota, nt_dims):
    TD = TILE_ROWS
    rows = slice(slot * TD, (slot + 1) * TD)

    @pl.when(step >= 1)
    def _():
        wait_units(slot, tot_ref[jnp.maximum(tau - ROUTE_TILES, 0)])

    meta = meta_ref[rows, :]
    u8 = lst_ref[slot * SUBLANES:(slot + 1) * SUBLANES, :].astype(BF16)
    u_row = lst_ref[slot * SUBLANES + 2:slot * SUBLANES + 3, :]
    gh = meta.astype(BF16).astype(F32)
    g1 = meta - gh
    gm = g1.astype(BF16).astype(F32)
    gl = g1 - gm
    lane_j = jnp.floor((lane_f + 0.5) * (1.0 / GATE_PARTS))
    lane_p = lane_f - GATE_PARTS * lane_j
    aux_g = jnp.zeros((TD, LANES), F32)
    aux_h = jnp.zeros((TD, LANES), F32)
    aux_l = jnp.zeros((TD, LANES), F32)
    conds = []
    for j in range(TOP_K):
        oh = lane_f == meta[:, j:j + 1]
        rank_col = meta[:, TOP_K + j:TOP_K + j + 1]
        ohb = jnp.where(oh, 1.0, 0.0).astype(BF16)
        rkb = jnp.where(oh, rank_col, 0.0).astype(BF16)
        m1 = lax.dot_general(u8, ohb, nt_dims, preferred_element_type=F32)
        m2 = lax.dot_general(ones8, rkb, nt_dims, preferred_element_type=F32)
        lpos = RUN_ALIGN * (LST_SPLIT * m1[0:1, :] + m1[1:2, :]) + m2[0:1, :]
        conds.append(r_iota == lpos)
        lpos_col = RUN_ALIGN * jnp.sum(jnp.where(oh, u_row, 0.0), axis=-1, keepdims=True) + rank_col
        hi_col = jnp.floor(lpos_col * (1.0 / LST_SPLIT))
        gc = 2 * TOP_K + j
        g_j = jnp.where(lane_p == 0, gh[:, gc:gc + 1], jnp.where(lane_p == 1, gm[:, gc:gc + 1], gl[:, gc:gc + 1]))
        aux_g = jnp.where(lane_j == j, g_j, aux_g)
        aux_h = jnp.where(lane_j == j, hi_col, aux_h)
        aux_l = jnp.where(lane_j == j, lpos_col - LST_SPLIT * hi_col, aux_l)
    perm = jnp.where(conds[0], 1.0, jnp.where(conds[1], 1.0, jnp.where(conds[2], 1.0,
                     jnp.where(conds[3], 1.0, 0.0)))).astype(BF16)
    sorted_x = jnp.dot(perm, x1_ref[rows, :].astype(BF16), preferred_element_type=F32)
    aux = jnp.concatenate([aux_g, aux_h, aux_l], axis=1).astype(BF16)
    sorted_aux = jnp.dot(perm, aux, preferred_element_type=F32)
    own_row = lax.broadcasted_iota(jnp.int32, (sorted_aux.shape[0], LANES), 0).astype(F32)
    match = LST_SPLIT * sorted_aux[:, LANES:2 * LANES] + sorted_aux[:, 2 * LANES:3 * LANES] == own_row
    sorted_g = jnp.where(match, sorted_aux[:, 0:LANES], 0.0)
    packed = pltpu.pack_elementwise([sorted_x[:, 0:XS_HALF], sorted_x[:, XS_HALF:D_MODEL]], packed_dtype=BF16)
    sorted_s[slot, :, 0:XS_HALF] = pltpu.bitcast(packed, F32)
    sorted_s[slot, :, XS_HALF:XS_WIDTH] = sorted_g

    _for_each_run(units_ref, tau,
                  lambda k, cnt, prio: run_copy(slot, l8_ref[k], g8_ref[k], cnt).start(priority=prio))


def _dispatch_call(sched, x1, meta, lst_rows, cap):
    T = x1.shape[0]
    RR = ROUTE_TILES * TILE_ROWS
    grid_spec = pltpu.PrefetchScalarGridSpec(
        num_scalar_prefetch=7,
        grid=(T // RR,),
        in_specs=[
            pl.BlockSpec((RR, D_MODEL), lambda i, *_: (i, 0)),
            pl.BlockSpec((RR, LANES), lambda i, *_: (i, 0)),
            pl.BlockSpec((ROUTE_TILES * SUBLANES, LANES), lambda i, *_: (i, 0)),
        ],
        out_specs=pl.BlockSpec(memory_space=pl.ANY),
        scratch_shapes=[
            pltpu.VMEM((2, SORT_ROWS, XS_WIDTH), F32),
            pltpu.VMEM((EXPERT_BLOCK, XS_WIDTH), F32),
            pltpu.SemaphoreType.DMA((2,)),
            pltpu.SemaphoreType.DMA(()),
            pltpu.SemaphoreType.DMA(()),
        ],
    )
    return pl.pallas_call(
        _dispatch_body,
        grid_spec=grid_spec,
        out_shape=jax.ShapeDtypeStruct((cap, XS_WIDTH), F32),
        compiler_params=pltpu.CompilerParams(
            dimension_semantics=("arbitrary",), vmem_limit_bytes=VMEM_LIMIT),
        name="dispatch",
    )(*sched, x1, meta, lst_rows)


def _expert_body(ite_ref, itb_ref, sz_ref, nxt_ref, par_ref, nit_ref, nv_ref,
                 xs_hbm, wg_hbm, bg_ref, wu_hbm, bu_ref, wd_hbm, bd_ref,
                 y_hbm, xbuf, ybuf, zbuf, wbuf, xsem, ysem, zsem, wsem, wg_s, wu_s, wd_s):
    n_iter = nit_ref[0]
    n_blocks = xs_hbm.shape[0] // EXPERT_BLOCK
    sizes = tuple(n * EXPERT_BLOCK for n in CHUNK_BLOCKS)

    def rows_of(i, rows):
        return pl.ds(pl.multiple_of(itb_ref[i] * EXPERT_BLOCK, EXPERT_BLOCK), rows)

    def x_copy(i, s, rows):
        return pltpu.make_async_copy(xs_hbm.at[rows_of(i, rows), :], xbuf.at[s, pl.ds(0, rows), :], xsem.at[s])

    def y_copy(i, s, rows):
        return pltpu.make_async_copy(ybuf.at[s, pl.ds(0, rows), :], y_hbm.at[rows_of(i, rows), :], ysem.at[s])

    def by_size(i, fn):
        for k, rows in enumerate(sizes):
            @pl.when(sz_ref[i] == k)
            def _():
                fn(rows)

    def zero_copy(m):
        rows = pl.ds(pl.multiple_of(m * EXPERT_BLOCK, EXPERT_BLOCK), EXPERT_BLOCK)
        return pltpu.make_async_copy(zbuf, y_hbm.at[rows, :], zsem)

    def weight_copies(e, s):
        return [pltpu.make_async_copy(w.at[e], wbuf.at[s, k], wsem.at[s])
                for k, w in enumerate((wg_hbm, wu_hbm, wd_hbm))]

    for c in weight_copies(ite_ref[0], par_ref[0]):
        c.start()
    by_size(0, lambda rows: x_copy(0, 0, rows).start())

    zbuf[...] = jnp.zeros_like(zbuf)

    def start_zero(m, carry):
        zero_copy(m).start()
        return carry
    lax.fori_loop(nv_ref[0], n_blocks, start_zero, 0)

    def ffn(s, e, rows):
        words = pltpu.bitcast(xbuf[s, 0:rows, 0:XS_HALF], jnp.uint32)
        halves = [pltpu.unpack_elementwise(words, index=k, packed_dtype=BF16, unpacked_dtype=F32)
                  for k in range(2)]
        xb = jnp.concatenate([h.astype(BF16) for h in halves], axis=1)
        gate = jnp.sum(xbuf[s, 0:rows, XS_HALF:XS_WIDTH], axis=-1, keepdims=True)
        hg = jnp.minimum(jnp.dot(xb, wg_s[...], preferred_element_type=F32) + bg_ref[e], SWIGLU_LIMIT)
        hu = jnp.clip(jnp.dot(xb, wu_s[...], preferred_element_type=F32) + bu_ref[e],
                      -SWIGLU_LIMIT, SWIGLU_LIMIT)
        hact = (hu + 1.0) * (hg * _sigmoid(SWIGLU_ALPHA * hg))
        y = jnp.dot(hact.astype(BF16), wd_s[...], preferred_element_type=F32) + bd_ref[e]
        ybuf[s, 0:rows, :] = y * gate

    def body(i, carry):
        s = lax.rem(i, 2)
        e = ite_ref[i]
        by_size(i, lambda rows: x_copy(i, s, rows).wait())

        @pl.when(i + 1 < n_iter)
        def _():
            by_size(i + 1, lambda rows: x_copy(i + 1, 1 - s, rows).start())

        @pl.when(jnp.logical_or(i == 0, e != ite_ref[jnp.maximum(i - 1, 0)]))
        def _():
            ws = par_ref[i]
            for c in weight_copies(e, ws):
                c.wait()

            @pl.when(nxt_ref[i] != e)
            def _():
                for c in weight_copies(nxt_ref[i], 1 - ws):
                    c.start()
            wg_s[...] = wbuf[ws, 0].astype(BF16)
            wu_s[...] = wbuf[ws, 1].astype(BF16)
            wd_s[...] = wbuf[ws, 2].astype(BF16)

        @pl.when(i >= 2)
        def _():
            by_size(i - 2, lambda rows: y_copy(i - 2, s, rows).wait())

        def chunk(rows):
            ffn(s, e, rows)
            y_copy(i, s, rows).start()
        by_size(i, chunk)
        return carry
    lax.fori_loop(0, n_iter, body, 0)

    @pl.when(n_iter >= 2)
    def _():
        by_size(n_iter - 2, lambda rows: y_copy(n_iter - 2, lax.rem(n_iter, 2), rows).wait())
    by_size(n_iter - 1, lambda rows: y_copy(n_iter - 1, lax.rem(n_iter - 1, 2), rows).wait())

    def wait_zero(m, carry):
        zero_copy(m).wait()
        return carry
    lax.fori_loop(nv_ref[0], n_blocks, wait_zero, 0)


def _expert_call(sched, xs, w_gate, b_gate, w_up, b_up, w_down, b_down):
    n_blocks = xs.shape[0] // EXPERT_BLOCK
    anyspec = pl.BlockSpec(memory_space=pl.ANY)
    bspec = pl.BlockSpec((N_EXPERTS, 1, D_MODEL), lambda i, *_: (0, 0, 0))
    grid_spec = pltpu.PrefetchScalarGridSpec(
        num_scalar_prefetch=len(sched),
        grid=(1,),
        in_specs=[anyspec, anyspec, bspec, anyspec, bspec, anyspec, bspec],
        out_specs=anyspec,
        scratch_shapes=[
            pltpu.VMEM((2, max(CHUNK_BLOCKS) * EXPERT_BLOCK, XS_WIDTH), F32),
            pltpu.VMEM((2, max(CHUNK_BLOCKS) * EXPERT_BLOCK, D_MODEL), F32),
            pltpu.VMEM((EXPERT_BLOCK, D_MODEL), F32),
            pltpu.VMEM((2, 3, D_MODEL, D_MODEL), F32),
            pltpu.SemaphoreType.DMA((2,)),
            pltpu.SemaphoreType.DMA((2,)),
            pltpu.SemaphoreType.DMA(()),
            pltpu.SemaphoreType.DMA((2,)),
            pltpu.VMEM((D_MODEL, D_MODEL), BF16),
            pltpu.VMEM((D_MODEL, D_MODEL), BF16),
            pltpu.VMEM((D_MODEL, D_MODEL), BF16),
        ],
    )
    return pl.pallas_call(
        _expert_body,
        grid_spec=grid_spec,
        out_shape=jax.ShapeDtypeStruct((n_blocks * EXPERT_BLOCK, D_MODEL), F32),
        compiler_params=pltpu.CompilerParams(
            dimension_semantics=("arbitrary",), vmem_limit_bytes=VMEM_LIMIT),
        name="experts",
    )(*sched, xs, w_gate, b_gate.reshape(N_EXPERTS, 1, D_MODEL),
      w_up, b_up.reshape(N_EXPERTS, 1, D_MODEL), w_down, b_down.reshape(N_EXPERTS, 1, D_MODEL))


def _combine_body(units_ref, g8_ref, l8_ref, tot_ref,
                  y_hbm, x1_ref, meta_ref, lst_ref, g_ref, b_ref, out_ref, ys_s, sems):
    step = pl.program_id(0)
    n_tiles = pl.num_programs(0) * ROUTE_TILES
    TD, S = TILE_ROWS, SORT_ROWS

    def run_copy(s, g_unit, l_unit, n_units):
        return pltpu.make_async_copy(y_hbm.at[_rows(g_unit, n_units), :],
                                     ys_s.at[s, _rows(l_unit, n_units), :], sems.at[s])

    def fetch(tile, s):
        _for_each_run(units_ref, tile,
                      lambda k, cnt, prio: run_copy(s, g8_ref[k], l8_ref[k], cnt).start(priority=prio))

    @pl.when(step == 0)
    def _():
        ys_s[...] = jnp.zeros_like(ys_s)
        fetch(0, 0)

    lane_f = lax.broadcasted_iota(jnp.int32, (TD, LANES), 1).astype(F32)
    s_iota = lax.broadcasted_iota(jnp.int32, (TD, S), 1).astype(F32)
    for h in range(ROUTE_TILES):
        tau = step * ROUTE_TILES + h
        slot = h % 2
        rows = slice(h * TD, (h + 1) * TD)

        @pl.when(tau + 1 < n_tiles)
        def _():
            fetch(tau + 1, 1 - slot)

        run_copy(slot, 0, 0, tot_ref[tau]).wait()

        meta = meta_ref[rows, :]
        u_row = lst_ref[h * SUBLANES + 2:h * SUBLANES + 3, :]
        conds = []
        for j in range(TOP_K):
            oh = lane_f == meta[:, j:j + 1]
            start8 = jnp.sum(jnp.where(oh, u_row, 0.0), axis=-1, keepdims=True)
            lpos = RUN_ALIGN * start8 + meta[:, TOP_K + j:TOP_K + j + 1]
            conds.append(s_iota == lpos)
        perm = jnp.where(conds[0], 1.0, jnp.where(conds[1], 1.0, jnp.where(conds[2], 1.0,
                         jnp.where(conds[3], 1.0, 0.0)))).astype(BF16)
        ffn = jnp.dot(perm, ys_s[slot].astype(BF16), preferred_element_type=F32)
        acc = DEEPNORM_ALPHA * x1_ref[rows, :] + ffn
        mu = jnp.mean(acc, axis=-1, keepdims=True)
        hc = acc - mu
        var = jnp.mean(hc * hc, axis=-1, keepdims=True)
        out_ref[rows, :] = hc * lax.rsqrt(var + LN_EPS) * g_ref[...] + b_ref[...]


def _combine_call(sched, y_rows, x1, meta, lst_rows, ln2_g, ln2_b):
    T = x1.shape[0]
    RR = ROUTE_TILES * TILE_ROWS
    grid_spec = pltpu.PrefetchScalarGridSpec(
        num_scalar_prefetch=4,
        grid=(T // RR,),
        in_specs=[
            pl.BlockSpec(memory_space=pl.ANY),
            pl.BlockSpec((RR, D_MODEL), lambda i, *_: (i, 0)),
            pl.BlockSpec((RR, LANES), lambda i, *_: (i, 0)),
            pl.BlockSpec((ROUTE_TILES * SUBLANES, LANES), lambda i, *_: (i, 0)),
            pl.BlockSpec((1, D_MODEL), lambda i, *_: (0, 0)),
            pl.BlockSpec((1, D_MODEL), lambda i, *_: (0, 0)),
        ],
        out_specs=pl.BlockSpec((RR, D_MODEL), lambda i, *_: (i, 0)),
        scratch_shapes=[pltpu.VMEM((2, SORT_ROWS, D_MODEL), F32), pltpu.SemaphoreType.DMA((2,))],
    )
    return pl.pallas_call(
        _combine_body,
        grid_spec=grid_spec,
        out_shape=jax.ShapeDtypeStruct((T, D_MODEL), F32),
        compiler_params=pltpu.CompilerParams(
            dimension_semantics=("arbitrary",), vmem_limit_bytes=VMEM_LIMIT),
        name="combine",
    )(*sched, y_rows, x1, meta, lst_rows, ln2_g, ln2_b)


def _np_consts():
    TL = TILE_ROWS
    r = np.arange(TL)
    same = (r[:, None] // CHUNK) == (r[None, :] // CHUNK)
    tril = (same & (r[None, :] <= r[:, None])).astype(np.float32)
    trils = (r[None, :] < r[:, None]).astype(np.float32)
    e128 = np.zeros((LANES, SSD_WIDTH), np.float32)
    for h in range(SSD_HEADS):
        e128[h, h * SSD_HEAD_DIM:(h + 1) * SSD_HEAD_DIM] = 1.0
    return tril, trils, e128


def kernel(x, w_in, hg_lower_bound, hg_norm_w, conv_w, conv_b, dt_bias, a_log, d_skip, ssd_norm_w, w_out,
           ln1_g, ln1_b, router_w, router_b, w_gate, b_gate, w_up, b_up, w_down, b_down, ln2_g, ln2_b):
    batch, seq, d = x.shape
    assert d == D_MODEL and seq % STEP_ROWS == 0 and w_in.shape[0] == DEPTH
    assert ROUTE_TILES == 2 and (batch * seq) % (ROUTE_TILES * TILE_ROWS) == 0
    T = batch * seq
    n_tiles = T // TILE_ROWS
    max_rows = T * TOP_K + n_tiles * N_EXPERTS * (RUN_ALIGN - 1)
    n_blocks = -(-max_rows // EXPERT_BLOCK) + N_EXPERTS
    cap = n_blocks * EXPERT_BLOCK
    assert SORT_ROWS >= TILE_ROWS * TOP_K + N_EXPERTS * (RUN_ALIGN - 1)

    tril, trils, e128 = _np_consts()
    w = w_in[0]
    pad_l = LANES - SSD_HEADS
    rw = jnp.pad(router_w[0], ((0, 0), (0, LANES - N_EXPERTS)))
    rwh = rw.astype(BF16)
    rwm = (rw - rwh.astype(F32)).astype(BF16)
    consts = [
        jnp.pad(w.astype(BF16), ((0, 0), (0, pad_l))),
        hg_lower_bound,
        hg_norm_w[0].reshape(1, HG_HEAD_DIM),
        conv_w[0],
        conv_b[0].reshape(1, SSD_CONV_DIM),
        jnp.pad(dt_bias[0], (0, pad_l)).reshape(1, LANES),
        jnp.pad(a_log[0], (0, pad_l)).reshape(1, LANES),
        jnp.repeat(d_skip[0], SSD_HEAD_DIM).reshape(1, SSD_WIDTH),
        ssd_norm_w[0].reshape(1, SSD_WIDTH),
        w_out[0].astype(BF16),
        ln1_g[0].reshape(1, D_MODEL),
        ln1_b[0].reshape(1, D_MODEL),
        jnp.concatenate([rwh, rwm], axis=1),
        rwh,
        jnp.pad(router_b[0], (0, LANES - N_EXPERTS)).reshape(1, LANES),
        jnp.asarray(tril, BF16), jnp.asarray(trils, BF16), jnp.asarray(e128, BF16),
    ]
    x1, meta, cnt = _mixer_call(x.reshape(T, D_MODEL), consts, batch, seq)

    counts = cnt.reshape(n_tiles, SUBLANES, LANES)[:, 0, :N_EXPERTS].astype(jnp.int32)
    c8 = (counts + RUN_ALIGN - 1) // RUN_ALIGN * RUN_ALIGN
    used = jnp.sum(c8, axis=0)
    region = (used + EXPERT_BLOCK - 1) // EXPERT_BLOCK * EXPERT_BLOCK
    region_end = jnp.cumsum(region)
    region_start = region_end - region
    gstart = region_start[None, :] + jnp.cumsum(c8, axis=0) - c8
    lstart = jnp.cumsum(c8, axis=1) - c8
    n_valid = region_end[-1] // EXPERT_BLOCK
    has = region > 0
    eidx = jnp.arange(N_EXPERTS, dtype=jnp.int32)
    suffix_min = lax.cummin(jnp.where(has, eidx, N_EXPERTS), reverse=True)
    nxt_e = jnp.concatenate([suffix_min[1:], jnp.full((1,), N_EXPERTS, jnp.int32)])
    nxt_e = jnp.where(nxt_e == N_EXPERTS, eidx, nxt_e)
    par_e = (jnp.cumsum(has.astype(jnp.int32)) - 1) % 2
    left = region // EXPERT_BLOCK
    n_of = []
    for nb in CHUNK_BLOCKS:
        n_of.append(left // nb)
        left = left % nb
    it_cnt = sum(n_of)
    it_end = jnp.cumsum(it_cnt)
    n_iter = it_end[-1]
    max_iter = n_blocks // max(CHUNK_BLOCKS) + (len(CHUNK_BLOCKS) - 1) * N_EXPERTS
    it = jnp.minimum(jnp.arange(max_iter, dtype=jnp.int32), n_iter - 1)
    it_e = jnp.minimum(jnp.sum(it_end[None, :] <= it[:, None], axis=1), N_EXPERTS - 1).astype(jnp.int32)
    it_onehot = it_e[:, None] == eidx[None, :]
    lookup = lambda tab: jnp.sum(jnp.where(it_onehot, tab[None, :].astype(jnp.int32), 0), axis=1)
    k = it - lookup(it_end - it_cnt)
    it_sz = jnp.zeros_like(it)
    it_blk = lookup(region_start // EXPERT_BLOCK)
    for idx, nb in enumerate(CHUNK_BLOCKS):
        n_here = lookup(n_of[idx])
        inside = jnp.logical_and(k >= 0, k < n_here)
        it_sz = jnp.where(inside, idx, it_sz)
        it_blk = it_blk + nb * jnp.clip(k, 0, n_here)
        k = k - n_here
    expert_sched = (it_e, it_blk.astype(jnp.int32), it_sz.astype(jnp.int32), lookup(nxt_e).astype(jnp.int32),
                    lookup(par_e).astype(jnp.int32),
                    n_iter.astype(jnp.int32).reshape(1), n_valid.astype(jnp.int32).reshape(1))
    as_units = lambda a: (a // RUN_ALIGN).astype(jnp.int32).reshape(-1)
    units, g8, l8 = as_units(c8), as_units(gstart), as_units(lstart)
    tot = (jnp.sum(c8, axis=1) // RUN_ALIGN).astype(jnp.int32)
    tail8 = as_units(region_start + used)
    tailn = as_units(region - used)
    lu = lstart // RUN_ALIGN
    lst3 = jnp.stack([lu // LST_SPLIT, lu % LST_SPLIT, lu], axis=1).astype(F32)
    lst_rows = jnp.pad(lst3, ((0, 0), (0, SUBLANES - 3), (0, LANES - N_EXPERTS))).reshape(
        n_tiles * SUBLANES, LANES)

    nv = n_valid.astype(jnp.int32).reshape(1)
    xs = _dispatch_call((units, g8, l8, tot, tail8, tailn, nv), x1, meta, lst_rows, cap)
    y_rows = _expert_call(expert_sched, xs, w_gate[0], b_gate[0], w_up[0], b_up[0], w_down[0], b_down[0])
    out = _combine_call((units, g8, l8, tot), y_rows, x1, meta, lst_rows,
                        ln2_g[0].reshape(1, D_MODEL), ln2_b[0].reshape(1, D_MODEL))
    return out.reshape(batch, seq, D_MODEL)
```
